```python
import functools
import jax, jax.numpy as jnp
from jax import lax
import numpy as np

D_MODEL = 1024
BATCH = 4
SEQ = 4096
DEPTH = 1
DEC_BATCH = 32
DEC_SEQ = 8
PAST_LEN = 16384
PAGE_SIZE = 128

A_GROUPS = ((128, 1), (512, 4), (2048, 16))
A_N_GROUPS = 3
A_HEADS = 8
A_HEAD_DIM = 64
A_WIDTH = A_HEADS * A_HEAD_DIM
A_STEPS = 128
A_BLOCK = 128
B_CHUNK = 128
B_GROUPS = 8
B_WIDTH = D_MODEL
B_GROUP_DIM = B_WIDTH // B_GROUPS
MOE_GROUPS = 4
MOE_EXPERTS_PER_GROUP = 4
MOE_EXPERTS = MOE_GROUPS * MOE_EXPERTS_PER_GROUP
MOE_TOP_K = 2
MOE_HIDDEN = 512
EPS = 1e-6
IN_COLS = 3 * A_N_GROUPS * A_WIDTH + 2 * B_WIDTH + 2 * D_MODEL

kernel_name = 'hybrid_dilated_attn_sgu_hmoe_step'


def alibi_slopes():
    n = A_N_GROUPS * A_HEADS
    e = jnp.arange(1, n + 1, dtype=jnp.float32)
    return jnp.exp2(-8.0 * e / n).reshape(A_N_GROUPS, A_HEADS)


def adaln(c, w, b, n):
    mod = jax.nn.silu(c) @ w + b
    return jnp.split(mod, n, axis=-1)


def rmsnorm_mod(x, gain, shift, scale):
    xf = x.astype(jnp.float32)
    y = xf * lax.rsqrt(jnp.mean(jnp.square(xf), axis=-1, keepdims=True) + EPS)
    y = y * gain.astype(jnp.float32) * (1.0 + scale[:, None, :].astype(jnp.float32)) + shift[:, None, :].astype(jnp.float32)
    return y.astype(x.dtype)


def layernorm(x, g, b):
    xf = x.astype(jnp.float32)
    mu = jnp.mean(xf, axis=-1, keepdims=True)
    var = jnp.mean(jnp.square(xf - mu), axis=-1, keepdims=True)
    return ((xf - mu) * lax.rsqrt(var + EPS) * g.astype(jnp.float32) + b.astype(jnp.float32)).astype(x.dtype)


def in_projection(h, w_in):
    z = h @ w_in
    b, s, _ = z.shape
    n_qkv = 3 * A_N_GROUPS * A_WIDTH
    qkv = z[..., :n_qkv].reshape(b, s, 3, A_N_GROUPS, A_HEADS, A_HEAD_DIM)
    o = n_qkv
    u = jax.nn.gelu(z[..., o:o + B_WIDTH])
    v = jax.nn.gelu(z[..., o + B_WIDTH:o + 2 * B_WIDTH])
    o = o + 2 * B_WIDTH
    ga = jax.nn.sigmoid(z[..., o:o + D_MODEL])
    gb = jax.nn.sigmoid(z[..., o + D_MODEL:o + 2 * D_MODEL])
    return qkv, u, v, ga, gb


def dilated_attn_prompt(q, k, v, dil, slopes):
    b, s, h, dh = q.shape
    n_steps = s // dil
    nb = -(-n_steps // A_BLOCK)
    pad = nb * A_BLOCK - n_steps

    def to_blocks(t):
        t = t.reshape(b, n_steps, dil, h, dh)
        t = jnp.pad(t, ((0, 0), (0, pad), (0, 0), (0, 0), (0, 0)))
        return t.reshape(b, nb, A_BLOCK, dil, h, dh)

    def with_prev(t):
        prev = jnp.pad(t[:, :-1], ((0, 0), (1, 0), (0, 0), (0, 0), (0, 0), (0, 0)))
        return jnp.concatenate([prev, t], axis=2)

    qb = to_blocks(q)
    kc = with_prev(to_blocks(k))
    vc = with_prev(to_blocks(v))
    scores = jnp.einsum('bnqrhd,bnkrhd->bnrhqk', qb, kc, preferred_element_type=jnp.float32) * (dh ** -0.5)
    qi = jnp.arange(A_BLOCK)[:, None]
    ki = jnp.arange(2 * A_BLOCK)[None, :]
    delta = qi + A_BLOCK - ki
    band = (delta >= 0) & (delta <= A_STEPS)
    has_prev = (jnp.arange(nb) > 0)[:, None, None] | (ki >= A_BLOCK)[None]
    valid = band[None] & has_prev
    dist = (delta * dil).astype(jnp.float32)
    scores = scores - slopes[:, None, None] * dist[None]
    scores = jnp.where(valid[None, :, None, None], scores, -jnp.inf)
    m = jnp.max(scores, axis=-1, keepdims=True)
    p = jnp.exp(scores - m)
    den = jnp.sum(p, axis=-1)
    o = jnp.einsum('bnrhqk,bnkrhd->bnqrhd', p, vc.astype(jnp.float32))
    den = jnp.transpose(den, (0, 1, 4, 2, 3))
    lse = jnp.transpose(m[..., 0], (0, 1, 4, 2, 3)) + jnp.log(den)
    o = o / den[..., None]
    o = o.reshape(b, nb * A_BLOCK, dil, h, dh)[:, :n_steps].reshape(b, s, h, dh)
    lse = lse.reshape(b, nb * A_BLOCK, dil, h)[:, :n_steps].reshape(b, s, h)
    return o, lse


def dilated_attn_sample(q, k_new, v_new, k_buf, v_buf, dil, slopes):
    buf, t, dh = k_buf.shape[1], q.shape[1], q.shape[-1]
    kc = jnp.concatenate([k_buf, k_new.astype(k_buf.dtype)], axis=1)
    vc = jnp.concatenate([v_buf, v_new.astype(v_buf.dtype)], axis=1)
    j = jnp.arange(A_STEPS + 1)
    idx = buf + jnp.arange(t)[:, None] - dil * j[None, :]
    valid = idx >= 0
    idx = jnp.maximum(idx, 0)
    kg = kc[:, idx]
    vg = vc[:, idx]
    scores = jnp.einsum('bthd,btjhd->bhtj', q, kg, preferred_element_type=jnp.float32) * (dh ** -0.5)
    scores = scores - slopes[:, None, None] * (dil * j).astype(jnp.float32)[None, None, :]
    scores = jnp.where(valid[None, None], scores, -jnp.inf)
    m = jnp.max(scores, axis=-1, keepdims=True)
    p = jnp.exp(scores - m)
    den = jnp.sum(p, axis=-1)
    o = jnp.einsum('bhtj,btjhd->bthd', p, vg.astype(jnp.float32)) / jnp.transpose(den, (0, 2, 1))[..., None]
    lse = jnp.transpose(m[..., 0] + jnp.log(den), (0, 2, 1))
    return o, lse


def combine_groups(outs, lses):
    w = jax.nn.softmax(jnp.stack(lses, axis=0), axis=0)
    return jnp.sum(w[..., None] * jnp.stack(outs, axis=0), axis=0)


def mixer_a_prompt(qkv, slopes):
    outs, lses, new_kv = [], [], []
    for g, (win, dil) in enumerate(A_GROUPS):
        q, k, v = qkv[:, :, 0, g], qkv[:, :, 1, g], qkv[:, :, 2, g]
        o, lse = dilated_attn_prompt(q, k, v, dil, slopes[g])
        outs.append(o)
        lses.append(lse)
        keep = min(win, q.shape[1])
        new_kv.append(jnp.stack([k[:, -keep:], v[:, -keep:]], axis=2))
    return combine_groups(outs, lses), new_kv


def mixer_a_sample(qkv, caches, slopes):
    outs, lses, new_kv = [], [], []
    for g, (win, dil) in enumerate(A_GROUPS):
        q, k, v = qkv[:, :, 0, g], qkv[:, :, 1, g], qkv[:, :, 2, g]
        kv_buf = caches[g]
        o, lse = dilated_attn_sample(q, k, v, kv_buf[:, :, 0], kv_buf[:, :, 1], dil, slopes[g])
        outs.append(o)
        lses.append(lse)
        new_kv.append(jnp.stack([k, v], axis=2))
    return combine_groups(outs, lses), new_kv


def spatial_gating(u, v, w_s, b_s):
    b, s, _ = v.shape
    cl = min(s, B_CHUNK)
    vb = v.reshape(b, s // cl, cl, B_GROUPS, B_GROUP_DIM)
    w = jnp.where(jnp.tril(jnp.ones((cl, cl), dtype=bool)), w_s[:, :cl, :cl], 0.0)
    mixed = jnp.einsum('gts,bnsgc->bntgc', w, vb) + jnp.transpose(b_s[:, :cl])[None, None, :, :, None]
    return u * mixed.reshape(b, s, B_WIDTH).astype(u.dtype)


def hier_moe(h, w_rg, b_rg, w_re, b_re, w_gate, w_up, w_down):
    b, s, d = h.shape
    x = h.reshape(b * s, d)
    g_logits = (x @ w_rg + b_rg).astype(jnp.float32)
    g_prob, g_idx = lax.top_k(jax.nn.softmax(g_logits, axis=-1), 1)
    e_logits = (jnp.einsum('nd,gde->nge', x, w_re) + b_re).astype(jnp.float32)
    e_logits = jnp.take_along_axis(e_logits, g_idx[:, :, None], axis=1)[:, 0]
    e_top, e_idx = lax.top_k(e_logits, MOE_TOP_K)
    weights = jax.nn.softmax(e_top, axis=-1) * g_prob
    expert = g_idx * MOE_EXPERTS_PER_GROUP + e_idx
    combine = jnp.einsum('nk,nke->ne', weights, jax.nn.one_hot(expert, MOE_EXPERTS, dtype=jnp.float32))
    hg = jnp.einsum('nd,edf->nef', x, w_gate)
    hu = jnp.einsum('nd,edf->nef', x, w_up)
    act = (jax.nn.silu(hg) * hu).astype(jnp.float32) * combine[:, :, None]
    y = jnp.einsum('nef,efd->nd', act.astype(x.dtype), w_down)
    return y.reshape(b, s, d)


def decoder_layer(x, c, attend, w_ada, b_ada, norm1_g, norm2_g, w_in, sgu_ln_g, sgu_ln_b, sgu_w, sgu_b,
                  w_pa, w_pb, w_o, w_rg, b_rg, w_re, b_re, w_gate, w_up, w_down):
    sh1, sc1, gt1, sh2, sc2, gt2 = adaln(c, w_ada, b_ada, 6)
    h = rmsnorm_mod(x, norm1_g, sh1, sc1)
    qkv, u, v, ga, gb = in_projection(h, w_in)
    o_a, new_kv = attend(qkv)
    vn = layernorm(v, sgu_ln_g, sgu_ln_b)
    o_b = spatial_gating(u, vn, sgu_w, sgu_b)
    b, s = x.shape[:2]
    pa = o_a.reshape(b, s, A_WIDTH).astype(x.dtype) @ w_pa
    pb = o_b @ w_pb
    x = x + gt1[:, None, :] * ((ga * pa + gb * pb) @ w_o)
    h = rmsnorm_mod(x, norm2_g, sh2, sc2)
    x = x + gt2[:, None, :] * hier_moe(h, w_rg, b_rg, w_re, b_re, w_gate, w_up, w_down)
    return x, new_kv, vn


def setup_inputs(seed: int = 0) -> dict:
    key = jax.random.key(seed)
    ks = iter(jax.random.split(key, 40))

    def nrm(shape, scale):
        return jax.random.normal(next(ks), shape, jnp.float32) * scale

    L, D = DEPTH, D_MODEL
    x_prompt = nrm((BATCH, SEQ, D), 1.0)
    x_sample = nrm((DEC_BATCH, DEC_SEQ, D), 1.0)
    cache_kv_w128 = nrm((L, DEC_BATCH, min(A_GROUPS[0][0], PAST_LEN), 2, A_HEADS, A_HEAD_DIM), 1.0)
    cache_kv_w512 = nrm((L, DEC_BATCH, min(A_GROUPS[1][0], PAST_LEN), 2, A_HEADS, A_HEAD_DIM), 1.0)
    cache_kv_w2048 = nrm((L, DEC_BATCH, min(A_GROUPS[2][0], PAST_LEN), 2, A_HEADS, A_HEAD_DIM), 1.0)
    c_prompt = nrm((BATCH, D), 1.0)
    c_sample = nrm((DEC_BATCH, D), 1.0)
    return {
        'x_prompt': x_prompt,
        'x_sample': x_sample,
        'cache_kv_w128': cache_kv_w128,
        'cache_kv_w512': cache_kv_w512,
        'cache_kv_w2048': cache_kv_w2048,
        'c_prompt': c_prompt,
        'c_sample': c_sample,
        'w_ada': nrm((L, D, 6 * D), 0.5 * D ** -0.5),
        'b_ada': nrm((L, 6 * D), 0.02),
        'norm1_g': 1.0 + nrm((L, D), 0.05),
        'norm2_g': 1.0 + nrm((L, D), 0.05),
        'w_in': nrm((L, D, IN_COLS), D ** -0.5),
        'sgu_ln_g': 1.0 + nrm((L, B_WIDTH), 0.05),
        'sgu_ln_b': nrm((L, B_WIDTH), 0.02),
        'sgu_w': jnp.tril(nrm((L, B_GROUPS, B_CHUNK, B_CHUNK), B_CHUNK ** -0.5)),
        'sgu_b': 1.0 + nrm((L, B_GROUPS, B_CHUNK), 0.1),
        'w_pa': nrm((L, A_WIDTH, D), A_WIDTH ** -0.5),
        'w_pb': nrm((L, B_WIDTH, D), B_WIDTH ** -0.5),
        'w_o': nrm((L, D, D), D ** -0.5),
        'w_route_group': nrm((L, D, MOE_GROUPS), D ** -0.5),
        'b_route_group': nrm((L, MOE_GROUPS), 0.01),
        'w_route_expert': nrm((L, MOE_GROUPS, D, MOE_EXPERTS_PER_GROUP), D ** -0.5),
        'b_route_expert': nrm((L, MOE_GROUPS, MOE_EXPERTS_PER_GROUP), 0.01),
        'w_gate': nrm((L, MOE_EXPERTS, D, MOE_HIDDEN), D ** -0.5),
        'w_up': nrm((L, MOE_EXPERTS, D, MOE_HIDDEN), D ** -0.5),
        'w_down': nrm((L, MOE_EXPERTS, MOE_HIDDEN, D), MOE_HIDDEN ** -0.5),
        'normf_g': 1.0 + nrm((D,), 0.05),
        'w_ada_final': nrm((D, 2 * D), 0.5 * D ** -0.5),
        'b_ada_final': nrm((2 * D,), 0.02),
    }


def reference(x_prompt, x_sample, cache_kv_w128, cache_kv_w512, cache_kv_w2048, c_prompt, c_sample,
              w_ada, b_ada, norm1_g, norm2_g, w_in, sgu_ln_g, sgu_ln_b, sgu_w, sgu_b, w_pa, w_pb, w_o,
              w_route_group, b_route_group, w_route_expert, b_route_expert, w_gate, w_up, w_down,
              normf_g, w_ada_final, b_ada_final):
    slopes = alibi_slopes()
    hp, hs = x_prompt, x_sample
    kv_p = ([], [], [])
    kv_s = ([], [], [])
    v_s = []
    for l in range(DEPTH):
        lw = (w_ada[l], b_ada[l], norm1_g[l], norm2_g[l], w_in[l], sgu_ln_g[l], sgu_ln_b[l], sgu_w[l], sgu_b[l],
              w_pa[l], w_pb[l], w_o[l], w_route_group[l], b_route_group[l], w_route_expert[l], b_route_expert[l],
              w_gate[l], w_up[l], w_down[l])
        attend_p = functools.partial(mixer_a_prompt, slopes=slopes)
        hp, new_p, _ = decoder_layer(hp, c_prompt, attend_p, *lw)
        attend_s = functools.partial(mixer_a_sample,
                                     caches=(cache_kv_w128[l], cache_kv_w512[l], cache_kv_w2048[l]),
                                     slopes=slopes)
        hs, new_s, vn_s = decoder_layer(hs, c_sample, attend_s, *lw)
        for g in range(A_N_GROUPS):
            kv_p[g].append(new_p[g])
            kv_s[g].append(new_s[g])
        v_s.append(vn_s)
    shp, scp = adaln(c_prompt, w_ada_final, b_ada_final, 2)
    y_prompt = rmsnorm_mod(hp, normf_g, shp, scp)
    shs, scs = adaln(c_sample, w_ada_final, b_ada_final, 2)
    y_sample = rmsnorm_mod(hs, normf_g, shs, scs)
    new_kv_w128_prompt = jnp.stack(kv_p[0], axis=0)
    new_kv_w512_prompt = jnp.stack(kv_p[1], axis=0)
    new_kv_w2048_prompt = jnp.stack(kv_p[2], axis=0)
    new_kv_w128_sample = jnp.stack(kv_s[0], axis=0)
    new_kv_w512_sample = jnp.stack(kv_s[1], axis=0)
    new_kv_w2048_sample = jnp.stack(kv_s[2], axis=0)
    new_sgu_v_sample = jnp.stack(v_s, axis=0)
    return (y_prompt, y_sample, new_kv_w128_prompt, new_kv_w512_prompt, new_kv_w2048_prompt,
            new_kv_w128_sample, new_kv_w512_sample, new_kv_w2048_sample, new_sgu_v_sample)
```

```python
import functools

import numpy as np
import jax
import jax.numpy as jnp
from jax import lax
from jax.experimental import pallas as pl
from jax.experimental.pallas import tpu as pltpu

F32 = jnp.float32
BF16 = jnp.bfloat16

D_MODEL = 1024
A_GROUPS = ((128, 1), (512, 4), (2048, 16))
N_GROUPS = 3
HEADS = 8
HEAD_DIM = 64
A_WIDTH = HEADS * HEAD_DIM
A_STEPS = 128
A_BLOCK = 128
B_WIDTH = 1024
B_GROUPS = 8
B_CHUNK = 128
MOE_GROUPS = 4
MOE_PER_GROUP = 4
MOE_EXPERTS = 16
MOE_HIDDEN = 512
EPS = 1e-6
N_QKV = 3 * N_GROUPS * A_WIDTH
IN_COLS = N_QKV + 2 * B_WIDTH + 2 * D_MODEL
COL_CHUNK = 512
LANES = 128
LSE_REP = LANES // HEADS
ROUTE_OFF = MOE_GROUPS
NEG = -1e30
VMEM_LIMIT = 56 * 1024 * 1024


def _sigmoid(x):
    return 1.0 / (1.0 + jnp.exp(-x))


def _gelu_tanh(x):
    return x * (0.5 * (1.0 + jnp.tanh(0.7978845608028654 * (x + 0.044715 * (x * x * x)))))


def _rmsnorm_mod(x, gain, shift, scale):
    y = x * lax.rsqrt(jnp.mean(x * x, axis=-1, keepdims=True) + EPS)
    return y * gain * (1.0 + scale) + shift


def _mod_spec(mod, tm):
    if mod.shape[1] == 1:
        return pl.BlockSpec((None, 1, D_MODEL), lambda b, s, *_: (b, 0, 0))
    return pl.BlockSpec((None, tm, D_MODEL), lambda b, s, *_: (b, s, 0))


def _const_spec(arr):
    nd = arr.ndim
    return pl.BlockSpec(arr.shape, lambda *_: (0,) * nd)


def _adaln_kernel(c_ref, w_ref, b_ref, o_ref):
    c = c_ref[...]
    a = (c * _sigmoid(c)).astype(BF16)
    o_ref[...] = jnp.dot(a, w_ref[...].astype(BF16), preferred_element_type=F32) + b_ref[...]


def _adaln(c, w, b, tn=1024):
    rows, ncols = c.shape[0], w.shape[1]
    return pl.pallas_call(
        _adaln_kernel,
        grid=(ncols // tn,),
        in_specs=[pl.BlockSpec((rows, D_MODEL), lambda j: (0, 0)),
                  pl.BlockSpec((D_MODEL, tn), lambda j: (0, j)),
                  pl.BlockSpec((1, tn), lambda j: (0, j))],
        out_specs=pl.BlockSpec((rows, tn), lambda j: (0, j)),
        out_shape=jax.ShapeDtypeStruct((rows, ncols), F32),
        compiler_params=pltpu.CompilerParams(dimension_semantics=("arbitrary",)),
        name="adaln",
    )(c, w, b.reshape(1, ncols))


def _inproj_kernel(x_ref, sh_ref, sc_ref, g_ref, w_ref, lng_ref, lnb_ref,
                   qkv_ref, u_ref, vn_ref, ga_ref, gb_ref, kv0_ref, kv1_ref, kv2_ref, *rest,
                   tm, keeps, n_s):
    s = pl.program_id(1)
    h = _rmsnorm_mod(x_ref[...], g_ref[...], sh_ref[...], sc_ref[...]).astype(BF16)

    def proj(c):
        return jnp.dot(h, w_ref[:, c * COL_CHUNK:(c + 1) * COL_CHUNK], preferred_element_type=F32)

    def cols(c):
        return slice(c * COL_CHUNK, (c + 1) * COL_CHUNK)

    for g in range(N_GROUPS):
        qkv_ref[:, cols(g)] = (proj(g) * (HEAD_DIM ** -0.5)).astype(BF16)

    kv_refs = (kv0_ref, kv1_ref, kv2_ref)
    for g in range(N_GROUPS):
        keep = keeps[g]
        for t in (1, 2):
            z = proj(t * N_GROUPS + g)
            qkv_ref[:, cols(t * N_GROUPS + g)] = z.astype(BF16)
            if keep >= tm:
                first = n_s - keep // tm

                @pl.when(s >= first)
                def _():
                    kv_refs[g][:, cols(t - 1)] = z
            else:
                @pl.when(s == n_s - 1)
                def _():
                    kv_refs[g][:, cols(t - 1)] = z[tm - keep:, :]

    base = N_QKV // COL_CHUNK
    for c in range(2):
        u_ref[:, cols(c)] = _gelu_tanh(proj(base + c)).astype(BF16)

    vs = [_gelu_tanh(proj(base + 2 + c)) for c in range(2)]
    mu = (jnp.sum(vs[0], axis=-1, keepdims=True) + jnp.sum(vs[1], axis=-1, keepdims=True)) * (1.0 / B_WIDTH)
    ds = [v - mu for v in vs]
    var = (jnp.sum(ds[0] * ds[0], axis=-1, keepdims=True)
           + jnp.sum(ds[1] * ds[1], axis=-1, keepdims=True)) * (1.0 / B_WIDTH)
    inv = lax.rsqrt(var + EPS)
    for c in range(2):
        vn = ds[c] * inv * lng_ref[:, cols(c)] + lnb_ref[:, cols(c)]
        vn_ref[:, cols(c)] = vn.astype(BF16)
        if rest:
            rest[0][:, cols(c)] = vn

    for c in range(2):
        ga_ref[:, cols(c)] = _sigmoid(proj(base + 4 + c)).astype(BF16)
        gb_ref[:, cols(c)] = _sigmoid(proj(base + 6 + c)).astype(BF16)


def _inproj(x, sh, sc, gain, w_in, ln_g, ln_b, *, tm, keeps, emit_vn_f32):
    bsz, seq, _ = x.shape
    n_s = seq // tm
    tok = lambda width: pl.BlockSpec((None, tm, width), lambda b, s: (b, s, 0))

    def kv_spec(keep):
        if keep >= tm:
            first = n_s - keep // tm
            return pl.BlockSpec((None, tm, 2 * A_WIDTH), lambda b, s: (b, jnp.maximum(s - first, 0), 0))
        return pl.BlockSpec((None, keep, 2 * A_WIDTH), lambda b, s: (b, 0, 0))

    out_specs = [tok(N_QKV), tok(B_WIDTH), tok(B_WIDTH), tok(D_MODEL), tok(D_MODEL)]
    out_shape = [jax.ShapeDtypeStruct((bsz, seq, N_QKV), BF16)]
    out_shape += [jax.ShapeDtypeStruct((bsz, seq, D_MODEL), BF16)] * 4
    for keep in keeps:
        out_specs.append(kv_spec(keep))
        out_shape.append(jax.ShapeDtypeStruct((bsz, keep, 2 * A_WIDTH), F32))
    if emit_vn_f32:
        out_specs.append(tok(B_WIDTH))
        out_shape.append(jax.ShapeDtypeStruct((bsz, seq, B_WIDTH), F32))

    return pl.pallas_call(
        functools.partial(_inproj_kernel, tm=tm, keeps=keeps, n_s=n_s),
        grid=(bsz, n_s),
        in_specs=[tok(D_MODEL), _mod_spec(sh, tm), _mod_spec(sc, tm), _const_spec(gain),
                  pl.BlockSpec(w_in.shape, lambda b, s: (0, 0), pipeline_mode=pl.Buffered(1)),
                  _const_spec(ln_g), _const_spec(ln_b)],
        out_specs=out_specs,
        out_shape=out_shape,
        compiler_params=pltpu.CompilerParams(dimension_semantics=("arbitrary", "arbitrary"),
                                             vmem_limit_bytes=VMEM_LIMIT),
        name="inproj",
    )(x, sh, sc, gain, w_in, ln_g, ln_b)


def _attn_kernel(q_ref, kp_ref, kc_ref, vp_ref, vc_ref, bias_ref, o_ref, lse_ref):
    q = q_ref[...]
    k = jnp.concatenate([kp_ref[...], kc_ref[...]], axis=0)
    v = jnp.concatenate([vp_ref[...], vc_ref[...]], axis=0)
    lane = lax.broadcasted_iota(jnp.int32, (A_BLOCK, LANES), 1)
    low = lane < HEAD_DIM
    lse_tile = jnp.zeros((A_BLOCK, LANES), F32)
    zero = jnp.zeros((), BF16)
    for j in range(HEADS // 2):
        pair = slice(j * LANES, (j + 1) * LANES)
        qp, kpair, vpair = q[:, pair], k[:, pair], v[:, pair]
        outs = []
        for e in range(2):
            h = 2 * j + e
            qm = jnp.where(low if e == 0 else jnp.logical_not(low), qp, zero)
            s = lax.dot_general(qm, kpair, (((1,), (1,)), ((), ())), preferred_element_type=F32)
            s = s + bias_ref[h]
            m = jnp.max(s, axis=-1, keepdims=True)
            p = jnp.exp(s - m)
            den = jnp.sum(p, axis=-1, keepdims=True)
            o = jnp.dot(p.astype(BF16), vpair, preferred_element_type=F32) / den
            lse = m + jnp.log(den)
            lse_tile = jnp.where((lane >= h * LSE_REP) & (lane < (h + 1) * LSE_REP), lse, lse_tile)
            outs.append(o)
        o_ref[:, pair] = jnp.where(low, outs[0], outs[1]).astype(BF16)
    lse_ref[...] = lse_tile


def _attn_bias(g):
    _, dil = A_GROUPS[g]
    n = N_GROUPS * HEADS
    e = np.arange(1, n + 1, dtype=np.float32)
    slopes = np.exp2(-8.0 * e / n).astype(np.float32).reshape(N_GROUPS, HEADS)[g]
    qi = np.arange(A_BLOCK)[:, None]
    ki = np.arange(2 * A_BLOCK)[None, :]
    delta = qi + A_BLOCK - ki
    band = (delta >= 0) & (delta <= A_STEPS)
    dist = (delta * dil).astype(np.float32)
    bias = -slopes[:, None, None] * dist[None]
    out = np.empty((2, HEADS, A_BLOCK, 2 * A_BLOCK), np.float32)
    out[1] = np.where(band[None], bias, NEG)
    out[0] = np.where((band & (ki >= A_BLOCK))[None], bias, NEG)
    return jnp.asarray(out)


def _attn_prompt(qkv, g):
    bsz, seq, _ = qkv.shape
    _, dil = A_GROUPS[g]
    steps = seq // dil
    nb = steps // A_BLOCK
    ncol = N_QKV // A_WIDTH
    qkv_v = qkv.reshape(bsz, steps, dil * N_QKV)
    bias = _attn_bias(g)

    def blk(t, prev):
        if prev:
            return pl.BlockSpec((None, A_BLOCK, A_WIDTH),
                                lambda b, r, n: (b, jnp.maximum(n - 1, 0), r * ncol + t * N_GROUPS + g))
        return pl.BlockSpec((None, A_BLOCK, A_WIDTH), lambda b, r, n: (b, n, r * ncol + t * N_GROUPS + g))

    o, lse = pl.pallas_call(
        _attn_kernel,
        grid=(bsz, dil, nb),
        in_specs=[blk(0, False), blk(1, True), blk(1, False), blk(2, True), blk(2, False),
                  pl.BlockSpec((None, HEADS, A_BLOCK, 2 * A_BLOCK),
                               lambda b, r, n: (jnp.minimum(n, 1), 0, 0, 0))],
        out_specs=[pl.BlockSpec((None, A_BLOCK, A_WIDTH), lambda b, r, n: (b, n, r)),
                   pl.BlockSpec((None, A_BLOCK, LANES), lambda b, r, n: (b, n, r))],
        out_shape=[jax.ShapeDtypeStruct((bsz, steps, dil * A_WIDTH), BF16),
                   jax.ShapeDtypeStruct((bsz, steps, dil * LANES), F32)],
        compiler_params=pltpu.CompilerParams(dimension_semantics=("arbitrary",) * 3),
        name=f"attn_prompt_g{g}",
    )(qkv_v, qkv_v, qkv_v, qkv_v, qkv_v, bias)
    return o.reshape(bsz, seq, A_WIDTH), lse.reshape(bsz, seq, LANES)


def _attn_sample_kernel(qkv_ref, c0_ref, c1_ref, c2_ref, bc0_ref, bc1_ref, bc2_ref,
                        bn0_ref, bn1_ref, bn2_ref, o_ref, *, t_new):
    n_rows = HEADS * t_new
    row = lax.broadcasted_iota(jnp.int32, (n_rows, A_WIDTH), 0)
    lane = lax.broadcasted_iota(jnp.int32, (n_rows, A_WIDTH), 1)
    head_mask = (row // t_new) == (lane // HEAD_DIM)
    pad = jnp.zeros((LANES - t_new, A_WIDTH), F32)
    outs, lses = [], []
    for g, (c_ref, bc_ref, bn_ref) in enumerate(((c0_ref, bc0_ref, bn0_ref), (c1_ref, bc1_ref, bn1_ref),
                                                 (c2_ref, bc2_ref, bn2_ref))):
        q = qkv_ref[:, g * A_WIDTH:(g + 1) * A_WIDTH].astype(F32)
        k_new = qkv_ref[:, (N_GROUPS + g) * A_WIDTH:(N_GROUPS + g + 1) * A_WIDTH].astype(F32)
        v_new = qkv_ref[:, (2 * N_GROUPS + g) * A_WIDTH:(2 * N_GROUPS + g + 1) * A_WIDTH].astype(F32)
        k_new = jnp.concatenate([k_new, pad], axis=0).astype(BF16)
        v_new = jnp.concatenate([v_new, pad], axis=0).astype(BF16)
        q_rows = jnp.where(head_mask, jnp.concatenate([q] * HEADS, axis=0), 0.0).astype(BF16)
        k_buf_t = c_ref[:A_WIDTH, :].astype(BF16)
        v_buf_t = c_ref[A_WIDTH:, :].astype(BF16)
        nt = (((1,), (1,)), ((), ()))
        s_buf = jnp.dot(q_rows, k_buf_t, preferred_element_type=F32) + bc_ref[...]
        s_new = lax.dot_general(q_rows, k_new, nt, preferred_element_type=F32) + bn_ref[...]
        m = jnp.maximum(jnp.max(s_buf, axis=-1, keepdims=True), jnp.max(s_new, axis=-1, keepdims=True))
        p_buf = jnp.exp(s_buf - m)
        p_new = jnp.exp(s_new - m)
        den = jnp.sum(p_buf, axis=-1, keepdims=True) + jnp.sum(p_new, axis=-1, keepdims=True)
        o = (lax.dot_general(p_buf.astype(BF16), v_buf_t, nt, preferred_element_type=F32)
             + jnp.dot(p_new.astype(BF16), v_new, preferred_element_type=F32)) / den
        outs.append(o)
        lses.append(m + jnp.log(den))
    top = jnp.maximum(jnp.maximum(lses[0], lses[1]), lses[2])
    ws = [jnp.exp(l - top) for l in lses]
    tot = ws[0] + ws[1] + ws[2]
    acc = (ws[0] / tot) * outs[0] + (ws[1] / tot) * outs[1] + (ws[2] / tot) * outs[2]
    acc = jnp.where(head_mask, acc, 0.0).reshape(HEADS, t_new, A_WIDTH)
    o_ref[...] = jnp.sum(acc, axis=0).astype(BF16)


def _sample_bias(g, t_new, buf):
    _, dil = A_GROUPS[g]
    n = N_GROUPS * HEADS
    e = np.arange(1, n + 1, dtype=np.float32)
    slopes = np.exp2(-8.0 * e / n).astype(np.float32).reshape(N_GROUPS, HEADS)[g]
    t = np.arange(t_new)[:, None]
    idx = np.concatenate([np.arange(buf), buf + np.arange(LANES)])[None, :]
    dist = buf + t - idx
    valid = (dist >= 0) & (dist % dil == 0) & (dist <= A_STEPS * dil) & (idx < buf + t_new)
    bias = -slopes[:, None, None] * dist.astype(np.float32)[None]
    bias = np.where(valid[None], bias, NEG).astype(np.float32).reshape(HEADS * t_new, buf + LANES)
    return jnp.asarray(bias[:, :buf]), jnp.asarray(bias[:, buf:])


def _attn_sample(qkv, caches):
    bsz, t_new, _ = qkv.shape
    cache_v = [jnp.transpose(c, (0, 2, 3, 4, 1)).reshape(bsz, 2 * A_WIDTH, c.shape[1]) for c in caches]
    biases = [_sample_bias(g, t_new, cache_v[g].shape[2]) for g in range(N_GROUPS)]
    bcs = [b[0] for b in biases]
    bns = [b[1] for b in biases]
    return pl.pallas_call(
        functools.partial(_attn_sample_kernel, t_new=t_new),
        grid=(bsz,),
        in_specs=[pl.BlockSpec((None, t_new, N_QKV), lambda b: (b, 0, 0))]
                 + [pl.BlockSpec((None, 2 * A_WIDTH, c.shape[2]), lambda b: (b, 0, 0)) for c in cache_v]
                 + [_const_spec(b) for b in bcs] + [_const_spec(b) for b in bns],
        out_specs=pl.BlockSpec((None, t_new, A_WIDTH), lambda b: (b, 0, 0)),
        out_shape=jax.ShapeDtypeStruct((bsz, t_new, A_WIDTH), BF16),
        compiler_params=pltpu.CompilerParams(dimension_semantics=("arbitrary",),
                                             vmem_limit_bytes=VMEM_LIMIT),
        name="attn_sample",
    )(qkv, *cache_v, *bcs, *bns)


def _route(logits):
    lane = lax.broadcasted_iota(jnp.int32, logits.shape, 1).astype(F32)
    ninf = -jnp.inf
    far = float(LANES)
    gl = jnp.where(lane < MOE_GROUPS, logits, ninf)
    gmax = jnp.max(gl, axis=-1, keepdims=True)
    g_prob = 1.0 / jnp.sum(jnp.exp(gl - gmax), axis=-1, keepdims=True)
    g_idx = jnp.min(jnp.where(gl == gmax, lane, far), axis=-1, keepdims=True)
    first = ROUTE_OFF + MOE_PER_GROUP * g_idx
    el = jnp.where((lane >= first) & (lane < first + MOE_PER_GROUP), logits, ninf)
    e1 = jnp.max(el, axis=-1, keepdims=True)
    i1 = jnp.min(jnp.where(el == e1, lane, far), axis=-1, keepdims=True)
    el2 = jnp.where(lane == i1, ninf, el)
    e2 = jnp.max(el2, axis=-1, keepdims=True)
    i2 = jnp.min(jnp.where(el2 == e2, lane, far), axis=-1, keepdims=True)
    t = jnp.exp(e2 - e1)
    w1 = 1.0 / (1.0 + t)
    w2 = t / (1.0 + t)
    return jnp.where(lane == i1, w1 * g_prob, jnp.where(lane == i2, w2 * g_prob, 0.0))


def _mix_kernel(*refs, tm, n_attn):
    x_ref = refs[0]
    if n_attn > 1:
        o_refs = refs[1:1 + n_attn]
        l_refs = refs[1 + n_attn:1 + 2 * n_attn]
        exp_ref = refs[1 + 2 * n_attn]
        pos = 2 + 2 * n_attn
    else:
        o_refs = refs[1:2]
        pos = 2
    (u_ref, vn_ref, ga_ref, gb_ref, sw_ref, sbt_ref, wpa_ref, wpb_ref, wo_ref,
     gt1_ref, sh2_ref, sc2_ref, n2g_ref, wr_ref, br_ref,
     x1_ref, h2_ref, comb_ref, ob_scr) = refs[pos:]

    if n_attn > 1:
        lses = [l[...] for l in l_refs]
        top = functools.reduce(jnp.maximum, lses)
        ws = [jnp.exp(l - top) for l in lses]
        tot = functools.reduce(lambda a, b: a + b, ws)
        o_a = None
        for g in range(n_attn):
            w = ws[g] / tot
            hi = w.astype(BF16)
            lo = (w - hi.astype(F32)).astype(BF16)
            w_exp = jnp.dot(jnp.concatenate([hi, lo], axis=1), exp_ref[...], preferred_element_type=F32)
            term = w_exp * o_refs[g][...].astype(F32)
            o_a = term if o_a is None else o_a + term
        o_a = o_a.astype(BF16)
    else:
        o_a = o_refs[0][...]

    r_i = lax.broadcasted_iota(jnp.int32, (B_CHUNK, B_CHUNK), 0)
    c_i = lax.broadcasted_iota(jnp.int32, (B_CHUNK, B_CHUNK), 1)
    tril = r_i >= c_i
    for gi in range(B_GROUPS):
        gcols = slice(gi * LANES, (gi + 1) * LANES)
        w_s = jnp.where(tril, sw_ref[gi], 0.0).astype(BF16)
        b_s = sbt_ref[:, gi:gi + 1]
        for c in range(tm // B_CHUNK):
            rows = slice(c * B_CHUNK, (c + 1) * B_CHUNK)
            mixed = jnp.dot(w_s, vn_ref[rows, gcols], preferred_element_type=F32) + b_s
            ob_scr[rows, gcols] = (u_ref[rows, gcols].astype(F32) * mixed).astype(BF16)

    pa = jnp.dot(o_a, wpa_ref[...], preferred_element_type=F32)
    pb = jnp.dot(ob_scr[...], wpb_ref[...], preferred_element_type=F32)
    mix = (ga_ref[...].astype(F32) * pa + gb_ref[...].astype(F32) * pb).astype(BF16)
    x1 = x_ref[...] + gt1_ref[...] * jnp.dot(mix, wo_ref[...], preferred_element_type=F32)
    x1_ref[...] = x1
    h2 = _rmsnorm_mod(x1, n2g_ref[...], sh2_ref[...], sc2_ref[...])
    h2_ref[...] = h2.astype(BF16)
    logits = jnp.dot(h2, wr_ref[...], preferred_element_type=F32, precision=lax.Precision.HIGHEST) + br_ref[...]
    comb_ref[...] = _route(logits)


def _expand_matrix():
    e = np.zeros((LANES, A_WIDTH), np.float32)
    for h in range(HEADS):
        e[h * LSE_REP, h * HEAD_DIM:(h + 1) * HEAD_DIM] = 1.0
    return jnp.asarray(np.concatenate([e, e], axis=0), dtype=BF16)


def _mix(x, attn_outs, attn_lses, u, vn, ga, gb, sgu_w, sgu_bt, w_pa, w_pb, w_o,
         gt1, sh2, sc2, n2g, w_r, b_r, *, tm):
    bsz, seq, _ = x.shape
    n_attn = len(attn_outs)
    tok = lambda width: pl.BlockSpec((None, tm, width), lambda b, s: (b, s, 0))
    ins = [x] + list(attn_outs)
    specs = [tok(D_MODEL)] + [tok(A_WIDTH)] * n_attn
    if n_attn > 1:
        e2 = _expand_matrix()
        ins += list(attn_lses) + [e2]
        specs += [tok(LANES)] * n_attn + [_const_spec(e2)]
    ins += [u, vn, ga, gb, sgu_w, sgu_bt, w_pa, w_pb, w_o, gt1, sh2, sc2, n2g, w_r, b_r]
    specs += [tok(B_WIDTH), tok(B_WIDTH), tok(D_MODEL), tok(D_MODEL),
              _const_spec(sgu_w), _const_spec(sgu_bt), _const_spec(w_pa), _const_spec(w_pb), _const_spec(w_o),
              _mod_spec(gt1, tm), _mod_spec(sh2, tm), _mod_spec(sc2, tm), _const_spec(n2g),
              _const_spec(w_r), _const_spec(b_r)]
    return pl.pallas_call(
        functools.partial(_mix_kernel, tm=tm, n_attn=n_attn),
        grid=(bsz, seq // tm),
        in_specs=specs,
        out_specs=[tok(D_MODEL), tok(D_MODEL), tok(LANES)],
        out_shape=[jax.ShapeDtypeStruct((bsz, seq, D_MODEL), F32),
                   jax.ShapeDtypeStruct((bsz, seq, D_MODEL), BF16),
                   jax.ShapeDtypeStruct((bsz, seq, LANES), F32)],
        scratch_shapes=[pltpu.VMEM((tm, B_WIDTH), BF16)],
        compiler_params=pltpu.CompilerParams(dimension_semantics=("arbitrary", "arbitrary"),
                                             vmem_limit_bytes=VMEM_LIMIT),
        name="mix",
    )(*ins)


def _moe_kernel(h_ref, comb_ref, wg_ref, wu_ref, wd_ref, x1_ref, gt2_ref, shf_ref, scf_ref, nfg_ref,
                y_ref, acc_ref):
    e = pl.program_id(2)

    @pl.when(e == 0)
    def _():
        acc_ref[...] = jnp.zeros_like(acc_ref)

    h = h_ref[...]
    hg = jnp.dot(h, wg_ref[...], preferred_element_type=F32)
    hu = jnp.dot(h, wu_ref[...], preferred_element_type=F32)
    comb = comb_ref[...]
    lane = lax.broadcasted_iota(jnp.int32, comb.shape, 1)
    cw = jnp.sum(jnp.where(lane == e + ROUTE_OFF, comb, 0.0), axis=-1, keepdims=True)
    act = (hg * _sigmoid(hg) * hu) * cw
    acc_ref[...] += jnp.dot(act.astype(BF16), wd_ref[...], preferred_element_type=F32)

    @pl.when(e == MOE_EXPERTS - 1)
    def _():
        x2 = x1_ref[...] + gt2_ref[...] * acc_ref[...]
        y_ref[...] = _rmsnorm_mod(x2, nfg_ref[...], shf_ref[...], scf_ref[...])


def _moe(h2, comb, w_gate, w_up, w_down, x1, gt2, shf, scf, nfg, *, tm):
    bsz, seq, _ = h2.shape
    tok = lambda width: pl.BlockSpec((None, tm, width), lambda b, s, e: (b, s, 0))
    return pl.pallas_call(
        _moe_kernel,
        grid=(bsz, seq // tm, MOE_EXPERTS),
        in_specs=[tok(D_MODEL), tok(LANES),
                  pl.BlockSpec((None, D_MODEL, MOE_HIDDEN), lambda b, s, e: (e, 0, 0)),
                  pl.BlockSpec((None, D_MODEL, MOE_HIDDEN), lambda b, s, e: (e, 0, 0)),
                  pl.BlockSpec((None, MOE_HIDDEN, D_MODEL), lambda b, s, e: (e, 0, 0)),
                  tok(D_MODEL), _mod_spec(gt2, tm), _mod_spec(shf, tm), _mod_spec(scf, tm), _const_spec(nfg)],
        out_specs=tok(D_MODEL),
        out_shape=jax.ShapeDtypeStruct((bsz, seq, D_MODEL), F32),
        scratch_shapes=[pltpu.VMEM((tm, D_MODEL), F32)],
        compiler_params=pltpu.CompilerParams(dimension_semantics=("arbitrary",) * 3,
                                             vmem_limit_bytes=VMEM_LIMIT),
        name="moe",
    )(h2, comb, w_gate, w_up, w_down, x1, gt2, shf, scf, nfg)


def kernel(x_prompt, x_sample, cache_kv_w128, cache_kv_w512, cache_kv_w2048, c_prompt, c_sample,
           w_ada, b_ada, norm1_g, norm2_g, w_in, sgu_ln_g, sgu_ln_b, sgu_w, sgu_b, w_pa, w_pb, w_o,
           w_route_group, b_route_group, w_route_expert, b_route_expert, w_gate, w_up, w_down,
           normf_g, w_ada_final, b_ada_final):
    depth = w_ada.shape[0]
    assert depth == 1
    l = 0
    bp, seq, _ = x_prompt.shape
    bs, t_new, _ = x_sample.shape
    n_samp = bs * t_new

    c_all = jnp.concatenate([c_prompt, c_sample], axis=0)
    pad_rows = (-c_all.shape[0]) % 8
    c_all = jnp.pad(c_all, ((0, pad_rows), (0, 0)))
    mod = _adaln(c_all, w_ada[l], b_ada[l])
    mod_f = _adaln(c_all, w_ada_final, b_ada_final)

    def split_mods(m, n, lo, hi, per_token):
        parts = jnp.split(m[lo:hi], n, axis=-1)
        if per_token:
            return [jnp.repeat(p, t_new, axis=0).reshape(1, n_samp, D_MODEL) for p in parts]
        return [p.reshape(hi - lo, 1, D_MODEL) for p in parts]

    mods_p = split_mods(mod, 6, 0, bp, False) + split_mods(mod_f, 2, 0, bp, False)
    mods_s = split_mods(mod, 6, bp, bp + bs, True) + split_mods(mod_f, 2, bp, bp + bs, True)

    row = lambda v: v.reshape(1, -1)
    w_in_b = w_in[l].astype(BF16)
    w_pa_b, w_pb_b, w_o_b = w_pa[l].astype(BF16), w_pb[l].astype(BF16), w_o[l].astype(BF16)
    w_gate_b, w_up_b, w_down_b = w_gate[l].astype(BF16), w_up[l].astype(BF16), w_down[l].astype(BF16)
    w_re = jnp.transpose(w_route_expert[l], (1, 0, 2)).reshape(D_MODEL, MOE_EXPERTS)
    w_r = jnp.pad(jnp.concatenate([w_route_group[l], w_re], axis=1),
                  ((0, 0), (0, LANES - MOE_GROUPS - MOE_EXPERTS)))
    b_r = jnp.pad(jnp.concatenate([b_route_group[l], b_route_expert[l].reshape(-1)]),
                  (0, LANES - MOE_GROUPS - MOE_EXPERTS)).reshape(1, LANES)

    def layer(x, mods, attend, sgu_w_eff, sgu_bt_eff, keeps, tm, emit_vn_f32):
        sh1, sc1, gt1, sh2, sc2, gt2, shf, scf = mods
        outs = _inproj(x, sh1, sc1, row(norm1_g[l]), w_in_b, row(sgu_ln_g[l]), row(sgu_ln_b[l]),
                       tm=tm, keeps=keeps, emit_vn_f32=emit_vn_f32)
        qkv, u, vn, ga, gb, kv0, kv1, kv2 = outs[:8]
        attn_outs, attn_lses = attend(qkv)
        x1, h2, comb = _mix(x, attn_outs, attn_lses, u, vn, ga, gb, sgu_w_eff, sgu_bt_eff,
                            w_pa_b, w_pb_b, w_o_b, gt1, sh2, sc2, row(norm2_g[l]), w_r, b_r, tm=tm)
        y = _moe(h2, comb, w_gate_b, w_up_b, w_down_b, x1, gt2, shf, scf, row(normf_g), tm=tm)
        return y, (kv0, kv1, kv2), outs[8:]

    def attend_p(qkv):
        res = [_attn_prompt(qkv, g) for g in range(N_GROUPS)]
        return [r[0] for r in res], [r[1] for r in res]

    keeps_p = tuple(min(win, seq) for win, _ in A_GROUPS)
    y_p, kv_p, _ = layer(x_prompt, mods_p, attend_p, sgu_w[l], jnp.transpose(sgu_b[l]), keeps_p, 256, False)

    caches = (cache_kv_w128[l], cache_kv_w512[l], cache_kv_w2048[l])

    def attend_s(qkv):
        return [_attn_sample(qkv.reshape(bs, t_new, N_QKV), caches).reshape(1, n_samp, A_WIDTH)], None

    cl = min(t_new, B_CHUNK)
    eye = jnp.eye(B_CHUNK // cl, dtype=F32)
    sgu_w_s = jnp.einsum('ab,gts->gatbs', eye, sgu_w[l][:, :cl, :cl]).reshape(B_GROUPS, B_CHUNK, B_CHUNK)
    sgu_bt_s = jnp.tile(jnp.transpose(sgu_b[l][:, :cl]), (B_CHUNK // cl, 1))
    y_s, kv_s, extra = layer(x_sample.reshape(1, n_samp, D_MODEL), mods_s, attend_s, sgu_w_s, sgu_bt_s,
                             (n_samp,) * N_GROUPS, n_samp, True)

    def kv_out(a, b):
        return a.reshape(depth, b, -1, 2, HEADS, HEAD_DIM)

    return (y_p, y_s.reshape(bs, t_new, D_MODEL),
            kv_out(kv_p[0], bp), kv_out(kv_p[1], bp), kv_out(kv_p[2], bp),
            kv_out(kv_s[0], bs), kv_out(kv_s[1], bs), kv_out(kv_s[2], bs),
            extra[0].reshape(depth, bs, t_new, B_WIDTH))
```

```python
import functools

import numpy as np
import jax
import jax.numpy as jnp
from jax import lax
from jax.experimental import pallas as pl
from jax.experimental.pallas import tpu as pltpu

F32 = jnp.float32
BF16 = jnp.bfloat16

D_MODEL = 1024
A_GROUPS = ((128, 1), (512, 4), (2048, 16))
N_GROUPS = 3
HEADS = 8
HEAD_DIM = 64
A_WIDTH = HEADS * HEAD_DIM
A_STEPS = 128
A_BLOCK = 128
B_WIDTH = 1024
B_GROUPS = 8
B_CHUNK = 128
MOE_GROUPS = 4
MOE_PER_GROUP = 4
MOE_EXPERTS = 16
MOE_HIDDEN = 512
EPS = 1e-6
N_QKV = 3 * N_GROUPS * A_WIDTH
IN_COLS = N_QKV + 2 * B_WIDTH + 2 * D_MODEL
COL_CHUNK = 512
LANES = 128
LSE_REP = LANES // HEADS
ROUTE_OFF = MOE_GROUPS
NEG = -1e30
VMEM_LIMIT = 56 * 1024 * 1024


def _sigmoid(x):
    return 1.0 / (1.0 + jnp.exp(-x))


def _gelu_tanh(x):
    return x * (0.5 * (1.0 + jnp.tanh(0.7978845608028654 * (x + 0.044715 * (x * x * x)))))


def _rmsnorm_mod(x, gain, shift, scale):
    y = x * lax.rsqrt(jnp.mean(x * x, axis=-1, keepdims=True) + EPS)
    return y * gain * (1.0 + scale) + shift


def _mod_spec(mod, tm):
    if mod.shape[1] == 1:
        return pl.BlockSpec((None, 1, D_MODEL), lambda b, s, *_: (b, 0, 0))
    return pl.BlockSpec((None, tm, D_MODEL), lambda b, s, *_: (b, s, 0))


def _const_spec(arr):
    nd = arr.ndim
    return pl.BlockSpec(arr.shape, lambda *_: (0,) * nd)


def _adaln_kernel(c_ref, w_ref, b_ref, o_ref):
    c = c_ref[...]
    a = (c * _sigmoid(c)).astype(BF16)
    o_ref[...] = jnp.dot(a, w_ref[...].astype(BF16), preferred_element_type=F32) + b_ref[...]


def _adaln(c, w, b, tn=1024):
    rows, ncols = c.shape[0], w.shape[1]
    return pl.pallas_call(
        _adaln_kernel,
        grid=(ncols // tn,),
        in_specs=[pl.BlockSpec((rows, D_MODEL), lambda j: (0, 0)),
                  pl.BlockSpec((D_MODEL, tn), lambda j: (0, j)),
                  pl.BlockSpec((1, tn), lambda j: (0, j))],
        out_specs=pl.BlockSpec((rows, tn), lambda j: (0, j)),
        out_shape=jax.ShapeDtypeStruct((rows, ncols), F32),
        compiler_params=pltpu.CompilerParams(dimension_semantics=("arbitrary",)),
        name="adaln",
    )(c, w, b.reshape(1, ncols))


def _inproj_kernel(x_ref, sh_ref, sc_ref, g_ref, w_ref, lng_ref, lnb_ref,
                   qkv0_ref, qkv1_ref, qkv2_ref, u_ref, vn_ref, ga_ref, gb_ref, kv0_ref, kv1_ref, kv2_ref,
                   *rest, tm, keeps, dils, emit_vn_f32):
    perm_scr = rest[-1]
    h = _rmsnorm_mod(x_ref[...], g_ref[...], sh_ref[...], sc_ref[...]).astype(BF16)

    def proj(c):
        return jnp.dot(h, w_ref[:, c * COL_CHUNK:(c + 1) * COL_CHUNK], preferred_element_type=F32)

    def cols(c):
        return slice(c * COL_CHUNK, (c + 1) * COL_CHUNK)

    qkv_refs = (qkv0_ref, qkv1_ref, qkv2_ref)

    def put_qkv(g, t, z):
        dil = dils[g]
        if dil == 1:
            qkv_refs[g][0, :, cols(t)] = z.astype(BF16)
            return
        slot = perm_scr.at[t % 2]
        n_slab = COL_CHUNK // LANES
        for j in range(n_slab):
            slot[j] = z[:, j * LANES:(j + 1) * LANES]
        for r in range(dil):
            rows = jnp.concatenate([slot[j, pl.ds(r, tm // dil, stride=dil), :] for j in range(n_slab)], axis=1)
            qkv_refs[g][r, :, cols(t)] = rows.astype(BF16)

    for g in range(N_GROUPS):
        put_qkv(g, 0, proj(g) * (HEAD_DIM ** -0.5))

    kv_refs = (kv0_ref, kv1_ref, kv2_ref)
    for g in range(N_GROUPS):
        keep = keeps[g]
        for t in (1, 2):
            z = proj(t * N_GROUPS + g)
            put_qkv(g, t, z)
            kv_refs[g][:, cols(t - 1)] = z if keep >= tm else z[tm - keep:, :]

    base = N_QKV // COL_CHUNK
    for c in range(2):
        u_ref[:, cols(c)] = _gelu_tanh(proj(base + c)).astype(BF16)

    vs = [_gelu_tanh(proj(base + 2 + c)) for c in range(2)]
    mu = (jnp.sum(vs[0], axis=-1, keepdims=True) + jnp.sum(vs[1], axis=-1, keepdims=True)) * (1.0 / B_WIDTH)
    ds = [v - mu for v in vs]
    var = (jnp.sum(ds[0] * ds[0], axis=-1, keepdims=True)
           + jnp.sum(ds[1] * ds[1], axis=-1, keepdims=True)) * (1.0 / B_WIDTH)
    inv = lax.rsqrt(var + EPS)
    for c in range(2):
        vn = ds[c] * inv * lng_ref[:, cols(c)] + lnb_ref[:, cols(c)]
        vn_ref[:, cols(c)] = vn.astype(BF16)
        if emit_vn_f32:
            rest[0][:, cols(c)] = vn

    for c in range(2):
        ga_ref[:, cols(c)] = _sigmoid(proj(base + 4 + c)).astype(BF16)
        gb_ref[:, cols(c)] = _sigmoid(proj(base + 6 + c)).astype(BF16)


def _inproj(x, sh, sc, gain, w_in, ln_g, ln_b, *, tm, keeps, dils, emit_vn_f32):
    bsz, seq, _ = x.shape
    n_s = seq // tm
    tok = lambda width: pl.BlockSpec((None, tm, width), lambda b, s: (b, s, 0))

    def kv_spec(keep):
        if keep >= tm:
            first = n_s - keep // tm
            return pl.BlockSpec((None, tm, 2 * A_WIDTH), lambda b, s: (b, jnp.maximum(s - first, 0), 0))
        return pl.BlockSpec((None, keep, 2 * A_WIDTH), lambda b, s: (b, 0, 0))

    out_specs, out_shape = [], []
    for dil in dils:
        out_specs.append(pl.BlockSpec((None, dil, tm // dil, 3 * A_WIDTH), lambda b, s: (b, 0, s, 0)))
        out_shape.append(jax.ShapeDtypeStruct((bsz, dil, seq // dil, 3 * A_WIDTH), BF16))
    out_specs += [tok(B_WIDTH), tok(B_WIDTH), tok(D_MODEL), tok(D_MODEL)]
    out_shape += [jax.ShapeDtypeStruct((bsz, seq, D_MODEL), BF16)] * 4
    for keep in keeps:
        out_specs.append(kv_spec(keep))
        out_shape.append(jax.ShapeDtypeStruct((bsz, keep, 2 * A_WIDTH), F32))
    if emit_vn_f32:
        out_specs.append(tok(B_WIDTH))
        out_shape.append(jax.ShapeDtypeStruct((bsz, seq, B_WIDTH), F32))

    return pl.pallas_call(
        functools.partial(_inproj_kernel, tm=tm, keeps=keeps, dils=dils, emit_vn_f32=emit_vn_f32),
        grid=(bsz, n_s),
        in_specs=[tok(D_MODEL), _mod_spec(sh, tm), _mod_spec(sc, tm), _const_spec(gain),
                  pl.BlockSpec(w_in.shape, lambda b, s: (0, 0), pipeline_mode=pl.Buffered(1)),
                  _const_spec(ln_g), _const_spec(ln_b)],
        out_specs=out_specs,
        out_shape=out_shape,
        scratch_shapes=[pltpu.VMEM((2, COL_CHUNK // LANES, tm, LANES), F32)],
        compiler_params=pltpu.CompilerParams(dimension_semantics=("arbitrary", "arbitrary"),
                                             vmem_limit_bytes=VMEM_LIMIT),
        name="inproj",
    )(x, sh, sc, gain, w_in, ln_g, ln_b)


def _attn_kernel(q_ref, kp_ref, kc_ref, vp_ref, vc_ref, bias_ref, o_ref, lse_ref):
    q = q_ref[...]
    k = jnp.concatenate([kp_ref[...], kc_ref[...]], axis=0)
    v = jnp.concatenate([vp_ref[...], vc_ref[...]], axis=0)
    lane = lax.broadcasted_iota(jnp.int32, (A_BLOCK, LANES), 1)
    low = lane < HEAD_DIM
    lse_tile = jnp.zeros((A_BLOCK, LANES), F32)
    zero = jnp.zeros((), BF16)
    for j in range(HEADS // 2):
        pair = slice(j * LANES, (j + 1) * LANES)
        qp, kpair, vpair = q[:, pair], k[:, pair], v[:, pair]
        outs = []
        for e in range(2):
            h = 2 * j + e
            qm = jnp.where(low if e == 0 else jnp.logical_not(low), qp, zero)
            s = lax.dot_general(qm, kpair, (((1,), (1,)), ((), ())), preferred_element_type=F32)
            s = s + bias_ref[h]
            m = jnp.max(s, axis=-1, keepdims=True)
            p = jnp.exp(s - m)
            den = jnp.sum(p, axis=-1, keepdims=True)
            o = jnp.dot(p.astype(BF16), vpair, preferred_element_type=F32) / den
            lse = m + jnp.log(den)
            lse_tile = jnp.where((lane >= h * LSE_REP) & (lane < (h + 1) * LSE_REP), lse, lse_tile)
            outs.append(o)
        o_ref[:, pair] = jnp.where(low, outs[0], outs[1]).astype(BF16)
    lse_ref[...] = lse_tile


def _attn_bias(g):
    _, dil = A_GROUPS[g]
    n = N_GROUPS * HEADS
    e = np.arange(1, n + 1, dtype=np.float32)
    slopes = np.exp2(-8.0 * e / n).astype(np.float32).reshape(N_GROUPS, HEADS)[g]
    qi = np.arange(A_BLOCK)[:, None]
    ki = np.arange(2 * A_BLOCK)[None, :]
    delta = qi + A_BLOCK - ki
    band = (delta >= 0) & (delta <= A_STEPS)
    dist = (delta * dil).astype(np.float32)
    bias = -slopes[:, None, None] * dist[None]
    out = np.empty((2, HEADS, A_BLOCK, 2 * A_BLOCK), np.float32)
    out[1] = np.where(band[None], bias, NEG)
    out[0] = np.where((band & (ki >= A_BLOCK))[None], bias, NEG)
    return jnp.asarray(out)


def _attn_prompt(qkv, g):
    bsz, dil, steps, _ = qkv.shape
    nb = steps // A_BLOCK
    bias = _attn_bias(g)

    def blk(t, prev):
        if prev:
            return pl.BlockSpec((None, None, A_BLOCK, A_WIDTH),
                                lambda b, r, n: (b, r, jnp.maximum(n - 1, 0), t))
        return pl.BlockSpec((None, None, A_BLOCK, A_WIDTH), lambda b, r, n: (b, r, n, t))

    return pl.pallas_call(
        _attn_kernel,
        grid=(bsz, dil, nb),
        in_specs=[blk(0, False), blk(1, True), blk(1, False), blk(2, True), blk(2, False),
                  pl.BlockSpec((None, HEADS, A_BLOCK, 2 * A_BLOCK),
                               lambda b, r, n: (jnp.minimum(n, 1), 0, 0, 0))],
        out_specs=[pl.BlockSpec((None, None, A_BLOCK, A_WIDTH), lambda b, r, n: (b, r, n, 0)),
                   pl.BlockSpec((None, None, A_BLOCK, LANES), lambda b, r, n: (b, r, n, 0))],
        out_shape=[jax.ShapeDtypeStruct((bsz, dil, steps, A_WIDTH), BF16),
                   jax.ShapeDtypeStruct((bsz, dil, steps, LANES), F32)],
        compiler_params=pltpu.CompilerParams(dimension_semantics=("arbitrary",) * 3),
        name=f"attn_prompt_g{g}",
    )(qkv, qkv, qkv, qkv, qkv, bias)


def _attn_sample_kernel(q0_ref, q1_ref, q2_ref, c0_ref, c1_ref, c2_ref, bc0_ref, bc1_ref, bc2_ref,
                        bn0_ref, bn1_ref, bn2_ref, o_ref, *, t_new):
    n_rows = HEADS * t_new
    row = lax.broadcasted_iota(jnp.int32, (n_rows, A_WIDTH), 0)
    lane = lax.broadcasted_iota(jnp.int32, (n_rows, A_WIDTH), 1)
    head_mask = (row // t_new) == (lane // HEAD_DIM)
    pad = jnp.zeros((LANES - t_new, A_WIDTH), F32)
    outs, lses = [], []
    for qkv_ref, c_ref, bc_ref, bn_ref in ((q0_ref, c0_ref, bc0_ref, bn0_ref), (q1_ref, c1_ref, bc1_ref, bn1_ref),
                                           (q2_ref, c2_ref, bc2_ref, bn2_ref)):
        q = qkv_ref[:, :A_WIDTH].astype(F32)
        k_new = qkv_ref[:, A_WIDTH:2 * A_WIDTH].astype(F32)
        v_new = qkv_ref[:, 2 * A_WIDTH:].astype(F32)
        k_new = jnp.concatenate([k_new, pad], axis=0).astype(BF16)
        v_new = jnp.concatenate([v_new, pad], axis=0).astype(BF16)
        q_rows = jnp.where(head_mask, jnp.concatenate([q] * HEADS, axis=0), 0.0).astype(BF16)
        k_buf_t = c_ref[:A_WIDTH, :].astype(BF16)
        v_buf_t = c_ref[A_WIDTH:, :].astype(BF16)
        nt = (((1,), (1,)), ((), ()))
        s_buf = jnp.dot(q_rows, k_buf_t, preferred_element_type=F32) + bc_ref[...]
        s_new = lax.dot_general(q_rows, k_new, nt, preferred_element_type=F32) + bn_ref[...]
        m = jnp.maximum(jnp.max(s_buf, axis=-1, keepdims=True), jnp.max(s_new, axis=-1, keepdims=True))
        p_buf = jnp.exp(s_buf - m)
        p_new = jnp.exp(s_new - m)
        den = jnp.sum(p_buf, axis=-1, keepdims=True) + jnp.sum(p_new, axis=-1, keepdims=True)
        o = (lax.dot_general(p_buf.astype(BF16), v_buf_t, nt, preferred_element_type=F32)
             + jnp.dot(p_new.astype(BF16), v_new, preferred_element_type=F32)) / den
        outs.append(o)
        lses.append(m + jnp.log(den))
    top = jnp.maximum(jnp.maximum(lses[0], lses[1]), lses[2])
    ws = [jnp.exp(l - top) for l in lses]
    tot = ws[0] + ws[1] + ws[2]
    acc = (ws[0] / tot) * outs[0] + (ws[1] / tot) * outs[1] + (ws[2] / tot) * outs[2]
    acc = jnp.where(head_mask, acc, 0.0).reshape(HEADS, t_new, A_WIDTH)
    o_ref[...] = jnp.sum(acc, axis=0).astype(BF16)


def _sample_bias(g, t_new, buf):
    _, dil = A_GROUPS[g]
    n = N_GROUPS * HEADS
    e = np.arange(1, n + 1, dtype=np.float32)
    slopes = np.exp2(-8.0 * e / n).astype(np.float32).reshape(N_GROUPS, HEADS)[g]
    t = np.arange(t_new)[:, None]
    idx = np.concatenate([np.arange(buf), buf + np.arange(LANES)])[None, :]
    dist = buf + t - idx
    valid = (dist >= 0) & (dist % dil == 0) & (dist <= A_STEPS * dil) & (idx < buf + t_new)
    bias = -slopes[:, None, None] * dist.astype(np.float32)[None]
    bias = np.where(valid[None], bias, NEG).astype(np.float32).reshape(HEADS * t_new, buf + LANES)
    return jnp.asarray(bias[:, :buf]), jnp.asarray(bias[:, buf:])


def _attn_sample(qkvs, caches):
    bsz, t_new, _ = qkvs[0].shape
    cache_v = [jnp.transpose(c, (0, 2, 3, 4, 1)).reshape(bsz, 2 * A_WIDTH, c.shape[1]) for c in caches]
    biases = [_sample_bias(g, t_new, cache_v[g].shape[2]) for g in range(N_GROUPS)]
    bcs = [b[0] for b in biases]
    bns = [b[1] for b in biases]
    return pl.pallas_call(
        functools.partial(_attn_sample_kernel, t_new=t_new),
        grid=(bsz,),
        in_specs=[pl.BlockSpec((None, t_new, 3 * A_WIDTH), lambda b: (b, 0, 0))] * N_GROUPS
                 + [pl.BlockSpec((None, 2 * A_WIDTH, c.shape[2]), lambda b: (b, 0, 0)) for c in cache_v]
                 + [_const_spec(b) for b in bcs] + [_const_spec(b) for b in bns],
        out_specs=pl.BlockSpec((None, t_new, A_WIDTH), lambda b: (b, 0, 0)),
        out_shape=jax.ShapeDtypeStruct((bsz, t_new, A_WIDTH), BF16),
        compiler_params=pltpu.CompilerParams(dimension_semantics=("arbitrary",),
                                             vmem_limit_bytes=VMEM_LIMIT),
        name="attn_sample",
    )(*qkvs, *cache_v, *bcs, *bns)


def _route(logits):
    lane = lax.broadcasted_iota(jnp.int32, logits.shape, 1).astype(F32)
    ninf = -jnp.inf
    far = float(LANES)
    gl = jnp.where(lane < MOE_GROUPS, logits, ninf)
    gmax = jnp.max(gl, axis=-1, keepdims=True)
    g_prob = 1.0 / jnp.sum(jnp.exp(gl - gmax), axis=-1, keepdims=True)
    g_idx = jnp.min(jnp.where(gl == gmax, lane, far), axis=-1, keepdims=True)
    first = ROUTE_OFF + MOE_PER_GROUP * g_idx
    el = jnp.where((lane >= first) & (lane < first + MOE_PER_GROUP), logits, ninf)
    e1 = jnp.max(el, axis=-1, keepdims=True)
    i1 = jnp.min(jnp.where(el == e1, lane, far), axis=-1, keepdims=True)
    el2 = jnp.where(lane == i1, ninf, el)
    e2 = jnp.max(el2, axis=-1, keepdims=True)
    i2 = jnp.min(jnp.where(el2 == e2, lane, far), axis=-1, keepdims=True)
    t = jnp.exp(e2 - e1)
    w1 = 1.0 / (1.0 + t)
    w2 = t / (1.0 + t)
    return jnp.where(lane == i1, w1 * g_prob, jnp.where(lane == i2, w2 * g_prob, 0.0))


def _mix_kernel(*refs, tm, dils):
    n_attn = len(dils)
    x_ref = refs[0]
    if n_attn > 1:
        o_refs = refs[1:1 + n_attn]
        l_refs = refs[1 + n_attn:1 + 2 * n_attn]
        exp_ref = refs[1 + 2 * n_attn]
        pos = 2 + 2 * n_attn
    else:
        o_refs = refs[1:2]
        pos = 2
    (u_ref, vn_ref, ga_ref, gb_ref, sw_ref, sbt_ref, wpa_ref, wpb_ref, wo_ref,
     gt1_ref, sh2_ref, sc2_ref, n2g_ref, wr_ref, br_ref,
     x1_ref, h2_ref, comb_ref, ob_scr) = refs[pos:pos + 19]

    def to_positions(ref, scr, dil):
        if dil == 1:
            return ref[0].astype(F32)
        n_slab = scr.shape[0]
        for r in range(dil):
            rows = ref[r].astype(F32)
            for j in range(n_slab):
                scr[j, pl.ds(r, tm // dil, stride=dil), :] = rows[:, j * LANES:(j + 1) * LANES]
        return jnp.concatenate([scr[j] for j in range(n_slab)], axis=1)

    if n_attn > 1:
        o_scr, l_scr = refs[pos + 19:]
        lses = [to_positions(l_refs[g], l_scr.at[g], dils[g]) for g in range(n_attn)]
        top = functools.reduce(jnp.maximum, lses)
        ws = [jnp.exp(l - top) for l in lses]
        tot = functools.reduce(lambda a, b: a + b, ws)
        o_a = None
        for g in range(n_attn):
            w = ws[g] / tot
            hi = w.astype(BF16)
            lo = (w - hi.astype(F32)).astype(BF16)
            w_exp = jnp.dot(jnp.concatenate([hi, lo], axis=1), exp_ref[...], preferred_element_type=F32)
            term = w_exp * to_positions(o_refs[g], o_scr.at[g], dils[g])
            o_a = term if o_a is None else o_a + term
        o_a = o_a.astype(BF16)
    else:
        o_a = o_refs[0][...]

    r_i = lax.broadcasted_iota(jnp.int32, (B_CHUNK, B_CHUNK), 0)
    c_i = lax.broadcasted_iota(jnp.int32, (B_CHUNK, B_CHUNK), 1)
    tril = r_i >= c_i
    for gi in range(B_GROUPS):
        gcols = slice(gi * LANES, (gi + 1) * LANES)
        w_s = jnp.where(tril, sw_ref[gi], 0.0).astype(BF16)
        b_s = sbt_ref[:, gi:gi + 1]
        for c in range(tm // B_CHUNK):
            rows = slice(c * B_CHUNK, (c + 1) * B_CHUNK)
            mixed = jnp.dot(w_s, vn_ref[rows, gcols], preferred_element_type=F32) + b_s
            ob_scr[rows, gcols] = (u_ref[rows, gcols].astype(F32) * mixed).astype(BF16)

    pa = jnp.dot(o_a, wpa_ref[...], preferred_element_type=F32)
    pb = jnp.dot(ob_scr[...], wpb_ref[...], preferred_element_type=F32)
    mix = (ga_ref[...].astype(F32) * pa + gb_ref[...].astype(F32) * pb).astype(BF16)
    x1 = x_ref[...] + gt1_ref[...] * jnp.dot(mix, wo_ref[...], preferred_element_type=F32)
    x1_ref[...] = x1
    h2 = _rmsnorm_mod(x1, n2g_ref[...], sh2_ref[...], sc2_ref[...])
    h2_ref[...] = h2.astype(BF16)
    logits = jnp.dot(h2, wr_ref[...], preferred_element_type=F32, precision=lax.Precision.HIGHEST) + br_ref[...]
    comb_ref[...] = _route(logits)


def _expand_matrix():
    e = np.zeros((LANES, A_WIDTH), np.float32)
    for h in range(HEADS):
        e[h * LSE_REP, h * HEAD_DIM:(h + 1) * HEAD_DIM] = 1.0
    return jnp.asarray(np.concatenate([e, e], axis=0), dtype=BF16)


def _mix(x, attn_outs, attn_lses, u, vn, ga, gb, sgu_w, sgu_bt, w_pa, w_pb, w_o,
         gt1, sh2, sc2, n2g, w_r, b_r, *, tm):
    bsz, seq, _ = x.shape
    n_attn = len(attn_outs)
    tok = lambda width: pl.BlockSpec((None, tm, width), lambda b, s: (b, s, 0))
    ins = [x] + list(attn_outs)
    scratch = [pltpu.VMEM((tm, B_WIDTH), BF16)]
    if n_attn > 1:
        dils = tuple(o.shape[1] for o in attn_outs)
        res = lambda dil, width: pl.BlockSpec((None, dil, tm // dil, width), lambda b, s: (b, 0, s, 0))
        e2 = _expand_matrix()
        ins += list(attn_lses) + [e2]
        specs = ([tok(D_MODEL)] + [res(d, A_WIDTH) for d in dils] + [res(d, LANES) for d in dils]
                 + [_const_spec(e2)])
        scratch += [pltpu.VMEM((n_attn, A_WIDTH // LANES, tm, LANES), F32), pltpu.VMEM((n_attn, 1, tm, LANES), F32)]
    else:
        dils = (1,)
        specs = [tok(D_MODEL), tok(A_WIDTH)]
    ins += [u, vn, ga, gb, sgu_w, sgu_bt, w_pa, w_pb, w_o, gt1, sh2, sc2, n2g, w_r, b_r]
    specs += [tok(B_WIDTH), tok(B_WIDTH), tok(D_MODEL), tok(D_MODEL),
              _const_spec(sgu_w), _const_spec(sgu_bt), _const_spec(w_pa), _const_spec(w_pb), _const_spec(w_o),
              _mod_spec(gt1, tm), _mod_spec(sh2, tm), _mod_spec(sc2, tm), _const_spec(n2g),
              _const_spec(w_r), _const_spec(b_r)]
    return pl.pallas_call(
        functools.partial(_mix_kernel, tm=tm, dils=dils),
        grid=(bsz, seq // tm),
        in_specs=specs,
        out_specs=[tok(D_MODEL), tok(D_MODEL), tok(LANES)],
        out_shape=[jax.ShapeDtypeStruct((bsz, seq, D_MODEL), F32),
                   jax.ShapeDtypeStruct((bsz, seq, D_MODEL), BF16),
                   jax.ShapeDtypeStruct((bsz, seq, LANES), F32)],
        scratch_shapes=scratch,
        compiler_params=pltpu.CompilerParams(dimension_semantics=("arbitrary", "arbitrary"),
                                             vmem_limit_bytes=VMEM_LIMIT),
        name="mix",
    )(*ins)


def _moe_kernel(h_ref, comb_ref, wg_ref, wu_ref, wd_ref, x1_ref, gt2_ref, shf_ref, scf_ref, nfg_ref,
                y_ref, acc_ref):
    e = pl.program_id(2)

    @pl.when(e == 0)
    def _():
        acc_ref[...] = jnp.zeros_like(acc_ref)

    h = h_ref[...]
    hg = jnp.dot(h, wg_ref[...], preferred_element_type=F32)
    hu = jnp.dot(h, wu_ref[...], preferred_element_type=F32)
    comb = comb_ref[...]
    lane = lax.broadcasted_iota(jnp.int32, comb.shape, 1)
    cw = jnp.sum(jnp.where(lane == e + ROUTE_OFF, comb, 0.0), axis=-1, keepdims=True)
    act = (hg * _sigmoid(hg) * hu) * cw
    acc_ref[...] += jnp.dot(act.astype(BF16), wd_ref[...], preferred_element_type=F32)

    @pl.when(e == MOE_EXPERTS - 1)
    def _():
        x2 = x1_ref[...] + gt2_ref[...] * acc_ref[...]
        y_ref[...] = _rmsnorm_mod(x2, nfg_ref[...], shf_ref[...], scf_ref[...])


def _moe(h2, comb, w_gate, w_up, w_down, x1, gt2, shf, scf, nfg, *, tm):
    bsz, seq, _ = h2.shape
    tok = lambda width: pl.BlockSpec((None, tm, width), lambda b, s, e: (b, s, 0))
    return pl.pallas_call(
        _moe_kernel,
        grid=(bsz, seq // tm, MOE_EXPERTS),
        in_specs=[tok(D_MODEL), tok(LANES),
                  pl.BlockSpec((None, D_MODEL, MOE_HIDDEN), lambda b, s, e: (e, 0, 0)),
                  pl.BlockSpec((None, D_MODEL, MOE_HIDDEN), lambda b, s, e: (e, 0, 0)),
                  pl.BlockSpec((None, MOE_HIDDEN, D_MODEL), lambda b, s, e: (e, 0, 0)),
                  tok(D_MODEL), _mod_spec(gt2, tm), _mod_spec(shf, tm), _mod_spec(scf, tm), _const_spec(nfg)],
        out_specs=tok(D_MODEL),
        out_shape=jax.ShapeDtypeStruct((bsz, seq, D_MODEL), F32),
        scratch_shapes=[pltpu.VMEM((tm, D_MODEL), F32)],
        compiler_params=pltpu.CompilerParams(dimension_semantics=("arbitrary",) * 3,
                                             vmem_limit_bytes=VMEM_LIMIT),
        name="moe",
    )(h2, comb, w_gate, w_up, w_down, x1, gt2, shf, scf, nfg)


def kernel(x_prompt, x_sample, cache_kv_w128, cache_kv_w512, cache_kv_w2048, c_prompt, c_sample,
           w_ada, b_ada, norm1_g, norm2_g, w_in, sgu_ln_g, sgu_ln_b, sgu_w, sgu_b, w_pa, w_pb, w_o,
           w_route_group, b_route_group, w_route_expert, b_route_expert, w_gate, w_up, w_down,
           normf_g, w_ada_final, b_ada_final):
    depth = w_ada.shape[0]
    assert depth == 1
    l = 0
    bp, seq, _ = x_prompt.shape
    bs, t_new, _ = x_sample.shape
    n_samp = bs * t_new

    c_all = jnp.concatenate([c_prompt, c_sample], axis=0)
    pad_rows = (-c_all.shape[0]) % 8
    c_all = jnp.pad(c_all, ((0, pad_rows), (0, 0)))
    mod = _adaln(c_all, w_ada[l], b_ada[l])
    mod_f = _adaln(c_all, w_ada_final, b_ada_final)

    def split_mods(m, n, lo, hi, per_token):
        parts = jnp.split(m[lo:hi], n, axis=-1)
        if per_token:
            return [jnp.repeat(p, t_new, axis=0).reshape(1, n_samp, D_MODEL) for p in parts]
        return [p.reshape(hi - lo, 1, D_MODEL) for p in parts]

    mods_p = split_mods(mod, 6, 0, bp, False) + split_mods(mod_f, 2, 0, bp, False)
    mods_s = split_mods(mod, 6, bp, bp + bs, True) + split_mods(mod_f, 2, bp, bp + bs, True)

    row = lambda v: v.reshape(1, -1)
    w_in_b = w_in[l].astype(BF16)
    w_pa_b, w_pb_b, w_o_b = w_pa[l].astype(BF16), w_pb[l].astype(BF16), w_o[l].astype(BF16)
    w_gate_b, w_up_b, w_down_b = w_gate[l].astype(BF16), w_up[l].astype(BF16), w_down[l].astype(BF16)
    w_re = jnp.transpose(w_route_expert[l], (1, 0, 2)).reshape(D_MODEL, MOE_EXPERTS)
    w_r = jnp.pad(jnp.concatenate([w_route_group[l], w_re], axis=1),
                  ((0, 0), (0, LANES - MOE_GROUPS - MOE_EXPERTS)))
    b_r = jnp.pad(jnp.concatenate([b_route_group[l], b_route_expert[l].reshape(-1)]),
                  (0, LANES - MOE_GROUPS - MOE_EXPERTS)).reshape(1, LANES)

    def layer(x, mods, attend, sgu_w_eff, sgu_bt_eff, keeps, dils, tm_in, tm, emit_vn_f32):
        sh1, sc1, gt1, sh2, sc2, gt2, shf, scf = mods
        outs = _inproj(x, sh1, sc1, row(norm1_g[l]), w_in_b, row(sgu_ln_g[l]), row(sgu_ln_b[l]),
                       tm=tm_in, keeps=keeps, dils=dils, emit_vn_f32=emit_vn_f32)
        qkvs = outs[:3]
        u, vn, ga, gb, kv0, kv1, kv2 = outs[3:10]
        attn_outs, attn_lses = attend(qkvs)
        x1, h2, comb = _mix(x, attn_outs, attn_lses, u, vn, ga, gb, sgu_w_eff, sgu_bt_eff,
                            w_pa_b, w_pb_b, w_o_b, gt1, sh2, sc2, row(norm2_g[l]), w_r, b_r, tm=tm)
        y = _moe(h2, comb, w_gate_b, w_up_b, w_down_b, x1, gt2, shf, scf, row(normf_g), tm=tm)
        return y, (kv0, kv1, kv2), outs[10:]

    def attend_p(qkvs):
        res = [_attn_prompt(qkvs[g], g) for g in range(N_GROUPS)]
        return [r[0] for r in res], [r[1] for r in res]

    keeps_p = tuple(min(win, seq) for win, _ in A_GROUPS)
    dils_p = tuple(dil for _, dil in A_GROUPS)
    y_p, kv_p, _ = layer(x_prompt, mods_p, attend_p, sgu_w[l], jnp.transpose(sgu_b[l]), keeps_p, dils_p,
                         256, 256, False)

    caches = (cache_kv_w128[l], cache_kv_w512[l], cache_kv_w2048[l])

    def attend_s(qkvs):
        qkvs = [q.reshape(bs, t_new, 3 * A_WIDTH) for q in qkvs]
        return [_attn_sample(qkvs, caches).reshape(1, n_samp, A_WIDTH)], None

    cl = min(t_new, B_CHUNK)
    eye = jnp.eye(B_CHUNK // cl, dtype=F32)
    sgu_w_s = jnp.einsum('ab,gts->gatbs', eye, sgu_w[l][:, :cl, :cl]).reshape(B_GROUPS, B_CHUNK, B_CHUNK)
    sgu_bt_s = jnp.tile(jnp.transpose(sgu_b[l][:, :cl]), (B_CHUNK // cl, 1))
    y_s, kv_s, extra = layer(x_sample.reshape(1, n_samp, D_MODEL), mods_s, attend_s, sgu_w_s, sgu_bt_s,
                             (n_samp,) * N_GROUPS, (1,) * N_GROUPS, n_samp, n_samp, True)

    def kv_out(a, b):
        return a.reshape(depth, b, -1, 2, HEADS, HEAD_DIM)

    return (y_p, y_s.reshape(bs, t_new, D_MODEL),
            kv_out(kv_p[0], bp), kv_out(kv_p[1], bp), kv_out(kv_p[2], bp),
            kv_out(kv_s[0], bs), kv_out(kv_s[1], bs), kv_out(kv_s[2], bs),
            extra[0].reshape(depth, bs, t_new, B_WIDTH))
```

```python
import functools

import numpy as np
import jax
import jax.numpy as jnp
from jax import lax
from jax.experimental import pallas as pl
from jax.experimental.pallas import tpu as pltpu
from jax.experimental.pallas import tpu_sc as plsc

F32 = jnp.float32
BF16 = jnp.bfloat16

D_MODEL = 1024
A_GROUPS = ((128, 1), (512, 4), (2048, 16))
N_GROUPS = 3
HEADS = 8
HEAD_DIM = 64
A_WIDTH = HEADS * HEAD_DIM
A_STEPS = 128
A_BLOCK = 128
B_WIDTH = 1024
B_GROUPS = 8
B_CHUNK = 128
MOE_GROUPS = 4
MOE_PER_GROUP = 4
MOE_EXPERTS = 16
MOE_HIDDEN = 512
EPS = 1e-6
N_QKV = 3 * N_GROUPS * A_WIDTH
IN_COLS = N_QKV + 2 * B_WIDTH + 2 * D_MODEL
COL_CHUNK = 512
LANES = 128
LSE_REP = LANES // HEADS
ROUTE_OFF = MOE_GROUPS
NEG = -1e30
VMEM_LIMIT = 56 * 1024 * 1024
MOE_TILE = 256
SC_WINDOW = 128
SC_CORES = 2


def _sigmoid(x):
    return 1.0 / (1.0 + jnp.exp(-x))


def _gelu_tanh(x):
    return x * (0.5 * (1.0 + jnp.tanh(0.7978845608028654 * (x + 0.044715 * (x * x * x)))))


def _rmsnorm_mod(x, gain, shift, scale):
    y = x * lax.rsqrt(jnp.mean(x * x, axis=-1, keepdims=True) + EPS)
    return y * gain * (1.0 + scale) + shift


def _mod_spec(mod, tm):
    if mod.shape[1] == 1:
        return pl.BlockSpec((None, 1, D_MODEL), lambda b, s, *_: (b, 0, 0))
    return pl.BlockSpec((None, tm, D_MODEL), lambda b, s, *_: (b, s, 0))


def _const_spec(arr):
    nd = arr.ndim
    return pl.BlockSpec(arr.shape, lambda *_: (0,) * nd)


def _adaln_kernel(c_ref, w_ref, b_ref, o_ref):
    c = c_ref[...]
    a = (c * _sigmoid(c)).astype(BF16)
    o_ref[...] = jnp.dot(a, w_ref[...].astype(BF16), preferred_element_type=F32) + b_ref[...]


def _adaln(c, w, b, tn=1024):
    rows, ncols = c.shape[0], w.shape[1]
    return pl.pallas_call(
        _adaln_kernel,
        grid=(ncols // tn,),
        in_specs=[pl.BlockSpec((rows, D_MODEL), lambda j: (0, 0)),
                  pl.BlockSpec((D_MODEL, tn), lambda j: (0, j)),
                  pl.BlockSpec((1, tn), lambda j: (0, j))],
        out_specs=pl.BlockSpec((rows, tn), lambda j: (0, j)),
        out_shape=jax.ShapeDtypeStruct((rows, ncols), F32),
        compiler_params=pltpu.CompilerParams(dimension_semantics=("arbitrary",)),
        name="adaln",
    )(c, w, b.reshape(1, ncols))


def _inproj_kernel(x_ref, sh_ref, sc_ref, g_ref, w_ref, lng_ref, lnb_ref,
                   qkv0_ref, qkv1_ref, qkv2_ref, u_ref, vn_ref, ga_ref, gb_ref, kv0_ref, kv1_ref, kv2_ref,
                   *rest, tm, keeps, dils, emit_vn_f32):
    perm_scr = rest[-1]
    h = _rmsnorm_mod(x_ref[...], g_ref[...], sh_ref[...], sc_ref[...]).astype(BF16)

    def proj(c):
        return jnp.dot(h, w_ref[:, c * COL_CHUNK:(c + 1) * COL_CHUNK], preferred_element_type=F32)

    def cols(c):
        return slice(c * COL_CHUNK, (c + 1) * COL_CHUNK)

    qkv_refs = (qkv0_ref, qkv1_ref, qkv2_ref)

    def put_qkv(g, t, z):
        dil = dils[g]
        if dil == 1:
            qkv_refs[g][0, :, cols(t)] = z.astype(BF16)
            return
        slot = perm_scr.at[t % 2]
        n_slab = COL_CHUNK // LANES
        for j in range(n_slab):
            slot[j] = z[:, j * LANES:(j + 1) * LANES]
        for r in range(dil):
            rows = jnp.concatenate([slot[j, pl.ds(r, tm // dil, stride=dil), :] for j in range(n_slab)], axis=1)
            qkv_refs[g][r, :, cols(t)] = rows.astype(BF16)

    for g in range(N_GROUPS):
        put_qkv(g, 0, proj(g) * (HEAD_DIM ** -0.5))

    kv_refs = (kv0_ref, kv1_ref, kv2_ref)
    for g in range(N_GROUPS):
        keep = keeps[g]
        for t in (1, 2):
            z = proj(t * N_GROUPS + g)
            put_qkv(g, t, z)
            kv_refs[g][:, cols(t - 1)] = z if keep >= tm else z[tm - keep:, :]

    base = N_QKV // COL_CHUNK
    for c in range(2):
        u_ref[:, cols(c)] = _gelu_tanh(proj(base + c)).astype(BF16)

    vs = [_gelu_tanh(proj(base + 2 + c)) for c in range(2)]
    mu = (jnp.sum(vs[0], axis=-1, keepdims=True) + jnp.sum(vs[1], axis=-1, keepdims=True)) * (1.0 / B_WIDTH)
    ds = [v - mu for v in vs]
    var = (jnp.sum(ds[0] * ds[0], axis=-1, keepdims=True)
           + jnp.sum(ds[1] * ds[1], axis=-1, keepdims=True)) * (1.0 / B_WIDTH)
    inv = lax.rsqrt(var + EPS)
    for c in range(2):
        vn = ds[c] * inv * lng_ref[:, cols(c)] + lnb_ref[:, cols(c)]
        vn_ref[:, cols(c)] = vn.astype(BF16)
        if emit_vn_f32:
            rest[0][:, cols(c)] = vn

    for c in range(2):
        ga_ref[:, cols(c)] = _sigmoid(proj(base + 4 + c)).astype(BF16)
        gb_ref[:, cols(c)] = _sigmoid(proj(base + 6 + c)).astype(BF16)


def _inproj(x, sh, sc, gain, w_in, ln_g, ln_b, *, tm, keeps, dils, emit_vn_f32):
    bsz, seq, _ = x.shape
    n_s = seq // tm
    tok = lambda width: pl.BlockSpec((None, tm, width), lambda b, s: (b, s, 0))

    def kv_spec(keep):
        if keep >= tm:
            first = n_s - keep // tm
            return pl.BlockSpec((None, tm, 2 * A_WIDTH), lambda b, s: (b, jnp.maximum(s - first, 0), 0))
        return pl.BlockSpec((None, keep, 2 * A_WIDTH), lambda b, s: (b, 0, 0))

    out_specs, out_shape = [], []
    for dil in dils:
        out_specs.append(pl.BlockSpec((None, dil, tm // dil, 3 * A_WIDTH), lambda b, s: (b, 0, s, 0)))
        out_shape.append(jax.ShapeDtypeStruct((bsz, dil, seq // dil, 3 * A_WIDTH), BF16))
    out_specs += [tok(B_WIDTH), tok(B_WIDTH), tok(D_MODEL), tok(D_MODEL)]
    out_shape += [jax.ShapeDtypeStruct((bsz, seq, D_MODEL), BF16)] * 4
    for keep in keeps:
        out_specs.append(kv_spec(keep))
        out_shape.append(jax.ShapeDtypeStruct((bsz, keep, 2 * A_WIDTH), F32))
    if emit_vn_f32:
        out_specs.append(tok(B_WIDTH))
        out_shape.append(jax.ShapeDtypeStruct((bsz, seq, B_WIDTH), F32))

    return pl.pallas_call(
        functools.partial(_inproj_kernel, tm=tm, keeps=keeps, dils=dils, emit_vn_f32=emit_vn_f32),
        grid=(bsz, n_s),
        in_specs=[tok(D_MODEL), _mod_spec(sh, tm), _mod_spec(sc, tm), _const_spec(gain),
                  pl.BlockSpec(w_in.shape, lambda b, s: (0, 0), pipeline_mode=pl.Buffered(1)),
                  _const_spec(ln_g), _const_spec(ln_b)],
        out_specs=out_specs,
        out_shape=out_shape,
        scratch_shapes=[pltpu.VMEM((2, COL_CHUNK // LANES, tm, LANES), F32)],
        compiler_params=pltpu.CompilerParams(dimension_semantics=("arbitrary", "arbitrary"),
                                             vmem_limit_bytes=VMEM_LIMIT),
        name="inproj",
    )(x, sh, sc, gain, w_in, ln_g, ln_b)


def _attn_kernel(q_ref, kp_ref, kc_ref, vp_ref, vc_ref, bias_ref, o_ref, lse_ref):
    q = q_ref[...]
    k = jnp.concatenate([kp_ref[...], kc_ref[...]], axis=0)
    v = jnp.concatenate([vp_ref[...], vc_ref[...]], axis=0)
    lane = lax.broadcasted_iota(jnp.int32, (A_BLOCK, LANES), 1)
    low = lane < HEAD_DIM
    lse_tile = jnp.zeros((A_BLOCK, LANES), F32)
    zero = jnp.zeros((), BF16)
    for j in range(HEADS // 2):
        pair = slice(j * LANES, (j + 1) * LANES)
        qp, kpair, vpair = q[:, pair], k[:, pair], v[:, pair]
        outs = []
        for e in range(2):
            h = 2 * j + e
            qm = jnp.where(low if e == 0 else jnp.logical_not(low), qp, zero)
            s = lax.dot_general(qm, kpair, (((1,), (1,)), ((), ())), preferred_element_type=F32)
            s = s + bias_ref[h]
            m = jnp.max(s, axis=-1, keepdims=True)
            p = jnp.exp(s - m)
            den = jnp.sum(p, axis=-1, keepdims=True)
            o = jnp.dot(p.astype(BF16), vpair, preferred_element_type=F32) / den
            lse = m + jnp.log(den)
            lse_tile = jnp.where((lane >= h * LSE_REP) & (lane < (h + 1) * LSE_REP), lse, lse_tile)
            outs.append(o)
        o_ref[:, pair] = jnp.where(low, outs[0], outs[1]).astype(BF16)
    lse_ref[...] = lse_tile


def _attn_bias(g):
    _, dil = A_GROUPS[g]
    n = N_GROUPS * HEADS
    e = np.arange(1, n + 1, dtype=np.float32)
    slopes = np.exp2(-8.0 * e / n).astype(np.float32).reshape(N_GROUPS, HEADS)[g]
    qi = np.arange(A_BLOCK)[:, None]
    ki = np.arange(2 * A_BLOCK)[None, :]
    delta = qi + A_BLOCK - ki
    band = (delta >= 0) & (delta <= A_STEPS)
    dist = (delta * dil).astype(np.float32)
    bias = -slopes[:, None, None] * dist[None]
    out = np.empty((2, HEADS, A_BLOCK, 2 * A_BLOCK), np.float32)
    out[1] = np.where(band[None], bias, NEG)
    out[0] = np.where((band & (ki >= A_BLOCK))[None], bias, NEG)
    return jnp.asarray(out)


def _attn_prompt(qkv, g):
    bsz, dil, steps, _ = qkv.shape
    nb = steps // A_BLOCK
    bias = _attn_bias(g)

    def blk(t, prev):
        if prev:
            return pl.BlockSpec((None, None, A_BLOCK, A_WIDTH),
                                lambda b, r, n: (b, r, jnp.maximum(n - 1, 0), t))
        return pl.BlockSpec((None, None, A_BLOCK, A_WIDTH), lambda b, r, n: (b, r, n, t))

    return pl.pallas_call(
        _attn_kernel,
        grid=(bsz, dil, nb),
        in_specs=[blk(0, False), blk(1, True), blk(1, False), blk(2, True), blk(2, False),
                  pl.BlockSpec((None, HEADS, A_BLOCK, 2 * A_BLOCK),
                               lambda b, r, n: (jnp.minimum(n, 1), 0, 0, 0))],
        out_specs=[pl.BlockSpec((None, None, A_BLOCK, A_WIDTH), lambda b, r, n: (b, r, n, 0)),
                   pl.BlockSpec((None, None, A_BLOCK, LANES), lambda b, r, n: (b, r, n, 0))],
        out_shape=[jax.ShapeDtypeStruct((bsz, dil, steps, A_WIDTH), BF16),
                   jax.ShapeDtypeStruct((bsz, dil, steps, LANES), F32)],
        compiler_params=pltpu.CompilerParams(dimension_semantics=("arbitrary",) * 3),
        name=f"attn_prompt_g{g}",
    )(qkv, qkv, qkv, qkv, qkv, bias)


def _attn_sample_kernel(q0_ref, q1_ref, q2_ref, c0_ref, c1_ref, c2_ref, bc0_ref, bc1_ref, bc2_ref,
                        bn0_ref, bn1_ref, bn2_ref, o_ref, *, t_new):
    n_rows = HEADS * t_new
    row = lax.broadcasted_iota(jnp.int32, (n_rows, A_WIDTH), 0)
    lane = lax.broadcasted_iota(jnp.int32, (n_rows, A_WIDTH), 1)
    head_mask = (row // t_new) == (lane // HEAD_DIM)
    pad = jnp.zeros((LANES - t_new, A_WIDTH), F32)
    outs, lses = [], []
    for qkv_ref, c_ref, bc_ref, bn_ref in ((q0_ref, c0_ref, bc0_ref, bn0_ref), (q1_ref, c1_ref, bc1_ref, bn1_ref),
                                           (q2_ref, c2_ref, bc2_ref, bn2_ref)):
        q = qkv_ref[:, :A_WIDTH].astype(F32)
        k_new = qkv_ref[:, A_WIDTH:2 * A_WIDTH].astype(F32)
        v_new = qkv_ref[:, 2 * A_WIDTH:].astype(F32)
        k_new = jnp.concatenate([k_new, pad], axis=0).astype(BF16)
        v_new = jnp.concatenate([v_new, pad], axis=0).astype(BF16)
        q_rows = jnp.where(head_mask, jnp.concatenate([q] * HEADS, axis=0), 0.0).astype(BF16)
        k_buf_t = c_ref[:A_WIDTH, :].astype(BF16)
        v_buf_t = c_ref[A_WIDTH:, :].astype(BF16)
        nt = (((1,), (1,)), ((), ()))
        s_buf = jnp.dot(q_rows, k_buf_t, preferred_element_type=F32) + bc_ref[...]
        s_new = lax.dot_general(q_rows, k_new, nt, preferred_element_type=F32) + bn_ref[...]
        m = jnp.maximum(jnp.max(s_buf, axis=-1, keepdims=True), jnp.max(s_new, axis=-1, keepdims=True))
        p_buf = jnp.exp(s_buf - m)
        p_new = jnp.exp(s_new - m)
        den = jnp.sum(p_buf, axis=-1, keepdims=True) + jnp.sum(p_new, axis=-1, keepdims=True)
        o = (lax.dot_general(p_buf.astype(BF16), v_buf_t, nt, preferred_element_type=F32)
             + jnp.dot(p_new.astype(BF16), v_new, preferred_element_type=F32)) / den
        outs.append(o)
        lses.append(m + jnp.log(den))
    top = jnp.maximum(jnp.maximum(lses[0], lses[1]), lses[2])
    ws = [jnp.exp(l - top) for l in lses]
    tot = ws[0] + ws[1] + ws[2]
    acc = (ws[0] / tot) * outs[0] + (ws[1] / tot) * outs[1] + (ws[2] / tot) * outs[2]
    acc = jnp.where(head_mask, acc, 0.0).reshape(HEADS, t_new, A_WIDTH)
    o_ref[...] = jnp.sum(acc, axis=0).astype(BF16)


def _sample_bias(g, t_new, buf):
    _, dil = A_GROUPS[g]
    n = N_GROUPS * HEADS
    e = np.arange(1, n + 1, dtype=np.float32)
    slopes = np.exp2(-8.0 * e / n).astype(np.float32).reshape(N_GROUPS, HEADS)[g]
    t = np.arange(t_new)[:, None]
    idx = np.concatenate([np.arange(buf), buf + np.arange(LANES)])[None, :]
    dist = buf + t - idx
    valid = (dist >= 0) & (dist % dil == 0) & (dist <= A_STEPS * dil) & (idx < buf + t_new)
    bias = -slopes[:, None, None] * dist.astype(np.float32)[None]
    bias = np.where(valid[None], bias, NEG).astype(np.float32).reshape(HEADS * t_new, buf + LANES)
    return jnp.asarray(bias[:, :buf]), jnp.asarray(bias[:, buf:])


def _attn_sample(qkvs, caches):
    bsz, t_new, _ = qkvs[0].shape
    cache_v = [jnp.transpose(c, (0, 2, 3, 4, 1)).reshape(bsz, 2 * A_WIDTH, c.shape[1]) for c in caches]
    biases = [_sample_bias(g, t_new, cache_v[g].shape[2]) for g in range(N_GROUPS)]
    bcs = [b[0] for b in biases]
    bns = [b[1] for b in biases]
    return pl.pallas_call(
        functools.partial(_attn_sample_kernel, t_new=t_new),
        grid=(bsz,),
        in_specs=[pl.BlockSpec((None, t_new, 3 * A_WIDTH), lambda b: (b, 0, 0))] * N_GROUPS
                 + [pl.BlockSpec((None, 2 * A_WIDTH, c.shape[2]), lambda b: (b, 0, 0)) for c in cache_v]
                 + [_const_spec(b) for b in bcs] + [_const_spec(b) for b in bns],
        out_specs=pl.BlockSpec((None, t_new, A_WIDTH), lambda b: (b, 0, 0)),
        out_shape=jax.ShapeDtypeStruct((bsz, t_new, A_WIDTH), BF16),
        compiler_params=pltpu.CompilerParams(dimension_semantics=("arbitrary",),
                                             vmem_limit_bytes=VMEM_LIMIT),
        name="attn_sample",
    )(*qkvs, *cache_v, *bcs, *bns)


def _route(logits):
    lane = lax.broadcasted_iota(jnp.int32, logits.shape, 1).astype(F32)
    ninf = -jnp.inf
    far = float(LANES)
    gl = jnp.where(lane < MOE_GROUPS, logits, ninf)
    gmax = jnp.max(gl, axis=-1, keepdims=True)
    g_prob = 1.0 / jnp.sum(jnp.exp(gl - gmax), axis=-1, keepdims=True)
    g_idx = jnp.min(jnp.where(gl == gmax, lane, far), axis=-1, keepdims=True)
    first = ROUTE_OFF + MOE_PER_GROUP * g_idx
    el = jnp.where((lane >= first) & (lane < first + MOE_PER_GROUP), logits, ninf)
    e1 = jnp.max(el, axis=-1, keepdims=True)
    i1 = jnp.min(jnp.where(el == e1, lane, far), axis=-1, keepdims=True)
    el2 = jnp.where(lane == i1, ninf, el)
    e2 = jnp.max(el2, axis=-1, keepdims=True)
    i2 = jnp.min(jnp.where(el2 == e2, lane, far), axis=-1, keepdims=True)
    t = jnp.exp(e2 - e1)
    w1 = 1.0 / (1.0 + t)
    w2 = t / (1.0 + t)
    return i1, i2, w1 * g_prob, w2 * g_prob, lane


def _pack_bf16_pairs(x):
    q = x.shape[1] // 4
    bits = lax.bitcast_convert_type(x.astype(BF16).astype(F32), jnp.int32)
    pack = lambda hi, lo: hi | lax.shift_right_logical(lo, 16)
    return pack(bits[:, :q], bits[:, 2 * q:3 * q]), pack(bits[:, q:2 * q], bits[:, 3 * q:])


def _unpack_bf16_pairs(pa, pb):
    hi = lambda p: lax.bitcast_convert_type(p & jnp.int32(-65536), F32)
    lo = lambda p: lax.bitcast_convert_type(lax.shift_left(p, 16), F32)
    return jnp.concatenate([hi(pa), hi(pb), lo(pa), lo(pb)], axis=1)


def _mix_kernel(*refs, tm, dils, routed):
    n_attn = len(dils)
    x_ref = refs[0]
    if n_attn > 1:
        o_refs = refs[1:1 + n_attn]
        l_refs = refs[1 + n_attn:1 + 2 * n_attn]
        exp_ref = refs[1 + 2 * n_attn]
        pos = 2 + 2 * n_attn
    else:
        o_refs = refs[1:2]
        pos = 2
    (u_ref, vn_ref, ga_ref, gb_ref, sw_ref, sbt_ref, wpa_ref, wpb_ref, wo_ref,
     gt1_ref, sh2_ref, sc2_ref, n2g_ref, wr_ref, br_ref) = refs[pos:pos + 15]
    n_out = 5 if routed else 3
    out_refs = refs[pos + 15:pos + 15 + n_out]
    scr = list(refs[pos + 15 + n_out:])
    x1_ref = out_refs[0]
    ob_scr = scr.pop(0)

    def to_positions(ref, scr, dil):
        if dil == 1:
            return ref[0].astype(F32)
        n_slab = scr.shape[0]
        for r in range(dil):
            rows = ref[r].astype(F32)
            for j in range(n_slab):
                scr[j, pl.ds(r, tm // dil, stride=dil), :] = rows[:, j * LANES:(j + 1) * LANES]
        return jnp.concatenate([scr[j] for j in range(n_slab)], axis=1)

    if n_attn > 1:
        o_scr, l_scr = scr[0], scr[1]
        lses = [to_positions(l_refs[g], l_scr.at[g], dils[g]) for g in range(n_attn)]
        top = functools.reduce(jnp.maximum, lses)
        ws = [jnp.exp(l - top) for l in lses]
        tot = functools.reduce(lambda a, b: a + b, ws)
        o_a = None
        for g in range(n_attn):
            w = ws[g] / tot
            hi = w.astype(BF16)
            lo = (w - hi.astype(F32)).astype(BF16)
            w_exp = jnp.dot(jnp.concatenate([hi, lo], axis=1), exp_ref[...], preferred_element_type=F32)
            term = w_exp * to_positions(o_refs[g], o_scr.at[g], dils[g])
            o_a = term if o_a is None else o_a + term
        o_a = o_a.astype(BF16)
    else:
        o_a = o_refs[0][...]

    r_i = lax.broadcasted_iota(jnp.int32, (B_CHUNK, B_CHUNK), 0)
    c_i = lax.broadcasted_iota(jnp.int32, (B_CHUNK, B_CHUNK), 1)
    tril = r_i >= c_i
    for gi in range(B_GROUPS):
        gcols = slice(gi * LANES, (gi + 1) * LANES)
        w_s = jnp.where(tril, sw_ref[gi], 0.0).astype(BF16)
        b_s = sbt_ref[:, gi:gi + 1]
        for c in range(tm // B_CHUNK):
            rows = slice(c * B_CHUNK, (c + 1) * B_CHUNK)
            mixed = jnp.dot(w_s, vn_ref[rows, gcols], preferred_element_type=F32) + b_s
            ob_scr[rows, gcols] = (u_ref[rows, gcols].astype(F32) * mixed).astype(BF16)

    pa = jnp.dot(o_a, wpa_ref[...], preferred_element_type=F32)
    pb = jnp.dot(ob_scr[...], wpb_ref[...], preferred_element_type=F32)
    mix = (ga_ref[...].astype(F32) * pa + gb_ref[...].astype(F32) * pb).astype(BF16)
    x1 = x_ref[...] + gt1_ref[...] * jnp.dot(mix, wo_ref[...], preferred_element_type=F32)
    x1_ref[...] = x1
    h2 = _rmsnorm_mod(x1, n2g_ref[...], sh2_ref[...], sc2_ref[...])
    logits = jnp.dot(h2, wr_ref[...], preferred_element_type=F32, precision=lax.Precision.HIGHEST) + br_ref[...]
    i1, i2, w1, w2, lane = _route(logits)
    if not routed:
        _, h2_ref, comb_ref = out_refs
        h2_ref[...] = h2.astype(BF16)
        comb_ref[...] = jnp.where(lane == i1, w1, jnp.where(lane == i2, w2, 0.0))
        return

    _, hpa_ref, hpb_ref, route_ref, cnt_ref = out_refs
    carry = scr[-1]

    @pl.when((pl.program_id(0) == 0) & (pl.program_id(1) == 0))
    def _():
        carry[...] = jnp.zeros_like(carry)

    hpa_ref[...], hpb_ref[...] = _pack_bf16_pairs(h2)
    sel = ((lane == i1) | (lane == i2))
    onehot = jnp.where(sel, 1.0, 0.0).astype(BF16)
    r_i = lax.broadcasted_iota(jnp.int32, (tm, tm), 0)
    c_i = lax.broadcasted_iota(jnp.int32, (tm, tm), 1)
    before = jnp.where(r_i > c_i, 1.0, 0.0).astype(BF16)
    rank = jnp.dot(before, onehot, preferred_element_type=F32) + carry[0:1, :]
    r1 = jnp.sum(jnp.where(lane == i1, rank, 0.0), axis=-1, keepdims=True)
    r2 = jnp.sum(jnp.where(lane == i2, rank, 0.0), axis=-1, keepdims=True)
    total = carry[0:1, :] + jnp.sum(onehot.astype(F32), axis=0, keepdims=True)
    carry[...] = jnp.broadcast_to(total, carry.shape)
    cnt_ref[...] = jnp.broadcast_to(total, cnt_ref.shape)
    fields = (i1 - ROUTE_OFF, i2 - ROUTE_OFF, w1, w2, r1, r2)
    route = jnp.zeros((tm, LANES), F32)
    for k, f in enumerate(fields):
        route = jnp.where(lane == float(k), f, route)
    route_ref[...] = route


def _expand_matrix():
    e = np.zeros((LANES, A_WIDTH), np.float32)
    for h in range(HEADS):
        e[h * LSE_REP, h * HEAD_DIM:(h + 1) * HEAD_DIM] = 1.0
    return jnp.asarray(np.concatenate([e, e], axis=0), dtype=BF16)


def _mix(x, attn_outs, attn_lses, u, vn, ga, gb, sgu_w, sgu_bt, w_pa, w_pb, w_o,
         gt1, sh2, sc2, n2g, w_r, b_r, *, tm, routed):
    bsz, seq, _ = x.shape
    n_attn = len(attn_outs)
    tok = lambda width: pl.BlockSpec((None, tm, width), lambda b, s: (b, s, 0))
    ins = [x] + list(attn_outs)
    scratch = [pltpu.VMEM((tm, B_WIDTH), BF16)]
    if n_attn > 1:
        dils = tuple(o.shape[1] for o in attn_outs)
        res = lambda dil, width: pl.BlockSpec((None, dil, tm // dil, width), lambda b, s: (b, 0, s, 0))
        e2 = _expand_matrix()
        ins += list(attn_lses) + [e2]
        specs = ([tok(D_MODEL)] + [res(d, A_WIDTH) for d in dils] + [res(d, LANES) for d in dils]
                 + [_const_spec(e2)])
        scratch += [pltpu.VMEM((n_attn, A_WIDTH // LANES, tm, LANES), F32), pltpu.VMEM((n_attn, 1, tm, LANES), F32)]
    else:
        dils = (1,)
        specs = [tok(D_MODEL), tok(A_WIDTH)]
    ins += [u, vn, ga, gb, sgu_w, sgu_bt, w_pa, w_pb, w_o, gt1, sh2, sc2, n2g, w_r, b_r]
    specs += [tok(B_WIDTH), tok(B_WIDTH), tok(D_MODEL), tok(D_MODEL),
              _const_spec(sgu_w), _const_spec(sgu_bt), _const_spec(w_pa), _const_spec(w_pb), _const_spec(w_o),
              _mod_spec(gt1, tm), _mod_spec(sh2, tm), _mod_spec(sc2, tm), _const_spec(n2g),
              _const_spec(w_r), _const_spec(b_r)]
    if routed:
        quarter = D_MODEL // 4
        out_specs = [tok(D_MODEL), tok(quarter), tok(quarter), tok(LANES),
                     pl.BlockSpec((8, LANES), lambda b, s: (0, 0))]
        out_shape = [jax.ShapeDtypeStruct((bsz, seq, D_MODEL), F32),
                     jax.ShapeDtypeStruct((bsz, seq, quarter), jnp.int32),
                     jax.ShapeDtypeStruct((bsz, seq, quarter), jnp.int32),
                     jax.ShapeDtypeStruct((bsz, seq, LANES), F32),
                     jax.ShapeDtypeStruct((8, LANES), F32)]
        scratch.append(pltpu.VMEM((8, LANES), F32))
    else:
        out_specs = [tok(D_MODEL), tok(D_MODEL), tok(LANES)]
        out_shape = [jax.ShapeDtypeStruct((bsz, seq, D_MODEL), F32),
                     jax.ShapeDtypeStruct((bsz, seq, D_MODEL), BF16),
                     jax.ShapeDtypeStruct((bsz, seq, LANES), F32)]
    return pl.pallas_call(
        functools.partial(_mix_kernel, tm=tm, dils=dils, routed=routed),
        grid=(bsz, seq // tm),
        in_specs=specs,
        out_specs=out_specs,
        out_shape=out_shape,
        scratch_shapes=scratch,
        compiler_params=pltpu.CompilerParams(dimension_semantics=("arbitrary", "arbitrary"),
                                             vmem_limit_bytes=VMEM_LIMIT),
        name="mix",
    )(*ins)


def _moe_kernel(h_ref, comb_ref, wg_ref, wu_ref, wd_ref, x1_ref, gt2_ref, shf_ref, scf_ref, nfg_ref,
                y_ref, acc_ref):
    e = pl.program_id(2)

    @pl.when(e == 0)
    def _():
        acc_ref[...] = jnp.zeros_like(acc_ref)

    h = h_ref[...]
    hg = jnp.dot(h, wg_ref[...], preferred_element_type=F32)
    hu = jnp.dot(h, wu_ref[...], preferred_element_type=F32)
    comb = comb_ref[...]
    lane = lax.broadcasted_iota(jnp.int32, comb.shape, 1)
    cw = jnp.sum(jnp.where(lane == e + ROUTE_OFF, comb, 0.0), axis=-1, keepdims=True)
    act = (hg * _sigmoid(hg) * hu) * cw
    acc_ref[...] += jnp.dot(act.astype(BF16), wd_ref[...], preferred_element_type=F32)

    @pl.when(e == MOE_EXPERTS - 1)
    def _():
        x2 = x1_ref[...] + gt2_ref[...] * acc_ref[...]
        y_ref[...] = _rmsnorm_mod(x2, nfg_ref[...], shf_ref[...], scf_ref[...])


def _moe(h2, comb, w_gate, w_up, w_down, x1, gt2, shf, scf, nfg, *, tm):
    bsz, seq, _ = h2.shape
    tok = lambda width: pl.BlockSpec((None, tm, width), lambda b, s, e: (b, s, 0))
    return pl.pallas_call(
        _moe_kernel,
        grid=(bsz, seq // tm, MOE_EXPERTS),
        in_specs=[tok(D_MODEL), tok(LANES),
                  pl.BlockSpec((None, D_MODEL, MOE_HIDDEN), lambda b, s, e: (e, 0, 0)),
                  pl.BlockSpec((None, D_MODEL, MOE_HIDDEN), lambda b, s, e: (e, 0, 0)),
                  pl.BlockSpec((None, MOE_HIDDEN, D_MODEL), lambda b, s, e: (e, 0, 0)),
                  tok(D_MODEL), _mod_spec(gt2, tm), _mod_spec(shf, tm), _mod_spec(scf, tm), _const_spec(nfg)],
        out_specs=tok(D_MODEL),
        out_shape=jax.ShapeDtypeStruct((bsz, seq, D_MODEL), F32),
        scratch_shapes=[pltpu.VMEM((tm, D_MODEL), F32)],
        compiler_params=pltpu.CompilerParams(dimension_semantics=("arbitrary",) * 3,
                                             vmem_limit_bytes=VMEM_LIMIT),
        name="moe",
    )(h2, comb, w_gate, w_up, w_down, x1, gt2, shf, scf, nfg)


def _sc_mesh():
    return plsc.VectorSubcoreMesh(core_axis_name="c", subcore_axis_name="s")


def _sc_scatter_rows(rows, pos1, pos2, n_out):
    n, width = rows.shape
    steps = n // SC_WINDOW // SC_CORES

    @pl.kernel(out_type=jax.ShapeDtypeStruct((n_out, width), rows.dtype), mesh=_sc_mesh(), scratch_types=[])
    def scatter(rows_hbm, p1_hbm, p2_hbm, out_hbm):
        def body(x_vmem, i1_vmem, i2_vmem):
            pltpu.sync_copy(x_vmem, out_hbm.at[i1_vmem.at[0]])
            pltpu.sync_copy(x_vmem, out_hbm.at[i2_vmem.at[0]])

        pltpu.emit_pipeline(
            body, grid=(SC_CORES, steps),
            in_specs=[pl.BlockSpec((SC_WINDOW, width), lambda c, i: (c * steps + i, 0)),
                      pl.BlockSpec((1, SC_WINDOW), lambda c, i: (0, c * steps + i)),
                      pl.BlockSpec((1, SC_WINDOW), lambda c, i: (0, c * steps + i))],
            out_specs=[],
            core_axis_name=("c", "s"),
            dimension_semantics=(pltpu.PARALLEL, pltpu.PARALLEL),
        )(rows_hbm, p1_hbm, p2_hbm)

    return scatter(rows, pos1, pos2)


def _sc_gather_rows(table, idx):
    m, width = idx.shape[1], table.shape[1]
    steps = m // SC_WINDOW // SC_CORES

    @pl.kernel(out_type=jax.ShapeDtypeStruct((m, width), table.dtype), mesh=_sc_mesh(), scratch_types=[])
    def gather(table_hbm, idx_hbm, out_hbm):
        def body(i_vmem, o_vmem):
            pltpu.sync_copy(table_hbm.at[i_vmem.at[0]], o_vmem)

        pltpu.emit_pipeline(
            body, grid=(SC_CORES, steps),
            in_specs=[pl.BlockSpec((1, SC_WINDOW), lambda c, i: (0, c * steps + i))],
            out_specs=[pl.BlockSpec((SC_WINDOW, width), lambda c, i: (c * steps + i, 0))],
            core_axis_name=("c", "s"),
            dimension_semantics=(pltpu.PARALLEL, pltpu.PARALLEL),
        )(idx_hbm, out_hbm)

    return gather(table, idx)


def _ffn_kernel(te_ref, nt_ref, xa_ref, xb_ref, wg_ref, wu_ref, wd_ref, oa_ref, ob_ref):
    del te_ref

    @pl.when(pl.program_id(0) < nt_ref[0])
    def _():
        x = _unpack_bf16_pairs(xa_ref[...], xb_ref[...]).astype(BF16)
        hg = jnp.dot(x, wg_ref[...], preferred_element_type=F32)
        hu = jnp.dot(x, wu_ref[...], preferred_element_type=F32)
        act = (hg * _sigmoid(hg) * hu).astype(BF16)
        oa_ref[...], ob_ref[...] = _pack_bf16_pairs(jnp.dot(act, wd_ref[...], preferred_element_type=F32))


def _ffn(xa, xb, tile_expert, n_tiles, w_gate, w_up, w_down):
    n_rows, quarter = xa.shape
    rows = pl.BlockSpec((MOE_TILE, quarter), lambda t, te, nt: (jnp.minimum(t, nt[0] - 1), 0))
    return pl.pallas_call(
        _ffn_kernel,
        grid_spec=pltpu.PrefetchScalarGridSpec(
            num_scalar_prefetch=2,
            grid=(n_rows // MOE_TILE,),
            in_specs=[rows, rows,
                      pl.BlockSpec((None, D_MODEL, MOE_HIDDEN), lambda t, te, nt: (te[t], 0, 0)),
                      pl.BlockSpec((None, D_MODEL, MOE_HIDDEN), lambda t, te, nt: (te[t], 0, 0)),
                      pl.BlockSpec((None, MOE_HIDDEN, D_MODEL), lambda t, te, nt: (te[t], 0, 0))],
            out_specs=[rows, rows]),
        out_shape=[jax.ShapeDtypeStruct((n_rows, quarter), jnp.int32)] * 2,
        compiler_params=pltpu.CompilerParams(dimension_semantics=("arbitrary",), vmem_limit_bytes=VMEM_LIMIT),
        name="moe_ffn",
    )(tile_expert, n_tiles, xa, xb, w_gate, w_up, w_down)


def _final_kernel(x1_ref, g1a_ref, g1b_ref, g2a_ref, g2b_ref, route_ref, gt2_ref, shf_ref, scf_ref, nfg_ref, y_ref):
    o1 = _unpack_bf16_pairs(g1a_ref[...], g1b_ref[...])
    o2 = _unpack_bf16_pairs(g2a_ref[...], g2b_ref[...])
    route = route_ref[...]
    moe = route[:, 2:3] * o1 + route[:, 3:4] * o2
    x2 = x1_ref[...] + gt2_ref[...] * moe
    y_ref[...] = _rmsnorm_mod(x2, nfg_ref[...], shf_ref[...], scf_ref[...])


def _final(x1, ga, gb, route, gt2, shf, scf, nfg, *, tm):
    bsz, seq, _ = x1.shape
    n_s = seq // tm
    n_blk = bsz * n_s
    quarter = ga.shape[1]
    tok = lambda width: pl.BlockSpec((None, tm, width), lambda b, s: (b, s, 0))
    first = pl.BlockSpec((tm, quarter), lambda b, s: (b * n_s + s, 0))
    second = pl.BlockSpec((tm, quarter), lambda b, s: (n_blk + b * n_s + s, 0))
    return pl.pallas_call(
        _final_kernel,
        grid=(bsz, n_s),
        in_specs=[tok(D_MODEL), first, first, second, second, tok(LANES),
                  _mod_spec(gt2, tm), _mod_spec(shf, tm), _mod_spec(scf, tm), _const_spec(nfg)],
        out_specs=tok(D_MODEL),
        out_shape=jax.ShapeDtypeStruct((bsz, seq, D_MODEL), F32),
        compiler_params=pltpu.CompilerParams(dimension_semantics=("arbitrary", "arbitrary")),
        name="moe_final",
    )(x1, ga, gb, ga, gb, route, gt2, shf, scf, nfg)


def _moe_routed(x1, hpa, hpb, route, counts, w_gate, w_up, w_down, gt2, shf, scf, nfg, *, tm):
    bsz, seq, quarter = hpa.shape
    n_tok = bsz * seq
    t_max = 2 * n_tok // MOE_TILE + MOE_EXPERTS
    cnt = counts[0, ROUTE_OFF:ROUTE_OFF + MOE_EXPERTS].astype(jnp.int32)
    tiles_e = (cnt + MOE_TILE - 1) // MOE_TILE
    tile_end = jnp.cumsum(tiles_e)
    n_tiles = tile_end[-1:]
    base = (tile_end - tiles_e) * MOE_TILE
    t_ids = jnp.minimum(jnp.arange(t_max, dtype=jnp.int32), n_tiles[0] - 1)
    tile_expert = jnp.sum((t_ids[:, None] >= tile_end[None, :]).astype(jnp.int32), axis=1)
    r = route.reshape(n_tok, LANES)
    e1, e2 = r[:, 0].astype(jnp.int32), r[:, 1].astype(jnp.int32)
    pos1 = (jnp.take(base, e1) + r[:, 4].astype(jnp.int32)).reshape(1, n_tok)
    pos2 = (jnp.take(base, e2) + r[:, 5].astype(jnp.int32)).reshape(1, n_tok)

    n_rows = t_max * MOE_TILE
    xa = _sc_scatter_rows(hpa.reshape(n_tok, quarter), pos1, pos2, n_rows)
    xb = _sc_scatter_rows(hpb.reshape(n_tok, quarter), pos1, pos2, n_rows)
    oa, ob = _ffn(xa, xb, tile_expert, n_tiles, w_gate, w_up, w_down)
    pos = jnp.concatenate([pos1, pos2], axis=1)
    ga = _sc_gather_rows(oa, pos)
    gb = _sc_gather_rows(ob, pos)
    return _final(x1, ga, gb, route, gt2, shf, scf, nfg, tm=tm)


def kernel(x_prompt, x_sample, cache_kv_w128, cache_kv_w512, cache_kv_w2048, c_prompt, c_sample,
           w_ada, b_ada, norm1_g, norm2_g, w_in, sgu_ln_g, sgu_ln_b, sgu_w, sgu_b, w_pa, w_pb, w_o,
           w_route_group, b_route_group, w_route_expert, b_route_expert, w_gate, w_up, w_down,
           normf_g, w_ada_final, b_ada_final):
    depth = w_ada.shape[0]
    assert depth == 1
    l = 0
    bp, seq, _ = x_prompt.shape
    bs, t_new, _ = x_sample.shape
    n_samp = bs * t_new

    c_all = jnp.concatenate([c_prompt, c_sample], axis=0)
    pad_rows = (-c_all.shape[0]) % 8
    c_all = jnp.pad(c_all, ((0, pad_rows), (0, 0)))
    mod = _adaln(c_all, w_ada[l], b_ada[l])
    mod_f = _adaln(c_all, w_ada_final, b_ada_final)

    def split_mods(m, n, lo, hi, per_token):
        parts = jnp.split(m[lo:hi], n, axis=-1)
        if per_token:
            return [jnp.repeat(p, t_new, axis=0).reshape(1, n_samp, D_MODEL) for p in parts]
        return [p.reshape(hi - lo, 1, D_MODEL) for p in parts]

    mods_p = split_mods(mod, 6, 0, bp, False) + split_mods(mod_f, 2, 0, bp, False)
    mods_s = split_mods(mod, 6, bp, bp + bs, True) + split_mods(mod_f, 2, bp, bp + bs, True)

    row = lambda v: v.reshape(1, -1)
    w_in_b = w_in[l].astype(BF16)
    w_pa_b, w_pb_b, w_o_b = w_pa[l].astype(BF16), w_pb[l].astype(BF16), w_o[l].astype(BF16)
    w_gate_b, w_up_b, w_down_b = w_gate[l].astype(BF16), w_up[l].astype(BF16), w_down[l].astype(BF16)
    w_re = jnp.transpose(w_route_expert[l], (1, 0, 2)).reshape(D_MODEL, MOE_EXPERTS)
    w_r = jnp.pad(jnp.concatenate([w_route_group[l], w_re], axis=1),
                  ((0, 0), (0, LANES - MOE_GROUPS - MOE_EXPERTS)))
    b_r = jnp.pad(jnp.concatenate([b_route_group[l], b_route_expert[l].reshape(-1)]),
                  (0, LANES - MOE_GROUPS - MOE_EXPERTS)).reshape(1, LANES)

    def layer(x, mods, attend, sgu_w_eff, sgu_bt_eff, keeps, dils, tm_in, tm, emit_vn_f32, routed):
        sh1, sc1, gt1, sh2, sc2, gt2, shf, scf = mods
        outs = _inproj(x, sh1, sc1, row(norm1_g[l]), w_in_b, row(sgu_ln_g[l]), row(sgu_ln_b[l]),
                       tm=tm_in, keeps=keeps, dils=dils, emit_vn_f32=emit_vn_f32)
        qkvs = outs[:3]
        u, vn, ga, gb, kv0, kv1, kv2 = outs[3:10]
        attn_outs, attn_lses = attend(qkvs)
        mixed = _mix(x, attn_outs, attn_lses, u, vn, ga, gb, sgu_w_eff, sgu_bt_eff,
                     w_pa_b, w_pb_b, w_o_b, gt1, sh2, sc2, row(norm2_g[l]), w_r, b_r, tm=tm, routed=routed)
        if routed:
            x1, hpa, hpb, route, counts = mixed
            y = _moe_routed(x1, hpa, hpb, route, counts, w_gate_b, w_up_b, w_down_b,
                            gt2, shf, scf, row(normf_g), tm=tm)
        else:
            x1, h2, comb = mixed
            y = _moe(h2, comb, w_gate_b, w_up_b, w_down_b, x1, gt2, shf, scf, row(normf_g), tm=tm)
        return y, (kv0, kv1, kv2), outs[10:]

    def attend_p(qkvs):
        res = [_attn_prompt(qkvs[g], g) for g in range(N_GROUPS)]
        return [r[0] for r in res], [r[1] for r in res]

    keeps_p = tuple(min(win, seq) for win, _ in A_GROUPS)
    dils_p = tuple(dil for _, dil in A_GROUPS)
    y_p, kv_p, _ = layer(x_prompt, mods_p, attend_p, sgu_w[l], jnp.transpose(sgu_b[l]), keeps_p, dils_p,
                         256, 256, False, True)

    caches = (cache_kv_w128[l], cache_kv_w512[l], cache_kv_w2048[l])

    def attend_s(qkvs):
        qkvs = [q.reshape(bs, t_new, 3 * A_WIDTH) for q in qkvs]
        return [_attn_sample(qkvs, caches).reshape(1, n_samp, A_WIDTH)], None

    cl = min(t_new, B_CHUNK)
    eye = jnp.eye(B_CHUNK // cl, dtype=F32)
    sgu_w_s = jnp.einsum('ab,gts->gatbs', eye, sgu_w[l][:, :cl, :cl]).reshape(B_GROUPS, B_CHUNK, B_CHUNK)
    sgu_bt_s = jnp.tile(jnp.transpose(sgu_b[l][:, :cl]), (B_CHUNK // cl, 1))
    y_s, kv_s, extra = layer(x_sample.reshape(1, n_samp, D_MODEL), mods_s, attend_s, sgu_w_s, sgu_bt_s,
                             (n_samp,) * N_GROUPS, (1,) * N_GROUPS, n_samp, n_samp, True, False)

    def kv_out(a, b):
        return a.reshape(depth, b, -1, 2, HEADS, HEAD_DIM)

    return (y_p, y_s.reshape(bs, t_new, D_MODEL),
            kv_out(kv_p[0], bp), kv_out(kv_p[1], bp), kv_out(kv_p[2], bp),
            kv_out(kv_s[0], bs), kv_out(kv_s[1], bs), kv_out(kv_s[2], bs),
            extra[0].reshape(depth, bs, t_new, B_WIDTH))
```

```python
import functools

import numpy as np
import jax
import jax.numpy as jnp
from jax import lax
from jax.experimental import pallas as pl
from jax.experimental.pallas import tpu as pltpu
from jax.experimental.pallas import tpu_sc as plsc

F32 = jnp.float32
BF16 = jnp.bfloat16

D_MODEL = 1024
A_GROUPS = ((128, 1), (512, 4), (2048, 16))
N_GROUPS = 3
HEADS = 8
HEAD_DIM = 64
A_WIDTH = HEADS * HEAD_DIM
A_STEPS = 128
A_BLOCK = 128
B_WIDTH = 1024
B_GROUPS = 8
B_CHUNK = 128
MOE_GROUPS = 4
MOE_PER_GROUP = 4
MOE_EXPERTS = 16
MOE_HIDDEN = 512
EPS = 1e-6
N_QKV = 3 * N_GROUPS * A_WIDTH
IN_COLS = N_QKV + 2 * B_WIDTH + 2 * D_MODEL
COL_CHUNK = 512
LANES = 128
LSE_REP = LANES // HEADS
ROUTE_OFF = MOE_GROUPS
NEG = -1e30
VMEM_LIMIT = 56 * 1024 * 1024
ATTN_Q_BLOCKS = 2
MOE_TILE = 256
SC_WINDOW = 128
SC_CORES = 2


def _sigmoid(x):
    return 1.0 / (1.0 + jnp.exp(-x))


def _gelu_tanh(x):
    return x * (0.5 * (1.0 + jnp.tanh(0.7978845608028654 * (x + 0.044715 * (x * x * x)))))


def _rmsnorm_mod(x, gain, shift, scale):
    y = x * lax.rsqrt(jnp.mean(x * x, axis=-1, keepdims=True) + EPS)
    return y * gain * (1.0 + scale) + shift


def _mod_spec(mod, tm):
    if mod.shape[1] == 1:
        return pl.BlockSpec((None, 1, D_MODEL), lambda b, s, *_: (b, 0, 0))
    return pl.BlockSpec((None, tm, D_MODEL), lambda b, s, *_: (b, s, 0))


def _const_spec(arr):
    nd = arr.ndim
    return pl.BlockSpec(arr.shape, lambda *_: (0,) * nd)


def _adaln_kernel(c_ref, w_ref, b_ref, o_ref):
    c = c_ref[...]
    a = (c * _sigmoid(c)).astype(BF16)
    o_ref[...] = jnp.dot(a, w_ref[...].astype(BF16), preferred_element_type=F32) + b_ref[...]


def _adaln(c, w, b, tn=1024):
    rows, ncols = c.shape[0], w.shape[1]
    return pl.pallas_call(
        _adaln_kernel,
        grid=(ncols // tn,),
        in_specs=[pl.BlockSpec((rows, D_MODEL), lambda j: (0, 0)),
                  pl.BlockSpec((D_MODEL, tn), lambda j: (0, j)),
                  pl.BlockSpec((1, tn), lambda j: (0, j))],
        out_specs=pl.BlockSpec((rows, tn), lambda j: (0, j)),
        out_shape=jax.ShapeDtypeStruct((rows, ncols), F32),
        compiler_params=pltpu.CompilerParams(dimension_semantics=("arbitrary",)),
        name="adaln",
    )(c, w, b.reshape(1, ncols))


def _inproj_kernel(x_ref, sh_ref, sc_ref, g_ref, w_ref, lng_ref, lnb_ref,
                   qkv0_ref, qkv1_ref, qkv2_ref, u_ref, vn_ref, ga_ref, gb_ref, kv0_ref, kv1_ref, kv2_ref,
                   *rest, tm, keeps, dils, emit_vn_f32):
    perm_scr = rest[-1]
    h = _rmsnorm_mod(x_ref[...], g_ref[...], sh_ref[...], sc_ref[...]).astype(BF16)

    def proj(c):
        return jnp.dot(h, w_ref[:, c * COL_CHUNK:(c + 1) * COL_CHUNK], preferred_element_type=F32)

    def cols(c):
        return slice(c * COL_CHUNK, (c + 1) * COL_CHUNK)

    qkv_refs = (qkv0_ref, qkv1_ref, qkv2_ref)

    def put_qkv(g, t, z):
        dil = dils[g]
        if dil == 1:
            qkv_refs[g][0, :, cols(t)] = z.astype(BF16)
            return
        slot = perm_scr.at[t % 2]
        n_slab = COL_CHUNK // LANES
        for j in range(n_slab):
            slot[j] = z[:, j * LANES:(j + 1) * LANES]
        for r in range(dil):
            rows = jnp.concatenate([slot[j, pl.ds(r, tm // dil, stride=dil), :] for j in range(n_slab)], axis=1)
            qkv_refs[g][r, :, cols(t)] = rows.astype(BF16)

    for g in range(N_GROUPS):
        put_qkv(g, 0, proj(g) * (HEAD_DIM ** -0.5))

    kv_refs = (kv0_ref, kv1_ref, kv2_ref)
    for g in range(N_GROUPS):
        keep = keeps[g]
        for t in (1, 2):
            z = proj(t * N_GROUPS + g)
            put_qkv(g, t, z)
            kv_refs[g][:, cols(t - 1)] = z if keep >= tm else z[tm - keep:, :]

    base = N_QKV // COL_CHUNK
    for c in range(2):
        u_ref[:, cols(c)] = _gelu_tanh(proj(base + c)).astype(BF16)

    vs = [_gelu_tanh(proj(base + 2 + c)) for c in range(2)]
    mu = (jnp.sum(vs[0], axis=-1, keepdims=True) + jnp.sum(vs[1], axis=-1, keepdims=True)) * (1.0 / B_WIDTH)
    ds = [v - mu for v in vs]
    var = (jnp.sum(ds[0] * ds[0], axis=-1, keepdims=True)
           + jnp.sum(ds[1] * ds[1], axis=-1, keepdims=True)) * (1.0 / B_WIDTH)
    inv = lax.rsqrt(var + EPS)
    for c in range(2):
        vn = ds[c] * inv * lng_ref[:, cols(c)] + lnb_ref[:, cols(c)]
        vn_ref[:, cols(c)] = vn.astype(BF16)
        if emit_vn_f32:
            rest[0][:, cols(c)] = vn

    for c in range(2):
        ga_ref[:, cols(c)] = _sigmoid(proj(base + 4 + c)).astype(BF16)
        gb_ref[:, cols(c)] = _sigmoid(proj(base + 6 + c)).astype(BF16)


def _inproj(x, sh, sc, gain, w_in, ln_g, ln_b, *, tm, keeps, dils, emit_vn_f32):
    bsz, seq, _ = x.shape
    n_s = seq // tm
    tok = lambda width: pl.BlockSpec((None, tm, width), lambda b, s: (b, s, 0))

    def kv_spec(keep):
        if keep >= tm:
            first = n_s - keep // tm
            return pl.BlockSpec((None, tm, 2 * A_WIDTH), lambda b, s: (b, jnp.maximum(s - first, 0), 0))
        return pl.BlockSpec((None, keep, 2 * A_WIDTH), lambda b, s: (b, 0, 0))

    out_specs, out_shape = [], []
    for dil in dils:
        out_specs.append(pl.BlockSpec((None, dil, tm // dil, 3 * A_WIDTH), lambda b, s: (b, 0, s, 0)))
        out_shape.append(jax.ShapeDtypeStruct((bsz, dil, seq // dil, 3 * A_WIDTH), BF16))
    out_specs += [tok(B_WIDTH), tok(B_WIDTH), tok(D_MODEL), tok(D_MODEL)]
    out_shape += [jax.ShapeDtypeStruct((bsz, seq, D_MODEL), BF16)] * 4
    for keep in keeps:
        out_specs.append(kv_spec(keep))
        out_shape.append(jax.ShapeDtypeStruct((bsz, keep, 2 * A_WIDTH), F32))
    if emit_vn_f32:
        out_specs.append(tok(B_WIDTH))
        out_shape.append(jax.ShapeDtypeStruct((bsz, seq, B_WIDTH), F32))

    return pl.pallas_call(
        functools.partial(_inproj_kernel, tm=tm, keeps=keeps, dils=dils, emit_vn_f32=emit_vn_f32),
        grid=(bsz, n_s),
        in_specs=[tok(D_MODEL), _mod_spec(sh, tm), _mod_spec(sc, tm), _const_spec(gain),
                  pl.BlockSpec(w_in.shape, lambda b, s: (0, 0), pipeline_mode=pl.Buffered(1)),
                  _const_spec(ln_g), _const_spec(ln_b)],
        out_specs=out_specs,
        out_shape=out_shape,
        scratch_shapes=[pltpu.VMEM((2, COL_CHUNK // LANES, tm, LANES), F32)],
        compiler_params=pltpu.CompilerParams(dimension_semantics=("arbitrary", "arbitrary"),
                                             vmem_limit_bytes=VMEM_LIMIT),
        name="inproj",
    )(x, sh, sc, gain, w_in, ln_g, ln_b)


def _attn_kernel(q_ref, kp_ref, kc_ref, vp_ref, vc_ref, bias0_ref, bias_ref, o_ref, lse_ref, *, q_blocks):
    k_all = jnp.concatenate([kp_ref[...], kc_ref[...]], axis=0)
    v_all = jnp.concatenate([vp_ref[...], vc_ref[...]], axis=0)
    lane = lax.broadcasted_iota(jnp.int32, (A_BLOCK, LANES), 1)
    low = lane < HEAD_DIM
    zero = jnp.zeros((), BF16)
    for i in range(q_blocks):
        rows = slice(i * A_BLOCK, (i + 1) * A_BLOCK)
        q = q_ref[rows, :]
        k = k_all[i * A_BLOCK:(i + 2) * A_BLOCK]
        v = v_all[i * A_BLOCK:(i + 2) * A_BLOCK]
        b_ref = bias0_ref if i == 0 else bias_ref
        lse_tile = jnp.zeros((A_BLOCK, LANES), F32)
        for j in range(HEADS // 2):
            pair = slice(j * LANES, (j + 1) * LANES)
            qp, kpair, vpair = q[:, pair], k[:, pair], v[:, pair]
            outs = []
            for e in range(2):
                h = 2 * j + e
                qm = jnp.where(low if e == 0 else jnp.logical_not(low), qp, zero)
                s = lax.dot_general(qm, kpair, (((1,), (1,)), ((), ())), preferred_element_type=F32)
                s = s + b_ref[h]
                m = jnp.max(s, axis=-1, keepdims=True)
                p = jnp.exp(s - m)
                den = jnp.sum(p, axis=-1, keepdims=True)
                o = jnp.dot(p.astype(BF16), vpair, preferred_element_type=F32) / den
                lse = m + jnp.log(den)
                lse_tile = jnp.where((lane >= h * LSE_REP) & (lane < (h + 1) * LSE_REP), lse, lse_tile)
                outs.append(o)
            o_ref[rows, pair] = jnp.where(low, outs[0], outs[1]).astype(BF16)
        lse_ref[rows, :] = lse_tile


def _attn_bias(g):
    _, dil = A_GROUPS[g]
    n = N_GROUPS * HEADS
    e = np.arange(1, n + 1, dtype=np.float32)
    slopes = np.exp2(-8.0 * e / n).astype(np.float32).reshape(N_GROUPS, HEADS)[g]
    qi = np.arange(A_BLOCK)[:, None]
    ki = np.arange(2 * A_BLOCK)[None, :]
    delta = qi + A_BLOCK - ki
    band = (delta >= 0) & (delta <= A_STEPS)
    dist = (delta * dil).astype(np.float32)
    bias = -slopes[:, None, None] * dist[None]
    out = np.empty((2, HEADS, A_BLOCK, 2 * A_BLOCK), np.float32)
    out[1] = np.where(band[None], bias, NEG)
    out[0] = np.where((band & (ki >= A_BLOCK))[None], bias, NEG)
    return jnp.asarray(out)


def _attn_prompt(qkv, g):
    bsz, dil, steps, _ = qkv.shape
    qb = ATTN_Q_BLOCKS
    rows = qb * A_BLOCK
    n_steps = steps // rows
    bias = _attn_bias(g)

    def blk(t, prev):
        if prev:
            return pl.BlockSpec((None, None, A_BLOCK, A_WIDTH),
                                lambda b, r, n: (b, r, jnp.maximum(n * qb - 1, 0), t))
        return pl.BlockSpec((None, None, rows, A_WIDTH), lambda b, r, n: (b, r, n, t))

    bias_blk = lambda pick: pl.BlockSpec((None, HEADS, A_BLOCK, 2 * A_BLOCK), lambda b, r, n: (pick(n), 0, 0, 0))
    return pl.pallas_call(
        functools.partial(_attn_kernel, q_blocks=qb),
        grid=(bsz, dil, n_steps),
        in_specs=[blk(0, False), blk(1, True), blk(1, False), blk(2, True), blk(2, False),
                  bias_blk(lambda n: jnp.minimum(n, 1)), bias_blk(lambda n: 1)],
        out_specs=[pl.BlockSpec((None, None, rows, A_WIDTH), lambda b, r, n: (b, r, n, 0)),
                   pl.BlockSpec((None, None, rows, LANES), lambda b, r, n: (b, r, n, 0))],
        out_shape=[jax.ShapeDtypeStruct((bsz, dil, steps, A_WIDTH), BF16),
                   jax.ShapeDtypeStruct((bsz, dil, steps, LANES), F32)],
        compiler_params=pltpu.CompilerParams(dimension_semantics=("arbitrary",) * 3),
        name=f"attn_prompt_g{g}",
    )(qkv, qkv, qkv, qkv, qkv, bias, bias)


def _attn_sample_kernel(q0_ref, q1_ref, q2_ref, c0_ref, c1_ref, c2_ref, bc0_ref, bc1_ref, bc2_ref,
                        bn0_ref, bn1_ref, bn2_ref, o_ref, *, t_new):
    n_rows = HEADS * t_new
    row = lax.broadcasted_iota(jnp.int32, (n_rows, A_WIDTH), 0)
    lane = lax.broadcasted_iota(jnp.int32, (n_rows, A_WIDTH), 1)
    head_mask = (row // t_new) == (lane // HEAD_DIM)
    pad = jnp.zeros((LANES - t_new, A_WIDTH), F32)
    outs, lses = [], []
    for qkv_ref, c_ref, bc_ref, bn_ref in ((q0_ref, c0_ref, bc0_ref, bn0_ref), (q1_ref, c1_ref, bc1_ref, bn1_ref),
                                           (q2_ref, c2_ref, bc2_ref, bn2_ref)):
        q = qkv_ref[:, :A_WIDTH].astype(F32)
        k_new = qkv_ref[:, A_WIDTH:2 * A_WIDTH].astype(F32)
        v_new = qkv_ref[:, 2 * A_WIDTH:].astype(F32)
        k_new = jnp.concatenate([k_new, pad], axis=0).astype(BF16)
        v_new = jnp.concatenate([v_new, pad], axis=0).astype(BF16)
        q_rows = jnp.where(head_mask, jnp.concatenate([q] * HEADS, axis=0), 0.0).astype(BF16)
        k_buf_t = c_ref[:A_WIDTH, :].astype(BF16)
        v_buf_t = c_ref[A_WIDTH:, :].astype(BF16)
        nt = (((1,), (1,)), ((), ()))
        s_buf = jnp.dot(q_rows, k_buf_t, preferred_element_type=F32) + bc_ref[...]
        s_new = lax.dot_general(q_rows, k_new, nt, preferred_element_type=F32) + bn_ref[...]
        m = jnp.maximum(jnp.max(s_buf, axis=-1, keepdims=True), jnp.max(s_new, axis=-1, keepdims=True))
        p_buf = jnp.exp(s_buf - m)
        p_new = jnp.exp(s_new - m)
        den = jnp.sum(p_buf, axis=-1, keepdims=True) + jnp.sum(p_new, axis=-1, keepdims=True)
        o = (lax.dot_general(p_buf.astype(BF16), v_buf_t, nt, preferred_element_type=F32)
             + jnp.dot(p_new.astype(BF16), v_new, preferred_element_type=F32)) / den
        outs.append(o)
        lses.append(m + jnp.log(den))
    top = jnp.maximum(jnp.maximum(lses[0], lses[1]), lses[2])
    ws = [jnp.exp(l - top) for l in lses]
    tot = ws[0] + ws[1] + ws[2]
    acc = (ws[0] / tot) * outs[0] + (ws[1] / tot) * outs[1] + (ws[2] / tot) * outs[2]
    acc = jnp.where(head_mask, acc, 0.0).reshape(HEADS, t_new, A_WIDTH)
    o_ref[...] = jnp.sum(acc, axis=0).astype(BF16)


def _sample_bias(g, t_new, buf):
    _, dil = A_GROUPS[g]
    n = N_GROUPS * HEADS
    e = np.arange(1, n + 1, dtype=np.float32)
    slopes = np.exp2(-8.0 * e / n).astype(np.float32).reshape(N_GROUPS, HEADS)[g]
    t = np.arange(t_new)[:, None]
    idx = np.concatenate([np.arange(buf), buf + np.arange(LANES)])[None, :]
    dist = buf + t - idx
    valid = (dist >= 0) & (dist % dil == 0) & (dist <= A_STEPS * dil) & (idx < buf + t_new)
    bias = -slopes[:, None, None] * dist.astype(np.float32)[None]
    bias = np.where(valid[None], bias, NEG).astype(np.float32).reshape(HEADS * t_new, buf + LANES)
    return jnp.asarray(bias[:, :buf]), jnp.asarray(bias[:, buf:])


def _attn_sample(qkvs, caches):
    bsz, t_new, _ = qkvs[0].shape
    cache_v = [jnp.transpose(c, (0, 2, 3, 4, 1)).reshape(bsz, 2 * A_WIDTH, c.shape[1]) for c in caches]
    biases = [_sample_bias(g, t_new, cache_v[g].shape[2]) for g in range(N_GROUPS)]
    bcs = [b[0] for b in biases]
    bns = [b[1] for b in biases]
    return pl.pallas_call(
        functools.partial(_attn_sample_kernel, t_new=t_new),
        grid=(bsz,),
        in_specs=[pl.BlockSpec((None, t_new, 3 * A_WIDTH), lambda b: (b, 0, 0))] * N_GROUPS
                 + [pl.BlockSpec((None, 2 * A_WIDTH, c.shape[2]), lambda b: (b, 0, 0)) for c in cache_v]
                 + [_const_spec(b) for b in bcs] + [_const_spec(b) for b in bns],
        out_specs=pl.BlockSpec((None, t_new, A_WIDTH), lambda b: (b, 0, 0)),
        out_shape=jax.ShapeDtypeStruct((bsz, t_new, A_WIDTH), BF16),
        compiler_params=pltpu.CompilerParams(dimension_semantics=("arbitrary",),
                                             vmem_limit_bytes=VMEM_LIMIT),
        name="attn_sample",
    )(*qkvs, *cache_v, *bcs, *bns)


def _first_max4(v):
    top = jnp.maximum(jnp.maximum(v[0], v[1]), jnp.maximum(v[2], v[3]))
    idx = jnp.where(v[0] == top, 0.0, jnp.where(v[1] == top, 1.0, jnp.where(v[2] == top, 2.0, 3.0)))
    return top, idx


def _route(lt):
    row = lambda k: lt[k:k + 1, :]
    g = [row(k) for k in range(MOE_GROUPS)]
    gmax, g_idx = _first_max4(g)
    g_prob = 1.0 / (jnp.exp(g[0] - gmax) + jnp.exp(g[1] - gmax) + jnp.exp(g[2] - gmax) + jnp.exp(g[3] - gmax))
    cand = []
    for k in range(MOE_PER_GROUP):
        c = row(ROUTE_OFF + MOE_PER_GROUP * (MOE_GROUPS - 1) + k)
        for gi in range(MOE_GROUPS - 2, -1, -1):
            c = jnp.where(g_idx == float(gi), row(ROUTE_OFF + MOE_PER_GROUP * gi + k), c)
        cand.append(c)
    e1, i1 = _first_max4(cand)
    rest = [jnp.where(i1 == float(k), -jnp.inf, cand[k]) for k in range(MOE_PER_GROUP)]
    e2, i2 = _first_max4(rest)
    t = jnp.exp(e2 - e1)
    w1 = 1.0 / (1.0 + t)
    w2 = t / (1.0 + t)
    return MOE_PER_GROUP * g_idx + i1, MOE_PER_GROUP * g_idx + i2, w1 * g_prob, w2 * g_prob


def _pack_bf16_pairs(x):
    q = x.shape[1] // 4
    bits = lax.bitcast_convert_type(x.astype(BF16).astype(F32), jnp.int32)
    pack = lambda hi, lo: hi | lax.shift_right_logical(lo, 16)
    return pack(bits[:, :q], bits[:, 2 * q:3 * q]), pack(bits[:, q:2 * q], bits[:, 3 * q:])


def _unpack_bf16_pairs(pa, pb):
    hi = lambda p: lax.bitcast_convert_type(p & jnp.int32(-65536), F32)
    lo = lambda p: lax.bitcast_convert_type(lax.shift_left(p, 16), F32)
    return jnp.concatenate([hi(pa), hi(pb), lo(pa), lo(pb)], axis=1)


def _mix_kernel(*refs, tm, dils, routed):
    n_attn = len(dils)
    x_ref = refs[0]
    if n_attn > 1:
        o_refs = refs[1:1 + n_attn]
        l_refs = refs[1 + n_attn:1 + 2 * n_attn]
        exp_ref = refs[1 + 2 * n_attn]
        pos = 2 + 2 * n_attn
    else:
        o_refs = refs[1:2]
        pos = 2
    (u_ref, vn_ref, ga_ref, gb_ref, sw_ref, sbt_ref, wpa_ref, wpb_ref, wo_ref,
     gt1_ref, sh2_ref, sc2_ref, n2g_ref, wrh_ref, wrl_ref, br_ref) = refs[pos:pos + 16]
    n_out = 6 if routed else 3
    out_refs = refs[pos + 16:pos + 16 + n_out]
    scr = list(refs[pos + 16 + n_out:])
    x1_ref = out_refs[0]
    ob_scr = scr.pop(0)

    def to_positions(ref, scr, dil):
        if dil == 1:
            return ref[0].astype(F32)
        n_slab = scr.shape[0]
        for r in range(dil):
            rows = ref[r].astype(F32)
            for j in range(n_slab):
                scr[j, pl.ds(r, tm // dil, stride=dil), :] = rows[:, j * LANES:(j + 1) * LANES]
        return jnp.concatenate([scr[j] for j in range(n_slab)], axis=1)

    if n_attn > 1:
        o_scr, l_scr = scr[0], scr[1]
        lses = [to_positions(l_refs[g], l_scr.at[g], dils[g]) for g in range(n_attn)]
        top = functools.reduce(jnp.maximum, lses)
        ws = [jnp.exp(l - top) for l in lses]
        tot = functools.reduce(lambda a, b: a + b, ws)
        o_a = None
        for g in range(n_attn):
            w = ws[g] / tot
            hi = w.astype(BF16)
            lo = (w - hi.astype(F32)).astype(BF16)
            w_exp = jnp.dot(jnp.concatenate([hi, lo], axis=1), exp_ref[...], preferred_element_type=F32)
            term = w_exp * to_positions(o_refs[g], o_scr.at[g], dils[g])
            o_a = term if o_a is None else o_a + term
        o_a = o_a.astype(BF16)
    else:
        o_a = o_refs[0][...]

    r_i = lax.broadcasted_iota(jnp.int32, (B_CHUNK, B_CHUNK), 0)
    c_i = lax.broadcasted_iota(jnp.int32, (B_CHUNK, B_CHUNK), 1)
    tril = r_i >= c_i
    for gi in range(B_GROUPS):
        gcols = slice(gi * LANES, (gi + 1) * LANES)
        w_s = jnp.where(tril, sw_ref[gi], 0.0).astype(BF16)
        b_s = sbt_ref[:, gi:gi + 1]
        n_chunk = tm // B_CHUNK
        vn_wide = jnp.concatenate([vn_ref[c * B_CHUNK:(c + 1) * B_CHUNK, gcols] for c in range(n_chunk)], axis=1)
        mixed = jnp.dot(w_s, vn_wide, preferred_element_type=F32) + b_s
        for c in range(n_chunk):
            rows = slice(c * B_CHUNK, (c + 1) * B_CHUNK)
            ob_scr[rows, gcols] = (u_ref[rows, gcols].astype(F32) * mixed[:, c * LANES:(c + 1) * LANES]).astype(BF16)

    pa = jnp.dot(o_a, wpa_ref[...], preferred_element_type=F32)
    pb = jnp.dot(ob_scr[...], wpb_ref[...], preferred_element_type=F32)
    mix = (ga_ref[...].astype(F32) * pa + gb_ref[...].astype(F32) * pb).astype(BF16)
    x1 = x_ref[...] + gt1_ref[...] * jnp.dot(mix, wo_ref[...], preferred_element_type=F32)
    x1_ref[...] = x1
    h2 = _rmsnorm_mod(x1, n2g_ref[...], sh2_ref[...], sc2_ref[...])
    h_hi = h2.astype(BF16)
    h_lo = (h2 - h_hi.astype(F32)).astype(BF16)
    nt = (((1,), (1,)), ((), ()))
    lt = (lax.dot_general(wrh_ref[...], h_hi, nt, preferred_element_type=F32)
          + lax.dot_general(wrh_ref[...], h_lo, nt, preferred_element_type=F32)
          + lax.dot_general(wrl_ref[...], h_hi, nt, preferred_element_type=F32)) + br_ref[...]
    ex1, ex2, w1, w2 = _route(lt)
    if not routed:
        _, h2_ref, comb_ref = out_refs
        h2_ref[...] = h_hi
        e_lane = lax.broadcasted_iota(jnp.int32, (LANES, tm), 0).astype(F32) - float(ROUTE_OFF)
        comb_t = jnp.where(e_lane == ex1, w1, jnp.where(e_lane == ex2, w2, 0.0))
        comb_ref[...] = jnp.transpose(comb_t)
        return

    _, hpa_ref, hpb_ref, route_ref, wcol_ref, cnt_ref = out_refs
    carry = scr[-1]
    hpa_ref[...], hpb_ref[...] = _pack_bf16_pairs(h2)
    e_iota = lax.broadcasted_iota(jnp.int32, (MOE_EXPERTS, tm), 0).astype(F32)
    hot1, hot2 = e_iota == ex1, e_iota == ex2
    onehot = jnp.where(hot1 | hot2, 1.0, 0.0)
    r_i = lax.broadcasted_iota(jnp.int32, (tm, tm), 0)
    c_i = lax.broadcasted_iota(jnp.int32, (tm, tm), 1)
    earlier = jnp.where(r_i < c_i, 1.0, 0.0).astype(BF16)
    first_step = (pl.program_id(0) == 0) & (pl.program_id(1) == 0)
    prev = jnp.where(first_step, 0.0, carry[:, 0:1])
    rank = jnp.dot(onehot.astype(BF16), earlier, preferred_element_type=F32) + prev
    r1 = jnp.sum(jnp.where(hot1, rank, 0.0), axis=0, keepdims=True)
    r2 = jnp.sum(jnp.where(hot2, rank, 0.0), axis=0, keepdims=True)
    total = prev + jnp.sum(onehot, axis=1, keepdims=True)
    carry[...] = jnp.broadcast_to(total, carry.shape)
    cnt_ref[...] = jnp.broadcast_to(total, cnt_ref.shape)
    fields = jnp.concatenate([ex1, ex2, w1, w2, r1, r2, jnp.zeros((2, tm), F32)], axis=0)
    route_ref[...] = fields
    wcol_ref[...] = jnp.transpose(jnp.concatenate([fields, jnp.zeros((LANES - 8, tm), F32)], axis=0))


def _expand_matrix():
    e = np.zeros((LANES, A_WIDTH), np.float32)
    for h in range(HEADS):
        e[h * LSE_REP, h * HEAD_DIM:(h + 1) * HEAD_DIM] = 1.0
    return jnp.asarray(np.concatenate([e, e], axis=0), dtype=BF16)


def _mix(x, attn_outs, attn_lses, u, vn, ga, gb, sgu_w, sgu_bt, w_pa, w_pb, w_o,
         gt1, sh2, sc2, n2g, w_r_hi, w_r_lo, b_r, *, tm, routed):
    bsz, seq, _ = x.shape
    n_attn = len(attn_outs)
    tok = lambda width: pl.BlockSpec((None, tm, width), lambda b, s: (b, s, 0))
    ins = [x] + list(attn_outs)
    scratch = [pltpu.VMEM((tm, B_WIDTH), BF16)]
    if n_attn > 1:
        dils = tuple(o.shape[1] for o in attn_outs)
        res = lambda dil, width: pl.BlockSpec((None, dil, tm // dil, width), lambda b, s: (b, 0, s, 0))
        e2 = _expand_matrix()
        ins += list(attn_lses) + [e2]
        specs = ([tok(D_MODEL)] + [res(d, A_WIDTH) for d in dils] + [res(d, LANES) for d in dils]
                 + [_const_spec(e2)])
        scratch += [pltpu.VMEM((n_attn, A_WIDTH // LANES, tm, LANES), F32), pltpu.VMEM((n_attn, 1, tm, LANES), F32)]
    else:
        dils = (1,)
        specs = [tok(D_MODEL), tok(A_WIDTH)]
    ins += [u, vn, ga, gb, sgu_w, sgu_bt, w_pa, w_pb, w_o, gt1, sh2, sc2, n2g, w_r_hi, w_r_lo, b_r]
    specs += [tok(B_WIDTH), tok(B_WIDTH), tok(D_MODEL), tok(D_MODEL),
              _const_spec(sgu_w), _const_spec(sgu_bt), _const_spec(w_pa), _const_spec(w_pb), _const_spec(w_o),
              _mod_spec(gt1, tm), _mod_spec(sh2, tm), _mod_spec(sc2, tm), _const_spec(n2g),
              _const_spec(w_r_hi), _const_spec(w_r_lo), _const_spec(b_r)]
    if routed:
        quarter = D_MODEL // 4
        out_specs = [tok(D_MODEL), tok(quarter), tok(quarter),
                     pl.BlockSpec((None, 8, tm), lambda b, s: (b, 0, s)), tok(LANES),
                     pl.BlockSpec((MOE_EXPERTS, LANES), lambda b, s: (0, 0))]
        out_shape = [jax.ShapeDtypeStruct((bsz, seq, D_MODEL), F32),
                     jax.ShapeDtypeStruct((bsz, seq, quarter), jnp.int32),
                     jax.ShapeDtypeStruct((bsz, seq, quarter), jnp.int32),
                     jax.ShapeDtypeStruct((bsz, 8, seq), F32),
                     jax.ShapeDtypeStruct((bsz, seq, LANES), F32),
                     jax.ShapeDtypeStruct((MOE_EXPERTS, LANES), F32)]
        scratch.append(pltpu.VMEM((MOE_EXPERTS, LANES), F32))
    else:
        out_specs = [tok(D_MODEL), tok(D_MODEL), tok(LANES)]
        out_shape = [jax.ShapeDtypeStruct((bsz, seq, D_MODEL), F32),
                     jax.ShapeDtypeStruct((bsz, seq, D_MODEL), BF16),
                     jax.ShapeDtypeStruct((bsz, seq, LANES), F32)]
    return pl.pallas_call(
        functools.partial(_mix_kernel, tm=tm, dils=dils, routed=routed),
        grid=(bsz, seq // tm),
        in_specs=specs,
        out_specs=out_specs,
        out_shape=out_shape,
        scratch_shapes=scratch,
        compiler_params=pltpu.CompilerParams(dimension_semantics=("arbitrary", "arbitrary"),
                                             vmem_limit_bytes=VMEM_LIMIT),
        name="mix",
    )(*ins)


def _moe_kernel(h_ref, comb_ref, wg_ref, wu_ref, wd_ref, x1_ref, gt2_ref, shf_ref, scf_ref, nfg_ref,
                y_ref, acc_ref):
    e = pl.program_id(2)

    @pl.when(e == 0)
    def _():
        acc_ref[...] = jnp.zeros_like(acc_ref)

    h = h_ref[...]
    hg = jnp.dot(h, wg_ref[...].astype(BF16), preferred_element_type=F32)
    hu = jnp.dot(h, wu_ref[...].astype(BF16), preferred_element_type=F32)
    comb = comb_ref[...]
    lane = lax.broadcasted_iota(jnp.int32, comb.shape, 1)
    cw = jnp.sum(jnp.where(lane == e + ROUTE_OFF, comb, 0.0), axis=-1, keepdims=True)
    act = (hg * _sigmoid(hg) * hu) * cw
    acc_ref[...] += jnp.dot(act.astype(BF16), wd_ref[...].astype(BF16), preferred_element_type=F32)

    @pl.when(e == MOE_EXPERTS - 1)
    def _():
        x2 = x1_ref[...] + gt2_ref[...] * acc_ref[...]
        y_ref[...] = _rmsnorm_mod(x2, nfg_ref[...], shf_ref[...], scf_ref[...])


def _moe(h2, comb, w_gate, w_up, w_down, x1, gt2, shf, scf, nfg, *, tm):
    bsz, seq, _ = h2.shape
    tok = lambda width: pl.BlockSpec((None, tm, width), lambda b, s, e: (b, s, 0))
    return pl.pallas_call(
        _moe_kernel,
        grid=(bsz, seq // tm, MOE_EXPERTS),
        in_specs=[tok(D_MODEL), tok(LANES),
                  pl.BlockSpec((None, D_MODEL, MOE_HIDDEN), lambda b, s, e: (e, 0, 0)),
                  pl.BlockSpec((None, D_MODEL, MOE_HIDDEN), lambda b, s, e: (e, 0, 0)),
                  pl.BlockSpec((None, MOE_HIDDEN, D_MODEL), lambda b, s, e: (e, 0, 0)),
                  tok(D_MODEL), _mod_spec(gt2, tm), _mod_spec(shf, tm), _mod_spec(scf, tm), _const_spec(nfg)],
        out_specs=tok(D_MODEL),
        out_shape=jax.ShapeDtypeStruct((bsz, seq, D_MODEL), F32),
        scratch_shapes=[pltpu.VMEM((tm, D_MODEL), F32)],
        compiler_params=pltpu.CompilerParams(dimension_semantics=("arbitrary",) * 3,
                                             vmem_limit_bytes=VMEM_LIMIT),
        name="moe",
    )(h2, comb, w_gate, w_up, w_down, x1, gt2, shf, scf, nfg)


def _sc_mesh():
    return plsc.VectorSubcoreMesh(core_axis_name="c", subcore_axis_name="s")


def _sc_scatter_rows(rows, pos1, pos2, n_out):
    n, width = rows.shape
    steps = n // SC_WINDOW // SC_CORES

    @pl.kernel(out_type=jax.ShapeDtypeStruct((n_out, width), rows.dtype), mesh=_sc_mesh(), scratch_types=[])
    def scatter(rows_hbm, p1_hbm, p2_hbm, out_hbm):
        def body(x_vmem, i1_vmem, i2_vmem):
            pltpu.sync_copy(x_vmem, out_hbm.at[i1_vmem.at[0]])
            pltpu.sync_copy(x_vmem, out_hbm.at[i2_vmem.at[0]])

        pltpu.emit_pipeline(
            body, grid=(SC_CORES, steps),
            in_specs=[pl.BlockSpec((SC_WINDOW, width), lambda c, i: (c * steps + i, 0)),
                      pl.BlockSpec((1, SC_WINDOW), lambda c, i: (0, c * steps + i)),
                      pl.BlockSpec((1, SC_WINDOW), lambda c, i: (0, c * steps + i))],
            out_specs=[],
            core_axis_name=("c", "s"),
            dimension_semantics=(pltpu.PARALLEL, pltpu.PARALLEL),
        )(rows_hbm, p1_hbm, p2_hbm)

    return scatter(rows, pos1, pos2)


def _sc_gather_rows(table, idx):
    m, width = idx.shape[1], table.shape[1]
    steps = m // SC_WINDOW // SC_CORES

    @pl.kernel(out_type=jax.ShapeDtypeStruct((m, width), table.dtype), mesh=_sc_mesh(), scratch_types=[])
    def gather(table_hbm, idx_hbm, out_hbm):
        def body(i_vmem, o_vmem):
            pltpu.sync_copy(table_hbm.at[i_vmem.at[0]], o_vmem)

        pltpu.emit_pipeline(
            body, grid=(SC_CORES, steps),
            in_specs=[pl.BlockSpec((1, SC_WINDOW), lambda c, i: (0, c * steps + i))],
            out_specs=[pl.BlockSpec((SC_WINDOW, width), lambda c, i: (c * steps + i, 0))],
            core_axis_name=("c", "s"),
            dimension_semantics=(pltpu.PARALLEL, pltpu.PARALLEL),
        )(idx_hbm, out_hbm)

    return gather(table, idx)


def _ffn_kernel(te_ref, nt_ref, xa_ref, xb_ref, wg_ref, wu_ref, wd_ref, oa_ref, ob_ref, wg_s, wu_s, wd_s):
    t = pl.program_id(0)
    live = t < nt_ref[0]

    @pl.when(live & ((t == 0) | (te_ref[t] != te_ref[jnp.maximum(t - 1, 0)])))
    def _():
        wg_s[...] = wg_ref[...].astype(BF16)
        wu_s[...] = wu_ref[...].astype(BF16)
        wd_s[...] = wd_ref[...].astype(BF16)

    @pl.when(live)
    def _():
        x = _unpack_bf16_pairs(xa_ref[...], xb_ref[...]).astype(BF16)
        hg = jnp.dot(x, wg_s[...], preferred_element_type=F32)
        hu = jnp.dot(x, wu_s[...], preferred_element_type=F32)
        act = (hg * _sigmoid(hg) * hu).astype(BF16)
        oa_ref[...], ob_ref[...] = _pack_bf16_pairs(jnp.dot(act, wd_s[...], preferred_element_type=F32))


def _ffn(xa, xb, tile_expert, n_tiles, w_gate, w_up, w_down):
    n_rows, quarter = xa.shape
    rows = pl.BlockSpec((MOE_TILE, quarter), lambda t, te, nt: (jnp.minimum(t, nt[0] - 1), 0))
    return pl.pallas_call(
        _ffn_kernel,
        grid_spec=pltpu.PrefetchScalarGridSpec(
            num_scalar_prefetch=2,
            grid=(n_rows // MOE_TILE,),
            in_specs=[rows, rows,
                      pl.BlockSpec((None, D_MODEL, MOE_HIDDEN), lambda t, te, nt: (te[t], 0, 0)),
                      pl.BlockSpec((None, D_MODEL, MOE_HIDDEN), lambda t, te, nt: (te[t], 0, 0)),
                      pl.BlockSpec((None, MOE_HIDDEN, D_MODEL), lambda t, te, nt: (te[t], 0, 0))],
            out_specs=[rows, rows],
            scratch_shapes=[pltpu.VMEM((D_MODEL, MOE_HIDDEN), BF16), pltpu.VMEM((D_MODEL, MOE_HIDDEN), BF16),
                            pltpu.VMEM((MOE_HIDDEN, D_MODEL), BF16)]),
        out_shape=[jax.ShapeDtypeStruct((n_rows, quarter), jnp.int32)] * 2,
        compiler_params=pltpu.CompilerParams(dimension_semantics=("arbitrary",), vmem_limit_bytes=VMEM_LIMIT),
        name="moe_ffn",
    )(tile_expert, n_tiles, xa, xb, w_gate, w_up, w_down)


def _final_kernel(x1_ref, g1a_ref, g1b_ref, g2a_ref, g2b_ref, route_ref, gt2_ref, shf_ref, scf_ref, nfg_ref, y_ref):
    o1 = _unpack_bf16_pairs(g1a_ref[...], g1b_ref[...])
    o2 = _unpack_bf16_pairs(g2a_ref[...], g2b_ref[...])
    route = route_ref[...]
    moe = route[:, 2:3] * o1 + route[:, 3:4] * o2
    x2 = x1_ref[...] + gt2_ref[...] * moe
    y_ref[...] = _rmsnorm_mod(x2, nfg_ref[...], shf_ref[...], scf_ref[...])


def _final(x1, ga, gb, route, gt2, shf, scf, nfg, *, tm):
    bsz, seq, _ = x1.shape
    n_s = seq // tm
    n_blk = bsz * n_s
    quarter = ga.shape[1]
    tok = lambda width: pl.BlockSpec((None, tm, width), lambda b, s: (b, s, 0))
    first = pl.BlockSpec((tm, quarter), lambda b, s: (b * n_s + s, 0))
    second = pl.BlockSpec((tm, quarter), lambda b, s: (n_blk + b * n_s + s, 0))
    return pl.pallas_call(
        _final_kernel,
        grid=(bsz, n_s),
        in_specs=[tok(D_MODEL), first, first, second, second, tok(LANES),
                  _mod_spec(gt2, tm), _mod_spec(shf, tm), _mod_spec(scf, tm), _const_spec(nfg)],
        out_specs=tok(D_MODEL),
        out_shape=jax.ShapeDtypeStruct((bsz, seq, D_MODEL), F32),
        compiler_params=pltpu.CompilerParams(dimension_semantics=("arbitrary", "arbitrary")),
        name="moe_final",
    )(x1, ga, gb, ga, gb, route, gt2, shf, scf, nfg)


def _moe_routed(x1, hpa, hpb, route_rows, route, counts, w_gate, w_up, w_down, gt2, shf, scf, nfg, *, tm):
    bsz, seq, quarter = hpa.shape
    n_tok = bsz * seq
    t_max = 2 * n_tok // MOE_TILE + MOE_EXPERTS
    cnt = counts[:, 0].astype(jnp.int32)
    tiles_e = (cnt + MOE_TILE - 1) // MOE_TILE
    tile_end = jnp.cumsum(tiles_e)
    n_tiles = tile_end[-1:]
    base = (tile_end - tiles_e) * MOE_TILE
    t_ids = jnp.minimum(jnp.arange(t_max, dtype=jnp.int32), n_tiles[0] - 1)
    tile_expert = jnp.sum((t_ids[:, None] >= tile_end[None, :]).astype(jnp.int32), axis=1)
    field = lambda k: route_rows[:, k, :].reshape(n_tok).astype(jnp.int32)
    pos1 = (jnp.take(base, field(0)) + field(4)).reshape(1, n_tok)
    pos2 = (jnp.take(base, field(1)) + field(5)).reshape(1, n_tok)

    n_rows = t_max * MOE_TILE
    xa = _sc_scatter_rows(hpa.reshape(n_tok, quarter), pos1, pos2, n_rows)
    xb = _sc_scatter_rows(hpb.reshape(n_tok, quarter), pos1, pos2, n_rows)
    oa, ob = _ffn(xa, xb, tile_expert, n_tiles, w_gate, w_up, w_down)
    pos = jnp.concatenate([pos1, pos2], axis=1)
    ga = _sc_gather_rows(oa, pos)
    gb = _sc_gather_rows(ob, pos)
    return _final(x1, ga, gb, route, gt2, shf, scf, nfg, tm=tm)


def kernel(x_prompt, x_sample, cache_kv_w128, cache_kv_w512, cache_kv_w2048, c_prompt, c_sample,
           w_ada, b_ada, norm1_g, norm2_g, w_in, sgu_ln_g, sgu_ln_b, sgu_w, sgu_b, w_pa, w_pb, w_o,
           w_route_group, b_route_group, w_route_expert, b_route_expert, w_gate, w_up, w_down,
           normf_g, w_ada_final, b_ada_final):
    depth = w_ada.shape[0]
    assert depth == 1
    l = 0
    bp, seq, _ = x_prompt.shape
    bs, t_new, _ = x_sample.shape
    n_samp = bs * t_new

    c_all = jnp.concatenate([c_prompt, c_sample], axis=0)
    pad_rows = (-c_all.shape[0]) % 8
    c_all = jnp.pad(c_all, ((0, pad_rows), (0, 0)))
    mod = _adaln(c_all, w_ada[l], b_ada[l])
    mod_f = _adaln(c_all, w_ada_final, b_ada_final)

    def split_mods(m, n, lo, hi, per_token):
        parts = jnp.split(m[lo:hi], n, axis=-1)
        if per_token:
            return [jnp.repeat(p, t_new, axis=0).reshape(1, n_samp, D_MODEL) for p in parts]
        return [p.reshape(hi - lo, 1, D_MODEL) for p in parts]

    mods_p = split_mods(mod, 6, 0, bp, False) + split_mods(mod_f, 2, 0, bp, False)
    mods_s = split_mods(mod, 6, bp, bp + bs, True) + split_mods(mod_f, 2, bp, bp + bs, True)

    row = lambda v: v.reshape(1, -1)
    w_in_b = w_in[l].astype(BF16)
    w_pa_b, w_pb_b, w_o_b = w_pa[l].astype(BF16), w_pb[l].astype(BF16), w_o[l].astype(BF16)
    w_gate_b, w_up_b, w_down_b = w_gate[l], w_up[l], w_down[l]
    w_re = jnp.transpose(w_route_expert[l], (0, 2, 1)).reshape(MOE_EXPERTS, D_MODEL)
    w_r = jnp.pad(jnp.concatenate([jnp.transpose(w_route_group[l]), w_re], axis=0),
                  ((0, LANES - MOE_GROUPS - MOE_EXPERTS), (0, 0)))
    w_r_hi = w_r.astype(BF16)
    w_r_lo = (w_r - w_r_hi.astype(F32)).astype(BF16)
    b_r = jnp.pad(jnp.concatenate([b_route_group[l], b_route_expert[l].reshape(-1)]),
                  (0, LANES - MOE_GROUPS - MOE_EXPERTS)).reshape(LANES, 1)

    def layer(x, mods, attend, sgu_w_eff, sgu_bt_eff, keeps, dils, tm_in, tm, emit_vn_f32, routed):
        sh1, sc1, gt1, sh2, sc2, gt2, shf, scf = mods
        outs = _inproj(x, sh1, sc1, row(norm1_g[l]), w_in_b, row(sgu_ln_g[l]), row(sgu_ln_b[l]),
                       tm=tm_in, keeps=keeps, dils=dils, emit_vn_f32=emit_vn_f32)
        qkvs = outs[:3]
        u, vn, ga, gb, kv0, kv1, kv2 = outs[3:10]
        attn_outs, attn_lses = attend(qkvs)
        mixed = _mix(x, attn_outs, attn_lses, u, vn, ga, gb, sgu_w_eff, sgu_bt_eff,
                     w_pa_b, w_pb_b, w_o_b, gt1, sh2, sc2, row(norm2_g[l]), w_r_hi, w_r_lo, b_r,
                     tm=tm, routed=routed)
        if routed:
            x1, hpa, hpb, route_rows, route, counts = mixed
            y = _moe_routed(x1, hpa, hpb, route_rows, route, counts, w_gate_b, w_up_b, w_down_b,
                            gt2, shf, scf, row(normf_g), tm=tm)
        else:
            x1, h2, comb = mixed
            y = _moe(h2, comb, w_gate_b, w_up_b, w_down_b, x1, gt2, shf, scf, row(normf_g), tm=tm)
        return y, (kv0, kv1, kv2), outs[10:]

    def attend_p(qkvs):
        res = [_attn_prompt(qkvs[g], g) for g in range(N_GROUPS)]
        return [r[0] for r in res], [r[1] for r in res]

    keeps_p = tuple(min(win, seq) for win, _ in A_GROUPS)
    dils_p = tuple(dil for _, dil in A_GROUPS)
    y_p, kv_p, _ = layer(x_prompt, mods_p, attend_p, sgu_w[l], jnp.transpose(sgu_b[l]), keeps_p, dils_p,
                         256, 256, False, True)

    caches = (cache_kv_w128[l], cache_kv_w512[l], cache_kv_w2048[l])

    def attend_s(qkvs):
        qkvs = [q.reshape(bs, t_new, 3 * A_WIDTH) for q in qkvs]
        return [_attn_sample(qkvs, caches).reshape(1, n_samp, A_WIDTH)], None

    cl = min(t_new, B_CHUNK)
    eye = jnp.eye(B_CHUNK // cl, dtype=F32)
    sgu_w_s = jnp.einsum('ab,gts->gatbs', eye, sgu_w[l][:, :cl, :cl]).reshape(B_GROUPS, B_CHUNK, B_CHUNK)
    sgu_bt_s = jnp.tile(jnp.transpose(sgu_b[l][:, :cl]), (B_CHUNK // cl, 1))
    y_s, kv_s, extra = layer(x_sample.reshape(1, n_samp, D_MODEL), mods_s, attend_s, sgu_w_s, sgu_bt_s,
                             (n_samp,) * N_GROUPS, (1,) * N_GROUPS, n_samp, n_samp, True, False)

    def kv_out(a, b):
        return a.reshape(depth, b, -1, 2, HEADS, HEAD_DIM)

    return (y_p, y_s.reshape(bs, t_new, D_MODEL),
            kv_out(kv_p[0], bp), kv_out(kv_p[1], bp), kv_out(kv_p[2], bp),
            kv_out(kv_s[0], bs), kv_out(kv_s[1], bs), kv_out(kv_s[2], bs),
            extra[0].reshape(depth, bs, t_new, B_WIDTH))
```

```python
import functools

import numpy as np
import jax
import jax.numpy as jnp
from jax import lax
from jax.experimental import pallas as pl
from jax.experimental.pallas import tpu as pltpu
from jax.experimental.pallas import tpu_sc as plsc

F32 = jnp.float32
BF16 = jnp.bfloat16

D_MODEL = 1024
A_GROUPS = ((128, 1), (512, 4), (2048, 16))
N_GROUPS = 3
HEADS = 8
HEAD_DIM = 64
A_WIDTH = HEADS * HEAD_DIM
A_STEPS = 128
A_BLOCK = 128
B_WIDTH = 1024
B_GROUPS = 8
B_CHUNK = 128
MOE_GROUPS = 4
MOE_PER_GROUP = 4
MOE_EXPERTS = 16
MOE_HIDDEN = 512
EPS = 1e-6
N_QKV = 3 * N_GROUPS * A_WIDTH
IN_COLS = N_QKV + 2 * B_WIDTH + 2 * D_MODEL
COL_CHUNK = 512
LANES = 128
LSE_REP = LANES // HEADS
ROUTE_OFF = MOE_GROUPS
NEG = -1e30
LOG2E = 1.4426950408889634
VMEM_LIMIT = 56 * 1024 * 1024
ATTN_BLOCKS_PER_STEP = 16
MIX_SUB_ROWS = 512
MOE_TILE = 512
SC_WINDOW = 128
SC_CORES = 2


def _sigmoid(x):
    return 1.0 / (1.0 + jnp.exp(-x))


def _gelu_tanh(x):
    return x * (0.5 * (1.0 + jnp.tanh(0.7978845608028654 * (x + 0.044715 * (x * x * x)))))


def _rmsnorm_mod(x, gain, shift, scale):
    y = x * lax.rsqrt(jnp.mean(x * x, axis=-1, keepdims=True) + EPS)
    return y * gain * (1.0 + scale) + shift


def _mod_spec(mod, tm):
    if mod.shape[1] == 1:
        return pl.BlockSpec((None, 1, D_MODEL), lambda b, s, *_: (b, 0, 0))
    return pl.BlockSpec((None, tm, D_MODEL), lambda b, s, *_: (b, s, 0))


def _const_spec(arr):
    nd = arr.ndim
    return pl.BlockSpec(arr.shape, lambda *_: (0,) * nd)


def _adaln_kernel(c_ref, w_ref, b_ref, o_ref):
    c = c_ref[...]
    a = (c * _sigmoid(c)).astype(BF16)
    o_ref[...] = jnp.dot(a, w_ref[...].astype(BF16), preferred_element_type=F32) + b_ref[...]


def _adaln(c, w, b, tn=1024):
    rows, ncols = c.shape[0], w.shape[1]
    return pl.pallas_call(
        _adaln_kernel,
        grid=(ncols // tn,),
        in_specs=[pl.BlockSpec((rows, D_MODEL), lambda j: (0, 0)),
                  pl.BlockSpec((D_MODEL, tn), lambda j: (0, j)),
                  pl.BlockSpec((1, tn), lambda j: (0, j))],
        out_specs=pl.BlockSpec((rows, tn), lambda j: (0, j)),
        out_shape=jax.ShapeDtypeStruct((rows, ncols), F32),
        compiler_params=pltpu.CompilerParams(dimension_semantics=("arbitrary",)),
        name="adaln",
    )(c, w, b.reshape(1, ncols))


def _inproj_kernel(x_ref, sh_ref, sc_ref, g_ref, w_ref, lng_ref, lnb_ref, *refs, tm, keeps, dils, emit_vn_f32,
                   has_after):
    if has_after:
        refs = refs[1:]
    qkv0_ref, qkv1_ref, qkv2_ref, u_ref, vn_ref, ga_ref, gb_ref, kv0_ref, kv1_ref, kv2_ref = refs[:10]
    rest = refs[10:]
    perm_scr = rest[-1]
    h = _rmsnorm_mod(x_ref[...], g_ref[...], sh_ref[...], sc_ref[...]).astype(BF16)

    def proj(c):
        return jnp.dot(h, w_ref[:, c * COL_CHUNK:(c + 1) * COL_CHUNK], preferred_element_type=F32)

    def cols(c):
        return slice(c * COL_CHUNK, (c + 1) * COL_CHUNK)

    qkv_refs = (qkv0_ref, qkv1_ref, qkv2_ref)

    def put_qkv(g, t, z):
        dil = dils[g]
        if dil == 1:
            qkv_refs[g][0, :, cols(t)] = z.astype(BF16)
            return
        slot = perm_scr.at[t % 2]
        n_slab = COL_CHUNK // LANES
        for j in range(n_slab):
            slot[j] = z[:, j * LANES:(j + 1) * LANES]
        for r in range(dil):
            rows = jnp.concatenate([slot[j, pl.ds(r, tm // dil, stride=dil), :] for j in range(n_slab)], axis=1)
            qkv_refs[g][r, :, cols(t)] = rows.astype(BF16)

    for g in range(N_GROUPS):
        put_qkv(g, 0, proj(g) * (HEAD_DIM ** -0.5 * LOG2E))

    kv_refs = (kv0_ref, kv1_ref, kv2_ref)
    for g in range(N_GROUPS):
        keep = keeps[g]
        for t in (1, 2):
            z = proj(t * N_GROUPS + g)
            put_qkv(g, t, z)
            kv_refs[g][:, cols(t - 1)] = z if keep >= tm else z[tm - keep:, :]

    base = N_QKV // COL_CHUNK
    for c in range(2):
        u_ref[:, cols(c)] = _gelu_tanh(proj(base + c)).astype(BF16)

    vs = [_gelu_tanh(proj(base + 2 + c)) for c in range(2)]
    mu = (jnp.sum(vs[0], axis=-1, keepdims=True) + jnp.sum(vs[1], axis=-1, keepdims=True)) * (1.0 / B_WIDTH)
    ds = [v - mu for v in vs]
    var = (jnp.sum(ds[0] * ds[0], axis=-1, keepdims=True)
           + jnp.sum(ds[1] * ds[1], axis=-1, keepdims=True)) * (1.0 / B_WIDTH)
    inv = lax.rsqrt(var + EPS)
    for c in range(2):
        vn = ds[c] * inv * lng_ref[:, cols(c)] + lnb_ref[:, cols(c)]
        vn_ref[:, cols(c)] = vn.astype(BF16)
        if emit_vn_f32:
            rest[0][:, cols(c)] = vn

    for c in range(2):
        ga_ref[:, cols(c)] = _sigmoid(proj(base + 4 + c)).astype(BF16)
        gb_ref[:, cols(c)] = _sigmoid(proj(base + 6 + c)).astype(BF16)


def _inproj(x, sh, sc, gain, w_in, ln_g, ln_b, *, tm, keeps, dils, emit_vn_f32, after=None):
    bsz, seq, _ = x.shape
    n_s = seq // tm
    tok = lambda width: pl.BlockSpec((None, tm, width), lambda b, s: (b, s, 0))

    def kv_spec(keep):
        if keep >= tm:
            first = n_s - keep // tm
            return pl.BlockSpec((None, tm, 2 * A_WIDTH), lambda b, s: (b, jnp.maximum(s - first, 0), 0))
        return pl.BlockSpec((None, keep, 2 * A_WIDTH), lambda b, s: (b, 0, 0))

    out_specs, out_shape = [], []
    for dil in dils:
        out_specs.append(pl.BlockSpec((None, dil, tm // dil, 3 * A_WIDTH), lambda b, s: (b, 0, s, 0)))
        out_shape.append(jax.ShapeDtypeStruct((bsz, dil, seq // dil, 3 * A_WIDTH), BF16))
    out_specs += [tok(B_WIDTH), tok(B_WIDTH), tok(D_MODEL), tok(D_MODEL)]
    out_shape += [jax.ShapeDtypeStruct((bsz, seq, D_MODEL), BF16)] * 4
    for keep in keeps:
        out_specs.append(kv_spec(keep))
        out_shape.append(jax.ShapeDtypeStruct((bsz, keep, 2 * A_WIDTH), F32))
    if emit_vn_f32:
        out_specs.append(tok(B_WIDTH))
        out_shape.append(jax.ShapeDtypeStruct((bsz, seq, B_WIDTH), F32))

    ins = [x, sh, sc, gain, w_in, ln_g, ln_b]
    in_specs = [tok(D_MODEL), _mod_spec(sh, tm), _mod_spec(sc, tm), _const_spec(gain),
                pl.BlockSpec(w_in.shape, lambda b, s: (0, 0), pipeline_mode=pl.Buffered(1)),
                _const_spec(ln_g), _const_spec(ln_b)]
    if after is not None:
        ins.append(after)
        in_specs.append(pl.BlockSpec(memory_space=pl.ANY))
    return pl.pallas_call(
        functools.partial(_inproj_kernel, tm=tm, keeps=keeps, dils=dils, emit_vn_f32=emit_vn_f32,
                          has_after=after is not None),
        grid=(bsz, n_s),
        in_specs=in_specs,
        out_specs=out_specs,
        out_shape=out_shape,
        scratch_shapes=[pltpu.VMEM((2, COL_CHUNK // LANES, tm, LANES), F32)],
        compiler_params=pltpu.CompilerParams(dimension_semantics=("arbitrary", "arbitrary"),
                                             vmem_limit_bytes=VMEM_LIMIT),
        name="inproj",
    )(*ins)


def _attn_kernel(q_ref, kp_ref, kc_ref, vp_ref, vc_ref, bias0_ref, bias_ref, o_ref, lse_ref, *, n_seq, q_blocks):
    lane = lax.broadcasted_iota(jnp.int32, (A_BLOCK, LANES), 1)
    low = lane < HEAD_DIM
    zero = jnp.zeros((), BF16)
    for seq_i, i in [(a, b) for a in range(n_seq) for b in range(q_blocks)]:
        rows = (seq_i, slice(i * A_BLOCK, (i + 1) * A_BLOCK))
        q = q_ref[rows[0], rows[1], :]
        key_rows = slice((i - 1) * A_BLOCK, (i + 1) * A_BLOCK)
        if i == 0:
            k = jnp.concatenate([kp_ref[seq_i], kc_ref[seq_i, :A_BLOCK, :]], axis=0)
            v = jnp.concatenate([vp_ref[seq_i], vc_ref[seq_i, :A_BLOCK, :]], axis=0)
        else:
            k, v = kc_ref[seq_i, key_rows, :], vc_ref[seq_i, key_rows, :]
        b_ref = bias0_ref if i == 0 else bias_ref
        lse_tile = jnp.zeros((A_BLOCK, LANES), F32)
        for j in range(HEADS // 2):
            pair = slice(j * LANES, (j + 1) * LANES)
            qp, kpair, vpair = q[:, pair], k[:, pair], v[:, pair]
            outs = []
            for e in range(2):
                h = 2 * j + e
                qm = jnp.where(low if e == 0 else jnp.logical_not(low), qp, zero)
                s = lax.dot_general(qm, kpair, (((1,), (1,)), ((), ())), preferred_element_type=F32)
                s = s + b_ref[h]
                m = jnp.max(s, axis=-1, keepdims=True)
                p = jnp.exp2(s - m)
                den = jnp.sum(p, axis=-1, keepdims=True)
                o = jnp.dot(p.astype(BF16), vpair, preferred_element_type=F32)
                first = h * LSE_REP
                lse_tile = jnp.where((lane >= first) & (lane < first + LSE_REP // 2), m,
                                     jnp.where((lane >= first + LSE_REP // 2) & (lane < first + LSE_REP), den,
                                               lse_tile))
                outs.append(o)
            o_ref[rows[0], rows[1], pair] = jnp.where(low, outs[0], outs[1]).astype(BF16)
        lse_ref[rows[0], rows[1], :] = lse_tile


def _attn_bias(g):
    _, dil = A_GROUPS[g]
    n = N_GROUPS * HEADS
    e = np.arange(1, n + 1, dtype=np.float32)
    slopes = np.exp2(-8.0 * e / n).astype(np.float32).reshape(N_GROUPS, HEADS)[g]
    qi = np.arange(A_BLOCK)[:, None]
    ki = np.arange(2 * A_BLOCK)[None, :]
    delta = qi + A_BLOCK - ki
    band = (delta >= 0) & (delta <= A_STEPS)
    dist = (delta * dil).astype(np.float32)
    bias = (-slopes[:, None, None] * dist[None] * np.float32(LOG2E)).astype(np.float32)
    out = np.empty((2, HEADS, A_BLOCK, 2 * A_BLOCK), np.float32)
    out[1] = np.where(band[None], bias, NEG)
    out[0] = np.where((band & (ki >= A_BLOCK))[None], bias, NEG)
    return jnp.asarray(out)


def _attn_prompt(qkv, g):
    bsz, dil, steps, _ = qkv.shape
    qb = min(ATTN_BLOCKS_PER_STEP, steps // A_BLOCK)
    ns = min(ATTN_BLOCKS_PER_STEP // qb, dil)
    rows = qb * A_BLOCK
    n_steps = steps // rows
    bias = _attn_bias(g)

    def blk(t, prev):
        if prev:
            return pl.BlockSpec((None, ns, A_BLOCK, A_WIDTH),
                                lambda b, r, n: (b, r, jnp.maximum(n * qb - 1, 0), t))
        return pl.BlockSpec((None, ns, rows, A_WIDTH), lambda b, r, n: (b, r, n, t))

    bias_blk = lambda pick: pl.BlockSpec((None, HEADS, A_BLOCK, 2 * A_BLOCK), lambda b, r, n: (pick(n), 0, 0, 0))
    return pl.pallas_call(
        functools.partial(_attn_kernel, n_seq=ns, q_blocks=qb),
        grid=(bsz, dil // ns, n_steps),
        in_specs=[blk(0, False), blk(1, True), blk(1, False), blk(2, True), blk(2, False),
                  bias_blk(lambda n: jnp.minimum(n, 1)), bias_blk(lambda n: 1)],
        out_specs=[pl.BlockSpec((None, ns, rows, A_WIDTH), lambda b, r, n: (b, r, n, 0)),
                   pl.BlockSpec((None, ns, rows, LANES), lambda b, r, n: (b, r, n, 0))],
        out_shape=[jax.ShapeDtypeStruct((bsz, dil, steps, A_WIDTH), BF16),
                   jax.ShapeDtypeStruct((bsz, dil, steps, LANES), F32)],
        compiler_params=pltpu.CompilerParams(dimension_semantics=("arbitrary",) * 3),
        name=f"attn_prompt_g{g}",
    )(qkv, qkv, qkv, qkv, qkv, bias, bias)


def _attn_sample_kernel(q0_ref, q1_ref, q2_ref, c0_ref, c1_ref, c2_ref, bc0_ref, bc1_ref, bc2_ref,
                        bn0_ref, bn1_ref, bn2_ref, after_ref, o_ref, *, t_new):
    del after_ref
    n_rows = HEADS * t_new
    row = lax.broadcasted_iota(jnp.int32, (n_rows, A_WIDTH), 0)
    lane = lax.broadcasted_iota(jnp.int32, (n_rows, A_WIDTH), 1)
    head_mask = (row // t_new) == (lane // HEAD_DIM)
    pad = jnp.zeros((LANES - t_new, A_WIDTH), F32)
    outs, lses = [], []
    for qkv_ref, c_ref, bc_ref, bn_ref in ((q0_ref, c0_ref, bc0_ref, bn0_ref), (q1_ref, c1_ref, bc1_ref, bn1_ref),
                                           (q2_ref, c2_ref, bc2_ref, bn2_ref)):
        q = qkv_ref[:, :A_WIDTH].astype(F32)
        k_new = qkv_ref[:, A_WIDTH:2 * A_WIDTH].astype(F32)
        v_new = qkv_ref[:, 2 * A_WIDTH:].astype(F32)
        k_new = jnp.concatenate([k_new, pad], axis=0).astype(BF16)
        v_new = jnp.concatenate([v_new, pad], axis=0).astype(BF16)
        q_rows = jnp.where(head_mask, jnp.concatenate([q] * HEADS, axis=0), 0.0).astype(BF16)
        k_buf_t = c_ref[:A_WIDTH, :].astype(BF16)
        v_buf_t = c_ref[A_WIDTH:, :].astype(BF16)
        nt = (((1,), (1,)), ((), ()))
        s_buf = jnp.dot(q_rows, k_buf_t, preferred_element_type=F32) + bc_ref[...]
        s_new = lax.dot_general(q_rows, k_new, nt, preferred_element_type=F32) + bn_ref[...]
        m = jnp.maximum(jnp.max(s_buf, axis=-1, keepdims=True), jnp.max(s_new, axis=-1, keepdims=True))
        p_buf = jnp.exp2(s_buf - m)
        p_new = jnp.exp2(s_new - m)
        den = jnp.sum(p_buf, axis=-1, keepdims=True) + jnp.sum(p_new, axis=-1, keepdims=True)
        o = (lax.dot_general(p_buf.astype(BF16), v_buf_t, nt, preferred_element_type=F32)
             + jnp.dot(p_new.astype(BF16), v_new, preferred_element_type=F32)) / den
        outs.append(o)
        lses.append(m + jnp.log2(den))
    top = jnp.maximum(jnp.maximum(lses[0], lses[1]), lses[2])
    ws = [jnp.exp2(l - top) for l in lses]
    tot = ws[0] + ws[1] + ws[2]
    acc = (ws[0] / tot) * outs[0] + (ws[1] / tot) * outs[1] + (ws[2] / tot) * outs[2]
    acc = jnp.where(head_mask, acc, 0.0).reshape(HEADS, t_new, A_WIDTH)
    o_ref[...] = jnp.sum(acc, axis=0).astype(BF16)


def _sample_bias(g, t_new, buf):
    _, dil = A_GROUPS[g]
    n = N_GROUPS * HEADS
    e = np.arange(1, n + 1, dtype=np.float32)
    slopes = np.exp2(-8.0 * e / n).astype(np.float32).reshape(N_GROUPS, HEADS)[g]
    t = np.arange(t_new)[:, None]
    idx = np.concatenate([np.arange(buf), buf + np.arange(LANES)])[None, :]
    dist = buf + t - idx
    valid = (dist >= 0) & (dist % dil == 0) & (dist <= A_STEPS * dil) & (idx < buf + t_new)
    bias = -slopes[:, None, None] * dist.astype(np.float32)[None] * np.float32(LOG2E)
    bias = np.where(valid[None], bias, NEG).astype(np.float32).reshape(HEADS * t_new, buf + LANES)
    return jnp.asarray(bias[:, :buf]), jnp.asarray(bias[:, buf:])


def _attn_sample(qkvs, caches, *, after):
    bsz, t_new, _ = qkvs[0].shape
    cache_v = [jnp.transpose(c, (0, 2, 3, 4, 1)).reshape(bsz, 2 * A_WIDTH, c.shape[1]) for c in caches]
    biases = [_sample_bias(g, t_new, cache_v[g].shape[2]) for g in range(N_GROUPS)]
    bcs = [b[0] for b in biases]
    bns = [b[1] for b in biases]
    return pl.pallas_call(
        functools.partial(_attn_sample_kernel, t_new=t_new),
        grid=(bsz,),
        in_specs=[pl.BlockSpec((None, t_new, 3 * A_WIDTH), lambda b: (b, 0, 0))] * N_GROUPS
                 + [pl.BlockSpec((None, 2 * A_WIDTH, c.shape[2]), lambda b: (b, 0, 0)) for c in cache_v]
                 + [_const_spec(b) for b in bcs] + [_const_spec(b) for b in bns]
                 + [pl.BlockSpec(memory_space=pl.ANY)],
        out_specs=pl.BlockSpec((None, t_new, A_WIDTH), lambda b: (b, 0, 0)),
        out_shape=jax.ShapeDtypeStruct((bsz, t_new, A_WIDTH), BF16),
        compiler_params=pltpu.CompilerParams(dimension_semantics=("arbitrary",),
                                             vmem_limit_bytes=VMEM_LIMIT),
        name="attn_sample",
    )(*qkvs, *cache_v, *bcs, *bns, after)


def _first_max4(v):
    top = jnp.maximum(jnp.maximum(v[0], v[1]), jnp.maximum(v[2], v[3]))
    idx = jnp.where(v[0] == top, 0.0, jnp.where(v[1] == top, 1.0, jnp.where(v[2] == top, 2.0, 3.0)))
    return top, idx


def _route(lt):
    row = lambda k: lt[k:k + 1, :]
    g = [row(k) for k in range(MOE_GROUPS)]
    gmax, g_idx = _first_max4(g)
    g_prob = 1.0 / (jnp.exp(g[0] - gmax) + jnp.exp(g[1] - gmax) + jnp.exp(g[2] - gmax) + jnp.exp(g[3] - gmax))
    cand = []
    for k in range(MOE_PER_GROUP):
        c = row(ROUTE_OFF + MOE_PER_GROUP * (MOE_GROUPS - 1) + k)
        for gi in range(MOE_GROUPS - 2, -1, -1):
            c = jnp.where(g_idx == float(gi), row(ROUTE_OFF + MOE_PER_GROUP * gi + k), c)
        cand.append(c)
    e1, i1 = _first_max4(cand)
    rest = [jnp.where(i1 == float(k), -jnp.inf, cand[k]) for k in range(MOE_PER_GROUP)]
    e2, i2 = _first_max4(rest)
    t = jnp.exp(e2 - e1)
    w1 = 1.0 / (1.0 + t)
    w2 = t / (1.0 + t)
    return MOE_PER_GROUP * g_idx + i1, MOE_PER_GROUP * g_idx + i2, w1 * g_prob, w2 * g_prob


def _pack_bf16_pairs(x):
    q = x.shape[1] // 4
    bits = lax.bitcast_convert_type(x.astype(BF16).astype(F32), jnp.int32)
    pack = lambda hi, lo: hi | lax.shift_right_logical(lo, 16)
    return pack(bits[:, :q], bits[:, 2 * q:3 * q]), pack(bits[:, q:2 * q], bits[:, 3 * q:])


def _unpack_bf16_pairs(pa, pb):
    hi = lambda p: lax.bitcast_convert_type(p & jnp.int32(-65536), F32)
    lo = lambda p: lax.bitcast_convert_type(lax.shift_left(p, 16), F32)
    return jnp.concatenate([hi(pa), hi(pb), lo(pa), lo(pb)], axis=1)


def _mix_kernel(*refs, tm, dils, routed, span):
    n_attn = len(dils)
    x_ref = refs[0]
    if n_attn > 1:
        o_refs = refs[1:1 + n_attn]
        l_refs = refs[1 + n_attn:1 + 2 * n_attn]
        exp_ref = refs[1 + 2 * n_attn]
        pos = 2 + 2 * n_attn
    else:
        o_refs = refs[1:2]
        pos = 2
    (u_ref, vn_ref, ga_ref, gb_ref, sw_ref, sbt_ref, wpa_ref, wpb_ref, wo_ref,
     gt1_ref, sh2_ref, sc2_ref, n2g_ref, wrh_ref, wrl_ref, br_ref) = refs[pos:pos + 16]
    n_out = 6 if routed else 3
    out_refs = refs[pos + 16:pos + 16 + n_out]
    scr = list(refs[pos + 16 + n_out:])
    x1_ref = out_refs[0]
    ob_scr = scr.pop(0)

    def to_positions(ref, scr, dil):
        if dil == 1:
            return ref[0].astype(F32)
        n_slab = scr.shape[0]
        for r in range(dil):
            rows = ref[r].astype(F32)
            for j in range(n_slab):
                scr[j, pl.ds(r, tm // dil, stride=dil), :] = rows[:, j * LANES:(j + 1) * LANES]
        return jnp.concatenate([scr[j] for j in range(n_slab)], axis=1)

    if n_attn > 1:
        o_scr, l_scr = scr[0], scr[1]
        stats = [to_positions(l_refs[g], l_scr.at[g], dils[g]) for g in range(n_attn)]
        lane = lax.broadcasted_iota(jnp.int32, stats[0].shape, 1)
        is_max = (lane & (LSE_REP - 1)) < LSE_REP // 2
        half = LSE_REP // 2
        maxes = [jnp.where(is_max, s, pltpu.roll(s, half, axis=1)) for s in stats]
        dens = [jnp.where(is_max, pltpu.roll(s, LANES - half, axis=1), s) for s in stats]
        lses = [m + jnp.log2(d) for m, d in zip(maxes, dens)]
        top = functools.reduce(jnp.maximum, lses)
        tot = functools.reduce(lambda a, b: a + b, [jnp.exp2(l - top) for l in lses])
        o_a = None
        for g in range(n_attn):
            w = jnp.exp2(maxes[g] - top) / tot
            hi = w.astype(BF16)
            lo = (w - hi.astype(F32)).astype(BF16)
            w_exp = jnp.dot(jnp.concatenate([hi, lo], axis=1), exp_ref[...], preferred_element_type=F32)
            term = w_exp * to_positions(o_refs[g], o_scr.at[g], dils[g])
            o_a = term if o_a is None else o_a + term
        o_a = o_a.astype(BF16)
    else:
        o_a = o_refs[0][...]

    r_i = lax.broadcasted_iota(jnp.int32, (span, span), 0)
    c_i = lax.broadcasted_iota(jnp.int32, (span, span), 1)
    tril = r_i >= c_i
    for gi in range(B_GROUPS):
        gcols = slice(gi * LANES, (gi + 1) * LANES)
        b_s = sbt_ref[:span, gi:gi + 1]
        if span == B_CHUNK:
            w_s = jnp.where(tril, sw_ref[gi], 0.0).astype(BF16)
            n_chunk = tm // span
            vn_wide = jnp.concatenate([vn_ref[c * span:(c + 1) * span, gcols] for c in range(n_chunk)], axis=1)
            mixed = jnp.dot(w_s, vn_wide, preferred_element_type=F32) + b_s
            for c in range(n_chunk):
                rows = slice(c * span, (c + 1) * span)
                ob_scr[rows, gcols] = (u_ref[rows, gcols].astype(F32)
                                       * mixed[:, c * LANES:(c + 1) * LANES]).astype(BF16)
        else:
            w_s = jnp.where(tril, sw_ref[gi, :span, :span], 0.0)
            vn3 = vn_ref[:, gcols].astype(F32).reshape(tm // span, span, LANES)
            mixed = jnp.broadcast_to(b_s, (span, LANES))[None]
            for s in range(span):
                w_col = jnp.broadcast_to(w_s[:, s:s + 1], (span, LANES))[None]
                mixed = mixed + w_col * jnp.broadcast_to(vn3[:, s:s + 1, :], vn3.shape)
            u3 = u_ref[:, gcols].astype(F32).reshape(tm // span, span, LANES)
            ob_scr[:, gcols] = (u3 * mixed).reshape(tm, LANES).astype(BF16)

    sub = min(tm, MIX_SUB_ROWS)
    if routed:
        carry = scr[-1]
        first_step = (pl.program_id(0) == 0) & (pl.program_id(1) == 0)
        seen = jnp.where(first_step, 0.0, carry[:, 0:1])
        r_i = lax.broadcasted_iota(jnp.int32, (sub, sub), 0)
        c_i = lax.broadcasted_iota(jnp.int32, (sub, sub), 1)
        earlier = jnp.where(r_i < c_i, 1.0, 0.0).astype(BF16)
        e_iota = lax.broadcasted_iota(jnp.int32, (MOE_EXPERTS, sub), 0).astype(F32)
    nt = (((1,), (1,)), ((), ()))
    for r0 in range(0, tm, sub):
        rs = slice(r0, r0 + sub)
        mod = lambda ref: ref[...] if ref.shape[0] == 1 else ref[rs, :]
        pa = jnp.dot(o_a[rs], wpa_ref[...], preferred_element_type=F32)
        pb = jnp.dot(ob_scr[rs, :], wpb_ref[...], preferred_element_type=F32)
        mix = (ga_ref[rs, :].astype(F32) * pa + gb_ref[rs, :].astype(F32) * pb).astype(BF16)
        x1 = x_ref[rs, :] + mod(gt1_ref) * jnp.dot(mix, wo_ref[...], preferred_element_type=F32)
        x1_ref[rs, :] = x1
        h2 = _rmsnorm_mod(x1, n2g_ref[...], mod(sh2_ref), mod(sc2_ref))
        h_hi = h2.astype(BF16)
        h_lo = (h2 - h_hi.astype(F32)).astype(BF16)
        lt = (lax.dot_general(wrh_ref[...], h_hi, nt, preferred_element_type=F32)
              + lax.dot_general(wrh_ref[...], h_lo, nt, preferred_element_type=F32)
              + lax.dot_general(wrl_ref[...], h_hi, nt, preferred_element_type=F32)) + br_ref[...]
        ex1, ex2, w1, w2 = _route(lt)
        if not routed:
            _, h2_ref, comb_ref = out_refs
            h2_ref[rs, :] = h_hi
            e_lane = lax.broadcasted_iota(jnp.int32, (LANES, sub), 0).astype(F32) - float(ROUTE_OFF)
            comb_t = jnp.where(e_lane == ex1, w1, jnp.where(e_lane == ex2, w2, 0.0))
            comb_ref[rs, :] = jnp.transpose(comb_t)
            continue

        _, hpa_ref, hpb_ref, route_ref, wcol_ref, cnt_ref = out_refs
        hpa_ref[rs, :], hpb_ref[rs, :] = _pack_bf16_pairs(h2)
        hot1, hot2 = e_iota == ex1, e_iota == ex2
        onehot = jnp.where(hot1 | hot2, 1.0, 0.0)
        rank = jnp.dot(onehot.astype(BF16), earlier, preferred_element_type=F32) + seen
        r1 = jnp.sum(jnp.where(hot1, rank, 0.0), axis=0, keepdims=True)
        r2 = jnp.sum(jnp.where(hot2, rank, 0.0), axis=0, keepdims=True)
        seen = seen + jnp.sum(onehot, axis=1, keepdims=True)
        fields = jnp.concatenate([ex1, ex2, w1, w2, r1, r2, jnp.zeros((2, sub), F32)], axis=0)
        route_ref[:, rs] = fields
        wcol_ref[rs, :] = jnp.transpose(jnp.concatenate([fields, jnp.zeros((LANES - 8, sub), F32)], axis=0))
    if routed:
        carry[...] = jnp.broadcast_to(seen, carry.shape)
        cnt_ref[...] = jnp.broadcast_to(seen, cnt_ref.shape)


def _expand_matrix():
    e = np.zeros((LANES, A_WIDTH), np.float32)
    for h in range(HEADS):
        e[h * LSE_REP, h * HEAD_DIM:(h + 1) * HEAD_DIM] = 1.0
    return jnp.asarray(np.concatenate([e, e], axis=0), dtype=BF16)


def _mix(x, attn_outs, attn_lses, u, vn, ga, gb, sgu_w, sgu_bt, w_pa, w_pb, w_o,
         gt1, sh2, sc2, n2g, w_r_hi, w_r_lo, b_r, *, tm, routed, span):
    bsz, seq, _ = x.shape
    n_attn = len(attn_outs)
    tok = lambda width: pl.BlockSpec((None, tm, width), lambda b, s: (b, s, 0))
    ins = [x] + list(attn_outs)
    scratch = [pltpu.VMEM((tm, B_WIDTH), BF16)]
    if n_attn > 1:
        dils = tuple(o.shape[1] for o in attn_outs)
        res = lambda dil, width: pl.BlockSpec((None, dil, tm // dil, width), lambda b, s: (b, 0, s, 0))
        e2 = _expand_matrix()
        ins += list(attn_lses) + [e2]
        specs = ([tok(D_MODEL)] + [res(d, A_WIDTH) for d in dils] + [res(d, LANES) for d in dils]
                 + [_const_spec(e2)])
        scratch += [pltpu.VMEM((n_attn, A_WIDTH // LANES, tm, LANES), F32), pltpu.VMEM((n_attn, 1, tm, LANES), F32)]
    else:
        dils = (1,)
        specs = [tok(D_MODEL), tok(A_WIDTH)]
    ins += [u, vn, ga, gb, sgu_w, sgu_bt, w_pa, w_pb, w_o, gt1, sh2, sc2, n2g, w_r_hi, w_r_lo, b_r]
    specs += [tok(B_WIDTH), tok(B_WIDTH), tok(D_MODEL), tok(D_MODEL),
              _const_spec(sgu_w), _const_spec(sgu_bt), _const_spec(w_pa), _const_spec(w_pb), _const_spec(w_o),
              _mod_spec(gt1, tm), _mod_spec(sh2, tm), _mod_spec(sc2, tm), _const_spec(n2g),
              _const_spec(w_r_hi), _const_spec(w_r_lo), _const_spec(b_r)]
    if routed:
        quarter = D_MODEL // 4
        out_specs = [tok(D_MODEL), tok(quarter), tok(quarter),
                     pl.BlockSpec((None, 8, tm), lambda b, s: (b, 0, s)), tok(LANES),
                     pl.BlockSpec((MOE_EXPERTS, LANES), lambda b, s: (0, 0))]
        out_shape = [jax.ShapeDtypeStruct((bsz, seq, D_MODEL), F32),
                     jax.ShapeDtypeStruct((bsz, seq, quarter), jnp.int32),
                     jax.ShapeDtypeStruct((bsz, seq, quarter), jnp.int32),
                     jax.ShapeDtypeStruct((bsz, 8, seq), F32),
                     jax.ShapeDtypeStruct((bsz, seq, LANES), F32),
                     jax.ShapeDtypeStruct((MOE_EXPERTS, LANES), F32)]
        scratch.append(pltpu.VMEM((MOE_EXPERTS, LANES), F32))
    else:
        out_specs = [tok(D_MODEL), tok(D_MODEL), tok(LANES)]
        out_shape = [jax.ShapeDtypeStruct((bsz, seq, D_MODEL), F32),
                     jax.ShapeDtypeStruct((bsz, seq, D_MODEL), BF16),
                     jax.ShapeDtypeStruct((bsz, seq, LANES), F32)]
    return pl.pallas_call(
        functools.partial(_mix_kernel, tm=tm, dils=dils, routed=routed, span=span),
        grid=(bsz, seq // tm),
        in_specs=specs,
        out_specs=out_specs,
        out_shape=out_shape,
        scratch_shapes=scratch,
        compiler_params=pltpu.CompilerParams(dimension_semantics=("arbitrary", "arbitrary"),
                                             vmem_limit_bytes=VMEM_LIMIT),
        name="mix",
    )(*ins)


def _moe_kernel(h_ref, comb_ref, wg_ref, wu_ref, wd_ref, x1_ref, gt2_ref, shf_ref, scf_ref, nfg_ref,
                y_ref, acc_ref):
    e = pl.program_id(2)

    @pl.when(e == 0)
    def _():
        acc_ref[...] = jnp.zeros_like(acc_ref)

    h = h_ref[...]
    hg = jnp.dot(h, wg_ref[...].astype(BF16), preferred_element_type=F32)
    hu = jnp.dot(h, wu_ref[...].astype(BF16), preferred_element_type=F32)
    comb = comb_ref[...]
    lane = lax.broadcasted_iota(jnp.int32, comb.shape, 1)
    cw = jnp.sum(jnp.where(lane == e + ROUTE_OFF, comb, 0.0), axis=-1, keepdims=True)
    act = (hg * _sigmoid(hg) * hu) * cw
    acc_ref[...] += jnp.dot(act.astype(BF16), wd_ref[...].astype(BF16), preferred_element_type=F32)

    @pl.when(e == MOE_EXPERTS - 1)
    def _():
        x2 = x1_ref[...] + gt2_ref[...] * acc_ref[...]
        y_ref[...] = _rmsnorm_mod(x2, nfg_ref[...], shf_ref[...], scf_ref[...])


def _moe(h2, comb, w_gate, w_up, w_down, x1, gt2, shf, scf, nfg, *, tm):
    bsz, seq, _ = h2.shape
    tok = lambda width: pl.BlockSpec((None, tm, width), lambda b, s, e: (b, s, 0))
    return pl.pallas_call(
        _moe_kernel,
        grid=(bsz, seq // tm, MOE_EXPERTS),
        in_specs=[tok(D_MODEL), tok(LANES),
                  pl.BlockSpec((None, D_MODEL, MOE_HIDDEN), lambda b, s, e: (e, 0, 0)),
                  pl.BlockSpec((None, D_MODEL, MOE_HIDDEN), lambda b, s, e: (e, 0, 0)),
                  pl.BlockSpec((None, MOE_HIDDEN, D_MODEL), lambda b, s, e: (e, 0, 0)),
                  tok(D_MODEL), _mod_spec(gt2, tm), _mod_spec(shf, tm), _mod_spec(scf, tm), _const_spec(nfg)],
        out_specs=tok(D_MODEL),
        out_shape=jax.ShapeDtypeStruct((bsz, seq, D_MODEL), F32),
        scratch_shapes=[pltpu.VMEM((tm, D_MODEL), F32)],
        compiler_params=pltpu.CompilerParams(dimension_semantics=("arbitrary",) * 3,
                                             vmem_limit_bytes=VMEM_LIMIT),
        name="moe",
    )(h2, comb, w_gate, w_up, w_down, x1, gt2, shf, scf, nfg)


def _sc_mesh():
    return plsc.VectorSubcoreMesh(core_axis_name="c", subcore_axis_name="s")


def _sc_scatter_rows(rows, pos1, pos2, n_out):
    n, width = rows.shape
    steps = n // SC_WINDOW // SC_CORES

    @pl.kernel(out_type=jax.ShapeDtypeStruct((n_out, width), rows.dtype), mesh=_sc_mesh(), scratch_types=[])
    def scatter(rows_hbm, p1_hbm, p2_hbm, out_hbm):
        def body(x_vmem, i1_vmem, i2_vmem):
            pltpu.sync_copy(x_vmem, out_hbm.at[i1_vmem.at[0]])
            pltpu.sync_copy(x_vmem, out_hbm.at[i2_vmem.at[0]])

        pltpu.emit_pipeline(
            body, grid=(SC_CORES, steps),
            in_specs=[pl.BlockSpec((SC_WINDOW, width), lambda c, i: (c * steps + i, 0)),
                      pl.BlockSpec((1, SC_WINDOW), lambda c, i: (0, c * steps + i)),
                      pl.BlockSpec((1, SC_WINDOW), lambda c, i: (0, c * steps + i))],
            out_specs=[],
            core_axis_name=("c", "s"),
            dimension_semantics=(pltpu.PARALLEL, pltpu.PARALLEL),
        )(rows_hbm, p1_hbm, p2_hbm)

    return scatter(rows, pos1, pos2)


def _sc_gather_rows(table, idx):
    m, width = idx.shape[1], table.shape[1]
    steps = m // SC_WINDOW // SC_CORES

    @pl.kernel(out_type=jax.ShapeDtypeStruct((m, width), table.dtype), mesh=_sc_mesh(), scratch_types=[])
    def gather(table_hbm, idx_hbm, out_hbm):
        def body(i_vmem, o_vmem):
            pltpu.sync_copy(table_hbm.at[i_vmem.at[0]], o_vmem)

        pltpu.emit_pipeline(
            body, grid=(SC_CORES, steps),
            in_specs=[pl.BlockSpec((1, SC_WINDOW), lambda c, i: (0, c * steps + i))],
            out_specs=[pl.BlockSpec((SC_WINDOW, width), lambda c, i: (c * steps + i, 0))],
            core_axis_name=("c", "s"),
            dimension_semantics=(pltpu.PARALLEL, pltpu.PARALLEL),
        )(idx_hbm, out_hbm)

    return gather(table, idx)


def _ffn_kernel(te_ref, nt_ref, xa_ref, xb_ref, wg_ref, wu_ref, wd_ref, after_ref, oa_ref, ob_ref,
                wg_s, wu_s, wd_s):
    del after_ref
    t = pl.program_id(0)
    live = t < nt_ref[0]

    @pl.when(live & ((t == 0) | (te_ref[t] != te_ref[jnp.maximum(t - 1, 0)])))
    def _():
        wg_s[...] = wg_ref[...].astype(BF16)
        wu_s[...] = wu_ref[...].astype(BF16)
        wd_s[...] = wd_ref[...].astype(BF16)

    @pl.when(live)
    def _():
        x = _unpack_bf16_pairs(xa_ref[...], xb_ref[...]).astype(BF16)
        hg = jnp.dot(x, wg_s[...], preferred_element_type=F32)
        hu = jnp.dot(x, wu_s[...], preferred_element_type=F32)
        act = (hg * _sigmoid(hg) * hu).astype(BF16)
        oa_ref[...], ob_ref[...] = _pack_bf16_pairs(jnp.dot(act, wd_s[...], preferred_element_type=F32))


def _ffn(xa, xb, tile_expert, n_tiles, w_gate, w_up, w_down, *, after):
    n_rows, quarter = xa.shape
    rows = pl.BlockSpec((MOE_TILE, quarter), lambda t, te, nt: (jnp.minimum(t, nt[0] - 1), 0))
    return pl.pallas_call(
        _ffn_kernel,
        grid_spec=pltpu.PrefetchScalarGridSpec(
            num_scalar_prefetch=2,
            grid=(n_rows // MOE_TILE,),
            in_specs=[rows, rows,
                      pl.BlockSpec((None, D_MODEL, MOE_HIDDEN), lambda t, te, nt: (te[t], 0, 0)),
                      pl.BlockSpec((None, D_MODEL, MOE_HIDDEN), lambda t, te, nt: (te[t], 0, 0)),
                      pl.BlockSpec((None, MOE_HIDDEN, D_MODEL), lambda t, te, nt: (te[t], 0, 0)),
                      pl.BlockSpec(memory_space=pl.ANY)],
            out_specs=[rows, rows],
            scratch_shapes=[pltpu.VMEM((D_MODEL, MOE_HIDDEN), BF16), pltpu.VMEM((D_MODEL, MOE_HIDDEN), BF16),
                            pltpu.VMEM((MOE_HIDDEN, D_MODEL), BF16)]),
        out_shape=[jax.ShapeDtypeStruct((n_rows, quarter), jnp.int32)] * 2,
        compiler_params=pltpu.CompilerParams(dimension_semantics=("arbitrary",), vmem_limit_bytes=VMEM_LIMIT),
        name="moe_ffn",
    )(tile_expert, n_tiles, xa, xb, w_gate, w_up, w_down, after)


def _final_kernel(x1_ref, g1a_ref, g1b_ref, g2a_ref, g2b_ref, route_ref, gt2_ref, shf_ref, scf_ref, nfg_ref, y_ref):
    o1 = _unpack_bf16_pairs(g1a_ref[...], g1b_ref[...])
    o2 = _unpack_bf16_pairs(g2a_ref[...], g2b_ref[...])
    route = route_ref[...]
    moe = route[:, 2:3] * o1 + route[:, 3:4] * o2
    x2 = x1_ref[...] + gt2_ref[...] * moe
    y_ref[...] = _rmsnorm_mod(x2, nfg_ref[...], shf_ref[...], scf_ref[...])


def _final(x1, ga, gb, route, gt2, shf, scf, nfg, *, tm):
    bsz, seq, _ = x1.shape
    n_s = seq // tm
    n_blk = bsz * n_s
    quarter = ga.shape[1]
    tok = lambda width: pl.BlockSpec((None, tm, width), lambda b, s: (b, s, 0))
    first = pl.BlockSpec((tm, quarter), lambda b, s: (b * n_s + s, 0))
    second = pl.BlockSpec((tm, quarter), lambda b, s: (n_blk + b * n_s + s, 0))
    return pl.pallas_call(
        _final_kernel,
        grid=(bsz, n_s),
        in_specs=[tok(D_MODEL), first, first, second, second, tok(LANES),
                  _mod_spec(gt2, tm), _mod_spec(shf, tm), _mod_spec(scf, tm), _const_spec(nfg)],
        out_specs=tok(D_MODEL),
        out_shape=jax.ShapeDtypeStruct((bsz, seq, D_MODEL), F32),
        compiler_params=pltpu.CompilerParams(dimension_semantics=("arbitrary", "arbitrary")),
        name="moe_final",
    )(x1, ga, gb, ga, gb, route, gt2, shf, scf, nfg)


def _positions_kernel(base_ref, route_ref, p1_ref, p2_ref):
    rows = route_ref[...]
    for e_row, r_row, out_ref in ((0, 4, p1_ref), (1, 5, p2_ref)):
        expert = rows[e_row:e_row + 1, :]
        start = jnp.zeros(expert.shape, jnp.int32)
        for e in range(MOE_EXPERTS):
            start = jnp.where(expert == float(e), base_ref[e], start)
        out_ref[...] = start + rows[r_row:r_row + 1, :].astype(jnp.int32)


def _positions(route_rows, base):
    bsz, _, seq = route_rows.shape
    out = pl.BlockSpec((None, 1, seq), lambda b, base: (b, 0, 0))
    return pl.pallas_call(
        _positions_kernel,
        grid_spec=pltpu.PrefetchScalarGridSpec(
            num_scalar_prefetch=1, grid=(bsz,),
            in_specs=[pl.BlockSpec((None, 8, seq), lambda b, base: (b, 0, 0))],
            out_specs=[out, out]),
        out_shape=[jax.ShapeDtypeStruct((bsz, 1, seq), jnp.int32)] * 2,
        compiler_params=pltpu.CompilerParams(dimension_semantics=("arbitrary",)),
        name="moe_positions",
    )(base, route_rows)


def _moe_dispatch(hpa, hpb, route_rows, counts):
    bsz, seq, quarter = hpa.shape
    n_tok = bsz * seq
    t_max = 2 * n_tok // MOE_TILE + MOE_EXPERTS
    cnt = counts[:, 0].astype(jnp.int32)
    tiles_e = (cnt + MOE_TILE - 1) // MOE_TILE
    tile_end = jnp.cumsum(tiles_e)
    n_tiles = tile_end[-1:]
    base = (tile_end - tiles_e) * MOE_TILE
    t_ids = jnp.minimum(jnp.arange(t_max, dtype=jnp.int32), n_tiles[0] - 1)
    tile_expert = jnp.sum((t_ids[:, None] >= tile_end[None, :]).astype(jnp.int32), axis=1)
    pos1, pos2 = (p.reshape(1, n_tok) for p in _positions(route_rows, base))
    n_rows = t_max * MOE_TILE
    xa = _sc_scatter_rows(hpa.reshape(n_tok, quarter), pos1, pos2, n_rows)
    xb = _sc_scatter_rows(hpb.reshape(n_tok, quarter), pos1, pos2, n_rows)
    return xa, xb, jnp.concatenate([pos1, pos2], axis=1), tile_expert, n_tiles


def kernel(x_prompt, x_sample, cache_kv_w128, cache_kv_w512, cache_kv_w2048, c_prompt, c_sample,
           w_ada, b_ada, norm1_g, norm2_g, w_in, sgu_ln_g, sgu_ln_b, sgu_w, sgu_b, w_pa, w_pb, w_o,
           w_route_group, b_route_group, w_route_expert, b_route_expert, w_gate, w_up, w_down,
           normf_g, w_ada_final, b_ada_final):
    depth = w_ada.shape[0]
    assert depth == 1
    l = 0
    bp, seq, _ = x_prompt.shape
    bs, t_new, _ = x_sample.shape
    n_samp = bs * t_new

    c_all = jnp.concatenate([c_prompt, c_sample], axis=0)
    pad_rows = (-c_all.shape[0]) % 8
    c_all = jnp.pad(c_all, ((0, pad_rows), (0, 0)))
    mod = _adaln(c_all, w_ada[l], b_ada[l])
    mod_f = _adaln(c_all, w_ada_final, b_ada_final)

    def split_mods(m, n, lo, hi, per_token):
        parts = jnp.split(m[lo:hi], n, axis=-1)
        if per_token:
            return [jnp.repeat(p, t_new, axis=0).reshape(1, n_samp, D_MODEL) for p in parts]
        return [p.reshape(hi - lo, 1, D_MODEL) for p in parts]

    mods_p = split_mods(mod, 6, 0, bp, False) + split_mods(mod_f, 2, 0, bp, False)
    mods_s = split_mods(mod, 6, bp, bp + bs, True) + split_mods(mod_f, 2, bp, bp + bs, True)

    row = lambda v: v.reshape(1, -1)
    w_in_b = w_in[l].astype(BF16)
    w_pa_b, w_pb_b, w_o_b = w_pa[l].astype(BF16), w_pb[l].astype(BF16), w_o[l].astype(BF16)
    w_gate_b, w_up_b, w_down_b = w_gate[l], w_up[l], w_down[l]
    w_re = jnp.transpose(w_route_expert[l], (0, 2, 1)).reshape(MOE_EXPERTS, D_MODEL)
    w_r = jnp.pad(jnp.concatenate([jnp.transpose(w_route_group[l]), w_re], axis=0),
                  ((0, LANES - MOE_GROUPS - MOE_EXPERTS), (0, 0)))
    w_r_hi = w_r.astype(BF16)
    w_r_lo = (w_r - w_r_hi.astype(F32)).astype(BF16)
    b_r = jnp.pad(jnp.concatenate([b_route_group[l], b_route_expert[l].reshape(-1)]),
                  (0, LANES - MOE_GROUPS - MOE_EXPERTS)).reshape(LANES, 1)

    def inproj(x, mods, keeps, dils, tm, emit_vn_f32, after=None):
        return _inproj(x, mods[0], mods[1], row(norm1_g[l]), w_in_b, row(sgu_ln_g[l]), row(sgu_ln_b[l]),
                       tm=tm, keeps=keeps, dils=dils, emit_vn_f32=emit_vn_f32, after=after)

    sgu_bt = jnp.transpose(sgu_b[l])

    def mix(x, mods, pin, attn_outs, attn_lses, tm, routed, span):
        u, vn, ga, gb = pin[3:7]
        return _mix(x, attn_outs, attn_lses, u, vn, ga, gb, sgu_w[l], sgu_bt,
                    w_pa_b, w_pb_b, w_o_b, mods[2], mods[3], mods[4], row(norm2_g[l]), w_r_hi, w_r_lo, b_r,
                    tm=tm, routed=routed, span=span)

    tm_p = 512
    keeps_p = tuple(min(win, seq) for win, _ in A_GROUPS)
    dils_p = tuple(dil for _, dil in A_GROUPS)
    pin = inproj(x_prompt, mods_p, keeps_p, dils_p, 256, False)
    attn_p = [_attn_prompt(pin[g], g) for g in range(N_GROUPS)]
    x1, hpa, hpb, route_rows, route, counts = mix(x_prompt, mods_p, pin, [r[0] for r in attn_p],
                                                  [r[1] for r in attn_p], tm_p, True, B_CHUNK)

    xs = x_sample.reshape(1, n_samp, D_MODEL)
    sin = inproj(xs, mods_s, (n_samp,) * N_GROUPS, (1,) * N_GROUPS, n_samp, True, after=counts)

    xa, xb, pos, tile_expert, n_tiles = _moe_dispatch(hpa, hpb, route_rows, counts)
    oa, ob = _ffn(xa, xb, tile_expert, n_tiles, w_gate_b, w_up_b, w_down_b, after=sin[0])

    caches = (cache_kv_w128[l], cache_kv_w512[l], cache_kv_w2048[l])
    o_s = _attn_sample([q.reshape(bs, t_new, 3 * A_WIDTH) for q in sin[:3]], caches, after=oa)
    x1_s, h2_s, comb_s = mix(xs, mods_s, sin, [o_s.reshape(1, n_samp, A_WIDTH)], None, n_samp, False,
                             min(t_new, B_CHUNK))
    y_s = _moe(h2_s, comb_s, w_gate_b, w_up_b, w_down_b, x1_s, mods_s[5], mods_s[6], mods_s[7], row(normf_g),
               tm=n_samp)

    ga_rows = _sc_gather_rows(oa, pos)
    gb_rows = _sc_gather_rows(ob, pos)
    y_p = _final(x1, ga_rows, gb_rows, route, mods_p[5], mods_p[6], mods_p[7], row(normf_g), tm=tm_p)
    kv_p, kv_s, extra = pin[7:10], sin[7:10], sin[10:]

    def kv_out(a, b):
        return a.reshape(depth, b, -1, 2, HEADS, HEAD_DIM)

    return (y_p, y_s.reshape(bs, t_new, D_MODEL),
            kv_out(kv_p[0], bp), kv_out(kv_p[1], bp), kv_out(kv_p[2], bp),
            kv_out(kv_s[0], bs), kv_out(kv_s[1], bs), kv_out(kv_s[2], bs),
            extra[0].reshape(depth, bs, t_new, B_WIDTH))
```

```python
import functools

import numpy as np
import jax
import jax.numpy as jnp
from jax import lax
from jax.experimental import pallas as pl
from jax.experimental.pallas import tpu as pltpu
from jax.experimental.pallas import tpu_sc as plsc

F32 = jnp.float32
BF16 = jnp.bfloat16

D_MODEL = 1024
A_GROUPS = ((128, 1), (512, 4), (2048, 16))
N_GROUPS = 3
HEADS = 8
HEAD_DIM = 64
A_WIDTH = HEADS * HEAD_DIM
A_STEPS = 128
A_BLOCK = 128
B_WIDTH = 1024
B_GROUPS = 8
B_CHUNK = 128
MOE_GROUPS = 4
MOE_PER_GROUP = 4
MOE_EXPERTS = 16
MOE_HIDDEN = 512
EPS = 1e-6
N_QKV = 3 * N_GROUPS * A_WIDTH
IN_COLS = N_QKV + 2 * B_WIDTH + 2 * D_MODEL
COL_CHUNK = 512
LANES = 128
LSE_REP = LANES // HEADS
ROUTE_OFF = MOE_GROUPS
NEG = -1e30
LOG2E = 1.4426950408889634
VMEM_LIMIT = 56 * 1024 * 1024
ATTN_BLOCKS_PER_STEP = 16
SAMPLE_CACHE_STREAMS = 8
SAMPLE_STREAM_BYTES = 1 << 20
MIX_SUB_ROWS = 512
MOE_TILE = 512
SC_WINDOW = 128
SC_CORES = 2


def _sigmoid(x):
    return 1.0 / (1.0 + jnp.exp(-x))


def _gelu_tanh(x):
    return x * (0.5 * (1.0 + jnp.tanh(0.7978845608028654 * (x + 0.044715 * (x * x * x)))))


def _rmsnorm_mod(x, gain, shift, scale):
    y = x * lax.rsqrt(jnp.mean(x * x, axis=-1, keepdims=True) + EPS)
    return y * gain * (1.0 + scale) + shift


def _mod_spec(mod, tm):
    if mod.shape[1] == 1:
        return pl.BlockSpec((None, 1, D_MODEL), lambda b, s, *_: (b, 0, 0))
    return pl.BlockSpec((None, tm, D_MODEL), lambda b, s, *_: (b, s, 0))


def _const_spec(arr):
    nd = arr.ndim
    return pl.BlockSpec(arr.shape, lambda *_: (0,) * nd)


def _adaln_kernel(c_ref, w_ref, b_ref, o_ref):
    c = c_ref[...]
    a = (c * _sigmoid(c)).astype(BF16)
    o_ref[...] = jnp.dot(a, w_ref[...].astype(BF16), preferred_element_type=F32) + b_ref[...]


def _adaln(c, w, b, tn=1024):
    rows, ncols = c.shape[0], w.shape[1]
    return pl.pallas_call(
        _adaln_kernel,
        grid=(ncols // tn,),
        in_specs=[pl.BlockSpec((rows, D_MODEL), lambda j: (0, 0)),
                  pl.BlockSpec((D_MODEL, tn), lambda j: (0, j)),
                  pl.BlockSpec((1, tn), lambda j: (0, j))],
        out_specs=pl.BlockSpec((rows, tn), lambda j: (0, j)),
        out_shape=jax.ShapeDtypeStruct((rows, ncols), F32),
        compiler_params=pltpu.CompilerParams(dimension_semantics=("arbitrary",)),
        name="adaln",
    )(c, w, b.reshape(1, ncols))


def _inproj_kernel(x_ref, sh_ref, sc_ref, g_ref, w_ref, lng_ref, lnb_ref, *refs, tm, keeps, dils, emit_vn_f32,
                   has_after):
    if has_after:
        refs = refs[1:]
    qkv0_ref, qkv1_ref, qkv2_ref, u_ref, vn_ref, ga_ref, gb_ref, kv0_ref, kv1_ref, kv2_ref = refs[:10]
    rest = refs[10:]
    perm_scr = rest[-1]
    h = _rmsnorm_mod(x_ref[...], g_ref[...], sh_ref[...], sc_ref[...]).astype(BF16)

    def proj(c):
        return jnp.dot(h, w_ref[:, c * COL_CHUNK:(c + 1) * COL_CHUNK], preferred_element_type=F32)

    def cols(c):
        return slice(c * COL_CHUNK, (c + 1) * COL_CHUNK)

    qkv_refs = (qkv0_ref, qkv1_ref, qkv2_ref)

    def put_qkv(g, t, z):
        dil = dils[g]
        if dil == 1:
            qkv_refs[g][0, :, cols(t)] = z.astype(BF16)
            return
        slot = perm_scr.at[t % 2]
        n_slab = COL_CHUNK // LANES
        for j in range(n_slab):
            slot[j] = z[:, j * LANES:(j + 1) * LANES]
        for r in range(dil):
            rows = jnp.concatenate([slot[j, pl.ds(r, tm // dil, stride=dil), :] for j in range(n_slab)], axis=1)
            qkv_refs[g][r, :, cols(t)] = rows.astype(BF16)

    for g in range(N_GROUPS):
        put_qkv(g, 0, proj(g) * (HEAD_DIM ** -0.5 * LOG2E))

    kv_refs = (kv0_ref, kv1_ref, kv2_ref)
    for g in range(N_GROUPS):
        keep = keeps[g]
        for t in (1, 2):
            z = proj(t * N_GROUPS + g)
            put_qkv(g, t, z)
            kv_refs[g][:, cols(t - 1)] = z if keep >= tm else z[tm - keep:, :]

    base = N_QKV // COL_CHUNK
    for c in range(2):
        u_ref[:, cols(c)] = _gelu_tanh(proj(base + c)).astype(BF16)

    vs = [_gelu_tanh(proj(base + 2 + c)) for c in range(2)]
    mu = (jnp.sum(vs[0], axis=-1, keepdims=True) + jnp.sum(vs[1], axis=-1, keepdims=True)) * (1.0 / B_WIDTH)
    ds = [v - mu for v in vs]
    var = (jnp.sum(ds[0] * ds[0], axis=-1, keepdims=True)
           + jnp.sum(ds[1] * ds[1], axis=-1, keepdims=True)) * (1.0 / B_WIDTH)
    inv = lax.rsqrt(var + EPS)
    for c in range(2):
        vn = ds[c] * inv * lng_ref[:, cols(c)] + lnb_ref[:, cols(c)]
        vn_ref[:, cols(c)] = vn.astype(BF16)
        if emit_vn_f32:
            rest[0][:, cols(c)] = vn

    for c in range(2):
        ga_ref[:, cols(c)] = _sigmoid(proj(base + 4 + c)).astype(BF16)
        gb_ref[:, cols(c)] = _sigmoid(proj(base + 6 + c)).astype(BF16)


def _inproj(x, sh, sc, gain, w_in, ln_g, ln_b, *, tm, keeps, dils, emit_vn_f32, after=None):
    bsz, seq, _ = x.shape
    n_s = seq // tm
    tok = lambda width: pl.BlockSpec((None, tm, width), lambda b, s: (b, s, 0))

    def kv_spec(keep):
        if keep >= tm:
            first = n_s - keep // tm
            return pl.BlockSpec((None, tm, 2 * A_WIDTH), lambda b, s: (b, jnp.maximum(s - first, 0), 0))
        return pl.BlockSpec((None, keep, 2 * A_WIDTH), lambda b, s: (b, 0, 0))

    out_specs, out_shape = [], []
    for dil in dils:
        out_specs.append(pl.BlockSpec((None, dil, tm // dil, 3 * A_WIDTH), lambda b, s: (b, 0, s, 0)))
        out_shape.append(jax.ShapeDtypeStruct((bsz, dil, seq // dil, 3 * A_WIDTH), BF16))
    out_specs += [tok(B_WIDTH), tok(B_WIDTH), tok(D_MODEL), tok(D_MODEL)]
    out_shape += [jax.ShapeDtypeStruct((bsz, seq, D_MODEL), BF16)] * 4
    for keep in keeps:
        out_specs.append(kv_spec(keep))
        out_shape.append(jax.ShapeDtypeStruct((bsz, keep, 2 * A_WIDTH), F32))
    if emit_vn_f32:
        out_specs.append(tok(B_WIDTH))
        out_shape.append(jax.ShapeDtypeStruct((bsz, seq, B_WIDTH), F32))

    ins = [x, sh, sc, gain, w_in, ln_g, ln_b]
    in_specs = [tok(D_MODEL), _mod_spec(sh, tm), _mod_spec(sc, tm), _const_spec(gain),
                pl.BlockSpec(w_in.shape, lambda b, s: (0, 0), pipeline_mode=pl.Buffered(1)),
                _const_spec(ln_g), _const_spec(ln_b)]
    if after is not None:
        ins.append(after)
        in_specs.append(pl.BlockSpec(memory_space=pl.ANY))
    return pl.pallas_call(
        functools.partial(_inproj_kernel, tm=tm, keeps=keeps, dils=dils, emit_vn_f32=emit_vn_f32,
                          has_after=after is not None),
        grid=(bsz, n_s),
        in_specs=in_specs,
        out_specs=out_specs,
        out_shape=out_shape,
        scratch_shapes=[pltpu.VMEM((2, COL_CHUNK // LANES, tm, LANES), F32)],
        compiler_params=pltpu.CompilerParams(dimension_semantics=("arbitrary", "arbitrary"),
                                             vmem_limit_bytes=VMEM_LIMIT),
        name="inproj",
    )(*ins)


def _attn_kernel(q_ref, kp_ref, kc_ref, vp_ref, vc_ref, bias0_ref, bias_ref, o_ref, lse_ref, *, n_seq, q_blocks):
    lane = lax.broadcasted_iota(jnp.int32, (A_BLOCK, LANES), 1)
    low = lane < HEAD_DIM
    zero = jnp.zeros((), BF16)
    for seq_i, i in [(a, b) for a in range(n_seq) for b in range(q_blocks)]:
        rows = (seq_i, slice(i * A_BLOCK, (i + 1) * A_BLOCK))
        q = q_ref[rows[0], rows[1], :]
        key_rows = slice((i - 1) * A_BLOCK, (i + 1) * A_BLOCK)
        if i == 0:
            k = jnp.concatenate([kp_ref[seq_i], kc_ref[seq_i, :A_BLOCK, :]], axis=0)
            v = jnp.concatenate([vp_ref[seq_i], vc_ref[seq_i, :A_BLOCK, :]], axis=0)
        else:
            k, v = kc_ref[seq_i, key_rows, :], vc_ref[seq_i, key_rows, :]
        b_ref = bias0_ref if i == 0 else bias_ref
        lse_tile = jnp.zeros((A_BLOCK, LANES), F32)
        for j in range(HEADS // 2):
            pair = slice(j * LANES, (j + 1) * LANES)
            qp, kpair, vpair = q[:, pair], k[:, pair], v[:, pair]
            outs = []
            for e in range(2):
                h = 2 * j + e
                qm = jnp.where(low if e == 0 else jnp.logical_not(low), qp, zero)
                s = lax.dot_general(qm, kpair, (((1,), (1,)), ((), ())), preferred_element_type=F32)
                s = s + b_ref[h]
                m = jnp.max(s, axis=-1, keepdims=True)
                p = jnp.exp2(s - m)
                den = jnp.sum(p, axis=-1, keepdims=True)
                o = jnp.dot(p.astype(BF16), vpair, preferred_element_type=F32)
                first = h * LSE_REP
                lse_tile = jnp.where((lane >= first) & (lane < first + LSE_REP // 2), m,
                                     jnp.where((lane >= first + LSE_REP // 2) & (lane < first + LSE_REP), den,
                                               lse_tile))
                outs.append(o)
            o_ref[rows[0], rows[1], pair] = jnp.where(low, outs[0], outs[1]).astype(BF16)
        lse_ref[rows[0], rows[1], :] = lse_tile


def _attn_bias(g):
    _, dil = A_GROUPS[g]
    n = N_GROUPS * HEADS
    e = np.arange(1, n + 1, dtype=np.float32)
    slopes = np.exp2(-8.0 * e / n).astype(np.float32).reshape(N_GROUPS, HEADS)[g]
    qi = np.arange(A_BLOCK)[:, None]
    ki = np.arange(2 * A_BLOCK)[None, :]
    delta = qi + A_BLOCK - ki
    band = (delta >= 0) & (delta <= A_STEPS)
    dist = (delta * dil).astype(np.float32)
    bias = (-slopes[:, None, None] * dist[None] * np.float32(LOG2E)).astype(np.float32)
    out = np.empty((2, HEADS, A_BLOCK, 2 * A_BLOCK), np.float32)
    out[1] = np.where(band[None], bias, NEG)
    out[0] = np.where((band & (ki >= A_BLOCK))[None], bias, NEG)
    return jnp.asarray(out)


def _attn_prompt(qkv, g):
    bsz, dil, steps, _ = qkv.shape
    qb = min(ATTN_BLOCKS_PER_STEP, steps // A_BLOCK)
    ns = min(ATTN_BLOCKS_PER_STEP // qb, dil)
    rows = qb * A_BLOCK
    n_steps = steps // rows
    bias = _attn_bias(g)

    def blk(t, prev):
        if prev:
            return pl.BlockSpec((None, ns, A_BLOCK, A_WIDTH),
                                lambda b, r, n: (b, r, jnp.maximum(n * qb - 1, 0), t))
        return pl.BlockSpec((None, ns, rows, A_WIDTH), lambda b, r, n: (b, r, n, t))

    bias_blk = lambda pick: pl.BlockSpec((None, HEADS, A_BLOCK, 2 * A_BLOCK), lambda b, r, n: (pick(n), 0, 0, 0))
    return pl.pallas_call(
        functools.partial(_attn_kernel, n_seq=ns, q_blocks=qb),
        grid=(bsz, dil // ns, n_steps),
        in_specs=[blk(0, False), blk(1, True), blk(1, False), blk(2, True), blk(2, False),
                  bias_blk(lambda n: jnp.minimum(n, 1)), bias_blk(lambda n: 1)],
        out_specs=[pl.BlockSpec((None, ns, rows, A_WIDTH), lambda b, r, n: (b, r, n, 0)),
                   pl.BlockSpec((None, ns, rows, LANES), lambda b, r, n: (b, r, n, 0))],
        out_shape=[jax.ShapeDtypeStruct((bsz, dil, steps, A_WIDTH), BF16),
                   jax.ShapeDtypeStruct((bsz, dil, steps, LANES), F32)],
        compiler_params=pltpu.CompilerParams(dimension_semantics=("arbitrary",) * 3),
        name=f"attn_prompt_g{g}",
    )(qkv, qkv, qkv, qkv, qkv, bias, bias)


def _attn_sample_kernel(*refs, t_new, splits):
    q0_ref, q1_ref, q2_ref = refs[:3]
    n_parts = sum(splits)
    part_refs = refs[3:3 + n_parts]
    bc0_ref, bc1_ref, bc2_ref, bn0_ref, bn1_ref, bn2_ref = refs[3 + n_parts:9 + n_parts]
    o_ref = refs[-1]
    starts = [sum(splits[:g]) for g in range(N_GROUPS)]
    caches = [part_refs[starts[g]:starts[g] + splits[g]] for g in range(N_GROUPS)]
    n_rows = HEADS * t_new
    row = lax.broadcasted_iota(jnp.int32, (n_rows, A_WIDTH), 0)
    lane = lax.broadcasted_iota(jnp.int32, (n_rows, A_WIDTH), 1)
    head_mask = (row // t_new) == (lane // HEAD_DIM)
    pad = jnp.zeros((LANES - t_new, A_WIDTH), F32)
    outs, lses = [], []
    for qkv_ref, parts, bc_ref, bn_ref in ((q0_ref, caches[0], bc0_ref, bn0_ref),
                                           (q1_ref, caches[1], bc1_ref, bn1_ref),
                                           (q2_ref, caches[2], bc2_ref, bn2_ref)):
        q = qkv_ref[:, :A_WIDTH].astype(F32)
        k_new = qkv_ref[:, A_WIDTH:2 * A_WIDTH].astype(F32)
        v_new = qkv_ref[:, 2 * A_WIDTH:].astype(F32)
        k_new = jnp.concatenate([k_new, pad], axis=0).astype(BF16)
        v_new = jnp.concatenate([v_new, pad], axis=0).astype(BF16)
        q_rows = jnp.where(head_mask, jnp.concatenate([q] * HEADS, axis=0), 0.0).astype(BF16)
        half = len(parts) // 2
        join = lambda rs: rs[0][...].astype(BF16) if len(rs) == 1 else jnp.concatenate(
            [r[...].astype(BF16) for r in rs], axis=0)
        k_buf_t = join(parts[:half])
        v_buf_t = join(parts[half:])
        nt = (((1,), (1,)), ((), ()))
        s_buf = jnp.dot(q_rows, k_buf_t, preferred_element_type=F32) + bc_ref[...]
        s_new = lax.dot_general(q_rows, k_new, nt, preferred_element_type=F32) + bn_ref[...]
        m = jnp.maximum(jnp.max(s_buf, axis=-1, keepdims=True), jnp.max(s_new, axis=-1, keepdims=True))
        p_buf = jnp.exp2(s_buf - m)
        p_new = jnp.exp2(s_new - m)
        den = jnp.sum(p_buf, axis=-1, keepdims=True) + jnp.sum(p_new, axis=-1, keepdims=True)
        o = (lax.dot_general(p_buf.astype(BF16), v_buf_t, nt, preferred_element_type=F32)
             + jnp.dot(p_new.astype(BF16), v_new, preferred_element_type=F32)) / den
        outs.append(o)
        lses.append(m + jnp.log2(den))
    top = jnp.maximum(jnp.maximum(lses[0], lses[1]), lses[2])
    ws = [jnp.exp2(l - top) for l in lses]
    tot = ws[0] + ws[1] + ws[2]
    acc = (ws[0] / tot) * outs[0] + (ws[1] / tot) * outs[1] + (ws[2] / tot) * outs[2]
    acc = jnp.where(head_mask, acc, 0.0).reshape(HEADS, t_new, A_WIDTH)
    o_ref[...] = jnp.sum(acc, axis=0).astype(BF16)


def _sample_bias(g, t_new, buf):
    _, dil = A_GROUPS[g]
    n = N_GROUPS * HEADS
    e = np.arange(1, n + 1, dtype=np.float32)
    slopes = np.exp2(-8.0 * e / n).astype(np.float32).reshape(N_GROUPS, HEADS)[g]
    t = np.arange(t_new)[:, None]
    idx = np.concatenate([np.arange(buf), buf + np.arange(LANES)])[None, :]
    dist = buf + t - idx
    valid = (dist >= 0) & (dist % dil == 0) & (dist <= A_STEPS * dil) & (idx < buf + t_new)
    bias = -slopes[:, None, None] * dist.astype(np.float32)[None] * np.float32(LOG2E)
    bias = np.where(valid[None], bias, NEG).astype(np.float32).reshape(HEADS * t_new, buf + LANES)
    return jnp.asarray(bias[:, :buf]), jnp.asarray(bias[:, buf:])


def _attn_sample(qkvs, caches, *, after):
    bsz, t_new, _ = qkvs[0].shape
    cache_v = [jnp.transpose(c, (0, 2, 3, 4, 1)).reshape(bsz, 2 * A_WIDTH, c.shape[1]) for c in caches]
    biases = [_sample_bias(g, t_new, cache_v[g].shape[2]) for g in range(N_GROUPS)]
    bcs = [b[0] for b in biases]
    bns = [b[1] for b in biases]
    splits = tuple(max(2, min(SAMPLE_CACHE_STREAMS, c.shape[2] * 2 * A_WIDTH * 4 // SAMPLE_STREAM_BYTES))
                   for c in cache_v)
    cache_ins, cache_specs = [], []
    for c, n in zip(cache_v, splits):
        for i in range(n):
            cache_ins.append(c)
            cache_specs.append(pl.BlockSpec((None, 2 * A_WIDTH // n, c.shape[2]), lambda b, i=i: (b, i, 0)))
    return pl.pallas_call(
        functools.partial(_attn_sample_kernel, t_new=t_new, splits=splits),
        grid=(bsz,),
        in_specs=[pl.BlockSpec((None, t_new, 3 * A_WIDTH), lambda b: (b, 0, 0))] * N_GROUPS
                 + cache_specs
                 + [_const_spec(b) for b in bcs] + [_const_spec(b) for b in bns]
                 + [pl.BlockSpec(memory_space=pl.ANY)],
        out_specs=pl.BlockSpec((None, t_new, A_WIDTH), lambda b: (b, 0, 0)),
        out_shape=jax.ShapeDtypeStruct((bsz, t_new, A_WIDTH), BF16),
        compiler_params=pltpu.CompilerParams(dimension_semantics=("arbitrary",),
                                             vmem_limit_bytes=VMEM_LIMIT),
        name="attn_sample",
    )(*qkvs, *cache_ins, *bcs, *bns, after)


def _first_max4(v):
    top = jnp.maximum(jnp.maximum(v[0], v[1]), jnp.maximum(v[2], v[3]))
    idx = jnp.where(v[0] == top, 0.0, jnp.where(v[1] == top, 1.0, jnp.where(v[2] == top, 2.0, 3.0)))
    return top, idx


def _route(lt):
    row = lambda k: lt[k:k + 1, :]
    g = [row(k) for k in range(MOE_GROUPS)]
    gmax, g_idx = _first_max4(g)
    g_prob = 1.0 / (jnp.exp(g[0] - gmax) + jnp.exp(g[1] - gmax) + jnp.exp(g[2] - gmax) + jnp.exp(g[3] - gmax))
    cand = []
    for k in range(MOE_PER_GROUP):
        c = row(ROUTE_OFF + MOE_PER_GROUP * (MOE_GROUPS - 1) + k)
        for gi in range(MOE_GROUPS - 2, -1, -1):
            c = jnp.where(g_idx == float(gi), row(ROUTE_OFF + MOE_PER_GROUP * gi + k), c)
        cand.append(c)
    e1, i1 = _first_max4(cand)
    rest = [jnp.where(i1 == float(k), -jnp.inf, cand[k]) for k in range(MOE_PER_GROUP)]
    e2, i2 = _first_max4(rest)
    t = jnp.exp(e2 - e1)
    w1 = 1.0 / (1.0 + t)
    w2 = t / (1.0 + t)
    return MOE_PER_GROUP * g_idx + i1, MOE_PER_GROUP * g_idx + i2, w1 * g_prob, w2 * g_prob


def _pack_bf16_pairs(x):
    q = x.shape[1] // 4
    bits = lax.bitcast_convert_type(x.astype(BF16).astype(F32), jnp.int32)
    pack = lambda hi, lo: hi | lax.shift_right_logical(lo, 16)
    return pack(bits[:, :q], bits[:, 2 * q:3 * q]), pack(bits[:, q:2 * q], bits[:, 3 * q:])


def _unpack_bf16_pairs(pa, pb):
    hi = lambda p: lax.bitcast_convert_type(p & jnp.int32(-65536), F32)
    lo = lambda p: lax.bitcast_convert_type(lax.shift_left(p, 16), F32)
    return jnp.concatenate([hi(pa), hi(pb), lo(pa), lo(pb)], axis=1)


def _mix_kernel(*refs, tm, dils, routed, span):
    n_attn = len(dils)
    x_ref = refs[0]
    if n_attn > 1:
        o_refs = refs[1:1 + n_attn]
        l_refs = refs[1 + n_attn:1 + 2 * n_attn]
        exp_ref = refs[1 + 2 * n_attn]
        pos = 2 + 2 * n_attn
    else:
        o_refs = refs[1:2]
        pos = 2
    (u_ref, vn_ref, ga_ref, gb_ref, sw_ref, sbt_ref, wpa_ref, wpb_ref, wo_ref,
     gt1_ref, sh2_ref, sc2_ref, n2g_ref, wrh_ref, wrl_ref, br_ref) = refs[pos:pos + 16]
    n_out = 6 if routed else 3
    out_refs = refs[pos + 16:pos + 16 + n_out]
    scr = list(refs[pos + 16 + n_out:])
    x1_ref = out_refs[0]
    ob_scr = scr.pop(0)

    def to_positions(ref, scr, dil):
        if dil == 1:
            return ref[0].astype(F32)
        n_slab = scr.shape[0]
        for r in range(dil):
            rows = ref[r].astype(F32)
            for j in range(n_slab):
                scr[j, pl.ds(r, tm // dil, stride=dil), :] = rows[:, j * LANES:(j + 1) * LANES]
        return jnp.concatenate([scr[j] for j in range(n_slab)], axis=1)

    if n_attn > 1:
        o_scr, l_scr = scr[0], scr[1]
        stats = [to_positions(l_refs[g], l_scr.at[g], dils[g]) for g in range(n_attn)]
        lane = lax.broadcasted_iota(jnp.int32, stats[0].shape, 1)
        is_max = (lane & (LSE_REP - 1)) < LSE_REP // 2
        half = LSE_REP // 2
        maxes = [jnp.where(is_max, s, pltpu.roll(s, half, axis=1)) for s in stats]
        dens = [jnp.where(is_max, pltpu.roll(s, LANES - half, axis=1), s) for s in stats]
        lses = [m + jnp.log2(d) for m, d in zip(maxes, dens)]
        top = functools.reduce(jnp.maximum, lses)
        tot = functools.reduce(lambda a, b: a + b, [jnp.exp2(l - top) for l in lses])
        o_a = None
        for g in range(n_attn):
            w = jnp.exp2(maxes[g] - top) / tot
            hi = w.astype(BF16)
            lo = (w - hi.astype(F32)).astype(BF16)
            w_exp = jnp.dot(jnp.concatenate([hi, lo], axis=1), exp_ref[...], preferred_element_type=F32)
            term = w_exp * to_positions(o_refs[g], o_scr.at[g], dils[g])
            o_a = term if o_a is None else o_a + term
        o_a = o_a.astype(BF16)
    else:
        o_a = o_refs[0][...]

    r_i = lax.broadcasted_iota(jnp.int32, (span, span), 0)
    c_i = lax.broadcasted_iota(jnp.int32, (span, span), 1)
    tril = r_i >= c_i
    for gi in range(B_GROUPS):
        gcols = slice(gi * LANES, (gi + 1) * LANES)
        b_s = sbt_ref[:span, gi:gi + 1]
        if span == B_CHUNK:
            w_s = jnp.where(tril, sw_ref[gi], 0.0).astype(BF16)
            n_chunk = tm // span
            vn_wide = jnp.concatenate([vn_ref[c * span:(c + 1) * span, gcols] for c in range(n_chunk)], axis=1)
            mixed = jnp.dot(w_s, vn_wide, preferred_element_type=F32) + b_s
            for c in range(n_chunk):
                rows = slice(c * span, (c + 1) * span)
                ob_scr[rows, gcols] = (u_ref[rows, gcols].astype(F32)
                                       * mixed[:, c * LANES:(c + 1) * LANES]).astype(BF16)
        else:
            w_s = jnp.where(tril, sw_ref[gi, :span, :span], 0.0)
            vn3 = vn_ref[:, gcols].astype(F32).reshape(tm // span, span, LANES)
            mixed = jnp.broadcast_to(b_s, (span, LANES))[None]
            for s in range(span):
                w_col = jnp.broadcast_to(w_s[:, s:s + 1], (span, LANES))[None]
                mixed = mixed + w_col * jnp.broadcast_to(vn3[:, s:s + 1, :], vn3.shape)
            u3 = u_ref[:, gcols].astype(F32).reshape(tm // span, span, LANES)
            ob_scr[:, gcols] = (u3 * mixed).reshape(tm, LANES).astype(BF16)

    sub = min(tm, MIX_SUB_ROWS)
    if routed:
        carry = scr[-1]
        first_step = (pl.program_id(0) == 0) & (pl.program_id(1) == 0)
        seen = jnp.where(first_step, 0.0, carry[:, 0:1])
        r_i = lax.broadcasted_iota(jnp.int32, (sub, sub), 0)
        c_i = lax.broadcasted_iota(jnp.int32, (sub, sub), 1)
        earlier = jnp.where(r_i < c_i, 1.0, 0.0).astype(BF16)
        e_iota = lax.broadcasted_iota(jnp.int32, (MOE_EXPERTS, sub), 0).astype(F32)
    nt = (((1,), (1,)), ((), ()))
    for r0 in range(0, tm, sub):
        rs = slice(r0, r0 + sub)
        mod = lambda ref: ref[...] if ref.shape[0] == 1 else ref[rs, :]
        pa = jnp.dot(o_a[rs], wpa_ref[...], preferred_element_type=F32)
        pb = jnp.dot(ob_scr[rs, :], wpb_ref[...], preferred_element_type=F32)
        mix = (ga_ref[rs, :].astype(F32) * pa + gb_ref[rs, :].astype(F32) * pb).astype(BF16)
        x1 = x_ref[rs, :] + mod(gt1_ref) * jnp.dot(mix, wo_ref[...], preferred_element_type=F32)
        x1_ref[rs, :] = x1
        h2 = _rmsnorm_mod(x1, n2g_ref[...], mod(sh2_ref), mod(sc2_ref))
        h_hi = h2.astype(BF16)
        h_lo = (h2 - h_hi.astype(F32)).astype(BF16)
        lt = (lax.dot_general(wrh_ref[...], h_hi, nt, preferred_element_type=F32)
              + lax.dot_general(wrh_ref[...], h_lo, nt, preferred_element_type=F32)
              + lax.dot_general(wrl_ref[...], h_hi, nt, preferred_element_type=F32)) + br_ref[...]
        ex1, ex2, w1, w2 = _route(lt)
        if not routed:
            _, h2_ref, comb_ref = out_refs
            h2_ref[rs, :] = h_hi
            e_lane = lax.broadcasted_iota(jnp.int32, (LANES, sub), 0).astype(F32) - float(ROUTE_OFF)
            comb_t = jnp.where(e_lane == ex1, w1, jnp.where(e_lane == ex2, w2, 0.0))
            comb_ref[rs, :] = jnp.transpose(comb_t)
            continue

        _, hpa_ref, hpb_ref, route_ref, wcol_ref, cnt_ref = out_refs
        hpa_ref[rs, :], hpb_ref[rs, :] = _pack_bf16_pairs(h2)
        hot1, hot2 = e_iota == ex1, e_iota == ex2
        onehot = jnp.where(hot1 | hot2, 1.0, 0.0)
        rank = jnp.dot(onehot.astype(BF16), earlier, preferred_element_type=F32) + seen
        r1 = jnp.sum(jnp.where(hot1, rank, 0.0), axis=0, keepdims=True)
        r2 = jnp.sum(jnp.where(hot2, rank, 0.0), axis=0, keepdims=True)
        seen = seen + jnp.sum(onehot, axis=1, keepdims=True)
        fields = jnp.concatenate([ex1, ex2, w1, w2, r1, r2, jnp.zeros((2, sub), F32)], axis=0)
        route_ref[:, rs] = fields
        wcol_ref[rs, :] = jnp.transpose(jnp.concatenate([fields, jnp.zeros((LANES - 8, sub), F32)], axis=0))
    if routed:
        carry[...] = jnp.broadcast_to(seen, carry.shape)
        cnt_ref[...] = jnp.broadcast_to(seen, cnt_ref.shape)


def _expand_matrix():
    e = np.zeros((LANES, A_WIDTH), np.float32)
    for h in range(HEADS):
        e[h * LSE_REP, h * HEAD_DIM:(h + 1) * HEAD_DIM] = 1.0
    return jnp.asarray(np.concatenate([e, e], axis=0), dtype=BF16)


def _mix(x, attn_outs, attn_lses, u, vn, ga, gb, sgu_w, sgu_bt, w_pa, w_pb, w_o,
         gt1, sh2, sc2, n2g, w_r_hi, w_r_lo, b_r, *, tm, routed, span):
    bsz, seq, _ = x.shape
    n_attn = len(attn_outs)
    tok = lambda width: pl.BlockSpec((None, tm, width), lambda b, s: (b, s, 0))
    ins = [x] + list(attn_outs)
    scratch = [pltpu.VMEM((tm, B_WIDTH), BF16)]
    if n_attn > 1:
        dils = tuple(o.shape[1] for o in attn_outs)
        res = lambda dil, width: pl.BlockSpec((None, dil, tm // dil, width), lambda b, s: (b, 0, s, 0))
        e2 = _expand_matrix()
        ins += list(attn_lses) + [e2]
        specs = ([tok(D_MODEL)] + [res(d, A_WIDTH) for d in dils] + [res(d, LANES) for d in dils]
                 + [_const_spec(e2)])
        scratch += [pltpu.VMEM((n_attn, A_WIDTH // LANES, tm, LANES), F32), pltpu.VMEM((n_attn, 1, tm, LANES), F32)]
    else:
        dils = (1,)
        specs = [tok(D_MODEL), tok(A_WIDTH)]
    ins += [u, vn, ga, gb, sgu_w, sgu_bt, w_pa, w_pb, w_o, gt1, sh2, sc2, n2g, w_r_hi, w_r_lo, b_r]
    specs += [tok(B_WIDTH), tok(B_WIDTH), tok(D_MODEL), tok(D_MODEL),
              _const_spec(sgu_w), _const_spec(sgu_bt), _const_spec(w_pa), _const_spec(w_pb), _const_spec(w_o),
              _mod_spec(gt1, tm), _mod_spec(sh2, tm), _mod_spec(sc2, tm), _const_spec(n2g),
              _const_spec(w_r_hi), _const_spec(w_r_lo), _const_spec(b_r)]
    if routed:
        quarter = D_MODEL // 4
        out_specs = [tok(D_MODEL), tok(quarter), tok(quarter),
                     pl.BlockSpec((None, 8, tm), lambda b, s: (b, 0, s)), tok(LANES),
                     pl.BlockSpec((MOE_EXPERTS, LANES), lambda b, s: (0, 0))]
        out_shape = [jax.ShapeDtypeStruct((bsz, seq, D_MODEL), F32),
                     jax.ShapeDtypeStruct((bsz, seq, quarter), jnp.int32),
                     jax.ShapeDtypeStruct((bsz, seq, quarter), jnp.int32),
                     jax.ShapeDtypeStruct((bsz, 8, seq), F32),
                     jax.ShapeDtypeStruct((bsz, seq, LANES), F32),
                     jax.ShapeDtypeStruct((MOE_EXPERTS, LANES), F32)]
        scratch.append(pltpu.VMEM((MOE_EXPERTS, LANES), F32))
    else:
        out_specs = [tok(D_MODEL), tok(D_MODEL), tok(LANES)]
        out_shape = [jax.ShapeDtypeStruct((bsz, seq, D_MODEL), F32),
                     jax.ShapeDtypeStruct((bsz, seq, D_MODEL), BF16),
                     jax.ShapeDtypeStruct((bsz, seq, LANES), F32)]
    return pl.pallas_call(
        functools.partial(_mix_kernel, tm=tm, dils=dils, routed=routed, span=span),
        grid=(bsz, seq // tm),
        in_specs=specs,
        out_specs=out_specs,
        out_shape=out_shape,
        scratch_shapes=scratch,
        compiler_params=pltpu.CompilerParams(dimension_semantics=("arbitrary", "arbitrary"),
                                             vmem_limit_bytes=VMEM_LIMIT),
        name="mix",
    )(*ins)


def _moe_kernel(h_ref, comb_ref, wg_ref, wu_ref, wd_ref, x1_ref, gt2_ref, shf_ref, scf_ref, nfg_ref,
                y_ref, acc_ref):
    e = pl.program_id(2)

    @pl.when(e == 0)
    def _():
        acc_ref[...] = jnp.zeros_like(acc_ref)

    h = h_ref[...]
    hg = jnp.dot(h, wg_ref[...].astype(BF16), preferred_element_type=F32)
    hu = jnp.dot(h, wu_ref[...].astype(BF16), preferred_element_type=F32)
    comb = comb_ref[...]
    lane = lax.broadcasted_iota(jnp.int32, comb.shape, 1)
    cw = jnp.sum(jnp.where(lane == e + ROUTE_OFF, comb, 0.0), axis=-1, keepdims=True)
    act = (hg * _sigmoid(hg) * hu) * cw
    acc_ref[...] += jnp.dot(act.astype(BF16), wd_ref[...].astype(BF16), preferred_element_type=F32)

    @pl.when(e == MOE_EXPERTS - 1)
    def _():
        x2 = x1_ref[...] + gt2_ref[...] * acc_ref[...]
        y_ref[...] = _rmsnorm_mod(x2, nfg_ref[...], shf_ref[...], scf_ref[...])


def _moe(h2, comb, w_gate, w_up, w_down, x1, gt2, shf, scf, nfg, *, tm):
    bsz, seq, _ = h2.shape
    tok = lambda width: pl.BlockSpec((None, tm, width), lambda b, s, e: (b, s, 0))
    return pl.pallas_call(
        _moe_kernel,
        grid=(bsz, seq // tm, MOE_EXPERTS),
        in_specs=[tok(D_MODEL), tok(LANES),
                  pl.BlockSpec((None, D_MODEL, MOE_HIDDEN), lambda b, s, e: (e, 0, 0)),
                  pl.BlockSpec((None, D_MODEL, MOE_HIDDEN), lambda b, s, e: (e, 0, 0)),
                  pl.BlockSpec((None, MOE_HIDDEN, D_MODEL), lambda b, s, e: (e, 0, 0)),
                  tok(D_MODEL), _mod_spec(gt2, tm), _mod_spec(shf, tm), _mod_spec(scf, tm), _const_spec(nfg)],
        out_specs=tok(D_MODEL),
        out_shape=jax.ShapeDtypeStruct((bsz, seq, D_MODEL), F32),
        scratch_shapes=[pltpu.VMEM((tm, D_MODEL), F32)],
        compiler_params=pltpu.CompilerParams(dimension_semantics=("arbitrary",) * 3,
                                             vmem_limit_bytes=VMEM_LIMIT),
        name="moe",
    )(h2, comb, w_gate, w_up, w_down, x1, gt2, shf, scf, nfg)


def _sc_mesh():
    return plsc.VectorSubcoreMesh(core_axis_name="c", subcore_axis_name="s")


def _sc_scatter_rows(rows, pos1, pos2, n_out):
    n, width = rows.shape
    steps = n // SC_WINDOW // SC_CORES

    @pl.kernel(out_type=jax.ShapeDtypeStruct((n_out, width), rows.dtype), mesh=_sc_mesh(), scratch_types=[])
    def scatter(rows_hbm, p1_hbm, p2_hbm, out_hbm):
        def body(x_vmem, i1_vmem, i2_vmem):
            pltpu.sync_copy(x_vmem, out_hbm.at[i1_vmem.at[0]])
            pltpu.sync_copy(x_vmem, out_hbm.at[i2_vmem.at[0]])

        pltpu.emit_pipeline(
            body, grid=(SC_CORES, steps),
            in_specs=[pl.BlockSpec((SC_WINDOW, width), lambda c, i: (c * steps + i, 0)),
                      pl.BlockSpec((1, SC_WINDOW), lambda c, i: (0, c * steps + i)),
                      pl.BlockSpec((1, SC_WINDOW), lambda c, i: (0, c * steps + i))],
            out_specs=[],
            core_axis_name=("c", "s"),
            dimension_semantics=(pltpu.PARALLEL, pltpu.PARALLEL),
        )(rows_hbm, p1_hbm, p2_hbm)

    return scatter(rows, pos1, pos2)


def _sc_gather_rows(table, idx):
    m, width = idx.shape[1], table.shape[1]
    steps = m // SC_WINDOW // SC_CORES

    @pl.kernel(out_type=jax.ShapeDtypeStruct((m, width), table.dtype), mesh=_sc_mesh(), scratch_types=[])
    def gather(table_hbm, idx_hbm, out_hbm):
        def body(i_vmem, o_vmem):
            pltpu.sync_copy(table_hbm.at[i_vmem.at[0]], o_vmem)

        pltpu.emit_pipeline(
            body, grid=(SC_CORES, steps),
            in_specs=[pl.BlockSpec((1, SC_WINDOW), lambda c, i: (0, c * steps + i))],
            out_specs=[pl.BlockSpec((SC_WINDOW, width), lambda c, i: (c * steps + i, 0))],
            core_axis_name=("c", "s"),
            dimension_semantics=(pltpu.PARALLEL, pltpu.PARALLEL),
        )(idx_hbm, out_hbm)

    return gather(table, idx)


def _ffn_kernel(te_ref, nt_ref, xa_ref, xb_ref, wg_ref, wu_ref, wd_ref, after_ref, oa_ref, ob_ref,
                wg_s, wu_s, wd_s):
    del after_ref
    t = pl.program_id(0)
    live = t < nt_ref[0]

    @pl.when(live & ((t == 0) | (te_ref[t] != te_ref[jnp.maximum(t - 1, 0)])))
    def _():
        wg_s[...] = wg_ref[...].astype(BF16)
        wu_s[...] = wu_ref[...].astype(BF16)
        wd_s[...] = wd_ref[...].astype(BF16)

    @pl.when(live)
    def _():
        x = _unpack_bf16_pairs(xa_ref[...], xb_ref[...]).astype(BF16)
        hg = jnp.dot(x, wg_s[...], preferred_element_type=F32)
        hu = jnp.dot(x, wu_s[...], preferred_element_type=F32)
        act = (hg * _sigmoid(hg) * hu).astype(BF16)
        oa_ref[...], ob_ref[...] = _pack_bf16_pairs(jnp.dot(act, wd_s[...], preferred_element_type=F32))


def _ffn(xa, xb, tile_expert, n_tiles, w_gate, w_up, w_down, *, after):
    n_rows, quarter = xa.shape
    rows = pl.BlockSpec((MOE_TILE, quarter), lambda t, te, nt: (jnp.minimum(t, nt[0] - 1), 0))
    return pl.pallas_call(
        _ffn_kernel,
        grid_spec=pltpu.PrefetchScalarGridSpec(
            num_scalar_prefetch=2,
            grid=(n_rows // MOE_TILE,),
            in_specs=[rows, rows,
                      pl.BlockSpec((None, D_MODEL, MOE_HIDDEN), lambda t, te, nt: (te[t], 0, 0)),
                      pl.BlockSpec((None, D_MODEL, MOE_HIDDEN), lambda t, te, nt: (te[t], 0, 0)),
                      pl.BlockSpec((None, MOE_HIDDEN, D_MODEL), lambda t, te, nt: (te[t], 0, 0)),
                      pl.BlockSpec(memory_space=pl.ANY)],
            out_specs=[rows, rows],
            scratch_shapes=[pltpu.VMEM((D_MODEL, MOE_HIDDEN), BF16), pltpu.VMEM((D_MODEL, MOE_HIDDEN), BF16),
                            pltpu.VMEM((MOE_HIDDEN, D_MODEL), BF16)]),
        out_shape=[jax.ShapeDtypeStruct((n_rows, quarter), jnp.int32)] * 2,
        compiler_params=pltpu.CompilerParams(dimension_semantics=("arbitrary",), vmem_limit_bytes=VMEM_LIMIT),
        name="moe_ffn",
    )(tile_expert, n_tiles, xa, xb, w_gate, w_up, w_down, after)


def _final_kernel(x1_ref, g1a_ref, g1b_ref, g2a_ref, g2b_ref, route_ref, gt2_ref, shf_ref, scf_ref, nfg_ref, y_ref):
    o1 = _unpack_bf16_pairs(g1a_ref[...], g1b_ref[...])
    o2 = _unpack_bf16_pairs(g2a_ref[...], g2b_ref[...])
    route = route_ref[...]
    moe = route[:, 2:3] * o1 + route[:, 3:4] * o2
    x2 = x1_ref[...] + gt2_ref[...] * moe
    y_ref[...] = _rmsnorm_mod(x2, nfg_ref[...], shf_ref[...], scf_ref[...])


def _final(x1, ga, gb, route, gt2, shf, scf, nfg, *, tm):
    bsz, seq, _ = x1.shape
    n_s = seq // tm
    n_blk = bsz * n_s
    quarter = ga.shape[1]
    tok = lambda width: pl.BlockSpec((None, tm, width), lambda b, s: (b, s, 0))
    first = pl.BlockSpec((tm, quarter), lambda b, s: (b * n_s + s, 0))
    second = pl.BlockSpec((tm, quarter), lambda b, s: (n_blk + b * n_s + s, 0))
    return pl.pallas_call(
        _final_kernel,
        grid=(bsz, n_s),
        in_specs=[tok(D_MODEL), first, first, second, second, tok(LANES),
                  _mod_spec(gt2, tm), _mod_spec(shf, tm), _mod_spec(scf, tm), _const_spec(nfg)],
        out_specs=tok(D_MODEL),
        out_shape=jax.ShapeDtypeStruct((bsz, seq, D_MODEL), F32),
        compiler_params=pltpu.CompilerParams(dimension_semantics=("arbitrary", "arbitrary")),
        name="moe_final",
    )(x1, ga, gb, ga, gb, route, gt2, shf, scf, nfg)


def _positions_kernel(base_ref, route_ref, p1_ref, p2_ref):
    rows = route_ref[...]
    for e_row, r_row, out_ref in ((0, 4, p1_ref), (1, 5, p2_ref)):
        expert = rows[e_row:e_row + 1, :]
        start = jnp.zeros(expert.shape, jnp.int32)
        for e in range(MOE_EXPERTS):
            start = jnp.where(expert == float(e), base_ref[e], start)
        out_ref[...] = start + rows[r_row:r_row + 1, :].astype(jnp.int32)


def _positions(route_rows, base):
    bsz, _, seq = route_rows.shape
    out = pl.BlockSpec((None, 1, seq), lambda b, base: (b, 0, 0))
    return pl.pallas_call(
        _positions_kernel,
        grid_spec=pltpu.PrefetchScalarGridSpec(
            num_scalar_prefetch=1, grid=(bsz,),
            in_specs=[pl.BlockSpec((None, 8, seq), lambda b, base: (b, 0, 0))],
            out_specs=[out, out]),
        out_shape=[jax.ShapeDtypeStruct((bsz, 1, seq), jnp.int32)] * 2,
        compiler_params=pltpu.CompilerParams(dimension_semantics=("arbitrary",)),
        name="moe_positions",
    )(base, route_rows)


def _moe_dispatch(hpa, hpb, route_rows, counts):
    bsz, seq, quarter = hpa.shape
    n_tok = bsz * seq
    t_max = 2 * n_tok // MOE_TILE + MOE_EXPERTS
    cnt = counts[:, 0].astype(jnp.int32)
    tiles_e = (cnt + MOE_TILE - 1) // MOE_TILE
    tile_end = jnp.cumsum(tiles_e)
    n_tiles = tile_end[-1:]
    base = (tile_end - tiles_e) * MOE_TILE
    t_ids = jnp.minimum(jnp.arange(t_max, dtype=jnp.int32), n_tiles[0] - 1)
    tile_expert = jnp.sum((t_ids[:, None] >= tile_end[None, :]).astype(jnp.int32), axis=1)
    pos1, pos2 = (p.reshape(1, n_tok) for p in _positions(route_rows, base))
    n_rows = t_max * MOE_TILE
    xa = _sc_scatter_rows(hpa.reshape(n_tok, quarter), pos1, pos2, n_rows)
    xb = _sc_scatter_rows(hpb.reshape(n_tok, quarter), pos1, pos2, n_rows)
    return xa, xb, jnp.concatenate([pos1, pos2], axis=1), tile_expert, n_tiles


def kernel(x_prompt, x_sample, cache_kv_w128, cache_kv_w512, cache_kv_w2048, c_prompt, c_sample,
           w_ada, b_ada, norm1_g, norm2_g, w_in, sgu_ln_g, sgu_ln_b, sgu_w, sgu_b, w_pa, w_pb, w_o,
           w_route_group, b_route_group, w_route_expert, b_route_expert, w_gate, w_up, w_down,
           normf_g, w_ada_final, b_ada_final):
    depth = w_ada.shape[0]
    assert depth == 1
    l = 0
    bp, seq, _ = x_prompt.shape
    bs, t_new, _ = x_sample.shape
    n_samp = bs * t_new

    c_all = jnp.concatenate([c_prompt, c_sample], axis=0)
    pad_rows = (-c_all.shape[0]) % 8
    c_all = jnp.pad(c_all, ((0, pad_rows), (0, 0)))
    mod = _adaln(c_all, w_ada[l], b_ada[l])
    mod_f = _adaln(c_all, w_ada_final, b_ada_final)

    def split_mods(m, n, lo, hi, per_token):
        parts = jnp.split(m[lo:hi], n, axis=-1)
        if per_token:
            return [jnp.repeat(p, t_new, axis=0).reshape(1, n_samp, D_MODEL) for p in parts]
        return [p.reshape(hi - lo, 1, D_MODEL) for p in parts]

    mods_p = split_mods(mod, 6, 0, bp, False) + split_mods(mod_f, 2, 0, bp, False)
    mods_s = split_mods(mod, 6, bp, bp + bs, True) + split_mods(mod_f, 2, bp, bp + bs, True)

    row = lambda v: v.reshape(1, -1)
    w_in_b = w_in[l].astype(BF16)
    w_pa_b, w_pb_b, w_o_b = w_pa[l].astype(BF16), w_pb[l].astype(BF16), w_o[l].astype(BF16)
    w_gate_b, w_up_b, w_down_b = w_gate[l], w_up[l], w_down[l]
    w_re = jnp.transpose(w_route_expert[l], (0, 2, 1)).reshape(MOE_EXPERTS, D_MODEL)
    w_r = jnp.pad(jnp.concatenate([jnp.transpose(w_route_group[l]), w_re], axis=0),
                  ((0, LANES - MOE_GROUPS - MOE_EXPERTS), (0, 0)))
    w_r_hi = w_r.astype(BF16)
    w_r_lo = (w_r - w_r_hi.astype(F32)).astype(BF16)
    b_r = jnp.pad(jnp.concatenate([b_route_group[l], b_route_expert[l].reshape(-1)]),
                  (0, LANES - MOE_GROUPS - MOE_EXPERTS)).reshape(LANES, 1)

    def inproj(x, mods, keeps, dils, tm, emit_vn_f32, after=None):
        return _inproj(x, mods[0], mods[1], row(norm1_g[l]), w_in_b, row(sgu_ln_g[l]), row(sgu_ln_b[l]),
                       tm=tm, keeps=keeps, dils=dils, emit_vn_f32=emit_vn_f32, after=after)

    sgu_bt = jnp.transpose(sgu_b[l])

    def mix(x, mods, pin, attn_outs, attn_lses, tm, routed, span):
        u, vn, ga, gb = pin[3:7]
        return _mix(x, attn_outs, attn_lses, u, vn, ga, gb, sgu_w[l], sgu_bt,
                    w_pa_b, w_pb_b, w_o_b, mods[2], mods[3], mods[4], row(norm2_g[l]), w_r_hi, w_r_lo, b_r,
                    tm=tm, routed=routed, span=span)

    tm_p = 512
    keeps_p = tuple(min(win, seq) for win, _ in A_GROUPS)
    dils_p = tuple(dil for _, dil in A_GROUPS)
    pin = inproj(x_prompt, mods_p, keeps_p, dils_p, 256, False)
    attn_p = [_attn_prompt(pin[g], g) for g in range(N_GROUPS)]
    x1, hpa, hpb, route_rows, route, counts = mix(x_prompt, mods_p, pin, [r[0] for r in attn_p],
                                                  [r[1] for r in attn_p], tm_p, True, B_CHUNK)

    xs = x_sample.reshape(1, n_samp, D_MODEL)
    sin = inproj(xs, mods_s, (n_samp,) * N_GROUPS, (1,) * N_GROUPS, n_samp, True, after=counts)

    xa, xb, pos, tile_expert, n_tiles = _moe_dispatch(hpa, hpb, route_rows, counts)
    oa, ob = _ffn(xa, xb, tile_expert, n_tiles, w_gate_b, w_up_b, w_down_b, after=sin[0])

    caches = (cache_kv_w128[l], cache_kv_w512[l], cache_kv_w2048[l])
    o_s = _attn_sample([q.reshape(bs, t_new, 3 * A_WIDTH) for q in sin[:3]], caches, after=oa)
    x1_s, h2_s, comb_s = mix(xs, mods_s, sin, [o_s.reshape(1, n_samp, A_WIDTH)], None, n_samp, False,
                             min(t_new, B_CHUNK))
    y_s = _moe(h2_s, comb_s, w_gate_b, w_up_b, w_down_b, x1_s, mods_s[5], mods_s[6], mods_s[7], row(normf_g),
               tm=n_samp)

    ga_rows = _sc_gather_rows(oa, pos)
    gb_rows = _sc_gather_rows(ob, pos)
    y_p = _final(x1, ga_rows, gb_rows, route, mods_p[5], mods_p[6], mods_p[7], row(normf_g), tm=tm_p)
    kv_p, kv_s, extra = pin[7:10], sin[7:10], sin[10:]

    def kv_out(a, b):
        return a.reshape(depth, b, -1, 2, HEADS, HEAD_DIM)

    return (y_p, y_s.reshape(bs, t_new, D_MODEL),
            kv_out(kv_p[0], bp), kv_out(kv_p[1], bp), kv_out(kv_p[2], bp),
            kv_out(kv_s[0], bs), kv_out(kv_s[1], bs), kv_out(kv_s[2], bs),
            extra[0].reshape(depth, bs, t_new, B_WIDTH))
```

```python
import functools

import numpy as np
import jax
import jax.numpy as jnp
from jax import lax
from jax.experimental import pallas as pl
from jax.experimental.pallas import tpu as pltpu
from jax.experimental.pallas import tpu_sc as plsc

F32 = jnp.float32
BF16 = jnp.bfloat16

D_MODEL = 1024
A_GROUPS = ((128, 1), (512, 4), (2048, 16))
N_GROUPS = 3
HEADS = 8
HEAD_DIM = 64
A_WIDTH = HEADS * HEAD_DIM
A_STEPS = 128
A_BLOCK = 128
B_WIDTH = 1024
B_GROUPS = 8
B_CHUNK = 128
MOE_GROUPS = 4
MOE_PER_GROUP = 4
MOE_EXPERTS = 16
MOE_HIDDEN = 512
EPS = 1e-6
N_QKV = 3 * N_GROUPS * A_WIDTH
IN_COLS = N_QKV + 2 * B_WIDTH + 2 * D_MODEL
COL_CHUNK = 512
LANES = 128
LSE_REP = LANES // HEADS
ROUTE_OFF = MOE_GROUPS
NEG = -1e30
LOG2E = 1.4426950408889634
VMEM_LIMIT = 56 * 1024 * 1024
ATTN_BLOCKS_PER_STEP = 16
MIX_SUB_ROWS = 512
MOE_TILE = 512
SC_WINDOW = 128
SC_CORES = 2


def _sigmoid(x):
    return 1.0 / (1.0 + jnp.exp(-x))


def _gelu_tanh(x):
    return x * (0.5 * (1.0 + jnp.tanh(0.7978845608028654 * (x + 0.044715 * (x * x * x)))))


def _rmsnorm_mod(x, gain, shift, scale):
    y = x * lax.rsqrt(jnp.mean(x * x, axis=-1, keepdims=True) + EPS)
    return y * gain * (1.0 + scale) + shift


def _mod_spec(mod, tm):
    if mod.shape[1] == 1:
        return pl.BlockSpec((None, 1, D_MODEL), lambda b, s, *_: (b, 0, 0))
    return pl.BlockSpec((None, tm, D_MODEL), lambda b, s, *_: (b, s, 0))


def _const_spec(arr):
    nd = arr.ndim
    return pl.BlockSpec(arr.shape, lambda *_: (0,) * nd)


def _adaln_kernel(c_ref, w_ref, b_ref, o_ref):
    c = c_ref[...]
    a = (c * _sigmoid(c)).astype(BF16)
    o_ref[...] = jnp.dot(a, w_ref[...].astype(BF16), preferred_element_type=F32) + b_ref[...]


def _adaln(c, w, b, tn=1024):
    rows, ncols = c.shape[0], w.shape[1]
    return pl.pallas_call(
        _adaln_kernel,
        grid=(ncols // tn,),
        in_specs=[pl.BlockSpec((rows, D_MODEL), lambda j: (0, 0)),
                  pl.BlockSpec((D_MODEL, tn), lambda j: (0, j)),
                  pl.BlockSpec((1, tn), lambda j: (0, j))],
        out_specs=pl.BlockSpec((rows, tn), lambda j: (0, j)),
        out_shape=jax.ShapeDtypeStruct((rows, ncols), F32),
        compiler_params=pltpu.CompilerParams(dimension_semantics=("arbitrary",)),
        name="adaln",
    )(c, w, b.reshape(1, ncols))


def _inproj_kernel(x_ref, sh_ref, sc_ref, g_ref, w_ref, lng_ref, lnb_ref, *refs, tm, keeps, dils, emit_vn_f32,
                   has_after, kv_feature_major):
    if has_after:
        refs = refs[1:]
    qkv0_ref, qkv1_ref, qkv2_ref, u_ref, vn_ref, ga_ref, gb_ref, kv0_ref, kv1_ref, kv2_ref = refs[:10]
    rest = refs[10:]
    perm_scr = rest[-1]
    h = _rmsnorm_mod(x_ref[...], g_ref[...], sh_ref[...], sc_ref[...]).astype(BF16)

    def proj(c):
        return jnp.dot(h, w_ref[:, c * COL_CHUNK:(c + 1) * COL_CHUNK], preferred_element_type=F32)

    def cols(c):
        return slice(c * COL_CHUNK, (c + 1) * COL_CHUNK)

    qkv_refs = (qkv0_ref, qkv1_ref, qkv2_ref)

    def put_qkv(g, t, z):
        dil = dils[g]
        if dil == 1:
            qkv_refs[g][0, :, cols(t)] = z.astype(BF16)
            return
        slot = perm_scr.at[t % 2]
        n_slab = COL_CHUNK // LANES
        for j in range(n_slab):
            slot[j] = z[:, j * LANES:(j + 1) * LANES]
        for r in range(dil):
            rows = jnp.concatenate([slot[j, pl.ds(r, tm // dil, stride=dil), :] for j in range(n_slab)], axis=1)
            qkv_refs[g][r, :, cols(t)] = rows.astype(BF16)

    for g in range(N_GROUPS):
        put_qkv(g, 0, proj(g) * (HEAD_DIM ** -0.5 * LOG2E))

    kv_refs = (kv0_ref, kv1_ref, kv2_ref)
    for g in range(N_GROUPS):
        keep = keeps[g]
        for t in (1, 2):
            z = proj(t * N_GROUPS + g)
            put_qkv(g, t, z)
            if kv_feature_major:
                zt = jnp.transpose(z if keep >= tm else z[tm - keep:, :])
                kv_refs[g][(t - 1) * A_WIDTH:t * A_WIDTH, :] = zt
            else:
                kv_refs[g][:, cols(t - 1)] = z if keep >= tm else z[tm - keep:, :]

    base = N_QKV // COL_CHUNK
    for c in range(2):
        u_ref[:, cols(c)] = _gelu_tanh(proj(base + c)).astype(BF16)

    vs = [_gelu_tanh(proj(base + 2 + c)) for c in range(2)]
    mu = (jnp.sum(vs[0], axis=-1, keepdims=True) + jnp.sum(vs[1], axis=-1, keepdims=True)) * (1.0 / B_WIDTH)
    ds = [v - mu for v in vs]
    var = (jnp.sum(ds[0] * ds[0], axis=-1, keepdims=True)
           + jnp.sum(ds[1] * ds[1], axis=-1, keepdims=True)) * (1.0 / B_WIDTH)
    inv = lax.rsqrt(var + EPS)
    for c in range(2):
        vn = ds[c] * inv * lng_ref[:, cols(c)] + lnb_ref[:, cols(c)]
        vn_ref[:, cols(c)] = vn.astype(BF16)
        if emit_vn_f32:
            rest[0][:, cols(c)] = vn

    for c in range(2):
        ga_ref[:, cols(c)] = _sigmoid(proj(base + 4 + c)).astype(BF16)
        gb_ref[:, cols(c)] = _sigmoid(proj(base + 6 + c)).astype(BF16)


def _inproj(x, sh, sc, gain, w_in, ln_g, ln_b, *, tm, keeps, dils, emit_vn_f32, kv_feature_major, after=None):
    bsz, seq, _ = x.shape
    n_s = seq // tm
    tok = lambda width: pl.BlockSpec((None, tm, width), lambda b, s: (b, s, 0))

    def kv_spec(keep):
        first = n_s - max(keep // tm, 1)
        rows = min(keep, tm)
        if kv_feature_major:
            return pl.BlockSpec((None, 2 * A_WIDTH, rows), lambda b, s: (b, 0, jnp.maximum(s - first, 0)))
        return pl.BlockSpec((None, rows, 2 * A_WIDTH), lambda b, s: (b, jnp.maximum(s - first, 0), 0))

    out_specs, out_shape = [], []
    for dil in dils:
        out_specs.append(pl.BlockSpec((None, dil, tm // dil, 3 * A_WIDTH), lambda b, s: (b, 0, s, 0)))
        out_shape.append(jax.ShapeDtypeStruct((bsz, dil, seq // dil, 3 * A_WIDTH), BF16))
    out_specs += [tok(B_WIDTH), tok(B_WIDTH), tok(D_MODEL), tok(D_MODEL)]
    out_shape += [jax.ShapeDtypeStruct((bsz, seq, D_MODEL), BF16)] * 4
    for keep in keeps:
        out_specs.append(kv_spec(keep))
        kv_shape = (bsz, 2 * A_WIDTH, keep) if kv_feature_major else (bsz, keep, 2 * A_WIDTH)
        out_shape.append(jax.ShapeDtypeStruct(kv_shape, F32))
    if emit_vn_f32:
        out_specs.append(tok(B_WIDTH))
        out_shape.append(jax.ShapeDtypeStruct((bsz, seq, B_WIDTH), F32))

    ins = [x, sh, sc, gain, w_in, ln_g, ln_b]
    in_specs = [tok(D_MODEL), _mod_spec(sh, tm), _mod_spec(sc, tm), _const_spec(gain),
                pl.BlockSpec(w_in.shape, lambda b, s: (0, 0), pipeline_mode=pl.Buffered(1)),
                _const_spec(ln_g), _const_spec(ln_b)]
    if after is not None:
        ins.append(after)
        in_specs.append(pl.BlockSpec(memory_space=pl.ANY))
    scratch = [pltpu.VMEM((2, COL_CHUNK // LANES, tm, LANES), F32)]
    return pl.pallas_call(
        functools.partial(_inproj_kernel, tm=tm, keeps=keeps, dils=dils, emit_vn_f32=emit_vn_f32,
                          has_after=after is not None, kv_feature_major=kv_feature_major),
        grid=(bsz, n_s),
        in_specs=in_specs,
        out_specs=out_specs,
        out_shape=out_shape,
        scratch_shapes=scratch,
        compiler_params=pltpu.CompilerParams(dimension_semantics=("arbitrary", "arbitrary"),
                                             vmem_limit_bytes=VMEM_LIMIT),
        name="inproj",
    )(*ins)


def _attn_kernel(q_ref, kp_ref, kc_ref, vp_ref, vc_ref, bias0_ref, bias_ref, o_ref, lse_ref, *, n_seq, q_blocks):
    lane = lax.broadcasted_iota(jnp.int32, (A_BLOCK, LANES), 1)
    low = lane < HEAD_DIM
    zero = jnp.zeros((), BF16)
    for seq_i, i in [(a, b) for a in range(n_seq) for b in range(q_blocks)]:
        rows = (seq_i, slice(i * A_BLOCK, (i + 1) * A_BLOCK))
        q = q_ref[rows[0], rows[1], :]
        key_rows = slice((i - 1) * A_BLOCK, (i + 1) * A_BLOCK)
        if i == 0:
            k = jnp.concatenate([kp_ref[seq_i], kc_ref[seq_i, :A_BLOCK, :]], axis=0)
            v = jnp.concatenate([vp_ref[seq_i], vc_ref[seq_i, :A_BLOCK, :]], axis=0)
        else:
            k, v = kc_ref[seq_i, key_rows, :], vc_ref[seq_i, key_rows, :]
        b_ref = bias0_ref if i == 0 else bias_ref
        lse_tile = jnp.zeros((A_BLOCK, LANES), F32)
        for j in range(HEADS // 2):
            pair = slice(j * LANES, (j + 1) * LANES)
            qp, kpair, vpair = q[:, pair], k[:, pair], v[:, pair]
            outs = []
            for e in range(2):
                h = 2 * j + e
                qm = jnp.where(low if e == 0 else jnp.logical_not(low), qp, zero)
                s = lax.dot_general(qm, kpair, (((1,), (1,)), ((), ())), preferred_element_type=F32)
                s = s + b_ref[h]
                m = jnp.max(s, axis=-1, keepdims=True)
                p = jnp.exp2(s - m)
                den = jnp.sum(p, axis=-1, keepdims=True)
                o = jnp.dot(p.astype(BF16), vpair, preferred_element_type=F32)
                first = h * LSE_REP
                lse_tile = jnp.where((lane >= first) & (lane < first + LSE_REP // 2), m,
                                     jnp.where((lane >= first + LSE_REP // 2) & (lane < first + LSE_REP), den,
                                               lse_tile))
                outs.append(o)
            o_ref[rows[0], rows[1], pair] = jnp.where(low, outs[0], outs[1]).astype(BF16)
        lse_ref[rows[0], rows[1], :] = lse_tile


def _attn_bias(g):
    _, dil = A_GROUPS[g]
    n = N_GROUPS * HEADS
    e = np.arange(1, n + 1, dtype=np.float32)
    slopes = np.exp2(-8.0 * e / n).astype(np.float32).reshape(N_GROUPS, HEADS)[g]
    qi = np.arange(A_BLOCK)[:, None]
    ki = np.arange(2 * A_BLOCK)[None, :]
    delta = qi + A_BLOCK - ki
    band = (delta >= 0) & (delta <= A_STEPS)
    dist = (delta * dil).astype(np.float32)
    bias = (-slopes[:, None, None] * dist[None] * np.float32(LOG2E)).astype(np.float32)
    out = np.empty((2, HEADS, A_BLOCK, 2 * A_BLOCK), np.float32)
    out[1] = np.where(band[None], bias, NEG)
    out[0] = np.where((band & (ki >= A_BLOCK))[None], bias, NEG)
    return jnp.asarray(out)


def _attn_prompt(qkv, g):
    bsz, dil, steps, _ = qkv.shape
    qb = min(ATTN_BLOCKS_PER_STEP, steps // A_BLOCK)
    ns = min(ATTN_BLOCKS_PER_STEP // qb, dil)
    rows = qb * A_BLOCK
    n_steps = steps // rows
    bias = _attn_bias(g)

    def blk(t, prev):
        if prev:
            return pl.BlockSpec((None, ns, A_BLOCK, A_WIDTH),
                                lambda b, r, n: (b, r, jnp.maximum(n * qb - 1, 0), t))
        return pl.BlockSpec((None, ns, rows, A_WIDTH), lambda b, r, n: (b, r, n, t))

    bias_blk = lambda pick: pl.BlockSpec((None, HEADS, A_BLOCK, 2 * A_BLOCK), lambda b, r, n: (pick(n), 0, 0, 0))
    return pl.pallas_call(
        functools.partial(_attn_kernel, n_seq=ns, q_blocks=qb),
        grid=(bsz, dil // ns, n_steps),
        in_specs=[blk(0, False), blk(1, True), blk(1, False), blk(2, True), blk(2, False),
                  bias_blk(lambda n: jnp.minimum(n, 1)), bias_blk(lambda n: 1)],
        out_specs=[pl.BlockSpec((None, ns, rows, A_WIDTH), lambda b, r, n: (b, r, n, 0)),
                   pl.BlockSpec((None, ns, rows, LANES), lambda b, r, n: (b, r, n, 0))],
        out_shape=[jax.ShapeDtypeStruct((bsz, dil, steps, A_WIDTH), BF16),
                   jax.ShapeDtypeStruct((bsz, dil, steps, LANES), F32)],
        compiler_params=pltpu.CompilerParams(dimension_semantics=("arbitrary",) * 3),
        name=f"attn_prompt_g{g}",
    )(qkv, qkv, qkv, qkv, qkv, bias, bias)


def _attn_sample_kernel(q0_ref, q1_ref, q2_ref, c0_ref, c1_ref, c2_ref, bc0_ref, bc1_ref, bc2_ref,
                        bn0_ref, bn1_ref, bn2_ref, after_ref, o_ref, *, t_new):
    del after_ref
    n_rows = HEADS * t_new
    row = lax.broadcasted_iota(jnp.int32, (n_rows, A_WIDTH), 0)
    lane = lax.broadcasted_iota(jnp.int32, (n_rows, A_WIDTH), 1)
    head_mask = (row // t_new) == (lane // HEAD_DIM)
    pad = jnp.zeros((LANES - t_new, A_WIDTH), F32)
    outs, lses = [], []
    for qkv_ref, c_ref, bc_ref, bn_ref in ((q0_ref, c0_ref, bc0_ref, bn0_ref), (q1_ref, c1_ref, bc1_ref, bn1_ref),
                                           (q2_ref, c2_ref, bc2_ref, bn2_ref)):
        q = qkv_ref[:, :A_WIDTH].astype(F32)
        k_new = qkv_ref[:, A_WIDTH:2 * A_WIDTH].astype(F32)
        v_new = qkv_ref[:, 2 * A_WIDTH:].astype(F32)
        k_new = jnp.concatenate([k_new, pad], axis=0).astype(BF16)
        v_new = jnp.concatenate([v_new, pad], axis=0).astype(BF16)
        q_rows = jnp.where(head_mask, jnp.concatenate([q] * HEADS, axis=0), 0.0).astype(BF16)
        k_buf_t = c_ref[:A_WIDTH, :].astype(BF16)
        v_buf_t = c_ref[A_WIDTH:, :].astype(BF16)
        nt = (((1,), (1,)), ((), ()))
        s_buf = jnp.dot(q_rows, k_buf_t, preferred_element_type=F32) + bc_ref[...]
        s_new = lax.dot_general(q_rows, k_new, nt, preferred_element_type=F32) + bn_ref[...]
        m = jnp.maximum(jnp.max(s_buf, axis=-1, keepdims=True), jnp.max(s_new, axis=-1, keepdims=True))
        p_buf = jnp.exp2(s_buf - m)
        p_new = jnp.exp2(s_new - m)
        den = jnp.sum(p_buf, axis=-1, keepdims=True) + jnp.sum(p_new, axis=-1, keepdims=True)
        o = (lax.dot_general(p_buf.astype(BF16), v_buf_t, nt, preferred_element_type=F32)
             + jnp.dot(p_new.astype(BF16), v_new, preferred_element_type=F32)) / den
        outs.append(o)
        lses.append(m + jnp.log2(den))
    top = jnp.maximum(jnp.maximum(lses[0], lses[1]), lses[2])
    ws = [jnp.exp2(l - top) for l in lses]
    tot = ws[0] + ws[1] + ws[2]
    acc = (ws[0] / tot) * outs[0] + (ws[1] / tot) * outs[1] + (ws[2] / tot) * outs[2]
    acc = jnp.where(head_mask, acc, 0.0).reshape(HEADS, t_new, A_WIDTH)
    o_ref[...] = jnp.sum(acc, axis=0).astype(BF16)


def _sample_bias(g, t_new, buf):
    _, dil = A_GROUPS[g]
    n = N_GROUPS * HEADS
    e = np.arange(1, n + 1, dtype=np.float32)
    slopes = np.exp2(-8.0 * e / n).astype(np.float32).reshape(N_GROUPS, HEADS)[g]
    t = np.arange(t_new)[:, None]
    idx = np.concatenate([np.arange(buf), buf + np.arange(LANES)])[None, :]
    dist = buf + t - idx
    valid = (dist >= 0) & (dist % dil == 0) & (dist <= A_STEPS * dil) & (idx < buf + t_new)
    bias = -slopes[:, None, None] * dist.astype(np.float32)[None] * np.float32(LOG2E)
    bias = np.where(valid[None], bias, NEG).astype(np.float32).reshape(HEADS * t_new, buf + LANES)
    return jnp.asarray(bias[:, :buf]), jnp.asarray(bias[:, buf:])


def _attn_sample(qkvs, caches, *, after):
    bsz, t_new, _ = qkvs[0].shape
    cache_v = [jnp.transpose(c, (0, 2, 3, 4, 1)).reshape(bsz, 2 * A_WIDTH, c.shape[1]) for c in caches]
    biases = [_sample_bias(g, t_new, cache_v[g].shape[2]) for g in range(N_GROUPS)]
    bcs = [b[0] for b in biases]
    bns = [b[1] for b in biases]
    return pl.pallas_call(
        functools.partial(_attn_sample_kernel, t_new=t_new),
        grid=(bsz,),
        in_specs=[pl.BlockSpec((None, t_new, 3 * A_WIDTH), lambda b: (b, 0, 0))] * N_GROUPS
                 + [pl.BlockSpec((None, 2 * A_WIDTH, c.shape[2]), lambda b: (b, 0, 0)) for c in cache_v]
                 + [_const_spec(b) for b in bcs] + [_const_spec(b) for b in bns]
                 + [pl.BlockSpec(memory_space=pl.ANY)],
        out_specs=pl.BlockSpec((None, t_new, A_WIDTH), lambda b: (b, 0, 0)),
        out_shape=jax.ShapeDtypeStruct((bsz, t_new, A_WIDTH), BF16),
        compiler_params=pltpu.CompilerParams(dimension_semantics=("arbitrary",),
                                             vmem_limit_bytes=VMEM_LIMIT),
        name="attn_sample",
    )(*qkvs, *cache_v, *bcs, *bns, after)


def _first_max4(v):
    top = jnp.maximum(jnp.maximum(v[0], v[1]), jnp.maximum(v[2], v[3]))
    idx = jnp.where(v[0] == top, 0.0, jnp.where(v[1] == top, 1.0, jnp.where(v[2] == top, 2.0, 3.0)))
    return top, idx


def _route(lt):
    row = lambda k: lt[k:k + 1, :]
    g = [row(k) for k in range(MOE_GROUPS)]
    gmax, g_idx = _first_max4(g)
    g_prob = 1.0 / (jnp.exp(g[0] - gmax) + jnp.exp(g[1] - gmax) + jnp.exp(g[2] - gmax) + jnp.exp(g[3] - gmax))
    cand = []
    for k in range(MOE_PER_GROUP):
        c = row(ROUTE_OFF + MOE_PER_GROUP * (MOE_GROUPS - 1) + k)
        for gi in range(MOE_GROUPS - 2, -1, -1):
            c = jnp.where(g_idx == float(gi), row(ROUTE_OFF + MOE_PER_GROUP * gi + k), c)
        cand.append(c)
    e1, i1 = _first_max4(cand)
    rest = [jnp.where(i1 == float(k), -jnp.inf, cand[k]) for k in range(MOE_PER_GROUP)]
    e2, i2 = _first_max4(rest)
    t = jnp.exp(e2 - e1)
    w1 = 1.0 / (1.0 + t)
    w2 = t / (1.0 + t)
    return MOE_PER_GROUP * g_idx + i1, MOE_PER_GROUP * g_idx + i2, w1 * g_prob, w2 * g_prob


def _pack_bf16_pairs(x):
    q = x.shape[1] // 4
    bits = lax.bitcast_convert_type(x.astype(BF16).astype(F32), jnp.int32)
    pack = lambda hi, lo: hi | lax.shift_right_logical(lo, 16)
    return pack(bits[:, :q], bits[:, 2 * q:3 * q]), pack(bits[:, q:2 * q], bits[:, 3 * q:])


def _unpack_bf16_pairs(pa, pb):
    hi = lambda p: lax.bitcast_convert_type(p & jnp.int32(-65536), F32)
    lo = lambda p: lax.bitcast_convert_type(lax.shift_left(p, 16), F32)
    return jnp.concatenate([hi(pa), hi(pb), lo(pa), lo(pb)], axis=1)


def _mix_kernel(*refs, tm, dils, routed, span):
    n_attn = len(dils)
    x_ref = refs[0]
    if n_attn > 1:
        o_refs = refs[1:1 + n_attn]
        l_refs = refs[1 + n_attn:1 + 2 * n_attn]
        exp_ref = refs[1 + 2 * n_attn]
        pos = 2 + 2 * n_attn
    else:
        o_refs = refs[1:2]
        pos = 2
    (u_ref, vn_ref, ga_ref, gb_ref, sw_ref, sbt_ref, wpa_ref, wpb_ref, wo_ref,
     gt1_ref, sh2_ref, sc2_ref, n2g_ref, wrh_ref, wrl_ref, br_ref) = refs[pos:pos + 16]
    n_out = 6 if routed else 3
    out_refs = refs[pos + 16:pos + 16 + n_out]
    scr = list(refs[pos + 16 + n_out:])
    x1_ref = out_refs[0]
    ob_scr = scr.pop(0)

    def to_positions(ref, scr, dil):
        if dil == 1:
            return ref[0].astype(F32)
        n_slab = scr.shape[0]
        for r in range(dil):
            rows = ref[r].astype(F32)
            for j in range(n_slab):
                scr[j, pl.ds(r, tm // dil, stride=dil), :] = rows[:, j * LANES:(j + 1) * LANES]
        return jnp.concatenate([scr[j] for j in range(n_slab)], axis=1)

    if n_attn > 1:
        o_scr, l_scr = scr[0], scr[1]
        stats = [to_positions(l_refs[g], l_scr.at[g], dils[g]) for g in range(n_attn)]
        lane = lax.broadcasted_iota(jnp.int32, stats[0].shape, 1)
        is_max = (lane & (LSE_REP - 1)) < LSE_REP // 2
        half = LSE_REP // 2
        maxes = [jnp.where(is_max, s, pltpu.roll(s, half, axis=1)) for s in stats]
        dens = [jnp.where(is_max, pltpu.roll(s, LANES - half, axis=1), s) for s in stats]
        lses = [m + jnp.log2(d) for m, d in zip(maxes, dens)]
        top = functools.reduce(jnp.maximum, lses)
        tot = functools.reduce(lambda a, b: a + b, [jnp.exp2(l - top) for l in lses])
        o_a = None
        for g in range(n_attn):
            w = jnp.exp2(maxes[g] - top) / tot
            hi = w.astype(BF16)
            lo = (w - hi.astype(F32)).astype(BF16)
            w_exp = jnp.dot(jnp.concatenate([hi, lo], axis=1), exp_ref[...], preferred_element_type=F32)
            term = w_exp * to_positions(o_refs[g], o_scr.at[g], dils[g])
            o_a = term if o_a is None else o_a + term
        o_a = o_a.astype(BF16)
    else:
        o_a = o_refs[0][...]

    r_i = lax.broadcasted_iota(jnp.int32, (span, span), 0)
    c_i = lax.broadcasted_iota(jnp.int32, (span, span), 1)
    tril = r_i >= c_i
    for gi in range(B_GROUPS):
        gcols = slice(gi * LANES, (gi + 1) * LANES)
        b_s = sbt_ref[:span, gi:gi + 1]
        if span == B_CHUNK:
            w_s = jnp.where(tril, sw_ref[gi], 0.0).astype(BF16)
            n_chunk = tm // span
            vn_wide = jnp.concatenate([vn_ref[c * span:(c + 1) * span, gcols] for c in range(n_chunk)], axis=1)
            mixed = jnp.dot(w_s, vn_wide, preferred_element_type=F32) + b_s
            for c in range(n_chunk):
                rows = slice(c * span, (c + 1) * span)
                ob_scr[rows, gcols] = (u_ref[rows, gcols].astype(F32)
                                       * mixed[:, c * LANES:(c + 1) * LANES]).astype(BF16)
        else:
            w_s = jnp.where(tril, sw_ref[gi, :span, :span], 0.0)
            vn3 = vn_ref[:, gcols].astype(F32).reshape(tm // span, span, LANES)
            mixed = jnp.broadcast_to(b_s, (span, LANES))[None]
            for s in range(span):
                w_col = jnp.broadcast_to(w_s[:, s:s + 1], (span, LANES))[None]
                mixed = mixed + w_col * jnp.broadcast_to(vn3[:, s:s + 1, :], vn3.shape)
            u3 = u_ref[:, gcols].astype(F32).reshape(tm // span, span, LANES)
            ob_scr[:, gcols] = (u3 * mixed).reshape(tm, LANES).astype(BF16)

    sub = min(tm, MIX_SUB_ROWS)
    if routed:
        carry = scr[-1]
        first_step = (pl.program_id(0) == 0) & (pl.program_id(1) == 0)
        seen = jnp.where(first_step, 0.0, carry[:, 0:1])
        r_i = lax.broadcasted_iota(jnp.int32, (sub, sub), 0)
        c_i = lax.broadcasted_iota(jnp.int32, (sub, sub), 1)
        earlier = jnp.where(r_i < c_i, 1.0, 0.0).astype(BF16)
        e_iota = lax.broadcasted_iota(jnp.int32, (MOE_EXPERTS, sub), 0).astype(F32)
    nt = (((1,), (1,)), ((), ()))
    for r0 in range(0, tm, sub):
        rs = slice(r0, r0 + sub)
        mod = lambda ref: ref[...] if ref.shape[0] == 1 else ref[rs, :]
        pa = jnp.dot(o_a[rs], wpa_ref[...], preferred_element_type=F32)
        pb = jnp.dot(ob_scr[rs, :], wpb_ref[...], preferred_element_type=F32)
        mix = (ga_ref[rs, :].astype(F32) * pa + gb_ref[rs, :].astype(F32) * pb).astype(BF16)
        x1 = x_ref[rs, :] + mod(gt1_ref) * jnp.dot(mix, wo_ref[...], preferred_element_type=F32)
        x1_ref[rs, :] = x1
        h2 = _rmsnorm_mod(x1, n2g_ref[...], mod(sh2_ref), mod(sc2_ref))
        h_hi = h2.astype(BF16)
        h_lo = (h2 - h_hi.astype(F32)).astype(BF16)
        lt = (lax.dot_general(wrh_ref[...], h_hi, nt, preferred_element_type=F32)
              + lax.dot_general(wrh_ref[...], h_lo, nt, preferred_element_type=F32)
              + lax.dot_general(wrl_ref[...], h_hi, nt, preferred_element_type=F32)) + br_ref[...]
        ex1, ex2, w1, w2 = _route(lt)
        if not routed:
            _, h2_ref, comb_ref = out_refs
            h2_ref[rs, :] = h_hi
            e_lane = lax.broadcasted_iota(jnp.int32, (LANES, sub), 0).astype(F32) - float(ROUTE_OFF)
            comb_t = jnp.where(e_lane == ex1, w1, jnp.where(e_lane == ex2, w2, 0.0))
            comb_ref[rs, :] = jnp.transpose(comb_t)
            continue

        _, hpa_ref, hpb_ref, route_ref, wcol_ref, cnt_ref = out_refs
        hpa_ref[rs, :], hpb_ref[rs, :] = _pack_bf16_pairs(h2)
        hot1, hot2 = e_iota == ex1, e_iota == ex2
        onehot = jnp.where(hot1 | hot2, 1.0, 0.0)
        rank = jnp.dot(onehot.astype(BF16), earlier, preferred_element_type=F32) + seen
        r1 = jnp.sum(jnp.where(hot1, rank, 0.0), axis=0, keepdims=True)
        r2 = jnp.sum(jnp.where(hot2, rank, 0.0), axis=0, keepdims=True)
        seen = seen + jnp.sum(onehot, axis=1, keepdims=True)
        fields = jnp.concatenate([ex1, ex2, w1, w2, r1, r2, jnp.zeros((2, sub), F32)], axis=0)
        route_ref[:, rs] = fields
        wcol_ref[rs, :] = jnp.transpose(jnp.concatenate([fields, jnp.zeros((LANES - 8, sub), F32)], axis=0))
    if routed:
        carry[...] = jnp.broadcast_to(seen, carry.shape)
        cnt_ref[...] = jnp.broadcast_to(seen, cnt_ref.shape)


def _expand_matrix():
    e = np.zeros((LANES, A_WIDTH), np.float32)
    for h in range(HEADS):
        e[h * LSE_REP, h * HEAD_DIM:(h + 1) * HEAD_DIM] = 1.0
    return jnp.asarray(np.concatenate([e, e], axis=0), dtype=BF16)


def _mix(x, attn_outs, attn_lses, u, vn, ga, gb, sgu_w, sgu_bt, w_pa, w_pb, w_o,
         gt1, sh2, sc2, n2g, w_r_hi, w_r_lo, b_r, *, tm, routed, span):
    bsz, seq, _ = x.shape
    n_attn = len(attn_outs)
    tok = lambda width: pl.BlockSpec((None, tm, width), lambda b, s: (b, s, 0))
    ins = [x] + list(attn_outs)
    scratch = [pltpu.VMEM((tm, B_WIDTH), BF16)]
    if n_attn > 1:
        dils = tuple(o.shape[1] for o in attn_outs)
        res = lambda dil, width: pl.BlockSpec((None, dil, tm // dil, width), lambda b, s: (b, 0, s, 0))
        e2 = _expand_matrix()
        ins += list(attn_lses) + [e2]
        specs = ([tok(D_MODEL)] + [res(d, A_WIDTH) for d in dils] + [res(d, LANES) for d in dils]
                 + [_const_spec(e2)])
        scratch += [pltpu.VMEM((n_attn, A_WIDTH // LANES, tm, LANES), F32), pltpu.VMEM((n_attn, 1, tm, LANES), F32)]
    else:
        dils = (1,)
        specs = [tok(D_MODEL), tok(A_WIDTH)]
    ins += [u, vn, ga, gb, sgu_w, sgu_bt, w_pa, w_pb, w_o, gt1, sh2, sc2, n2g, w_r_hi, w_r_lo, b_r]
    specs += [tok(B_WIDTH), tok(B_WIDTH), tok(D_MODEL), tok(D_MODEL),
              _const_spec(sgu_w), _const_spec(sgu_bt), _const_spec(w_pa), _const_spec(w_pb), _const_spec(w_o),
              _mod_spec(gt1, tm), _mod_spec(sh2, tm), _mod_spec(sc2, tm), _const_spec(n2g),
              _const_spec(w_r_hi), _const_spec(w_r_lo), _const_spec(b_r)]
    if routed:
        quarter = D_MODEL // 4
        out_specs = [tok(D_MODEL), tok(quarter), tok(quarter),
                     pl.BlockSpec((None, 8, tm), lambda b, s: (b, 0, s)), tok(LANES),
                     pl.BlockSpec((MOE_EXPERTS, LANES), lambda b, s: (0, 0))]
        out_shape = [jax.ShapeDtypeStruct((bsz, seq, D_MODEL), F32),
                     jax.ShapeDtypeStruct((bsz, seq, quarter), jnp.int32),
                     jax.ShapeDtypeStruct((bsz, seq, quarter), jnp.int32),
                     jax.ShapeDtypeStruct((bsz, 8, seq), F32),
                     jax.ShapeDtypeStruct((bsz, seq, LANES), F32),
                     jax.ShapeDtypeStruct((MOE_EXPERTS, LANES), F32)]
        scratch.append(pltpu.VMEM((MOE_EXPERTS, LANES), F32))
    else:
        out_specs = [tok(D_MODEL), tok(D_MODEL), tok(LANES)]
        out_shape = [jax.ShapeDtypeStruct((bsz, seq, D_MODEL), F32),
                     jax.ShapeDtypeStruct((bsz, seq, D_MODEL), BF16),
                     jax.ShapeDtypeStruct((bsz, seq, LANES), F32)]
    return pl.pallas_call(
        functools.partial(_mix_kernel, tm=tm, dils=dils, routed=routed, span=span),
        grid=(bsz, seq // tm),
        in_specs=specs,
        out_specs=out_specs,
        out_shape=out_shape,
        scratch_shapes=scratch,
        compiler_params=pltpu.CompilerParams(dimension_semantics=("arbitrary", "arbitrary"),
                                             vmem_limit_bytes=VMEM_LIMIT),
        name="mix",
    )(*ins)


def _moe_kernel(h_ref, comb_ref, wg_ref, wu_ref, wd_ref, x1_ref, gt2_ref, shf_ref, scf_ref, nfg_ref,
                y_ref, acc_ref):
    e = pl.program_id(2)

    @pl.when(e == 0)
    def _():
        acc_ref[...] = jnp.zeros_like(acc_ref)

    h = h_ref[...]
    hg = jnp.dot(h, wg_ref[...].astype(BF16), preferred_element_type=F32)
    hu = jnp.dot(h, wu_ref[...].astype(BF16), preferred_element_type=F32)
    comb = comb_ref[...]
    lane = lax.broadcasted_iota(jnp.int32, comb.shape, 1)
    cw = jnp.sum(jnp.where(lane == e + ROUTE_OFF, comb, 0.0), axis=-1, keepdims=True)
    act = (hg * _sigmoid(hg) * hu) * cw
    acc_ref[...] += jnp.dot(act.astype(BF16), wd_ref[...].astype(BF16), preferred_element_type=F32)

    @pl.when(e == MOE_EXPERTS - 1)
    def _():
        x2 = x1_ref[...] + gt2_ref[...] * acc_ref[...]
        y_ref[...] = _rmsnorm_mod(x2, nfg_ref[...], shf_ref[...], scf_ref[...])


def _moe(h2, comb, w_gate, w_up, w_down, x1, gt2, shf, scf, nfg, *, tm):
    bsz, seq, _ = h2.shape
    tok = lambda width: pl.BlockSpec((None, tm, width), lambda b, s, e: (b, s, 0))
    return pl.pallas_call(
        _moe_kernel,
        grid=(bsz, seq // tm, MOE_EXPERTS),
        in_specs=[tok(D_MODEL), tok(LANES),
                  pl.BlockSpec((None, D_MODEL, MOE_HIDDEN), lambda b, s, e: (e, 0, 0)),
                  pl.BlockSpec((None, D_MODEL, MOE_HIDDEN), lambda b, s, e: (e, 0, 0)),
                  pl.BlockSpec((None, MOE_HIDDEN, D_MODEL), lambda b, s, e: (e, 0, 0)),
                  tok(D_MODEL), _mod_spec(gt2, tm), _mod_spec(shf, tm), _mod_spec(scf, tm), _const_spec(nfg)],
        out_specs=tok(D_MODEL),
        out_shape=jax.ShapeDtypeStruct((bsz, seq, D_MODEL), F32),
        scratch_shapes=[pltpu.VMEM((tm, D_MODEL), F32)],
        compiler_params=pltpu.CompilerParams(dimension_semantics=("arbitrary",) * 3,
                                             vmem_limit_bytes=VMEM_LIMIT),
        name="moe",
    )(h2, comb, w_gate, w_up, w_down, x1, gt2, shf, scf, nfg)


def _sc_mesh():
    return plsc.VectorSubcoreMesh(core_axis_name="c", subcore_axis_name="s")


def _sc_scatter_rows(rows, pos1, pos2, n_out):
    n, width = rows.shape
    steps = n // SC_WINDOW // SC_CORES

    @pl.kernel(out_type=jax.ShapeDtypeStruct((n_out, width), rows.dtype), mesh=_sc_mesh(), scratch_types=[])
    def scatter(rows_hbm, p1_hbm, p2_hbm, out_hbm):
        def body(x_vmem, i1_vmem, i2_vmem):
            pltpu.sync_copy(x_vmem, out_hbm.at[i1_vmem.at[0]])
            pltpu.sync_copy(x_vmem, out_hbm.at[i2_vmem.at[0]])

        pltpu.emit_pipeline(
            body, grid=(SC_CORES, steps),
            in_specs=[pl.BlockSpec((SC_WINDOW, width), lambda c, i: (c * steps + i, 0)),
                      pl.BlockSpec((1, SC_WINDOW), lambda c, i: (0, c * steps + i)),
                      pl.BlockSpec((1, SC_WINDOW), lambda c, i: (0, c * steps + i))],
            out_specs=[],
            core_axis_name=("c", "s"),
            dimension_semantics=(pltpu.PARALLEL, pltpu.PARALLEL),
        )(rows_hbm, p1_hbm, p2_hbm)

    return scatter(rows, pos1, pos2)


def _sc_gather_rows(table, idx):
    m, width = idx.shape[1], table.shape[1]
    steps = m // SC_WINDOW // SC_CORES

    @pl.kernel(out_type=jax.ShapeDtypeStruct((m, width), table.dtype), mesh=_sc_mesh(), scratch_types=[])
    def gather(table_hbm, idx_hbm, out_hbm):
        def body(i_vmem, o_vmem):
            pltpu.sync_copy(table_hbm.at[i_vmem.at[0]], o_vmem)

        pltpu.emit_pipeline(
            body, grid=(SC_CORES, steps),
            in_specs=[pl.BlockSpec((1, SC_WINDOW), lambda c, i: (0, c * steps + i))],
            out_specs=[pl.BlockSpec((SC_WINDOW, width), lambda c, i: (c * steps + i, 0))],
            core_axis_name=("c", "s"),
            dimension_semantics=(pltpu.PARALLEL, pltpu.PARALLEL),
        )(idx_hbm, out_hbm)

    return gather(table, idx)


def _ffn_kernel(te_ref, nt_ref, xa_ref, xb_ref, wg_ref, wu_ref, wd_ref, after_ref, oa_ref, ob_ref,
                wg_s, wu_s, wd_s):
    del after_ref
    t = pl.program_id(0)
    live = t < nt_ref[0]

    @pl.when(live & ((t == 0) | (te_ref[t] != te_ref[jnp.maximum(t - 1, 0)])))
    def _():
        wg_s[...] = wg_ref[...].astype(BF16)
        wu_s[...] = wu_ref[...].astype(BF16)
        wd_s[...] = wd_ref[...].astype(BF16)

    @pl.when(live)
    def _():
        x = _unpack_bf16_pairs(xa_ref[...], xb_ref[...]).astype(BF16)
        hg = jnp.dot(x, wg_s[...], preferred_element_type=F32)
        hu = jnp.dot(x, wu_s[...], preferred_element_type=F32)
        act = (hg * _sigmoid(hg) * hu).astype(BF16)
        oa_ref[...], ob_ref[...] = _pack_bf16_pairs(jnp.dot(act, wd_s[...], preferred_element_type=F32))


def _ffn(xa, xb, tile_expert, n_tiles, w_gate, w_up, w_down, *, after):
    n_rows, quarter = xa.shape
    rows = pl.BlockSpec((MOE_TILE, quarter), lambda t, te, nt: (jnp.minimum(t, nt[0] - 1), 0))
    return pl.pallas_call(
        _ffn_kernel,
        grid_spec=pltpu.PrefetchScalarGridSpec(
            num_scalar_prefetch=2,
            grid=(n_rows // MOE_TILE,),
            in_specs=[rows, rows,
                      pl.BlockSpec((None, D_MODEL, MOE_HIDDEN), lambda t, te, nt: (te[t], 0, 0)),
                      pl.BlockSpec((None, D_MODEL, MOE_HIDDEN), lambda t, te, nt: (te[t], 0, 0)),
                      pl.BlockSpec((None, MOE_HIDDEN, D_MODEL), lambda t, te, nt: (te[t], 0, 0)),
                      pl.BlockSpec(memory_space=pl.ANY)],
            out_specs=[rows, rows],
            scratch_shapes=[pltpu.VMEM((D_MODEL, MOE_HIDDEN), BF16), pltpu.VMEM((D_MODEL, MOE_HIDDEN), BF16),
                            pltpu.VMEM((MOE_HIDDEN, D_MODEL), BF16)]),
        out_shape=[jax.ShapeDtypeStruct((n_rows, quarter), jnp.int32)] * 2,
        compiler_params=pltpu.CompilerParams(dimension_semantics=("arbitrary",), vmem_limit_bytes=VMEM_LIMIT),
        name="moe_ffn",
    )(tile_expert, n_tiles, xa, xb, w_gate, w_up, w_down, after)


def _final_kernel(x1_ref, g1a_ref, g1b_ref, g2a_ref, g2b_ref, route_ref, gt2_ref, shf_ref, scf_ref, nfg_ref, y_ref):
    o1 = _unpack_bf16_pairs(g1a_ref[...], g1b_ref[...])
    o2 = _unpack_bf16_pairs(g2a_ref[...], g2b_ref[...])
    route = route_ref[...]
    moe = route[:, 2:3] * o1 + route[:, 3:4] * o2
    x2 = x1_ref[...] + gt2_ref[...] * moe
    y_ref[...] = _rmsnorm_mod(x2, nfg_ref[...], shf_ref[...], scf_ref[...])


def _final(x1, ga, gb, route, gt2, shf, scf, nfg, *, tm):
    bsz, seq, _ = x1.shape
    n_s = seq // tm
    n_blk = bsz * n_s
    quarter = ga.shape[1]
    tok = lambda width: pl.BlockSpec((None, tm, width), lambda b, s: (b, s, 0))
    first = pl.BlockSpec((tm, quarter), lambda b, s: (b * n_s + s, 0))
    second = pl.BlockSpec((tm, quarter), lambda b, s: (n_blk + b * n_s + s, 0))
    return pl.pallas_call(
        _final_kernel,
        grid=(bsz, n_s),
        in_specs=[tok(D_MODEL), first, first, second, second, tok(LANES),
                  _mod_spec(gt2, tm), _mod_spec(shf, tm), _mod_spec(scf, tm), _const_spec(nfg)],
        out_specs=tok(D_MODEL),
        out_shape=jax.ShapeDtypeStruct((bsz, seq, D_MODEL), F32),
        compiler_params=pltpu.CompilerParams(dimension_semantics=("arbitrary", "arbitrary")),
        name="moe_final",
    )(x1, ga, gb, ga, gb, route, gt2, shf, scf, nfg)


def _positions_kernel(base_ref, route_ref, p1_ref, p2_ref):
    rows = route_ref[...]
    for e_row, r_row, out_ref in ((0, 4, p1_ref), (1, 5, p2_ref)):
        expert = rows[e_row:e_row + 1, :]
        start = jnp.zeros(expert.shape, jnp.int32)
        for e in range(MOE_EXPERTS):
            start = jnp.where(expert == float(e), base_ref[e], start)
        out_ref[...] = start + rows[r_row:r_row + 1, :].astype(jnp.int32)


def _positions(route_rows, base):
    bsz, _, seq = route_rows.shape
    out = pl.BlockSpec((None, 1, seq), lambda b, base: (b, 0, 0))
    return pl.pallas_call(
        _positions_kernel,
        grid_spec=pltpu.PrefetchScalarGridSpec(
            num_scalar_prefetch=1, grid=(bsz,),
            in_specs=[pl.BlockSpec((None, 8, seq), lambda b, base: (b, 0, 0))],
            out_specs=[out, out]),
        out_shape=[jax.ShapeDtypeStruct((bsz, 1, seq), jnp.int32)] * 2,
        compiler_params=pltpu.CompilerParams(dimension_semantics=("arbitrary",)),
        name="moe_positions",
    )(base, route_rows)


def _moe_dispatch(hpa, hpb, route_rows, counts):
    bsz, seq, quarter = hpa.shape
    n_tok = bsz * seq
    t_max = 2 * n_tok // MOE_TILE + MOE_EXPERTS
    cnt = counts[:, 0].astype(jnp.int32)
    tiles_e = (cnt + MOE_TILE - 1) // MOE_TILE
    tile_end = jnp.cumsum(tiles_e)
    n_tiles = tile_end[-1:]
    base = (tile_end - tiles_e) * MOE_TILE
    t_ids = jnp.minimum(jnp.arange(t_max, dtype=jnp.int32), n_tiles[0] - 1)
    tile_expert = jnp.sum((t_ids[:, None] >= tile_end[None, :]).astype(jnp.int32), axis=1)
    pos1, pos2 = (p.reshape(1, n_tok) for p in _positions(route_rows, base))
    n_rows = t_max * MOE_TILE
    xa = _sc_scatter_rows(hpa.reshape(n_tok, quarter), pos1, pos2, n_rows)
    xb = _sc_scatter_rows(hpb.reshape(n_tok, quarter), pos1, pos2, n_rows)
    return xa, xb, jnp.concatenate([pos1, pos2], axis=1), tile_expert, n_tiles


def kernel(x_prompt, x_sample, cache_kv_w128, cache_kv_w512, cache_kv_w2048, c_prompt, c_sample,
           w_ada, b_ada, norm1_g, norm2_g, w_in, sgu_ln_g, sgu_ln_b, sgu_w, sgu_b, w_pa, w_pb, w_o,
           w_route_group, b_route_group, w_route_expert, b_route_expert, w_gate, w_up, w_down,
           normf_g, w_ada_final, b_ada_final):
    depth = w_ada.shape[0]
    assert depth == 1
    l = 0
    bp, seq, _ = x_prompt.shape
    bs, t_new, _ = x_sample.shape
    n_samp = bs * t_new

    c_all = jnp.concatenate([c_prompt, c_sample], axis=0)
    pad_rows = (-c_all.shape[0]) % 8
    c_all = jnp.pad(c_all, ((0, pad_rows), (0, 0)))
    mod = _adaln(c_all, w_ada[l], b_ada[l])
    mod_f = _adaln(c_all, w_ada_final, b_ada_final)

    def split_mods(m, n, lo, hi, per_token):
        parts = jnp.split(m[lo:hi], n, axis=-1)
        if per_token:
            return [jnp.repeat(p, t_new, axis=0).reshape(1, n_samp, D_MODEL) for p in parts]
        return [p.reshape(hi - lo, 1, D_MODEL) for p in parts]

    mods_p = split_mods(mod, 6, 0, bp, False) + split_mods(mod_f, 2, 0, bp, False)
    mods_s = split_mods(mod, 6, bp, bp + bs, True) + split_mods(mod_f, 2, bp, bp + bs, True)

    row = lambda v: v.reshape(1, -1)
    w_in_b = w_in[l].astype(BF16)
    w_pa_b, w_pb_b, w_o_b = w_pa[l].astype(BF16), w_pb[l].astype(BF16), w_o[l].astype(BF16)
    w_gate_b, w_up_b, w_down_b = w_gate[l], w_up[l], w_down[l]
    w_re = jnp.transpose(w_route_expert[l], (0, 2, 1)).reshape(MOE_EXPERTS, D_MODEL)
    w_r = jnp.pad(jnp.concatenate([jnp.transpose(w_route_group[l]), w_re], axis=0),
                  ((0, LANES - MOE_GROUPS - MOE_EXPERTS), (0, 0)))
    w_r_hi = w_r.astype(BF16)
    w_r_lo = (w_r - w_r_hi.astype(F32)).astype(BF16)
    b_r = jnp.pad(jnp.concatenate([b_route_group[l], b_route_expert[l].reshape(-1)]),
                  (0, LANES - MOE_GROUPS - MOE_EXPERTS)).reshape(LANES, 1)

    def inproj(x, mods, keeps, dils, tm, emit_vn_f32, kv_feature_major, after=None):
        return _inproj(x, mods[0], mods[1], row(norm1_g[l]), w_in_b, row(sgu_ln_g[l]), row(sgu_ln_b[l]),
                       tm=tm, keeps=keeps, dils=dils, emit_vn_f32=emit_vn_f32,
                       kv_feature_major=kv_feature_major, after=after)

    sgu_bt = jnp.transpose(sgu_b[l])

    def mix(x, mods, pin, attn_outs, attn_lses, tm, routed, span):
        u, vn, ga, gb = pin[3:7]
        return _mix(x, attn_outs, attn_lses, u, vn, ga, gb, sgu_w[l], sgu_bt,
                    w_pa_b, w_pb_b, w_o_b, mods[2], mods[3], mods[4], row(norm2_g[l]), w_r_hi, w_r_lo, b_r,
                    tm=tm, routed=routed, span=span)

    tm_p = 512
    keeps_p = tuple(min(win, seq) for win, _ in A_GROUPS)
    dils_p = tuple(dil for _, dil in A_GROUPS)
    pin = inproj(x_prompt, mods_p, keeps_p, dils_p, 256, False, True)
    attn_p = [_attn_prompt(pin[g], g) for g in range(N_GROUPS)]
    x1, hpa, hpb, route_rows, route, counts = mix(x_prompt, mods_p, pin, [r[0] for r in attn_p],
                                                  [r[1] for r in attn_p], tm_p, True, B_CHUNK)

    xs = x_sample.reshape(1, n_samp, D_MODEL)
    sin = inproj(xs, mods_s, (n_samp,) * N_GROUPS, (1,) * N_GROUPS, n_samp, True, False, after=counts)

    xa, xb, pos, tile_expert, n_tiles = _moe_dispatch(hpa, hpb, route_rows, counts)
    oa, ob = _ffn(xa, xb, tile_expert, n_tiles, w_gate_b, w_up_b, w_down_b, after=sin[0])

    caches = (cache_kv_w128[l], cache_kv_w512[l], cache_kv_w2048[l])
    o_s = _attn_sample([q.reshape(bs, t_new, 3 * A_WIDTH) for q in sin[:3]], caches, after=oa)
    x1_s, h2_s, comb_s = mix(xs, mods_s, sin, [o_s.reshape(1, n_samp, A_WIDTH)], None, n_samp, False,
                             min(t_new, B_CHUNK))
    y_s = _moe(h2_s, comb_s, w_gate_b, w_up_b, w_down_b, x1_s, mods_s[5], mods_s[6], mods_s[7], row(normf_g),
               tm=n_samp)

    ga_rows = _sc_gather_rows(oa, pos)
    gb_rows = _sc_gather_rows(ob, pos)
    y_p = _final(x1, ga_rows, gb_rows, route, mods_p[5], mods_p[6], mods_p[7], row(normf_g), tm=tm_p)
    kv_p, kv_s, extra = pin[7:10], sin[7:10], sin[10:]

    def kv_out(a, b):
        return a.reshape(depth, b, -1, 2, HEADS, HEAD_DIM)

    def kv_out_t(a, b):
        a = a.reshape(b, 2, HEADS, HEAD_DIM, a.shape[-1])
        return jnp.transpose(a, (0, 4, 1, 2, 3)).reshape(depth, b, -1, 2, HEADS, HEAD_DIM)

    return (y_p, y_s.reshape(bs, t_new, D_MODEL),
            kv_out_t(kv_p[0], bp), kv_out_t(kv_p[1], bp), kv_out_t(kv_p[2], bp),
            kv_out(kv_s[0], bs), kv_out(kv_s[1], bs), kv_out(kv_s[2], bs),
            extra[0].reshape(depth, bs, t_new, B_WIDTH))
```

```python
import functools

import numpy as np
import jax
import jax.numpy as jnp
from jax import lax
from jax.experimental import pallas as pl
from jax.experimental.pallas import tpu as pltpu
from jax.experimental.pallas import tpu_sc as plsc

F32 = jnp.float32
BF16 = jnp.bfloat16

D_MODEL = 1024
A_GROUPS = ((128, 1), (512, 4), (2048, 16))
N_GROUPS = 3
HEADS = 8
HEAD_DIM = 64
A_WIDTH = HEADS * HEAD_DIM
A_STEPS = 128
A_BLOCK = 128
B_WIDTH = 1024
B_GROUPS = 8
B_CHUNK = 128
MOE_GROUPS = 4
MOE_PER_GROUP = 4
MOE_EXPERTS = 16
MOE_HIDDEN = 512
EPS = 1e-6
N_QKV = 3 * N_GROUPS * A_WIDTH
IN_COLS = N_QKV + 2 * B_WIDTH + 2 * D_MODEL
COL_CHUNK = 512
LANES = 128
LSE_REP = LANES // HEADS
ROUTE_OFF = MOE_GROUPS
ROUTE_ROWS = 32
NEG = -1e30
LOG2E = 1.4426950408889634
VMEM_LIMIT = 56 * 1024 * 1024
ATTN_BLOCKS_PER_STEP = 16
MIX_SUB_ROWS = 512
MOE_TILE = 512
SC_WINDOW = 128
SC_CORES = 2


def _sigmoid(x):
    return 1.0 / (1.0 + jnp.exp(-x))


def _gelu_tanh(x):
    return x * (0.5 * (1.0 + jnp.tanh(0.7978845608028654 * (x + 0.044715 * (x * x * x)))))


def _rmsnorm_mod(x, gain, shift, scale):
    y = x * lax.rsqrt(jnp.mean(x * x, axis=-1, keepdims=True) + EPS)
    return y * gain * (1.0 + scale) + shift


def _mod_spec(mod, tm):
    if mod.shape[1] == 1:
        return pl.BlockSpec((None, 1, D_MODEL), lambda b, s, *_: (b, 0, 0))
    return pl.BlockSpec((None, tm, D_MODEL), lambda b, s, *_: (b, s, 0))


def _const_spec(arr):
    nd = arr.ndim
    return pl.BlockSpec(arr.shape, lambda *_: (0,) * nd)


def _adaln_kernel(c_ref, w_ref, b_ref, *rest):
    o_ref = rest[-1]
    c = c_ref[...]
    a = (c * _sigmoid(c)).astype(BF16)
    o_ref[...] = jnp.dot(a, w_ref[...].astype(BF16), preferred_element_type=F32) + b_ref[...]


def _adaln(c, w, b, tn=1024, after=None):
    rows, ncols = c.shape[0], w.shape[1]
    ins = [c, w, b.reshape(1, ncols)]
    in_specs = [pl.BlockSpec((rows, D_MODEL), lambda j: (0, 0)),
                pl.BlockSpec((D_MODEL, tn), lambda j: (0, j)),
                pl.BlockSpec((1, tn), lambda j: (0, j))]
    if after is not None:
        ins.append(after)
        in_specs.append(pl.BlockSpec(memory_space=pl.ANY))
    return pl.pallas_call(
        _adaln_kernel,
        grid=(ncols // tn,),
        in_specs=in_specs,
        out_specs=pl.BlockSpec((rows, tn), lambda j: (0, j)),
        out_shape=jax.ShapeDtypeStruct((rows, ncols), F32),
        compiler_params=pltpu.CompilerParams(dimension_semantics=("arbitrary",)),
        name="adaln",
    )(*ins)


def _inproj_kernel(x_ref, sh_ref, sc_ref, g_ref, w_ref, lng_ref, lnb_ref, *refs, tm, keeps, dils, emit_vn_f32,
                   has_after, kv_feature_major):
    if has_after:
        refs = refs[1:]
    qkv0_ref, qkv1_ref, qkv2_ref, u_ref, vn_ref, ga_ref, gb_ref, kv0_ref, kv1_ref, kv2_ref = refs[:10]
    rest = refs[10:]
    perm_scr = rest[-1]
    h = _rmsnorm_mod(x_ref[...], g_ref[...], sh_ref[...], sc_ref[...]).astype(BF16)

    def proj(c):
        return jnp.dot(h, w_ref[:, c * COL_CHUNK:(c + 1) * COL_CHUNK], preferred_element_type=F32)

    def cols(c):
        return slice(c * COL_CHUNK, (c + 1) * COL_CHUNK)

    qkv_refs = (qkv0_ref, qkv1_ref, qkv2_ref)

    def put_qkv(g, t, z):
        dil = dils[g]
        if dil == 1:
            qkv_refs[g][0, :, cols(t)] = z.astype(BF16)
            return
        slot = perm_scr.at[t % 2]
        n_slab = COL_CHUNK // LANES
        for j in range(n_slab):
            slot[j] = z[:, j * LANES:(j + 1) * LANES]
        for r in range(dil):
            rows = jnp.concatenate([slot[j, pl.ds(r, tm // dil, stride=dil), :] for j in range(n_slab)], axis=1)
            qkv_refs[g][r, :, cols(t)] = rows.astype(BF16)

    for g in range(N_GROUPS):
        put_qkv(g, 0, proj(g) * (HEAD_DIM ** -0.5 * LOG2E))

    kv_refs = (kv0_ref, kv1_ref, kv2_ref)
    for g in range(N_GROUPS):
        keep = keeps[g]
        for t in (1, 2):
            z = proj(t * N_GROUPS + g)
            put_qkv(g, t, z)
            if kv_feature_major:
                zt = jnp.transpose(z if keep >= tm else z[tm - keep:, :])
                kv_refs[g][(t - 1) * A_WIDTH:t * A_WIDTH, :] = zt
            else:
                kv_refs[g][:, cols(t - 1)] = z if keep >= tm else z[tm - keep:, :]

    base = N_QKV // COL_CHUNK
    for c in range(2):
        u_ref[:, cols(c)] = _gelu_tanh(proj(base + c)).astype(BF16)

    vs = [_gelu_tanh(proj(base + 2 + c)) for c in range(2)]
    mu = (jnp.sum(vs[0], axis=-1, keepdims=True) + jnp.sum(vs[1], axis=-1, keepdims=True)) * (1.0 / B_WIDTH)
    ds = [v - mu for v in vs]
    var = (jnp.sum(ds[0] * ds[0], axis=-1, keepdims=True)
           + jnp.sum(ds[1] * ds[1], axis=-1, keepdims=True)) * (1.0 / B_WIDTH)
    inv = lax.rsqrt(var + EPS)
    for c in range(2):
        vn = ds[c] * inv * lng_ref[:, cols(c)] + lnb_ref[:, cols(c)]
        vn_ref[:, cols(c)] = vn.astype(BF16)
        if emit_vn_f32:
            rest[0][:, cols(c)] = vn

    for c in range(2):
        ga_ref[:, cols(c)] = _sigmoid(proj(base + 4 + c)).astype(BF16)
        gb_ref[:, cols(c)] = _sigmoid(proj(base + 6 + c)).astype(BF16)


def _inproj(x, sh, sc, gain, w_in, ln_g, ln_b, *, tm, keeps, dils, emit_vn_f32, kv_feature_major, after=None):
    bsz, seq, _ = x.shape
    n_s = seq // tm
    tok = lambda width: pl.BlockSpec((None, tm, width), lambda b, s: (b, s, 0))

    def kv_spec(keep):
        first = n_s - max(keep // tm, 1)
        rows = min(keep, tm)
        if kv_feature_major:
            return pl.BlockSpec((None, 2 * A_WIDTH, rows), lambda b, s: (b, 0, jnp.maximum(s - first, 0)))
        return pl.BlockSpec((None, rows, 2 * A_WIDTH), lambda b, s: (b, jnp.maximum(s - first, 0), 0))

    out_specs, out_shape = [], []
    for dil in dils:
        out_specs.append(pl.BlockSpec((None, dil, tm // dil, 3 * A_WIDTH), lambda b, s: (b, 0, s, 0)))
        out_shape.append(jax.ShapeDtypeStruct((bsz, dil, seq // dil, 3 * A_WIDTH), BF16))
    out_specs += [tok(B_WIDTH), tok(B_WIDTH), tok(D_MODEL), tok(D_MODEL)]
    out_shape += [jax.ShapeDtypeStruct((bsz, seq, D_MODEL), BF16)] * 4
    for keep in keeps:
        out_specs.append(kv_spec(keep))
        kv_shape = (bsz, 2 * A_WIDTH, keep) if kv_feature_major else (bsz, keep, 2 * A_WIDTH)
        out_shape.append(jax.ShapeDtypeStruct(kv_shape, F32))
    if emit_vn_f32:
        out_specs.append(tok(B_WIDTH))
        out_shape.append(jax.ShapeDtypeStruct((bsz, seq, B_WIDTH), F32))

    ins = [x, sh, sc, gain, w_in, ln_g, ln_b]
    in_specs = [tok(D_MODEL), _mod_spec(sh, tm), _mod_spec(sc, tm), _const_spec(gain),
                pl.BlockSpec(w_in.shape, lambda b, s: (0, 0), pipeline_mode=pl.Buffered(1)),
                _const_spec(ln_g), _const_spec(ln_b)]
    if after is not None:
        ins.append(after)
        in_specs.append(pl.BlockSpec(memory_space=pl.ANY))
    scratch = [pltpu.VMEM((2, COL_CHUNK // LANES, tm, LANES), F32)]
    return pl.pallas_call(
        functools.partial(_inproj_kernel, tm=tm, keeps=keeps, dils=dils, emit_vn_f32=emit_vn_f32,
                          has_after=after is not None, kv_feature_major=kv_feature_major),
        grid=(bsz, n_s),
        in_specs=in_specs,
        out_specs=out_specs,
        out_shape=out_shape,
        scratch_shapes=scratch,
        compiler_params=pltpu.CompilerParams(dimension_semantics=("arbitrary", "arbitrary"),
                                             vmem_limit_bytes=VMEM_LIMIT),
        name="inproj",
    )(*ins)


def _attn_kernel(q_ref, kp_ref, kc_ref, vp_ref, vc_ref, bias0_ref, bias_ref, o_ref, lse_ref, *, n_seq, q_blocks):
    lane = lax.broadcasted_iota(jnp.int32, (A_BLOCK, LANES), 1)
    low = lane < HEAD_DIM
    zero = jnp.zeros((), BF16)
    for seq_i, i in [(a, b) for a in range(n_seq) for b in range(q_blocks)]:
        rows = (seq_i, slice(i * A_BLOCK, (i + 1) * A_BLOCK))
        q = q_ref[rows[0], rows[1], :]
        key_rows = slice((i - 1) * A_BLOCK, (i + 1) * A_BLOCK)
        if i == 0:
            k = jnp.concatenate([kp_ref[seq_i], kc_ref[seq_i, :A_BLOCK, :]], axis=0)
            v = jnp.concatenate([vp_ref[seq_i], vc_ref[seq_i, :A_BLOCK, :]], axis=0)
        else:
            k, v = kc_ref[seq_i, key_rows, :], vc_ref[seq_i, key_rows, :]
        b_ref = bias0_ref if i == 0 else bias_ref
        lse_tile = jnp.zeros((A_BLOCK, LANES), F32)
        for j in range(HEADS // 2):
            pair = slice(j * LANES, (j + 1) * LANES)
            qp, kpair, vpair = q[:, pair], k[:, pair], v[:, pair]
            outs = []
            for e in range(2):
                h = 2 * j + e
                qm = jnp.where(low if e == 0 else jnp.logical_not(low), qp, zero)
                s = lax.dot_general(qm, kpair, (((1,), (1,)), ((), ())), preferred_element_type=F32)
                s = s + b_ref[h]
                m = jnp.max(s, axis=-1, keepdims=True)
                p = jnp.exp2(s - m)
                den = jnp.sum(p, axis=-1, keepdims=True)
                o = jnp.dot(p.astype(BF16), vpair, preferred_element_type=F32)
                first = h * LSE_REP
                lse_tile = jnp.where((lane >= first) & (lane < first + LSE_REP // 2), m,
                                     jnp.where((lane >= first + LSE_REP // 2) & (lane < first + LSE_REP), den,
                                               lse_tile))
                outs.append(o)
            o_ref[rows[0], rows[1], pair] = jnp.where(low, outs[0], outs[1]).astype(BF16)
        lse_ref[rows[0], rows[1], :] = lse_tile


def _attn_bias(g):
    _, dil = A_GROUPS[g]
    n = N_GROUPS * HEADS
    e = np.arange(1, n + 1, dtype=np.float32)
    slopes = np.exp2(-8.0 * e / n).astype(np.float32).reshape(N_GROUPS, HEADS)[g]
    qi = np.arange(A_BLOCK)[:, None]
    ki = np.arange(2 * A_BLOCK)[None, :]
    delta = qi + A_BLOCK - ki
    band = (delta >= 0) & (delta <= A_STEPS)
    dist = (delta * dil).astype(np.float32)
    bias = (-slopes[:, None, None] * dist[None] * np.float32(LOG2E)).astype(np.float32)
    out = np.empty((2, HEADS, A_BLOCK, 2 * A_BLOCK), np.float32)
    out[1] = np.where(band[None], bias, NEG)
    out[0] = np.where((band & (ki >= A_BLOCK))[None], bias, NEG)
    return jnp.asarray(out)


def _attn_prompt(qkv, g):
    bsz, dil, steps, _ = qkv.shape
    qb = min(ATTN_BLOCKS_PER_STEP, steps // A_BLOCK)
    ns = min(ATTN_BLOCKS_PER_STEP // qb, dil)
    rows = qb * A_BLOCK
    n_steps = steps // rows
    bias = _attn_bias(g)

    def blk(t, prev):
        if prev:
            return pl.BlockSpec((None, ns, A_BLOCK, A_WIDTH),
                                lambda b, r, n: (b, r, jnp.maximum(n * qb - 1, 0), t))
        return pl.BlockSpec((None, ns, rows, A_WIDTH), lambda b, r, n: (b, r, n, t))

    bias_blk = lambda pick: pl.BlockSpec((None, HEADS, A_BLOCK, 2 * A_BLOCK), lambda b, r, n: (pick(n), 0, 0, 0))
    return pl.pallas_call(
        functools.partial(_attn_kernel, n_seq=ns, q_blocks=qb),
        grid=(bsz, dil // ns, n_steps),
        in_specs=[blk(0, False), blk(1, True), blk(1, False), blk(2, True), blk(2, False),
                  bias_blk(lambda n: jnp.minimum(n, 1)), bias_blk(lambda n: 1)],
        out_specs=[pl.BlockSpec((None, ns, rows, A_WIDTH), lambda b, r, n: (b, r, n, 0)),
                   pl.BlockSpec((None, ns, rows, LANES), lambda b, r, n: (b, r, n, 0))],
        out_shape=[jax.ShapeDtypeStruct((bsz, dil, steps, A_WIDTH), BF16),
                   jax.ShapeDtypeStruct((bsz, dil, steps, LANES), F32)],
        compiler_params=pltpu.CompilerParams(dimension_semantics=("arbitrary",) * 3),
        name=f"attn_prompt_g{g}",
    )(qkv, qkv, qkv, qkv, qkv, bias, bias)


def _attn_sample_kernel(q0_ref, q1_ref, q2_ref, c0_ref, c1_ref, c2_ref, bc0_ref, bc1_ref, bc2_ref,
                        bn0_ref, bn1_ref, bn2_ref, after_ref, o_ref, *, t_new):
    del after_ref
    n_rows = HEADS * t_new
    row = lax.broadcasted_iota(jnp.int32, (n_rows, A_WIDTH), 0)
    lane = lax.broadcasted_iota(jnp.int32, (n_rows, A_WIDTH), 1)
    head_mask = (row // t_new) == (lane // HEAD_DIM)
    pad = jnp.zeros((LANES - t_new, A_WIDTH), F32)
    outs, lses = [], []
    for qkv_ref, c_ref, bc_ref, bn_ref in ((q0_ref, c0_ref, bc0_ref, bn0_ref), (q1_ref, c1_ref, bc1_ref, bn1_ref),
                                           (q2_ref, c2_ref, bc2_ref, bn2_ref)):
        q = qkv_ref[:, :A_WIDTH].astype(F32)
        k_new = qkv_ref[:, A_WIDTH:2 * A_WIDTH].astype(F32)
        v_new = qkv_ref[:, 2 * A_WIDTH:].astype(F32)
        k_new = jnp.concatenate([k_new, pad], axis=0).astype(BF16)
        v_new = jnp.concatenate([v_new, pad], axis=0).astype(BF16)
        q_rows = jnp.where(head_mask, jnp.concatenate([q] * HEADS, axis=0), 0.0).astype(BF16)
        k_buf_t = c_ref[:A_WIDTH, :].astype(BF16)
        v_buf_t = c_ref[A_WIDTH:, :].astype(BF16)
        nt = (((1,), (1,)), ((), ()))
        s_buf = jnp.dot(q_rows, k_buf_t, preferred_element_type=F32) + bc_ref[...]
        s_new = lax.dot_general(q_rows, k_new, nt, preferred_element_type=F32) + bn_ref[...]
        m = jnp.maximum(jnp.max(s_buf, axis=-1, keepdims=True), jnp.max(s_new, axis=-1, keepdims=True))
        p_buf = jnp.exp2(s_buf - m)
        p_new = jnp.exp2(s_new - m)
        den = jnp.sum(p_buf, axis=-1, keepdims=True) + jnp.sum(p_new, axis=-1, keepdims=True)
        o = (lax.dot_general(p_buf.astype(BF16), v_buf_t, nt, preferred_element_type=F32)
             + jnp.dot(p_new.astype(BF16), v_new, preferred_element_type=F32)) / den
        outs.append(o)
        lses.append(m + jnp.log2(den))
    top = jnp.maximum(jnp.maximum(lses[0], lses[1]), lses[2])
    ws = [jnp.exp2(l - top) for l in lses]
    tot = ws[0] + ws[1] + ws[2]
    acc = (ws[0] / tot) * outs[0] + (ws[1] / tot) * outs[1] + (ws[2] / tot) * outs[2]
    acc = jnp.where(head_mask, acc, 0.0).reshape(HEADS, t_new, A_WIDTH)
    o_ref[...] = jnp.sum(acc, axis=0).astype(BF16)


def _sample_bias(g, t_new, buf):
    _, dil = A_GROUPS[g]
    n = N_GROUPS * HEADS
    e = np.arange(1, n + 1, dtype=np.float32)
    slopes = np.exp2(-8.0 * e / n).astype(np.float32).reshape(N_GROUPS, HEADS)[g]
    t = np.arange(t_new)[:, None]
    idx = np.concatenate([np.arange(buf), buf + np.arange(LANES)])[None, :]
    dist = buf + t - idx
    valid = (dist >= 0) & (dist % dil == 0) & (dist <= A_STEPS * dil) & (idx < buf + t_new)
    bias = -slopes[:, None, None] * dist.astype(np.float32)[None] * np.float32(LOG2E)
    bias = np.where(valid[None], bias, NEG).astype(np.float32).reshape(HEADS * t_new, buf + LANES)
    return jnp.asarray(bias[:, :buf]), jnp.asarray(bias[:, buf:])


def _attn_sample(qkvs, caches, *, after):
    bsz, t_new, _ = qkvs[0].shape
    cache_v = [jnp.transpose(c, (0, 2, 3, 4, 1)).reshape(bsz, 2 * A_WIDTH, c.shape[1]) for c in caches]
    biases = [_sample_bias(g, t_new, cache_v[g].shape[2]) for g in range(N_GROUPS)]
    bcs = [b[0] for b in biases]
    bns = [b[1] for b in biases]
    return pl.pallas_call(
        functools.partial(_attn_sample_kernel, t_new=t_new),
        grid=(bsz,),
        in_specs=[pl.BlockSpec((None, t_new, 3 * A_WIDTH), lambda b: (b, 0, 0))] * N_GROUPS
                 + [pl.BlockSpec((None, 2 * A_WIDTH, c.shape[2]), lambda b: (b, 0, 0)) for c in cache_v]
                 + [_const_spec(b) for b in bcs] + [_const_spec(b) for b in bns]
                 + [pl.BlockSpec(memory_space=pl.ANY)],
        out_specs=pl.BlockSpec((None, t_new, A_WIDTH), lambda b: (b, 0, 0)),
        out_shape=jax.ShapeDtypeStruct((bsz, t_new, A_WIDTH), BF16),
        compiler_params=pltpu.CompilerParams(dimension_semantics=("arbitrary",),
                                             vmem_limit_bytes=VMEM_LIMIT),
        name="attn_sample",
    )(*qkvs, *cache_v, *bcs, *bns, after)


def _first_max4(v):
    top = jnp.maximum(jnp.maximum(v[0], v[1]), jnp.maximum(v[2], v[3]))
    idx = jnp.where(v[0] == top, 0.0, jnp.where(v[1] == top, 1.0, jnp.where(v[2] == top, 2.0, 3.0)))
    return top, idx


def _route(lt):
    row = lambda k: lt[k:k + 1, :]
    g = [row(k) for k in range(MOE_GROUPS)]
    gmax, g_idx = _first_max4(g)
    g_prob = 1.0 / (jnp.exp(g[0] - gmax) + jnp.exp(g[1] - gmax) + jnp.exp(g[2] - gmax) + jnp.exp(g[3] - gmax))
    cand = []
    for k in range(MOE_PER_GROUP):
        c = row(ROUTE_OFF + MOE_PER_GROUP * (MOE_GROUPS - 1) + k)
        for gi in range(MOE_GROUPS - 2, -1, -1):
            c = jnp.where(g_idx == float(gi), row(ROUTE_OFF + MOE_PER_GROUP * gi + k), c)
        cand.append(c)
    e1, i1 = _first_max4(cand)
    rest = [jnp.where(i1 == float(k), -jnp.inf, cand[k]) for k in range(MOE_PER_GROUP)]
    e2, i2 = _first_max4(rest)
    t = jnp.exp(e2 - e1)
    w1 = 1.0 / (1.0 + t)
    w2 = t / (1.0 + t)
    return MOE_PER_GROUP * g_idx + i1, MOE_PER_GROUP * g_idx + i2, w1 * g_prob, w2 * g_prob


def _pack_bf16_pairs(x):
    q = x.shape[1] // 4
    bits = lax.bitcast_convert_type(x.astype(BF16).astype(F32), jnp.int32)
    pack = lambda hi, lo: hi | lax.shift_right_logical(lo, 16)
    return pack(bits[:, :q], bits[:, 2 * q:3 * q]), pack(bits[:, q:2 * q], bits[:, 3 * q:])


def _unpack_bf16_pairs(pa, pb):
    hi = lambda p: lax.bitcast_convert_type(p & jnp.int32(-65536), F32)
    lo = lambda p: lax.bitcast_convert_type(lax.shift_left(p, 16), F32)
    return jnp.concatenate([hi(pa), hi(pb), lo(pa), lo(pb)], axis=1)


def _mix_kernel(*refs, tm, dils, routed, span):
    n_attn = len(dils)
    x_ref = refs[0]
    if n_attn > 1:
        o_refs = refs[1:1 + n_attn]
        l_refs = refs[1 + n_attn:1 + 2 * n_attn]
        exp_ref = refs[1 + 2 * n_attn]
        pos = 2 + 2 * n_attn
    else:
        o_refs = refs[1:2]
        pos = 2
    (u_ref, vn_ref, ga_ref, gb_ref, sw_ref, sbt_ref, wpa_ref, wpb_ref, wo_ref,
     gt1_ref, sh2_ref, sc2_ref, n2g_ref, wrh_ref, wrl_ref, br_ref) = refs[pos:pos + 16]
    n_out = 6 if routed else 3
    out_refs = refs[pos + 16:pos + 16 + n_out]
    scr = list(refs[pos + 16 + n_out:])
    x1_ref = out_refs[0]
    ob_scr = scr.pop(0)

    def to_positions(ref, scr, dil):
        if dil == 1:
            return ref[0].astype(F32)
        n_slab = scr.shape[0]
        for r in range(dil):
            rows = ref[r].astype(F32)
            for j in range(n_slab):
                scr[j, pl.ds(r, tm // dil, stride=dil), :] = rows[:, j * LANES:(j + 1) * LANES]
        return jnp.concatenate([scr[j] for j in range(n_slab)], axis=1)

    if n_attn > 1:
        o_scr, l_scr = scr[0], scr[1]
        stats = [to_positions(l_refs[g], l_scr.at[g], dils[g]) for g in range(n_attn)]
        lane = lax.broadcasted_iota(jnp.int32, stats[0].shape, 1)
        is_max = (lane & (LSE_REP - 1)) < LSE_REP // 2
        half = LSE_REP // 2
        maxes = [jnp.where(is_max, s, pltpu.roll(s, half, axis=1)) for s in stats]
        dens = [jnp.where(is_max, pltpu.roll(s, LANES - half, axis=1), s) for s in stats]
        lses = [m + jnp.log2(d) for m, d in zip(maxes, dens)]
        top = functools.reduce(jnp.maximum, lses)
        tot = functools.reduce(lambda a, b: a + b, [jnp.exp2(l - top) for l in lses])
        o_a = None
        for g in range(n_attn):
            w = jnp.exp2(maxes[g] - top) / tot
            hi = w.astype(BF16)
            lo = (w - hi.astype(F32)).astype(BF16)
            w_exp = jnp.dot(jnp.concatenate([hi, lo], axis=1), exp_ref[...], preferred_element_type=F32)
            term = w_exp * to_positions(o_refs[g], o_scr.at[g], dils[g])
            o_a = term if o_a is None else o_a + term
        o_a = o_a.astype(BF16)
    else:
        o_a = o_refs[0][...]

    r_i = lax.broadcasted_iota(jnp.int32, (span, span), 0)
    c_i = lax.broadcasted_iota(jnp.int32, (span, span), 1)
    tril = r_i >= c_i
    for gi in range(B_GROUPS):
        gcols = slice(gi * LANES, (gi + 1) * LANES)
        b_s = sbt_ref[:span, gi:gi + 1]
        if span == B_CHUNK:
            w_s = jnp.where(tril, sw_ref[gi], 0.0).astype(BF16)
            n_chunk = tm // span
            vn_wide = jnp.concatenate([vn_ref[c * span:(c + 1) * span, gcols] for c in range(n_chunk)], axis=1)
            mixed = jnp.dot(w_s, vn_wide, preferred_element_type=F32) + b_s
            for c in range(n_chunk):
                rows = slice(c * span, (c + 1) * span)
                ob_scr[rows, gcols] = (u_ref[rows, gcols].astype(F32)
                                       * mixed[:, c * LANES:(c + 1) * LANES]).astype(BF16)
        else:
            w_s = jnp.where(tril, sw_ref[gi, :span, :span], 0.0)
            vn3 = vn_ref[:, gcols].astype(F32).reshape(tm // span, span, LANES)
            mixed = jnp.broadcast_to(b_s, (span, LANES))[None]
            for s in range(span):
                w_col = jnp.broadcast_to(w_s[:, s:s + 1], (span, LANES))[None]
                mixed = mixed + w_col * jnp.broadcast_to(vn3[:, s:s + 1, :], vn3.shape)
            u3 = u_ref[:, gcols].astype(F32).reshape(tm // span, span, LANES)
            ob_scr[:, gcols] = (u3 * mixed).reshape(tm, LANES).astype(BF16)

    sub = min(tm, MIX_SUB_ROWS)
    if routed:
        carry = scr[-1]
        first_step = (pl.program_id(0) == 0) & (pl.program_id(1) == 0)
        seen = jnp.where(first_step, 0.0, carry[:, 0:1])
        r_i = lax.broadcasted_iota(jnp.int32, (sub, sub), 0)
        c_i = lax.broadcasted_iota(jnp.int32, (sub, sub), 1)
        earlier = jnp.where(r_i < c_i, 1.0, 0.0).astype(BF16)
        e_iota = lax.broadcasted_iota(jnp.int32, (MOE_EXPERTS, sub), 0).astype(F32)
    nt = (((1,), (1,)), ((), ()))
    for r0 in range(0, tm, sub):
        rs = slice(r0, r0 + sub)
        mod = lambda ref: ref[...] if ref.shape[0] == 1 else ref[rs, :]
        pa = jnp.dot(o_a[rs], wpa_ref[...], preferred_element_type=F32)
        pb = jnp.dot(ob_scr[rs, :], wpb_ref[...], preferred_element_type=F32)
        mix = (ga_ref[rs, :].astype(F32) * pa + gb_ref[rs, :].astype(F32) * pb).astype(BF16)
        x1 = x_ref[rs, :] + mod(gt1_ref) * jnp.dot(mix, wo_ref[...], preferred_element_type=F32)
        x1_ref[rs, :] = x1
        h2 = _rmsnorm_mod(x1, n2g_ref[...], mod(sh2_ref), mod(sc2_ref))
        h_hi = h2.astype(BF16)
        h_lo = (h2 - h_hi.astype(F32)).astype(BF16)
        lt = (lax.dot_general(wrh_ref[...], h_hi, nt, preferred_element_type=F32)
              + lax.dot_general(wrh_ref[...], h_lo, nt, preferred_element_type=F32)
              + lax.dot_general(wrl_ref[...], h_hi, nt, preferred_element_type=F32)) + br_ref[...]
        ex1, ex2, w1, w2 = _route(lt)
        if not routed:
            _, h2_ref, comb_ref = out_refs
            h2_ref[rs, :] = h_hi
            e_lane = lax.broadcasted_iota(jnp.int32, (LANES, sub), 0).astype(F32) - float(ROUTE_OFF)
            comb_t = jnp.where(e_lane == ex1, w1, jnp.where(e_lane == ex2, w2, 0.0))
            comb_ref[rs, :] = jnp.transpose(comb_t)
            continue

        _, hpa_ref, hpb_ref, route_ref, wcol_ref, cnt_ref = out_refs
        hpa_ref[rs, :], hpb_ref[rs, :] = _pack_bf16_pairs(h2)
        hot1, hot2 = e_iota == ex1, e_iota == ex2
        onehot = jnp.where(hot1 | hot2, 1.0, 0.0)
        rank = jnp.dot(onehot.astype(BF16), earlier, preferred_element_type=F32) + seen
        r1 = jnp.sum(jnp.where(hot1, rank, 0.0), axis=0, keepdims=True)
        r2 = jnp.sum(jnp.where(hot2, rank, 0.0), axis=0, keepdims=True)
        seen = seen + jnp.sum(onehot, axis=1, keepdims=True)
        fields = jnp.concatenate([ex1, ex2, w1, w2, r1, r2, jnp.zeros((2, sub), F32)], axis=0)
        route_ref[:, rs] = fields
        wcol_ref[rs, :] = jnp.transpose(jnp.concatenate([fields, jnp.zeros((LANES - 8, sub), F32)], axis=0))
    if routed:
        carry[...] = jnp.broadcast_to(seen, carry.shape)
        cnt_ref[...] = jnp.broadcast_to(seen, cnt_ref.shape)


def _expand_matrix():
    e = np.zeros((LANES, A_WIDTH), np.float32)
    for h in range(HEADS):
        e[h * LSE_REP, h * HEAD_DIM:(h + 1) * HEAD_DIM] = 1.0
    return jnp.asarray(np.concatenate([e, e], axis=0), dtype=BF16)


def _mix(x, attn_outs, attn_lses, u, vn, ga, gb, sgu_w, sgu_bt, w_pa, w_pb, w_o,
         gt1, sh2, sc2, n2g, w_r_hi, w_r_lo, b_r, *, tm, routed, span):
    bsz, seq, _ = x.shape
    n_attn = len(attn_outs)
    tok = lambda width: pl.BlockSpec((None, tm, width), lambda b, s: (b, s, 0))
    ins = [x] + list(attn_outs)
    scratch = [pltpu.VMEM((tm, B_WIDTH), BF16)]
    if n_attn > 1:
        dils = tuple(o.shape[1] for o in attn_outs)
        res = lambda dil, width: pl.BlockSpec((None, dil, tm // dil, width), lambda b, s: (b, 0, s, 0))
        e2 = _expand_matrix()
        ins += list(attn_lses) + [e2]
        specs = ([tok(D_MODEL)] + [res(d, A_WIDTH) for d in dils] + [res(d, LANES) for d in dils]
                 + [_const_spec(e2)])
        scratch += [pltpu.VMEM((n_attn, A_WIDTH // LANES, tm, LANES), F32), pltpu.VMEM((n_attn, 1, tm, LANES), F32)]
    else:
        dils = (1,)
        specs = [tok(D_MODEL), tok(A_WIDTH)]
    ins += [u, vn, ga, gb, sgu_w, sgu_bt, w_pa, w_pb, w_o, gt1, sh2, sc2, n2g, w_r_hi, w_r_lo, b_r]
    specs += [tok(B_WIDTH), tok(B_WIDTH), tok(D_MODEL), tok(D_MODEL),
              _const_spec(sgu_w), _const_spec(sgu_bt), _const_spec(w_pa), _const_spec(w_pb), _const_spec(w_o),
              _mod_spec(gt1, tm), _mod_spec(sh2, tm), _mod_spec(sc2, tm), _const_spec(n2g),
              _const_spec(w_r_hi), _const_spec(w_r_lo), _const_spec(b_r)]
    if routed:
        quarter = D_MODEL // 4
        out_specs = [tok(D_MODEL), tok(quarter), tok(quarter),
                     pl.BlockSpec((None, 8, tm), lambda b, s: (b, 0, s)), tok(LANES),
                     pl.BlockSpec((MOE_EXPERTS, LANES), lambda b, s: (0, 0))]
        out_shape = [jax.ShapeDtypeStruct((bsz, seq, D_MODEL), F32),
                     jax.ShapeDtypeStruct((bsz, seq, quarter), jnp.int32),
                     jax.ShapeDtypeStruct((bsz, seq, quarter), jnp.int32),
                     jax.ShapeDtypeStruct((bsz, 8, seq), F32),
                     jax.ShapeDtypeStruct((bsz, seq, LANES), F32),
                     jax.ShapeDtypeStruct((MOE_EXPERTS, LANES), F32)]
        scratch.append(pltpu.VMEM((MOE_EXPERTS, LANES), F32))
    else:
        out_specs = [tok(D_MODEL), tok(D_MODEL), tok(LANES)]
        out_shape = [jax.ShapeDtypeStruct((bsz, seq, D_MODEL), F32),
                     jax.ShapeDtypeStruct((bsz, seq, D_MODEL), BF16),
                     jax.ShapeDtypeStruct((bsz, seq, LANES), F32)]
    return pl.pallas_call(
        functools.partial(_mix_kernel, tm=tm, dils=dils, routed=routed, span=span),
        grid=(bsz, seq // tm),
        in_specs=specs,
        out_specs=out_specs,
        out_shape=out_shape,
        scratch_shapes=scratch,
        compiler_params=pltpu.CompilerParams(dimension_semantics=("arbitrary", "arbitrary"),
                                             vmem_limit_bytes=VMEM_LIMIT),
        name="mix",
    )(*ins)


def _moe_kernel(h_ref, comb_ref, wg_ref, wu_ref, wd_ref, x1_ref, gt2_ref, shf_ref, scf_ref, nfg_ref,
                y_ref, acc_ref):
    e = pl.program_id(2)

    @pl.when(e == 0)
    def _():
        acc_ref[...] = jnp.zeros_like(acc_ref)

    h = h_ref[...]
    hg = jnp.dot(h, wg_ref[...].astype(BF16), preferred_element_type=F32)
    hu = jnp.dot(h, wu_ref[...].astype(BF16), preferred_element_type=F32)
    comb = comb_ref[...]
    lane = lax.broadcasted_iota(jnp.int32, comb.shape, 1)
    cw = jnp.sum(jnp.where(lane == e + ROUTE_OFF, comb, 0.0), axis=-1, keepdims=True)
    act = (hg * _sigmoid(hg) * hu) * cw
    acc_ref[...] += jnp.dot(act.astype(BF16), wd_ref[...].astype(BF16), preferred_element_type=F32)

    @pl.when(e == MOE_EXPERTS - 1)
    def _():
        x2 = x1_ref[...] + gt2_ref[...] * acc_ref[...]
        y_ref[...] = _rmsnorm_mod(x2, nfg_ref[...], shf_ref[...], scf_ref[...])


def _moe(h2, comb, w_gate, w_up, w_down, x1, gt2, shf, scf, nfg, *, tm):
    bsz, seq, _ = h2.shape
    tok = lambda width: pl.BlockSpec((None, tm, width), lambda b, s, e: (b, s, 0))
    return pl.pallas_call(
        _moe_kernel,
        grid=(bsz, seq // tm, MOE_EXPERTS),
        in_specs=[tok(D_MODEL), tok(LANES),
                  pl.BlockSpec((None, D_MODEL, MOE_HIDDEN), lambda b, s, e: (e, 0, 0)),
                  pl.BlockSpec((None, D_MODEL, MOE_HIDDEN), lambda b, s, e: (e, 0, 0)),
                  pl.BlockSpec((None, MOE_HIDDEN, D_MODEL), lambda b, s, e: (e, 0, 0)),
                  tok(D_MODEL), _mod_spec(gt2, tm), _mod_spec(shf, tm), _mod_spec(scf, tm), _const_spec(nfg)],
        out_specs=tok(D_MODEL),
        out_shape=jax.ShapeDtypeStruct((bsz, seq, D_MODEL), F32),
        scratch_shapes=[pltpu.VMEM((tm, D_MODEL), F32)],
        compiler_params=pltpu.CompilerParams(dimension_semantics=("arbitrary",) * 3,
                                             vmem_limit_bytes=VMEM_LIMIT),
        name="moe",
    )(h2, comb, w_gate, w_up, w_down, x1, gt2, shf, scf, nfg)


def _sc_mesh():
    return plsc.VectorSubcoreMesh(core_axis_name="c", subcore_axis_name="s")


def _sc_scatter_rows(rows, pos1, pos2, n_out):
    n, width = rows.shape
    steps = n // SC_WINDOW // SC_CORES

    @pl.kernel(out_type=jax.ShapeDtypeStruct((n_out, width), rows.dtype), mesh=_sc_mesh(), scratch_types=[])
    def scatter(rows_hbm, p1_hbm, p2_hbm, out_hbm):
        def body(x_vmem, i1_vmem, i2_vmem):
            pltpu.sync_copy(x_vmem, out_hbm.at[i1_vmem.at[0]])
            pltpu.sync_copy(x_vmem, out_hbm.at[i2_vmem.at[0]])

        pltpu.emit_pipeline(
            body, grid=(SC_CORES, steps),
            in_specs=[pl.BlockSpec((SC_WINDOW, width), lambda c, i: (c * steps + i, 0)),
                      pl.BlockSpec((1, SC_WINDOW), lambda c, i: (0, c * steps + i)),
                      pl.BlockSpec((1, SC_WINDOW), lambda c, i: (0, c * steps + i))],
            out_specs=[],
            core_axis_name=("c", "s"),
            dimension_semantics=(pltpu.PARALLEL, pltpu.PARALLEL),
        )(rows_hbm, p1_hbm, p2_hbm)

    return scatter(rows, pos1, pos2)


def _sc_gather_rows(table, idx):
    m, width = idx.shape[1], table.shape[1]
    steps = m // SC_WINDOW // SC_CORES

    @pl.kernel(out_type=jax.ShapeDtypeStruct((m, width), table.dtype), mesh=_sc_mesh(), scratch_types=[])
    def gather(table_hbm, idx_hbm, out_hbm):
        def body(i_vmem, o_vmem):
            pltpu.sync_copy(table_hbm.at[i_vmem.at[0]], o_vmem)

        pltpu.emit_pipeline(
            body, grid=(SC_CORES, steps),
            in_specs=[pl.BlockSpec((1, SC_WINDOW), lambda c, i: (0, c * steps + i))],
            out_specs=[pl.BlockSpec((SC_WINDOW, width), lambda c, i: (c * steps + i, 0))],
            core_axis_name=("c", "s"),
            dimension_semantics=(pltpu.PARALLEL, pltpu.PARALLEL),
        )(idx_hbm, out_hbm)

    return gather(table, idx)


def _ffn_kernel(te_ref, nt_ref, xa_ref, xb_ref, wg_ref, wu_ref, wd_ref, *rest):
    oa_ref, ob_ref, wg_s, wu_s, wd_s = rest[-5:]
    t = pl.program_id(0)
    live = t < nt_ref[0]

    @pl.when(live & ((t == 0) | (te_ref[t] != te_ref[jnp.maximum(t - 1, 0)])))
    def _():
        wg_s[...] = wg_ref[...].astype(BF16)
        wu_s[...] = wu_ref[...].astype(BF16)
        wd_s[...] = wd_ref[...].astype(BF16)

    @pl.when(live)
    def _():
        x = _unpack_bf16_pairs(xa_ref[...], xb_ref[...]).astype(BF16)
        hg = jnp.dot(x, wg_s[...], preferred_element_type=F32)
        hu = jnp.dot(x, wu_s[...], preferred_element_type=F32)
        act = (hg * _sigmoid(hg) * hu).astype(BF16)
        oa_ref[...], ob_ref[...] = _pack_bf16_pairs(jnp.dot(act, wd_s[...], preferred_element_type=F32))


def _ffn(xa, xb, tile_expert, n_tiles, w_gate, w_up, w_down, *, after):
    n_rows, quarter = xa.shape
    rows = pl.BlockSpec((MOE_TILE, quarter), lambda t, te, nt: (jnp.minimum(t, nt[0] - 1), 0))
    return pl.pallas_call(
        _ffn_kernel,
        grid_spec=pltpu.PrefetchScalarGridSpec(
            num_scalar_prefetch=2,
            grid=(n_rows // MOE_TILE,),
            in_specs=[rows, rows,
                      pl.BlockSpec((None, D_MODEL, MOE_HIDDEN), lambda t, te, nt: (te[t], 0, 0)),
                      pl.BlockSpec((None, D_MODEL, MOE_HIDDEN), lambda t, te, nt: (te[t], 0, 0)),
                      pl.BlockSpec((None, MOE_HIDDEN, D_MODEL), lambda t, te, nt: (te[t], 0, 0))]
                     + [pl.BlockSpec(memory_space=pl.ANY)] * len(after),
            out_specs=[rows, rows],
            scratch_shapes=[pltpu.VMEM((D_MODEL, MOE_HIDDEN), BF16), pltpu.VMEM((D_MODEL, MOE_HIDDEN), BF16),
                            pltpu.VMEM((MOE_HIDDEN, D_MODEL), BF16)]),
        out_shape=[jax.ShapeDtypeStruct((n_rows, quarter), jnp.int32)] * 2,
        compiler_params=pltpu.CompilerParams(dimension_semantics=("arbitrary",), vmem_limit_bytes=VMEM_LIMIT),
        name="moe_ffn",
    )(tile_expert, n_tiles, xa, xb, w_gate, w_up, w_down, *after)


def _final_kernel(x1_ref, g1a_ref, g1b_ref, g2a_ref, g2b_ref, route_ref, gt2_ref, shf_ref, scf_ref, nfg_ref, y_ref):
    o1 = _unpack_bf16_pairs(g1a_ref[...], g1b_ref[...])
    o2 = _unpack_bf16_pairs(g2a_ref[...], g2b_ref[...])
    route = route_ref[...]
    moe = route[:, 2:3] * o1 + route[:, 3:4] * o2
    x2 = x1_ref[...] + gt2_ref[...] * moe
    y_ref[...] = _rmsnorm_mod(x2, nfg_ref[...], shf_ref[...], scf_ref[...])


def _final(x1, ga, gb, route, gt2, shf, scf, nfg, *, tm):
    bsz, seq, _ = x1.shape
    n_s = seq // tm
    n_blk = bsz * n_s
    quarter = ga.shape[1]
    tok = lambda width: pl.BlockSpec((None, tm, width), lambda b, s: (b, s, 0))
    first = pl.BlockSpec((tm, quarter), lambda b, s: (b * n_s + s, 0))
    second = pl.BlockSpec((tm, quarter), lambda b, s: (n_blk + b * n_s + s, 0))
    return pl.pallas_call(
        _final_kernel,
        grid=(bsz, n_s),
        in_specs=[tok(D_MODEL), first, first, second, second, tok(LANES),
                  _mod_spec(gt2, tm), _mod_spec(shf, tm), _mod_spec(scf, tm), _const_spec(nfg)],
        out_specs=tok(D_MODEL),
        out_shape=jax.ShapeDtypeStruct((bsz, seq, D_MODEL), F32),
        compiler_params=pltpu.CompilerParams(dimension_semantics=("arbitrary", "arbitrary")),
        name="moe_final",
    )(x1, ga, gb, ga, gb, route, gt2, shf, scf, nfg)


def _positions_kernel(base_ref, route_ref, p1_ref, p2_ref):
    rows = route_ref[...]
    for e_row, r_row, out_ref in ((0, 4, p1_ref), (1, 5, p2_ref)):
        expert = rows[e_row:e_row + 1, :]
        start = jnp.zeros(expert.shape, jnp.int32)
        for e in range(MOE_EXPERTS):
            start = jnp.where(expert == float(e), base_ref[e], start)
        out_ref[...] = start + rows[r_row:r_row + 1, :].astype(jnp.int32)


def _positions(route_rows, base):
    bsz, _, seq = route_rows.shape
    out = pl.BlockSpec((None, 1, seq), lambda b, base: (b, 0, 0))
    return pl.pallas_call(
        _positions_kernel,
        grid_spec=pltpu.PrefetchScalarGridSpec(
            num_scalar_prefetch=1, grid=(bsz,),
            in_specs=[pl.BlockSpec((None, 8, seq), lambda b, base: (b, 0, 0))],
            out_specs=[out, out]),
        out_shape=[jax.ShapeDtypeStruct((bsz, 1, seq), jnp.int32)] * 2,
        compiler_params=pltpu.CompilerParams(dimension_semantics=("arbitrary",)),
        name="moe_positions",
    )(base, route_rows)


def _moe_dispatch(hpa, hpb, route_rows, counts):
    bsz, seq, quarter = hpa.shape
    n_tok = bsz * seq
    t_max = 2 * n_tok // MOE_TILE + MOE_EXPERTS
    cnt = counts[:, 0].astype(jnp.int32)
    tiles_e = (cnt + MOE_TILE - 1) // MOE_TILE
    tile_end = jnp.cumsum(tiles_e)
    n_tiles = tile_end[-1:]
    base = (tile_end - tiles_e) * MOE_TILE
    t_ids = jnp.minimum(jnp.arange(t_max, dtype=jnp.int32), n_tiles[0] - 1)
    tile_expert = jnp.sum((t_ids[:, None] >= tile_end[None, :]).astype(jnp.int32), axis=1)
    pos1, pos2 = (p.reshape(1, n_tok) for p in _positions(route_rows, base))
    n_rows = t_max * MOE_TILE
    xa = _sc_scatter_rows(hpa.reshape(n_tok, quarter), pos1, pos2, n_rows)
    xb = _sc_scatter_rows(hpb.reshape(n_tok, quarter), pos1, pos2, n_rows)
    return xa, xb, jnp.concatenate([pos1, pos2], axis=1), tile_expert, n_tiles


def kernel(x_prompt, x_sample, cache_kv_w128, cache_kv_w512, cache_kv_w2048, c_prompt, c_sample,
           w_ada, b_ada, norm1_g, norm2_g, w_in, sgu_ln_g, sgu_ln_b, sgu_w, sgu_b, w_pa, w_pb, w_o,
           w_route_group, b_route_group, w_route_expert, b_route_expert, w_gate, w_up, w_down,
           normf_g, w_ada_final, b_ada_final):
    depth = w_ada.shape[0]
    assert depth == 1
    l = 0
    bp, seq, _ = x_prompt.shape
    bs, t_new, _ = x_sample.shape
    n_samp = bs * t_new

    c_all = jnp.concatenate([c_prompt, c_sample], axis=0)
    pad_rows = (-c_all.shape[0]) % 8
    c_all = jnp.pad(c_all, ((0, pad_rows), (0, 0)))
    mod = _adaln(c_all, w_ada[l], b_ada[l])

    def split_mods(m, n, lo, hi, per_token):
        parts = jnp.split(m[lo:hi], n, axis=-1)
        if per_token:
            return [jnp.repeat(p, t_new, axis=0).reshape(1, n_samp, D_MODEL) for p in parts]
        return [p.reshape(hi - lo, 1, D_MODEL) for p in parts]

    mods_p = split_mods(mod, 6, 0, bp, False)
    mods_s = split_mods(mod, 6, bp, bp + bs, True)

    row = lambda v: v.reshape(1, -1)
    w_in_b = w_in[l].astype(BF16)
    w_pa_b, w_pb_b, w_o_b = w_pa[l].astype(BF16), w_pb[l].astype(BF16), w_o[l].astype(BF16)
    w_gate_b, w_up_b, w_down_b = w_gate[l], w_up[l], w_down[l]
    w_re = jnp.transpose(w_route_expert[l], (0, 2, 1)).reshape(MOE_EXPERTS, D_MODEL)
    w_r = jnp.pad(jnp.concatenate([jnp.transpose(w_route_group[l]), w_re], axis=0),
                  ((0, ROUTE_ROWS - MOE_GROUPS - MOE_EXPERTS), (0, 0)))
    w_r_hi = w_r.astype(BF16)
    w_r_lo = (w_r - w_r_hi.astype(F32)).astype(BF16)
    b_r = jnp.pad(jnp.concatenate([b_route_group[l], b_route_expert[l].reshape(-1)]),
                  (0, ROUTE_ROWS - MOE_GROUPS - MOE_EXPERTS)).reshape(ROUTE_ROWS, 1)

    def inproj(x, mods, keeps, dils, tm, emit_vn_f32, kv_feature_major, after=None):
        return _inproj(x, mods[0], mods[1], row(norm1_g[l]), w_in_b, row(sgu_ln_g[l]), row(sgu_ln_b[l]),
                       tm=tm, keeps=keeps, dils=dils, emit_vn_f32=emit_vn_f32,
                       kv_feature_major=kv_feature_major, after=after)

    sgu_bt = jnp.transpose(sgu_b[l])

    def mix(x, mods, pin, attn_outs, attn_lses, tm, routed, span):
        u, vn, ga, gb = pin[3:7]
        return _mix(x, attn_outs, attn_lses, u, vn, ga, gb, sgu_w[l], sgu_bt,
                    w_pa_b, w_pb_b, w_o_b, mods[2], mods[3], mods[4], row(norm2_g[l]), w_r_hi, w_r_lo, b_r,
                    tm=tm, routed=routed, span=span)

    tm_p = 512
    keeps_p = tuple(min(win, seq) for win, _ in A_GROUPS)
    dils_p = tuple(dil for _, dil in A_GROUPS)
    pin = inproj(x_prompt, mods_p, keeps_p, dils_p, 256, False, True)
    attn_p = [_attn_prompt(pin[g], g) for g in range(N_GROUPS)]
    x1, hpa, hpb, route_rows, route, counts = mix(x_prompt, mods_p, pin, [r[0] for r in attn_p],
                                                  [r[1] for r in attn_p], tm_p, True, B_CHUNK)

    xs = x_sample.reshape(1, n_samp, D_MODEL)
    sin = inproj(xs, mods_s, (n_samp,) * N_GROUPS, (1,) * N_GROUPS, n_samp, True, False, after=counts)
    mod_f = _adaln(c_all, w_ada_final, b_ada_final, after=counts)
    fin_p = split_mods(mod_f, 2, 0, bp, False)
    fin_s = split_mods(mod_f, 2, bp, bp + bs, True)

    xa, xb, pos, tile_expert, n_tiles = _moe_dispatch(hpa, hpb, route_rows, counts)
    oa, ob = _ffn(xa, xb, tile_expert, n_tiles, w_gate_b, w_up_b, w_down_b, after=(sin[0], fin_s[0]))

    caches = (cache_kv_w128[l], cache_kv_w512[l], cache_kv_w2048[l])
    o_s = _attn_sample([q.reshape(bs, t_new, 3 * A_WIDTH) for q in sin[:3]], caches, after=oa)
    x1_s, h2_s, comb_s = mix(xs, mods_s, sin, [o_s.reshape(1, n_samp, A_WIDTH)], None, n_samp, False,
                             min(t_new, B_CHUNK))
    y_s = _moe(h2_s, comb_s, w_gate_b, w_up_b, w_down_b, x1_s, mods_s[5], fin_s[0], fin_s[1], row(normf_g),
               tm=n_samp)

    ga_rows = _sc_gather_rows(oa, pos)
    gb_rows = _sc_gather_rows(ob, pos)
    y_p = _final(x1, ga_rows, gb_rows, route, mods_p[5], fin_p[0], fin_p[1], row(normf_g), tm=tm_p)
    kv_p, kv_s, extra = pin[7:10], sin[7:10], sin[10:]

    def kv_out(a, b):
        return a.reshape(depth, b, -1, 2, HEADS, HEAD_DIM)

    def kv_out_t(a, b):
        a = a.reshape(b, 2, HEADS, HEAD_DIM, a.shape[-1])
        return jnp.transpose(a, (0, 4, 1, 2, 3)).reshape(depth, b, -1, 2, HEADS, HEAD_DIM)

    return (y_p, y_s.reshape(bs, t_new, D_MODEL),
            kv_out_t(kv_p[0], bp), kv_out_t(kv_p[1], bp), kv_out_t(kv_p[2], bp),
            kv_out(kv_s[0], bs), kv_out(kv_s[1], bs), kv_out(kv_s[2], bs),
            extra[0].reshape(depth, bs, t_new, B_WIDTH))
```

```python
import functools

import numpy as np
import jax
import jax.numpy as jnp
from jax import lax
from jax.experimental import pallas as pl
from jax.experimental.pallas import tpu as pltpu
from jax.experimental.pallas import tpu_sc as plsc

F32 = jnp.float32
BF16 = jnp.bfloat16

D_MODEL = 1024
A_GROUPS = ((128, 1), (512, 4), (2048, 16))
N_GROUPS = 3
HEADS = 8
HEAD_DIM = 64
A_WIDTH = HEADS * HEAD_DIM
A_STEPS = 128
A_BLOCK = 128
B_WIDTH = 1024
B_GROUPS = 8
B_CHUNK = 128
MOE_GROUPS = 4
MOE_PER_GROUP = 4
MOE_EXPERTS = 16
MOE_HIDDEN = 512
EPS = 1e-6
N_QKV = 3 * N_GROUPS * A_WIDTH
IN_COLS = N_QKV + 2 * B_WIDTH + 2 * D_MODEL
COL_CHUNK = 512
LANES = 128
LSE_REP = LANES // HEADS
ROUTE_OFF = MOE_GROUPS
ROUTE_ROWS = 32
NEG = -1e30
LOG2E = 1.4426950408889634
VMEM_LIMIT = 56 * 1024 * 1024
ATTN_BLOCKS_PER_STEP = 16
MIX_SUB_ROWS = 512
MOE_TILE = 512
SC_WINDOW = 128
SC_CORES = 2


def _sigmoid(x):
    return 1.0 / (1.0 + jnp.exp(-x))


def _gelu_tanh(x):
    return x * (0.5 * (1.0 + jnp.tanh(0.7978845608028654 * (x + 0.044715 * (x * x * x)))))


def _rmsnorm_mod(x, gain, shift, scale):
    y = x * lax.rsqrt(jnp.mean(x * x, axis=-1, keepdims=True) + EPS)
    return y * gain * (1.0 + scale) + shift


def _mod_spec(mod, tm, seq):
    if mod.shape[1] == 1:
        return pl.BlockSpec((None, 1, D_MODEL), lambda b, s, *_: (b, 0, 0))
    return pl.BlockSpec((None, tm * mod.shape[1] // seq, D_MODEL), lambda b, s, *_: (b, s, 0))


def _mod_rows(ref, rows):
    m = ref[...]
    if m.shape[0] in (1, rows):
        return m
    rep = rows // m.shape[0]
    return jnp.broadcast_to(m[:, None, :], (m.shape[0], rep, m.shape[1])).reshape(rows, m.shape[1])


def _const_spec(arr):
    nd = arr.ndim
    return pl.BlockSpec(arr.shape, lambda *_: (0,) * nd)


def _adaln_kernel(c_ref, w_ref, b_ref, *rest):
    o_ref = rest[-1]
    c = c_ref[...]
    a = (c * _sigmoid(c)).astype(BF16)
    o_ref[...] = jnp.dot(a, w_ref[...].astype(BF16), preferred_element_type=F32) + b_ref[...]


def _adaln(c, w, b, tn=1024, after=None):
    rows, ncols = c.shape[0], w.shape[1]
    ins = [c, w, b.reshape(1, ncols)]
    in_specs = [pl.BlockSpec((rows, D_MODEL), lambda j: (0, 0)),
                pl.BlockSpec((D_MODEL, tn), lambda j: (0, j)),
                pl.BlockSpec((1, tn), lambda j: (0, j))]
    if after is not None:
        ins.append(after)
        in_specs.append(pl.BlockSpec(memory_space=pl.ANY))
    return pl.pallas_call(
        _adaln_kernel,
        grid=(ncols // tn,),
        in_specs=in_specs,
        out_specs=pl.BlockSpec((rows, tn), lambda j: (0, j)),
        out_shape=jax.ShapeDtypeStruct((rows, ncols), F32),
        compiler_params=pltpu.CompilerParams(dimension_semantics=("arbitrary",)),
        name="adaln",
    )(*ins)


def _inproj_kernel(x_ref, sh_ref, sc_ref, g_ref, w_ref, lng_ref, lnb_ref, *refs, tm, keeps, dils, emit_vn_f32,
                   has_after, kv_feature_major):
    if has_after:
        refs = refs[1:]
    qkv0_ref, qkv1_ref, qkv2_ref, u_ref, vn_ref, ga_ref, gb_ref, kv0_ref, kv1_ref, kv2_ref = refs[:10]
    rest = refs[10:]
    perm_scr = rest[-1]
    h = _rmsnorm_mod(x_ref[...], g_ref[...], _mod_rows(sh_ref, tm), _mod_rows(sc_ref, tm)).astype(BF16)

    def proj(c):
        return jnp.dot(h, w_ref[:, c * COL_CHUNK:(c + 1) * COL_CHUNK], preferred_element_type=F32)

    def cols(c):
        return slice(c * COL_CHUNK, (c + 1) * COL_CHUNK)

    qkv_refs = (qkv0_ref, qkv1_ref, qkv2_ref)

    def put_qkv(g, t, z):
        dil = dils[g]
        if dil == 1:
            qkv_refs[g][0, :, cols(t)] = z.astype(BF16)
            return
        slot = perm_scr.at[t % 2]
        n_slab = COL_CHUNK // LANES
        for j in range(n_slab):
            slot[j] = z[:, j * LANES:(j + 1) * LANES]
        for r in range(dil):
            rows = jnp.concatenate([slot[j, pl.ds(r, tm // dil, stride=dil), :] for j in range(n_slab)], axis=1)
            qkv_refs[g][r, :, cols(t)] = rows.astype(BF16)

    for g in range(N_GROUPS):
        put_qkv(g, 0, proj(g) * (HEAD_DIM ** -0.5 * LOG2E))

    kv_refs = (kv0_ref, kv1_ref, kv2_ref)
    for g in range(N_GROUPS):
        keep = keeps[g]
        for t in (1, 2):
            z = proj(t * N_GROUPS + g)
            put_qkv(g, t, z)
            if kv_feature_major:
                zt = jnp.transpose(z if keep >= tm else z[tm - keep:, :])
                kv_refs[g][(t - 1) * A_WIDTH:t * A_WIDTH, :] = zt
            else:
                kv_refs[g][:, cols(t - 1)] = z if keep >= tm else z[tm - keep:, :]

    base = N_QKV // COL_CHUNK
    for c in range(2):
        u_ref[:, cols(c)] = _gelu_tanh(proj(base + c)).astype(BF16)

    vs = [_gelu_tanh(proj(base + 2 + c)) for c in range(2)]
    mu = (jnp.sum(vs[0], axis=-1, keepdims=True) + jnp.sum(vs[1], axis=-1, keepdims=True)) * (1.0 / B_WIDTH)
    ds = [v - mu for v in vs]
    var = (jnp.sum(ds[0] * ds[0], axis=-1, keepdims=True)
           + jnp.sum(ds[1] * ds[1], axis=-1, keepdims=True)) * (1.0 / B_WIDTH)
    inv = lax.rsqrt(var + EPS)
    for c in range(2):
        vn = ds[c] * inv * lng_ref[:, cols(c)] + lnb_ref[:, cols(c)]
        vn_ref[:, cols(c)] = vn.astype(BF16)
        if emit_vn_f32:
            rest[0][:, cols(c)] = vn

    for c in range(2):
        ga_ref[:, cols(c)] = _sigmoid(proj(base + 4 + c)).astype(BF16)
        gb_ref[:, cols(c)] = _sigmoid(proj(base + 6 + c)).astype(BF16)


def _inproj(x, sh, sc, gain, w_in, ln_g, ln_b, *, tm, keeps, dils, emit_vn_f32, kv_feature_major, after=None):
    bsz, seq, _ = x.shape
    n_s = seq // tm
    tok = lambda width: pl.BlockSpec((None, tm, width), lambda b, s: (b, s, 0))

    def kv_spec(keep):
        first = n_s - max(keep // tm, 1)
        rows = min(keep, tm)
        if kv_feature_major:
            return pl.BlockSpec((None, 2 * A_WIDTH, rows), lambda b, s: (b, 0, jnp.maximum(s - first, 0)))
        return pl.BlockSpec((None, rows, 2 * A_WIDTH), lambda b, s: (b, jnp.maximum(s - first, 0), 0))

    out_specs, out_shape = [], []
    for dil in dils:
        out_specs.append(pl.BlockSpec((None, dil, tm // dil, 3 * A_WIDTH), lambda b, s: (b, 0, s, 0)))
        out_shape.append(jax.ShapeDtypeStruct((bsz, dil, seq // dil, 3 * A_WIDTH), BF16))
    out_specs += [tok(B_WIDTH), tok(B_WIDTH), tok(D_MODEL), tok(D_MODEL)]
    out_shape += [jax.ShapeDtypeStruct((bsz, seq, D_MODEL), BF16)] * 4
    for keep in keeps:
        out_specs.append(kv_spec(keep))
        kv_shape = (bsz, 2 * A_WIDTH, keep) if kv_feature_major else (bsz, keep, 2 * A_WIDTH)
        out_shape.append(jax.ShapeDtypeStruct(kv_shape, F32))
    if emit_vn_f32:
        out_specs.append(tok(B_WIDTH))
        out_shape.append(jax.ShapeDtypeStruct((bsz, seq, B_WIDTH), F32))

    ins = [x, sh, sc, gain, w_in, ln_g, ln_b]
    in_specs = [tok(D_MODEL), _mod_spec(sh, tm, seq), _mod_spec(sc, tm, seq), _const_spec(gain),
                pl.BlockSpec(w_in.shape, lambda b, s: (0, 0), pipeline_mode=pl.Buffered(1)),
                _const_spec(ln_g), _const_spec(ln_b)]
    if after is not None:
        ins.append(after)
        in_specs.append(pl.BlockSpec(memory_space=pl.ANY))
    scratch = [pltpu.VMEM((2, COL_CHUNK // LANES, tm, LANES), F32)]
    return pl.pallas_call(
        functools.partial(_inproj_kernel, tm=tm, keeps=keeps, dils=dils, emit_vn_f32=emit_vn_f32,
                          has_after=after is not None, kv_feature_major=kv_feature_major),
        grid=(bsz, n_s),
        in_specs=in_specs,
        out_specs=out_specs,
        out_shape=out_shape,
        scratch_shapes=scratch,
        compiler_params=pltpu.CompilerParams(dimension_semantics=("arbitrary", "arbitrary"),
                                             vmem_limit_bytes=VMEM_LIMIT),
        name="inproj",
    )(*ins)


def _attn_kernel(q_ref, kp_ref, kc_ref, vp_ref, vc_ref, bias0_ref, bias_ref, o_ref, lse_ref, *, n_seq, q_blocks):
    lane = lax.broadcasted_iota(jnp.int32, (A_BLOCK, LANES), 1)
    low = lane < HEAD_DIM
    zero = jnp.zeros((), BF16)
    for seq_i, i in [(a, b) for a in range(n_seq) for b in range(q_blocks)]:
        rows = (seq_i, slice(i * A_BLOCK, (i + 1) * A_BLOCK))
        q = q_ref[rows[0], rows[1], :]
        key_rows = slice((i - 1) * A_BLOCK, (i + 1) * A_BLOCK)
        if i == 0:
            k = jnp.concatenate([kp_ref[seq_i], kc_ref[seq_i, :A_BLOCK, :]], axis=0)
            v = jnp.concatenate([vp_ref[seq_i], vc_ref[seq_i, :A_BLOCK, :]], axis=0)
        else:
            k, v = kc_ref[seq_i, key_rows, :], vc_ref[seq_i, key_rows, :]
        b_ref = bias0_ref if i == 0 else bias_ref
        lse_tile = jnp.zeros((A_BLOCK, LANES), F32)
        for j in range(HEADS // 2):
            pair = slice(j * LANES, (j + 1) * LANES)
            qp, kpair, vpair = q[:, pair], k[:, pair], v[:, pair]
            outs = []
            for e in range(2):
                h = 2 * j + e
                qm = jnp.where(low if e == 0 else jnp.logical_not(low), qp, zero)
                s = lax.dot_general(qm, kpair, (((1,), (1,)), ((), ())), preferred_element_type=F32)
                s = s + b_ref[h]
                m = jnp.max(s, axis=-1, keepdims=True)
                p = jnp.exp2(s - m)
                den = jnp.sum(p, axis=-1, keepdims=True)
                o = jnp.dot(p.astype(BF16), vpair, preferred_element_type=F32)
                first = h * LSE_REP
                lse_tile = jnp.where((lane >= first) & (lane < first + LSE_REP // 2), m,
                                     jnp.where((lane >= first + LSE_REP // 2) & (lane < first + LSE_REP), den,
                                               lse_tile))
                outs.append(o)
            o_ref[rows[0], rows[1], pair] = jnp.where(low, outs[0], outs[1]).astype(BF16)
        lse_ref[rows[0], rows[1], :] = lse_tile


def _attn_bias(g):
    _, dil = A_GROUPS[g]
    n = N_GROUPS * HEADS
    e = np.arange(1, n + 1, dtype=np.float32)
    slopes = np.exp2(-8.0 * e / n).astype(np.float32).reshape(N_GROUPS, HEADS)[g]
    qi = np.arange(A_BLOCK)[:, None]
    ki = np.arange(2 * A_BLOCK)[None, :]
    delta = qi + A_BLOCK - ki
    band = (delta >= 0) & (delta <= A_STEPS)
    dist = (delta * dil).astype(np.float32)
    bias = (-slopes[:, None, None] * dist[None] * np.float32(LOG2E)).astype(np.float32)
    out = np.empty((2, HEADS, A_BLOCK, 2 * A_BLOCK), np.float32)
    out[1] = np.where(band[None], bias, NEG)
    out[0] = np.where((band & (ki >= A_BLOCK))[None], bias, NEG)
    return jnp.asarray(out)


def _attn_prompt(qkv, g):
    bsz, dil, steps, _ = qkv.shape
    qb = min(ATTN_BLOCKS_PER_STEP, steps // A_BLOCK)
    ns = min(ATTN_BLOCKS_PER_STEP // qb, dil)
    rows = qb * A_BLOCK
    n_steps = steps // rows
    bias = _attn_bias(g)

    def blk(t, prev):
        if prev:
            return pl.BlockSpec((None, ns, A_BLOCK, A_WIDTH),
                                lambda b, r, n: (b, r, jnp.maximum(n * qb - 1, 0), t))
        return pl.BlockSpec((None, ns, rows, A_WIDTH), lambda b, r, n: (b, r, n, t))

    bias_blk = lambda pick: pl.BlockSpec((None, HEADS, A_BLOCK, 2 * A_BLOCK), lambda b, r, n: (pick(n), 0, 0, 0))
    return pl.pallas_call(
        functools.partial(_attn_kernel, n_seq=ns, q_blocks=qb),
        grid=(bsz, dil // ns, n_steps),
        in_specs=[blk(0, False), blk(1, True), blk(1, False), blk(2, True), blk(2, False),
                  bias_blk(lambda n: jnp.minimum(n, 1)), bias_blk(lambda n: 1)],
        out_specs=[pl.BlockSpec((None, ns, rows, A_WIDTH), lambda b, r, n: (b, r, n, 0)),
                   pl.BlockSpec((None, ns, rows, LANES), lambda b, r, n: (b, r, n, 0))],
        out_shape=[jax.ShapeDtypeStruct((bsz, dil, steps, A_WIDTH), BF16),
                   jax.ShapeDtypeStruct((bsz, dil, steps, LANES), F32)],
        compiler_params=pltpu.CompilerParams(dimension_semantics=("arbitrary",) * 3),
        name=f"attn_prompt_g{g}",
    )(qkv, qkv, qkv, qkv, qkv, bias, bias)


def _attn_sample_kernel(q0_ref, q1_ref, q2_ref, c0_ref, c1_ref, c2_ref, bc0_ref, bc1_ref, bc2_ref,
                        bn0_ref, bn1_ref, bn2_ref, after_ref, o_ref, *, t_new):
    del after_ref
    n_rows = HEADS * t_new
    row = lax.broadcasted_iota(jnp.int32, (n_rows, A_WIDTH), 0)
    lane = lax.broadcasted_iota(jnp.int32, (n_rows, A_WIDTH), 1)
    head_mask = (row // t_new) == (lane // HEAD_DIM)
    pad = jnp.zeros((LANES - t_new, A_WIDTH), F32)
    outs, lses = [], []
    for qkv_ref, c_ref, bc_ref, bn_ref in ((q0_ref, c0_ref, bc0_ref, bn0_ref), (q1_ref, c1_ref, bc1_ref, bn1_ref),
                                           (q2_ref, c2_ref, bc2_ref, bn2_ref)):
        q = qkv_ref[:, :A_WIDTH].astype(F32)
        k_new = qkv_ref[:, A_WIDTH:2 * A_WIDTH].astype(F32)
        v_new = qkv_ref[:, 2 * A_WIDTH:].astype(F32)
        k_new = jnp.concatenate([k_new, pad], axis=0).astype(BF16)
        v_new = jnp.concatenate([v_new, pad], axis=0).astype(BF16)
        q_rows = jnp.where(head_mask, jnp.concatenate([q] * HEADS, axis=0), 0.0).astype(BF16)
        k_buf_t = c_ref[:A_WIDTH, :].astype(BF16)
        v_buf_t = c_ref[A_WIDTH:, :].astype(BF16)
        nt = (((1,), (1,)), ((), ()))
        s_buf = jnp.dot(q_rows, k_buf_t, preferred_element_type=F32) + bc_ref[...]
        s_new = lax.dot_general(q_rows, k_new, nt, preferred_element_type=F32) + bn_ref[...]
        m = jnp.maximum(jnp.max(s_buf, axis=-1, keepdims=True), jnp.max(s_new, axis=-1, keepdims=True))
        p_buf = jnp.exp2(s_buf - m)
        p_new = jnp.exp2(s_new - m)
        den = jnp.sum(p_buf, axis=-1, keepdims=True) + jnp.sum(p_new, axis=-1, keepdims=True)
        o = (lax.dot_general(p_buf.astype(BF16), v_buf_t, nt, preferred_element_type=F32)
             + jnp.dot(p_new.astype(BF16), v_new, preferred_element_type=F32)) / den
        outs.append(o)
        lses.append(m + jnp.log2(den))
    top = jnp.maximum(jnp.maximum(lses[0], lses[1]), lses[2])
    ws = [jnp.exp2(l - top) for l in lses]
    tot = ws[0] + ws[1] + ws[2]
    acc = (ws[0] / tot) * outs[0] + (ws[1] / tot) * outs[1] + (ws[2] / tot) * outs[2]
    acc = jnp.where(head_mask, acc, 0.0).reshape(HEADS, t_new, A_WIDTH)
    o_ref[...] = jnp.sum(acc, axis=0).astype(BF16)


def _sample_bias(g, t_new, buf):
    _, dil = A_GROUPS[g]
    n = N_GROUPS * HEADS
    e = np.arange(1, n + 1, dtype=np.float32)
    slopes = np.exp2(-8.0 * e / n).astype(np.float32).reshape(N_GROUPS, HEADS)[g]
    t = np.arange(t_new)[:, None]
    idx = np.concatenate([np.arange(buf), buf + np.arange(LANES)])[None, :]
    dist = buf + t - idx
    valid = (dist >= 0) & (dist % dil == 0) & (dist <= A_STEPS * dil) & (idx < buf + t_new)
    bias = -slopes[:, None, None] * dist.astype(np.float32)[None] * np.float32(LOG2E)
    bias = np.where(valid[None], bias, NEG).astype(np.float32).reshape(HEADS * t_new, buf + LANES)
    return jnp.asarray(bias[:, :buf]), jnp.asarray(bias[:, buf:])


def _attn_sample(qkvs, caches, *, after):
    bsz, t_new, _ = qkvs[0].shape
    cache_v = [jnp.transpose(c, (0, 2, 3, 4, 1)).reshape(bsz, 2 * A_WIDTH, c.shape[1]) for c in caches]
    biases = [_sample_bias(g, t_new, cache_v[g].shape[2]) for g in range(N_GROUPS)]
    bcs = [b[0] for b in biases]
    bns = [b[1] for b in biases]
    return pl.pallas_call(
        functools.partial(_attn_sample_kernel, t_new=t_new),
        grid=(bsz,),
        in_specs=[pl.BlockSpec((None, t_new, 3 * A_WIDTH), lambda b: (b, 0, 0))] * N_GROUPS
                 + [pl.BlockSpec((None, 2 * A_WIDTH, c.shape[2]), lambda b: (b, 0, 0)) for c in cache_v]
                 + [_const_spec(b) for b in bcs] + [_const_spec(b) for b in bns]
                 + [pl.BlockSpec(memory_space=pl.ANY)],
        out_specs=pl.BlockSpec((None, t_new, A_WIDTH), lambda b: (b, 0, 0)),
        out_shape=jax.ShapeDtypeStruct((bsz, t_new, A_WIDTH), BF16),
        compiler_params=pltpu.CompilerParams(dimension_semantics=("arbitrary",),
                                             vmem_limit_bytes=VMEM_LIMIT),
        name="attn_sample",
    )(*qkvs, *cache_v, *bcs, *bns, after)


def _first_max4(v):
    top = jnp.maximum(jnp.maximum(v[0], v[1]), jnp.maximum(v[2], v[3]))
    idx = jnp.where(v[0] == top, 0.0, jnp.where(v[1] == top, 1.0, jnp.where(v[2] == top, 2.0, 3.0)))
    return top, idx


def _route(lt):
    row = lambda k: lt[k:k + 1, :]
    g = [row(k) for k in range(MOE_GROUPS)]
    gmax, g_idx = _first_max4(g)
    g_prob = 1.0 / (jnp.exp(g[0] - gmax) + jnp.exp(g[1] - gmax) + jnp.exp(g[2] - gmax) + jnp.exp(g[3] - gmax))
    cand = []
    for k in range(MOE_PER_GROUP):
        c = row(ROUTE_OFF + MOE_PER_GROUP * (MOE_GROUPS - 1) + k)
        for gi in range(MOE_GROUPS - 2, -1, -1):
            c = jnp.where(g_idx == float(gi), row(ROUTE_OFF + MOE_PER_GROUP * gi + k), c)
        cand.append(c)
    e1, i1 = _first_max4(cand)
    rest = [jnp.where(i1 == float(k), -jnp.inf, cand[k]) for k in range(MOE_PER_GROUP)]
    e2, i2 = _first_max4(rest)
    t = jnp.exp(e2 - e1)
    w1 = 1.0 / (1.0 + t)
    w2 = t / (1.0 + t)
    return MOE_PER_GROUP * g_idx + i1, MOE_PER_GROUP * g_idx + i2, w1 * g_prob, w2 * g_prob


def _pack_bf16_pairs(x):
    q = x.shape[1] // 4
    bits = lax.bitcast_convert_type(x.astype(BF16).astype(F32), jnp.int32)
    pack = lambda hi, lo: hi | lax.shift_right_logical(lo, 16)
    return pack(bits[:, :q], bits[:, 2 * q:3 * q]), pack(bits[:, q:2 * q], bits[:, 3 * q:])


def _unpack_bf16_pairs(pa, pb):
    hi = lambda p: lax.bitcast_convert_type(p & jnp.int32(-65536), F32)
    lo = lambda p: lax.bitcast_convert_type(lax.shift_left(p, 16), F32)
    return jnp.concatenate([hi(pa), hi(pb), lo(pa), lo(pb)], axis=1)


def _mix_kernel(*refs, tm, dils, routed, span):
    n_attn = len(dils)
    x_ref = refs[0]
    if n_attn > 1:
        o_refs = refs[1:1 + n_attn]
        l_refs = refs[1 + n_attn:1 + 2 * n_attn]
        exp_ref = refs[1 + 2 * n_attn]
        pos = 2 + 2 * n_attn
    else:
        o_refs = refs[1:2]
        pos = 2
    (u_ref, vn_ref, ga_ref, gb_ref, sw_ref, sbt_ref, wpa_ref, wpb_ref, wo_ref,
     gt1_ref, sh2_ref, sc2_ref, n2g_ref, wrh_ref, wrl_ref, br_ref) = refs[pos:pos + 16]
    n_out = 6 if routed else 3
    out_refs = refs[pos + 16:pos + 16 + n_out]
    scr = list(refs[pos + 16 + n_out:])
    x1_ref = out_refs[0]
    ob_scr = scr.pop(0)

    def to_positions(ref, scr, dil):
        if dil == 1:
            return ref[0].astype(F32)
        n_slab = scr.shape[0]
        for r in range(dil):
            rows = ref[r].astype(F32)
            for j in range(n_slab):
                scr[j, pl.ds(r, tm // dil, stride=dil), :] = rows[:, j * LANES:(j + 1) * LANES]
        return jnp.concatenate([scr[j] for j in range(n_slab)], axis=1)

    if n_attn > 1:
        o_scr, l_scr = scr[0], scr[1]
        stats = [to_positions(l_refs[g], l_scr.at[g], dils[g]) for g in range(n_attn)]
        lane = lax.broadcasted_iota(jnp.int32, stats[0].shape, 1)
        is_max = (lane & (LSE_REP - 1)) < LSE_REP // 2
        half = LSE_REP // 2
        maxes = [jnp.where(is_max, s, pltpu.roll(s, half, axis=1)) for s in stats]
        dens = [jnp.where(is_max, pltpu.roll(s, LANES - half, axis=1), s) for s in stats]
        lses = [m + jnp.log2(d) for m, d in zip(maxes, dens)]
        top = functools.reduce(jnp.maximum, lses)
        tot = functools.reduce(lambda a, b: a + b, [jnp.exp2(l - top) for l in lses])
        o_a = None
        for g in range(n_attn):
            w = jnp.exp2(maxes[g] - top) / tot
            hi = w.astype(BF16)
            lo = (w - hi.astype(F32)).astype(BF16)
            w_exp = jnp.dot(jnp.concatenate([hi, lo], axis=1), exp_ref[...], preferred_element_type=F32)
            term = w_exp * to_positions(o_refs[g], o_scr.at[g], dils[g])
            o_a = term if o_a is None else o_a + term
        o_a = o_a.astype(BF16)
    else:
        o_a = o_refs[0][...]

    r_i = lax.broadcasted_iota(jnp.int32, (span, span), 0)
    c_i = lax.broadcasted_iota(jnp.int32, (span, span), 1)
    tril = r_i >= c_i
    for gi in range(B_GROUPS):
        gcols = slice(gi * LANES, (gi + 1) * LANES)
        b_s = sbt_ref[:span, gi:gi + 1]
        if span == B_CHUNK:
            w_s = jnp.where(tril, sw_ref[gi], 0.0).astype(BF16)
            n_chunk = tm // span
            vn_wide = jnp.concatenate([vn_ref[c * span:(c + 1) * span, gcols] for c in range(n_chunk)], axis=1)
            mixed = jnp.dot(w_s, vn_wide, preferred_element_type=F32) + b_s
            for c in range(n_chunk):
                rows = slice(c * span, (c + 1) * span)
                ob_scr[rows, gcols] = (u_ref[rows, gcols].astype(F32)
                                       * mixed[:, c * LANES:(c + 1) * LANES]).astype(BF16)
        else:
            w_s = jnp.where(tril, sw_ref[gi, :span, :span], 0.0)
            vn3 = vn_ref[:, gcols].astype(F32).reshape(tm // span, span, LANES)
            mixed = jnp.broadcast_to(b_s, (span, LANES))[None]
            for s in range(span):
                w_col = jnp.broadcast_to(w_s[:, s:s + 1], (span, LANES))[None]
                mixed = mixed + w_col * jnp.broadcast_to(vn3[:, s:s + 1, :], vn3.shape)
            u3 = u_ref[:, gcols].astype(F32).reshape(tm // span, span, LANES)
            ob_scr[:, gcols] = (u3 * mixed).reshape(tm, LANES).astype(BF16)

    sub = min(tm, MIX_SUB_ROWS)
    if routed:
        carry = scr[-1]
        first_step = (pl.program_id(0) == 0) & (pl.program_id(1) == 0)
        seen = jnp.where(first_step, 0.0, carry[:, 0:1])
        r_i = lax.broadcasted_iota(jnp.int32, (sub, sub), 0)
        c_i = lax.broadcasted_iota(jnp.int32, (sub, sub), 1)
        earlier = jnp.where(r_i < c_i, 1.0, 0.0).astype(BF16)
        e_iota = lax.broadcasted_iota(jnp.int32, (MOE_EXPERTS, sub), 0).astype(F32)
    nt = (((1,), (1,)), ((), ()))
    for r0 in range(0, tm, sub):
        rs = slice(r0, r0 + sub)
        mod = lambda ref: ref[...] if ref.shape[0] == 1 else _mod_rows(ref, tm)[rs, :]
        pa = jnp.dot(o_a[rs], wpa_ref[...], preferred_element_type=F32)
        pb = jnp.dot(ob_scr[rs, :], wpb_ref[...], preferred_element_type=F32)
        mix = (ga_ref[rs, :].astype(F32) * pa + gb_ref[rs, :].astype(F32) * pb).astype(BF16)
        x1 = x_ref[rs, :] + mod(gt1_ref) * jnp.dot(mix, wo_ref[...], preferred_element_type=F32)
        x1_ref[rs, :] = x1
        h2 = _rmsnorm_mod(x1, n2g_ref[...], mod(sh2_ref), mod(sc2_ref))
        h_hi = h2.astype(BF16)
        h_lo = (h2 - h_hi.astype(F32)).astype(BF16)
        lt = (lax.dot_general(wrh_ref[...], h_hi, nt, preferred_element_type=F32)
              + lax.dot_general(wrh_ref[...], h_lo, nt, preferred_element_type=F32)
              + lax.dot_general(wrl_ref[...], h_hi, nt, preferred_element_type=F32)) + br_ref[...]
        ex1, ex2, w1, w2 = _route(lt)
        if not routed:
            _, h2_ref, comb_ref = out_refs
            h2_ref[rs, :] = h_hi
            e_lane = lax.broadcasted_iota(jnp.int32, (LANES, sub), 0).astype(F32) - float(ROUTE_OFF)
            comb_t = jnp.where(e_lane == ex1, w1, jnp.where(e_lane == ex2, w2, 0.0))
            comb_ref[rs, :] = jnp.transpose(comb_t)
            continue

        _, hpa_ref, hpb_ref, route_ref, wcol_ref, cnt_ref = out_refs
        hpa_ref[rs, :], hpb_ref[rs, :] = _pack_bf16_pairs(h2)
        hot1, hot2 = e_iota == ex1, e_iota == ex2
        onehot = jnp.where(hot1 | hot2, 1.0, 0.0)
        rank = jnp.dot(onehot.astype(BF16), earlier, preferred_element_type=F32) + seen
        r1 = jnp.sum(jnp.where(hot1, rank, 0.0), axis=0, keepdims=True)
        r2 = jnp.sum(jnp.where(hot2, rank, 0.0), axis=0, keepdims=True)
        seen = seen + jnp.sum(onehot, axis=1, keepdims=True)
        fields = jnp.concatenate([ex1, ex2, w1, w2, r1, r2, jnp.zeros((2, sub), F32)], axis=0)
        route_ref[:, rs] = fields
        wcol_ref[rs, :] = jnp.transpose(jnp.concatenate([fields, jnp.zeros((LANES - 8, sub), F32)], axis=0))
    if routed:
        carry[...] = jnp.broadcast_to(seen, carry.shape)
        cnt_ref[...] = jnp.broadcast_to(seen, cnt_ref.shape)


def _expand_matrix():
    e = np.zeros((LANES, A_WIDTH), np.float32)
    for h in range(HEADS):
        e[h * LSE_REP, h * HEAD_DIM:(h + 1) * HEAD_DIM] = 1.0
    return jnp.asarray(np.concatenate([e, e], axis=0), dtype=BF16)


def _mix(x, attn_outs, attn_lses, u, vn, ga, gb, sgu_w, sgu_bt, w_pa, w_pb, w_o,
         gt1, sh2, sc2, n2g, w_r_hi, w_r_lo, b_r, *, tm, routed, span):
    bsz, seq, _ = x.shape
    n_attn = len(attn_outs)
    tok = lambda width: pl.BlockSpec((None, tm, width), lambda b, s: (b, s, 0))
    ins = [x] + list(attn_outs)
    scratch = [pltpu.VMEM((tm, B_WIDTH), BF16)]
    if n_attn > 1:
        dils = tuple(o.shape[1] for o in attn_outs)
        res = lambda dil, width: pl.BlockSpec((None, dil, tm // dil, width), lambda b, s: (b, 0, s, 0))
        e2 = _expand_matrix()
        ins += list(attn_lses) + [e2]
        specs = ([tok(D_MODEL)] + [res(d, A_WIDTH) for d in dils] + [res(d, LANES) for d in dils]
                 + [_const_spec(e2)])
        scratch += [pltpu.VMEM((n_attn, A_WIDTH // LANES, tm, LANES), F32), pltpu.VMEM((n_attn, 1, tm, LANES), F32)]
    else:
        dils = (1,)
        specs = [tok(D_MODEL), tok(A_WIDTH)]
    ins += [u, vn, ga, gb, sgu_w, sgu_bt, w_pa, w_pb, w_o, gt1, sh2, sc2, n2g, w_r_hi, w_r_lo, b_r]
    specs += [tok(B_WIDTH), tok(B_WIDTH), tok(D_MODEL), tok(D_MODEL),
              _const_spec(sgu_w), _const_spec(sgu_bt), _const_spec(w_pa), _const_spec(w_pb), _const_spec(w_o),
              _mod_spec(gt1, tm, seq), _mod_spec(sh2, tm, seq), _mod_spec(sc2, tm, seq), _const_spec(n2g),
              _const_spec(w_r_hi), _const_spec(w_r_lo), _const_spec(b_r)]
    if routed:
        quarter = D_MODEL // 4
        out_specs = [tok(D_MODEL), tok(quarter), tok(quarter),
                     pl.BlockSpec((None, 8, tm), lambda b, s: (b, 0, s)), tok(LANES),
                     pl.BlockSpec((MOE_EXPERTS, LANES), lambda b, s: (0, 0))]
        out_shape = [jax.ShapeDtypeStruct((bsz, seq, D_MODEL), F32),
                     jax.ShapeDtypeStruct((bsz, seq, quarter), jnp.int32),
                     jax.ShapeDtypeStruct((bsz, seq, quarter), jnp.int32),
                     jax.ShapeDtypeStruct((bsz, 8, seq), F32),
                     jax.ShapeDtypeStruct((bsz, seq, LANES), F32),
                     jax.ShapeDtypeStruct((MOE_EXPERTS, LANES), F32)]
        scratch.append(pltpu.VMEM((MOE_EXPERTS, LANES), F32))
    else:
        out_specs = [tok(D_MODEL), tok(D_MODEL), tok(LANES)]
        out_shape = [jax.ShapeDtypeStruct((bsz, seq, D_MODEL), F32),
                     jax.ShapeDtypeStruct((bsz, seq, D_MODEL), BF16),
                     jax.ShapeDtypeStruct((bsz, seq, LANES), F32)]
    return pl.pallas_call(
        functools.partial(_mix_kernel, tm=tm, dils=dils, routed=routed, span=span),
        grid=(bsz, seq // tm),
        in_specs=specs,
        out_specs=out_specs,
        out_shape=out_shape,
        scratch_shapes=scratch,
        compiler_params=pltpu.CompilerParams(dimension_semantics=("arbitrary", "arbitrary"),
                                             vmem_limit_bytes=VMEM_LIMIT),
        name="mix",
    )(*ins)


def _moe_kernel(h_ref, comb_ref, wg_ref, wu_ref, wd_ref, x1_ref, gt2_ref, shf_ref, scf_ref, nfg_ref,
                y_ref, acc_ref):
    e = pl.program_id(2)

    @pl.when(e == 0)
    def _():
        acc_ref[...] = jnp.zeros_like(acc_ref)

    h = h_ref[...]
    hg = jnp.dot(h, wg_ref[...].astype(BF16), preferred_element_type=F32)
    hu = jnp.dot(h, wu_ref[...].astype(BF16), preferred_element_type=F32)
    comb = comb_ref[...]
    lane = lax.broadcasted_iota(jnp.int32, comb.shape, 1)
    cw = jnp.sum(jnp.where(lane == e + ROUTE_OFF, comb, 0.0), axis=-1, keepdims=True)
    act = (hg * _sigmoid(hg) * hu) * cw
    acc_ref[...] += jnp.dot(act.astype(BF16), wd_ref[...].astype(BF16), preferred_element_type=F32)

    @pl.when(e == MOE_EXPERTS - 1)
    def _():
        rows = acc_ref.shape[0]
        x2 = x1_ref[...] + _mod_rows(gt2_ref, rows) * acc_ref[...]
        y_ref[...] = _rmsnorm_mod(x2, nfg_ref[...], _mod_rows(shf_ref, rows), _mod_rows(scf_ref, rows))


def _moe(h2, comb, w_gate, w_up, w_down, x1, gt2, shf, scf, nfg, *, tm):
    bsz, seq, _ = h2.shape
    tok = lambda width: pl.BlockSpec((None, tm, width), lambda b, s, e: (b, s, 0))
    return pl.pallas_call(
        _moe_kernel,
        grid=(bsz, seq // tm, MOE_EXPERTS),
        in_specs=[tok(D_MODEL), tok(LANES),
                  pl.BlockSpec((None, D_MODEL, MOE_HIDDEN), lambda b, s, e: (e, 0, 0)),
                  pl.BlockSpec((None, D_MODEL, MOE_HIDDEN), lambda b, s, e: (e, 0, 0)),
                  pl.BlockSpec((None, MOE_HIDDEN, D_MODEL), lambda b, s, e: (e, 0, 0)),
                  tok(D_MODEL), _mod_spec(gt2, tm, seq), _mod_spec(shf, tm, seq), _mod_spec(scf, tm, seq),
                  _const_spec(nfg)],
        out_specs=tok(D_MODEL),
        out_shape=jax.ShapeDtypeStruct((bsz, seq, D_MODEL), F32),
        scratch_shapes=[pltpu.VMEM((tm, D_MODEL), F32)],
        compiler_params=pltpu.CompilerParams(dimension_semantics=("arbitrary",) * 3,
                                             vmem_limit_bytes=VMEM_LIMIT),
        name="moe",
    )(h2, comb, w_gate, w_up, w_down, x1, gt2, shf, scf, nfg)


def _sc_mesh():
    return plsc.VectorSubcoreMesh(core_axis_name="c", subcore_axis_name="s")


def _sc_scatter_rows(rows, pos1, pos2, n_out):
    n, width = rows.shape
    steps = n // SC_WINDOW // SC_CORES

    @pl.kernel(out_type=jax.ShapeDtypeStruct((n_out, width), rows.dtype), mesh=_sc_mesh(), scratch_types=[])
    def scatter(rows_hbm, p1_hbm, p2_hbm, out_hbm):
        def body(x_vmem, i1_vmem, i2_vmem):
            pltpu.sync_copy(x_vmem, out_hbm.at[i1_vmem.at[0]])
            pltpu.sync_copy(x_vmem, out_hbm.at[i2_vmem.at[0]])

        pltpu.emit_pipeline(
            body, grid=(SC_CORES, steps),
            in_specs=[pl.BlockSpec((SC_WINDOW, width), lambda c, i: (c * steps + i, 0)),
                      pl.BlockSpec((1, SC_WINDOW), lambda c, i: (0, c * steps + i)),
                      pl.BlockSpec((1, SC_WINDOW), lambda c, i: (0, c * steps + i))],
            out_specs=[],
            core_axis_name=("c", "s"),
            dimension_semantics=(pltpu.PARALLEL, pltpu.PARALLEL),
        )(rows_hbm, p1_hbm, p2_hbm)

    return scatter(rows, pos1, pos2)


def _sc_gather_rows(table, idx):
    m, width = idx.shape[1], table.shape[1]
    steps = m // SC_WINDOW // SC_CORES

    @pl.kernel(out_type=jax.ShapeDtypeStruct((m, width), table.dtype), mesh=_sc_mesh(), scratch_types=[])
    def gather(table_hbm, idx_hbm, out_hbm):
        def body(i_vmem, o_vmem):
            pltpu.sync_copy(table_hbm.at[i_vmem.at[0]], o_vmem)

        pltpu.emit_pipeline(
            body, grid=(SC_CORES, steps),
            in_specs=[pl.BlockSpec((1, SC_WINDOW), lambda c, i: (0, c * steps + i))],
            out_specs=[pl.BlockSpec((SC_WINDOW, width), lambda c, i: (c * steps + i, 0))],
            core_axis_name=("c", "s"),
            dimension_semantics=(pltpu.PARALLEL, pltpu.PARALLEL),
        )(idx_hbm, out_hbm)

    return gather(table, idx)


def _ffn_kernel(te_ref, nt_ref, xa_ref, xb_ref, wg_ref, wu_ref, wd_ref, *rest):
    oa_ref, ob_ref, wg_s, wu_s, wd_s = rest[-5:]
    t = pl.program_id(0)
    live = t < nt_ref[0]

    @pl.when(live & ((t == 0) | (te_ref[t] != te_ref[jnp.maximum(t - 1, 0)])))
    def _():
        wg_s[...] = wg_ref[...].astype(BF16)
        wu_s[...] = wu_ref[...].astype(BF16)
        wd_s[...] = wd_ref[...].astype(BF16)

    @pl.when(live)
    def _():
        x = _unpack_bf16_pairs(xa_ref[...], xb_ref[...]).astype(BF16)
        hg = jnp.dot(x, wg_s[...], preferred_element_type=F32)
        hu = jnp.dot(x, wu_s[...], preferred_element_type=F32)
        act = (hg * _sigmoid(hg) * hu).astype(BF16)
        oa_ref[...], ob_ref[...] = _pack_bf16_pairs(jnp.dot(act, wd_s[...], preferred_element_type=F32))


def _ffn(xa, xb, tile_expert, n_tiles, w_gate, w_up, w_down, *, after):
    n_rows, quarter = xa.shape
    rows = pl.BlockSpec((MOE_TILE, quarter), lambda t, te, nt: (jnp.minimum(t, nt[0] - 1), 0))
    return pl.pallas_call(
        _ffn_kernel,
        grid_spec=pltpu.PrefetchScalarGridSpec(
            num_scalar_prefetch=2,
            grid=(n_rows // MOE_TILE,),
            in_specs=[rows, rows,
                      pl.BlockSpec((None, D_MODEL, MOE_HIDDEN), lambda t, te, nt: (te[t], 0, 0)),
                      pl.BlockSpec((None, D_MODEL, MOE_HIDDEN), lambda t, te, nt: (te[t], 0, 0)),
                      pl.BlockSpec((None, MOE_HIDDEN, D_MODEL), lambda t, te, nt: (te[t], 0, 0))]
                     + [pl.BlockSpec(memory_space=pl.ANY)] * len(after),
            out_specs=[rows, rows],
            scratch_shapes=[pltpu.VMEM((D_MODEL, MOE_HIDDEN), BF16), pltpu.VMEM((D_MODEL, MOE_HIDDEN), BF16),
                            pltpu.VMEM((MOE_HIDDEN, D_MODEL), BF16)]),
        out_shape=[jax.ShapeDtypeStruct((n_rows, quarter), jnp.int32)] * 2,
        compiler_params=pltpu.CompilerParams(dimension_semantics=("arbitrary",), vmem_limit_bytes=VMEM_LIMIT),
        name="moe_ffn",
    )(tile_expert, n_tiles, xa, xb, w_gate, w_up, w_down, *after)


def _final_kernel(x1_ref, g1a_ref, g1b_ref, g2a_ref, g2b_ref, route_ref, gt2_ref, shf_ref, scf_ref, nfg_ref, y_ref):
    o1 = _unpack_bf16_pairs(g1a_ref[...], g1b_ref[...])
    o2 = _unpack_bf16_pairs(g2a_ref[...], g2b_ref[...])
    route = route_ref[...]
    moe = route[:, 2:3] * o1 + route[:, 3:4] * o2
    x2 = x1_ref[...] + gt2_ref[...] * moe
    y_ref[...] = _rmsnorm_mod(x2, nfg_ref[...], shf_ref[...], scf_ref[...])


def _final(x1, ga, gb, route, gt2, shf, scf, nfg, *, tm):
    bsz, seq, _ = x1.shape
    n_s = seq // tm
    n_blk = bsz * n_s
    quarter = ga.shape[1]
    tok = lambda width: pl.BlockSpec((None, tm, width), lambda b, s: (b, s, 0))
    first = pl.BlockSpec((tm, quarter), lambda b, s: (b * n_s + s, 0))
    second = pl.BlockSpec((tm, quarter), lambda b, s: (n_blk + b * n_s + s, 0))
    return pl.pallas_call(
        _final_kernel,
        grid=(bsz, n_s),
        in_specs=[tok(D_MODEL), first, first, second, second, tok(LANES),
                  _mod_spec(gt2, tm, seq), _mod_spec(shf, tm, seq), _mod_spec(scf, tm, seq), _const_spec(nfg)],
        out_specs=tok(D_MODEL),
        out_shape=jax.ShapeDtypeStruct((bsz, seq, D_MODEL), F32),
        compiler_params=pltpu.CompilerParams(dimension_semantics=("arbitrary", "arbitrary")),
        name="moe_final",
    )(x1, ga, gb, ga, gb, route, gt2, shf, scf, nfg)


def _positions_kernel(base_ref, route_ref, p1_ref, p2_ref):
    rows = route_ref[...]
    for e_row, r_row, out_ref in ((0, 4, p1_ref), (1, 5, p2_ref)):
        expert = rows[e_row:e_row + 1, :]
        start = jnp.zeros(expert.shape, jnp.int32)
        for e in range(MOE_EXPERTS):
            start = jnp.where(expert == float(e), base_ref[e], start)
        out_ref[...] = start + rows[r_row:r_row + 1, :].astype(jnp.int32)


def _positions(route_rows, base):
    bsz, _, seq = route_rows.shape
    out = pl.BlockSpec((None, 1, seq), lambda b, base: (b, 0, 0))
    return pl.pallas_call(
        _positions_kernel,
        grid_spec=pltpu.PrefetchScalarGridSpec(
            num_scalar_prefetch=1, grid=(bsz,),
            in_specs=[pl.BlockSpec((None, 8, seq), lambda b, base: (b, 0, 0))],
            out_specs=[out, out]),
        out_shape=[jax.ShapeDtypeStruct((bsz, 1, seq), jnp.int32)] * 2,
        compiler_params=pltpu.CompilerParams(dimension_semantics=("arbitrary",)),
        name="moe_positions",
    )(base, route_rows)


def _moe_dispatch(hpa, hpb, route_rows, counts):
    bsz, seq, quarter = hpa.shape
    n_tok = bsz * seq
    t_max = 2 * n_tok // MOE_TILE + MOE_EXPERTS
    cnt = counts[:, 0].astype(jnp.int32)
    tiles_e = (cnt + MOE_TILE - 1) // MOE_TILE
    tile_end = jnp.cumsum(tiles_e)
    n_tiles = tile_end[-1:]
    base = (tile_end - tiles_e) * MOE_TILE
    t_ids = jnp.minimum(jnp.arange(t_max, dtype=jnp.int32), n_tiles[0] - 1)
    tile_expert = jnp.sum((t_ids[:, None] >= tile_end[None, :]).astype(jnp.int32), axis=1)
    pos1, pos2 = (p.reshape(1, n_tok) for p in _positions(route_rows, base))
    n_rows = t_max * MOE_TILE
    xa = _sc_scatter_rows(hpa.reshape(n_tok, quarter), pos1, pos2, n_rows)
    xb = _sc_scatter_rows(hpb.reshape(n_tok, quarter), pos1, pos2, n_rows)
    return xa, xb, jnp.concatenate([pos1, pos2], axis=1), tile_expert, n_tiles


def kernel(x_prompt, x_sample, cache_kv_w128, cache_kv_w512, cache_kv_w2048, c_prompt, c_sample,
           w_ada, b_ada, norm1_g, norm2_g, w_in, sgu_ln_g, sgu_ln_b, sgu_w, sgu_b, w_pa, w_pb, w_o,
           w_route_group, b_route_group, w_route_expert, b_route_expert, w_gate, w_up, w_down,
           normf_g, w_ada_final, b_ada_final):
    depth = w_ada.shape[0]
    assert depth == 1
    l = 0
    bp, seq, _ = x_prompt.shape
    bs, t_new, _ = x_sample.shape
    n_samp = bs * t_new

    c_all = jnp.concatenate([c_prompt, c_sample], axis=0)
    pad_rows = (-c_all.shape[0]) % 8
    c_all = jnp.pad(c_all, ((0, pad_rows), (0, 0)))
    mod = _adaln(c_all, w_ada[l], b_ada[l])

    def split_mods(m, n, lo, hi, per_token):
        parts = jnp.split(m[lo:hi], n, axis=-1)
        if per_token:
            return [p.reshape(1, hi - lo, D_MODEL) for p in parts]
        return [p.reshape(hi - lo, 1, D_MODEL) for p in parts]

    mods_p = split_mods(mod, 6, 0, bp, False)
    mods_s = split_mods(mod, 6, bp, bp + bs, True)

    row = lambda v: v.reshape(1, -1)
    w_in_b = w_in[l].astype(BF16)
    w_pa_b, w_pb_b, w_o_b = w_pa[l].astype(BF16), w_pb[l].astype(BF16), w_o[l].astype(BF16)
    w_gate_b, w_up_b, w_down_b = w_gate[l], w_up[l], w_down[l]
    w_re = jnp.transpose(w_route_expert[l], (0, 2, 1)).reshape(MOE_EXPERTS, D_MODEL)
    w_r = jnp.pad(jnp.concatenate([jnp.transpose(w_route_group[l]), w_re], axis=0),
                  ((0, ROUTE_ROWS - MOE_GROUPS - MOE_EXPERTS), (0, 0)))
    w_r_hi = w_r.astype(BF16)
    w_r_lo = (w_r - w_r_hi.astype(F32)).astype(BF16)
    b_r = jnp.pad(jnp.concatenate([b_route_group[l], b_route_expert[l].reshape(-1)]),
                  (0, ROUTE_ROWS - MOE_GROUPS - MOE_EXPERTS)).reshape(ROUTE_ROWS, 1)

    def inproj(x, mods, keeps, dils, tm, emit_vn_f32, kv_feature_major, after=None):
        return _inproj(x, mods[0], mods[1], row(norm1_g[l]), w_in_b, row(sgu_ln_g[l]), row(sgu_ln_b[l]),
                       tm=tm, keeps=keeps, dils=dils, emit_vn_f32=emit_vn_f32,
                       kv_feature_major=kv_feature_major, after=after)

    sgu_bt = jnp.transpose(sgu_b[l])

    def mix(x, mods, pin, attn_outs, attn_lses, tm, routed, span):
        u, vn, ga, gb = pin[3:7]
        return _mix(x, attn_outs, attn_lses, u, vn, ga, gb, sgu_w[l], sgu_bt,
                    w_pa_b, w_pb_b, w_o_b, mods[2], mods[3], mods[4], row(norm2_g[l]), w_r_hi, w_r_lo, b_r,
                    tm=tm, routed=routed, span=span)

    tm_p = 512
    keeps_p = tuple(min(win, seq) for win, _ in A_GROUPS)
    dils_p = tuple(dil for _, dil in A_GROUPS)
    pin = inproj(x_prompt, mods_p, keeps_p, dils_p, 256, False, True)
    attn_p = [_attn_prompt(pin[g], g) for g in range(N_GROUPS)]
    x1, hpa, hpb, route_rows, route, counts = mix(x_prompt, mods_p, pin, [r[0] for r in attn_p],
                                                  [r[1] for r in attn_p], tm_p, True, B_CHUNK)

    xs = x_sample.reshape(1, n_samp, D_MODEL)
    sin = inproj(xs, mods_s, (n_samp,) * N_GROUPS, (1,) * N_GROUPS, n_samp, True, False, after=counts)
    mod_f = _adaln(c_all, w_ada_final, b_ada_final, after=counts)
    fin_p = split_mods(mod_f, 2, 0, bp, False)
    fin_s = split_mods(mod_f, 2, bp, bp + bs, True)

    xa, xb, pos, tile_expert, n_tiles = _moe_dispatch(hpa, hpb, route_rows, counts)
    kv_s_out = [kv.reshape(depth, bs, -1, 2, HEADS, HEAD_DIM) for kv in sin[7:10]]
    oa, ob = _ffn(xa, xb, tile_expert, n_tiles, w_gate_b, w_up_b, w_down_b, after=(sin[0], fin_s[0], *kv_s_out))

    caches = (cache_kv_w128[l], cache_kv_w512[l], cache_kv_w2048[l])
    o_s = _attn_sample([q.reshape(bs, t_new, 3 * A_WIDTH) for q in sin[:3]], caches, after=oa)
    x1_s, h2_s, comb_s = mix(xs, mods_s, sin, [o_s.reshape(1, n_samp, A_WIDTH)], None, n_samp, False,
                             min(t_new, B_CHUNK))
    y_s = _moe(h2_s, comb_s, w_gate_b, w_up_b, w_down_b, x1_s, mods_s[5], fin_s[0], fin_s[1], row(normf_g),
               tm=n_samp)

    ga_rows = _sc_gather_rows(oa, pos)
    gb_rows = _sc_gather_rows(ob, pos)
    y_p = _final(x1, ga_rows, gb_rows, route, mods_p[5], fin_p[0], fin_p[1], row(normf_g), tm=tm_p)
    kv_p, extra = pin[7:10], sin[10:]

    def kv_out_t(a, b):
        a = a.reshape(b, 2, HEADS, HEAD_DIM, a.shape[-1])
        return jnp.transpose(a, (0, 4, 1, 2, 3)).reshape(depth, b, -1, 2, HEADS, HEAD_DIM)

    return (y_p, y_s.reshape(bs, t_new, D_MODEL),
            kv_out_t(kv_p[0], bp), kv_out_t(kv_p[1], bp), kv_out_t(kv_p[2], bp),
            kv_s_out[0], kv_s_out[1], kv_s_out[2],
            extra[0].reshape(depth, bs, t_new, B_WIDTH))
```

```python
import functools

import numpy as np
import jax
import jax.numpy as jnp
from jax import lax
from jax.experimental import pallas as pl
from jax.experimental.pallas import tpu as pltpu
from jax.experimental.pallas import tpu_sc as plsc

F32 = jnp.float32
BF16 = jnp.bfloat16

D_MODEL = 1024
A_GROUPS = ((128, 1), (512, 4), (2048, 16))
N_GROUPS = 3
HEADS = 8
HEAD_DIM = 64
A_WIDTH = HEADS * HEAD_DIM
A_STEPS = 128
A_BLOCK = 128
B_WIDTH = 1024
B_GROUPS = 8
B_CHUNK = 128
MOE_GROUPS = 4
MOE_PER_GROUP = 4
MOE_EXPERTS = 16
MOE_HIDDEN = 512
EPS = 1e-6
N_QKV = 3 * N_GROUPS * A_WIDTH
IN_COLS = N_QKV + 2 * B_WIDTH + 2 * D_MODEL
COL_CHUNK = 512
LANES = 128
LSE_REP = LANES // HEADS
ROUTE_OFF = MOE_GROUPS
ROUTE_ROWS = 32
NEG = -1e30
LOG2E = 1.4426950408889634
VMEM_LIMIT = 56 * 1024 * 1024
ATTN_BLOCKS_PER_STEP = 16
MIX_SUB_ROWS = 512
MOE_TILE = 512
SC_WINDOW = 128
SC_CORES = 2


def _sigmoid(x):
    return 1.0 / (1.0 + jnp.exp(-x))


def _gelu_tanh(x):
    return x * (0.5 * (1.0 + jnp.tanh(0.7978845608028654 * (x + 0.044715 * (x * x * x)))))


def _rmsnorm_mod(x, gain, shift, scale):
    y = x * lax.rsqrt(jnp.mean(x * x, axis=-1, keepdims=True) + EPS)
    return y * gain * (1.0 + scale) + shift


def _mod_spec(mod, tm, seq):
    if mod.shape[1] == 1:
        return pl.BlockSpec((None, 1, D_MODEL), lambda b, s, *_: (b, 0, 0))
    return pl.BlockSpec((None, tm * mod.shape[1] // seq, D_MODEL), lambda b, s, *_: (b, s, 0))


def _mod_rows(ref, rows):
    m = ref[...]
    if m.shape[0] in (1, rows):
        return m
    rep = rows // m.shape[0]
    return jnp.broadcast_to(m[:, None, :], (m.shape[0], rep, m.shape[1])).reshape(rows, m.shape[1])


def _const_spec(arr):
    nd = arr.ndim
    return pl.BlockSpec(arr.shape, lambda *_: (0,) * nd)


def _adaln_kernel(c_ref, w_ref, b_ref, *rest):
    o_ref = rest[-1]
    c = c_ref[...]
    a = (c * _sigmoid(c)).astype(BF16)
    o_ref[...] = jnp.dot(a, w_ref[...].astype(BF16), preferred_element_type=F32) + b_ref[...]


def _adaln(c, w, b, tn=1024, after=None):
    rows, ncols = c.shape[0], w.shape[1]
    ins = [c, w, b.reshape(1, ncols)]
    in_specs = [pl.BlockSpec((rows, D_MODEL), lambda j: (0, 0)),
                pl.BlockSpec((D_MODEL, tn), lambda j: (0, j)),
                pl.BlockSpec((1, tn), lambda j: (0, j))]
    if after is not None:
        ins.append(after)
        in_specs.append(pl.BlockSpec(memory_space=pl.ANY))
    return pl.pallas_call(
        _adaln_kernel,
        grid=(ncols // tn,),
        in_specs=in_specs,
        out_specs=pl.BlockSpec((rows, tn), lambda j: (0, j)),
        out_shape=jax.ShapeDtypeStruct((rows, ncols), F32),
        compiler_params=pltpu.CompilerParams(dimension_semantics=("arbitrary",)),
        name="adaln",
    )(*ins)


def _inproj_kernel(x_ref, sh_ref, sc_ref, g_ref, w_ref, lng_ref, lnb_ref, *refs, tm, keeps, dils, emit_vn_f32,
                   has_after, kv_feature_major):
    if has_after:
        refs = refs[1:]
    qkv0_ref, qkv1_ref, qkv2_ref, u_ref, vn_ref, ga_ref, gb_ref, kv0_ref, kv1_ref, kv2_ref = refs[:10]
    rest = refs[10:]
    perm_scr = rest[-1]
    h = _rmsnorm_mod(x_ref[...], g_ref[...], _mod_rows(sh_ref, tm), _mod_rows(sc_ref, tm)).astype(BF16)

    def proj(c):
        return jnp.dot(h, w_ref[:, c * COL_CHUNK:(c + 1) * COL_CHUNK], preferred_element_type=F32)

    def cols(c):
        return slice(c * COL_CHUNK, (c + 1) * COL_CHUNK)

    qkv_refs = (qkv0_ref, qkv1_ref, qkv2_ref)

    def put_qkv(g, t, z):
        dil = dils[g]
        if dil == 1:
            qkv_refs[g][0, :, cols(t)] = z.astype(BF16)
            return
        slot = perm_scr.at[t % 2]
        n_slab = COL_CHUNK // LANES
        for j in range(n_slab):
            slot[j] = z[:, j * LANES:(j + 1) * LANES]
        for r in range(dil):
            rows = jnp.concatenate([slot[j, pl.ds(r, tm // dil, stride=dil), :] for j in range(n_slab)], axis=1)
            qkv_refs[g][r, :, cols(t)] = rows.astype(BF16)

    for g in range(N_GROUPS):
        put_qkv(g, 0, proj(g) * (HEAD_DIM ** -0.5 * LOG2E))

    kv_refs = (kv0_ref, kv1_ref, kv2_ref)
    for g in range(N_GROUPS):
        keep = keeps[g]
        for t in (1, 2):
            z = proj(t * N_GROUPS + g)
            put_qkv(g, t, z)
            if kv_feature_major:
                zt = jnp.transpose(z if keep >= tm else z[tm - keep:, :])
                kv_refs[g][(t - 1) * A_WIDTH:t * A_WIDTH, :] = zt
            else:
                kv_refs[g][:, cols(t - 1)] = z if keep >= tm else z[tm - keep:, :]

    base = N_QKV // COL_CHUNK
    for c in range(2):
        u_ref[:, cols(c)] = _gelu_tanh(proj(base + c)).astype(BF16)

    vs = [_gelu_tanh(proj(base + 2 + c)) for c in range(2)]
    mu = (jnp.sum(vs[0], axis=-1, keepdims=True) + jnp.sum(vs[1], axis=-1, keepdims=True)) * (1.0 / B_WIDTH)
    ds = [v - mu for v in vs]
    var = (jnp.sum(ds[0] * ds[0], axis=-1, keepdims=True)
           + jnp.sum(ds[1] * ds[1], axis=-1, keepdims=True)) * (1.0 / B_WIDTH)
    inv = lax.rsqrt(var + EPS)
    for c in range(2):
        vn = ds[c] * inv * lng_ref[:, cols(c)] + lnb_ref[:, cols(c)]
        vn_ref[:, cols(c)] = vn.astype(BF16)
        if emit_vn_f32:
            rest[0][:, cols(c)] = vn

    for c in range(2):
        ga_ref[:, cols(c)] = _sigmoid(proj(base + 4 + c)).astype(BF16)
        gb_ref[:, cols(c)] = _sigmoid(proj(base + 6 + c)).astype(BF16)


def _inproj(x, sh, sc, gain, w_in, ln_g, ln_b, *, tm, keeps, dils, emit_vn_f32, kv_feature_major, after=None):
    bsz, seq, _ = x.shape
    n_s = seq // tm
    tok = lambda width: pl.BlockSpec((None, tm, width), lambda b, s: (b, s, 0))

    def kv_spec(keep):
        first = n_s - max(keep // tm, 1)
        rows = min(keep, tm)
        if kv_feature_major:
            return pl.BlockSpec((None, 2 * A_WIDTH, rows), lambda b, s: (b, 0, jnp.maximum(s - first, 0)))
        return pl.BlockSpec((None, rows, 2 * A_WIDTH), lambda b, s: (b, jnp.maximum(s - first, 0), 0))

    out_specs, out_shape = [], []
    for dil in dils:
        out_specs.append(pl.BlockSpec((None, dil, tm // dil, 3 * A_WIDTH), lambda b, s: (b, 0, s, 0)))
        out_shape.append(jax.ShapeDtypeStruct((bsz, dil, seq // dil, 3 * A_WIDTH), BF16))
    out_specs += [tok(B_WIDTH), tok(B_WIDTH), tok(D_MODEL), tok(D_MODEL)]
    out_shape += [jax.ShapeDtypeStruct((bsz, seq, D_MODEL), BF16)] * 4
    for keep in keeps:
        out_specs.append(kv_spec(keep))
        kv_shape = (bsz, 2 * A_WIDTH, keep) if kv_feature_major else (bsz, keep, 2 * A_WIDTH)
        out_shape.append(jax.ShapeDtypeStruct(kv_shape, F32))
    if emit_vn_f32:
        out_specs.append(tok(B_WIDTH))
        out_shape.append(jax.ShapeDtypeStruct((bsz, seq, B_WIDTH), F32))

    ins = [x, sh, sc, gain, w_in, ln_g, ln_b]
    in_specs = [tok(D_MODEL), _mod_spec(sh, tm, seq), _mod_spec(sc, tm, seq), _const_spec(gain),
                pl.BlockSpec(w_in.shape, lambda b, s: (0, 0), pipeline_mode=pl.Buffered(1)),
                _const_spec(ln_g), _const_spec(ln_b)]
    if after is not None:
        ins.append(after)
        in_specs.append(pl.BlockSpec(memory_space=pl.ANY))
    scratch = [pltpu.VMEM((2, COL_CHUNK // LANES, tm, LANES), F32)]
    return pl.pallas_call(
        functools.partial(_inproj_kernel, tm=tm, keeps=keeps, dils=dils, emit_vn_f32=emit_vn_f32,
                          has_after=after is not None, kv_feature_major=kv_feature_major),
        grid=(bsz, n_s),
        in_specs=in_specs,
        out_specs=out_specs,
        out_shape=out_shape,
        scratch_shapes=scratch,
        compiler_params=pltpu.CompilerParams(dimension_semantics=("arbitrary", "arbitrary"),
                                             vmem_limit_bytes=VMEM_LIMIT),
        name="inproj",
    )(*ins)


def _attn_kernel(q_ref, kp_ref, kc_ref, vp_ref, vc_ref, bias0_ref, bias_ref, o_ref, lse_ref, *, n_seq, q_blocks):
    lane = lax.broadcasted_iota(jnp.int32, (A_BLOCK, LANES), 1)
    low = lane < HEAD_DIM
    zero = jnp.zeros((), BF16)
    for seq_i, i in [(a, b) for a in range(n_seq) for b in range(q_blocks)]:
        rows = (seq_i, slice(i * A_BLOCK, (i + 1) * A_BLOCK))
        q = q_ref[rows[0], rows[1], :]
        key_rows = slice((i - 1) * A_BLOCK, (i + 1) * A_BLOCK)
        if i == 0:
            k = jnp.concatenate([kp_ref[seq_i], kc_ref[seq_i, :A_BLOCK, :]], axis=0)
            v = jnp.concatenate([vp_ref[seq_i], vc_ref[seq_i, :A_BLOCK, :]], axis=0)
        else:
            k, v = kc_ref[seq_i, key_rows, :], vc_ref[seq_i, key_rows, :]
        b_ref = bias0_ref if i == 0 else bias_ref
        lse_tile = jnp.zeros((A_BLOCK, LANES), F32)
        for j in range(HEADS // 2):
            pair = slice(j * LANES, (j + 1) * LANES)
            qp, kpair, vpair = q[:, pair], k[:, pair], v[:, pair]
            outs = []
            for e in range(2):
                h = 2 * j + e
                qm = jnp.where(low if e == 0 else jnp.logical_not(low), qp, zero)
                s = lax.dot_general(qm, kpair, (((1,), (1,)), ((), ())), preferred_element_type=F32)
                s = s + b_ref[h]
                m = jnp.max(s, axis=-1, keepdims=True)
                p = jnp.exp2(s - m)
                den = jnp.sum(p, axis=-1, keepdims=True)
                o = jnp.dot(p.astype(BF16), vpair, preferred_element_type=F32)
                first = h * LSE_REP
                lse_tile = jnp.where((lane >= first) & (lane < first + LSE_REP // 2), m,
                                     jnp.where((lane >= first + LSE_REP // 2) & (lane < first + LSE_REP), den,
                                               lse_tile))
                outs.append(o)
            o_ref[rows[0], rows[1], pair] = jnp.where(low, outs[0], outs[1]).astype(BF16)
        lse_ref[rows[0], rows[1], :] = lse_tile


def _attn_bias(g):
    _, dil = A_GROUPS[g]
    n = N_GROUPS * HEADS
    e = np.arange(1, n + 1, dtype=np.float32)
    slopes = np.exp2(-8.0 * e / n).astype(np.float32).reshape(N_GROUPS, HEADS)[g]
    qi = np.arange(A_BLOCK)[:, None]
    ki = np.arange(2 * A_BLOCK)[None, :]
    delta = qi + A_BLOCK - ki
    band = (delta >= 0) & (delta <= A_STEPS)
    dist = (delta * dil).astype(np.float32)
    bias = (-slopes[:, None, None] * dist[None] * np.float32(LOG2E)).astype(np.float32)
    out = np.empty((2, HEADS, A_BLOCK, 2 * A_BLOCK), np.float32)
    out[1] = np.where(band[None], bias, NEG)
    out[0] = np.where((band & (ki >= A_BLOCK))[None], bias, NEG)
    return jnp.asarray(out)


def _attn_prompt(qkv, g):
    bsz, dil, steps, _ = qkv.shape
    qb = min(ATTN_BLOCKS_PER_STEP, steps // A_BLOCK)
    ns = min(ATTN_BLOCKS_PER_STEP // qb, dil)
    rows = qb * A_BLOCK
    n_steps = steps // rows
    bias = _attn_bias(g)

    def blk(t, prev):
        if prev:
            return pl.BlockSpec((None, ns, A_BLOCK, A_WIDTH),
                                lambda b, r, n: (b, r, jnp.maximum(n * qb - 1, 0), t))
        return pl.BlockSpec((None, ns, rows, A_WIDTH), lambda b, r, n: (b, r, n, t))

    bias_blk = lambda pick: pl.BlockSpec((None, HEADS, A_BLOCK, 2 * A_BLOCK), lambda b, r, n: (pick(n), 0, 0, 0))
    return pl.pallas_call(
        functools.partial(_attn_kernel, n_seq=ns, q_blocks=qb),
        grid=(bsz, dil // ns, n_steps),
        in_specs=[blk(0, False), blk(1, True), blk(1, False), blk(2, True), blk(2, False),
                  bias_blk(lambda n: jnp.minimum(n, 1)), bias_blk(lambda n: 1)],
        out_specs=[pl.BlockSpec((None, ns, rows, A_WIDTH), lambda b, r, n: (b, r, n, 0)),
                   pl.BlockSpec((None, ns, rows, LANES), lambda b, r, n: (b, r, n, 0))],
        out_shape=[jax.ShapeDtypeStruct((bsz, dil, steps, A_WIDTH), BF16),
                   jax.ShapeDtypeStruct((bsz, dil, steps, LANES), F32)],
        compiler_params=pltpu.CompilerParams(dimension_semantics=("arbitrary",) * 3),
        name=f"attn_prompt_g{g}",
    )(qkv, qkv, qkv, qkv, qkv, bias, bias)


def _attn_sample_kernel(q0_ref, q1_ref, q2_ref, c0_ref, c1_ref, c2_ref, bc0_ref, bc1_ref, bc2_ref,
                        bn0_ref, bn1_ref, bn2_ref, after_ref, o_ref, *, t_new):
    del after_ref
    n_rows = HEADS * t_new
    row = lax.broadcasted_iota(jnp.int32, (n_rows, A_WIDTH), 0)
    lane = lax.broadcasted_iota(jnp.int32, (n_rows, A_WIDTH), 1)
    head_mask = (row // t_new) == (lane // HEAD_DIM)
    pad = jnp.zeros((LANES - t_new, A_WIDTH), F32)
    outs, lses = [], []
    for qkv_ref, c_ref, bc_ref, bn_ref in ((q0_ref, c0_ref, bc0_ref, bn0_ref), (q1_ref, c1_ref, bc1_ref, bn1_ref),
                                           (q2_ref, c2_ref, bc2_ref, bn2_ref)):
        q = qkv_ref[:, :A_WIDTH].astype(F32)
        k_new = qkv_ref[:, A_WIDTH:2 * A_WIDTH].astype(F32)
        v_new = qkv_ref[:, 2 * A_WIDTH:].astype(F32)
        k_new = jnp.concatenate([k_new, pad], axis=0).astype(BF16)
        v_new = jnp.concatenate([v_new, pad], axis=0).astype(BF16)
        q_rows = jnp.where(head_mask, jnp.concatenate([q] * HEADS, axis=0), 0.0).astype(BF16)
        k_buf_t = c_ref[:A_WIDTH, :].astype(BF16)
        v_buf_t = c_ref[A_WIDTH:, :].astype(BF16)
        nt = (((1,), (1,)), ((), ()))
        s_buf = jnp.dot(q_rows, k_buf_t, preferred_element_type=F32) + bc_ref[...]
        s_new = lax.dot_general(q_rows, k_new, nt, preferred_element_type=F32) + bn_ref[...]
        m = jnp.maximum(jnp.max(s_buf, axis=-1, keepdims=True), jnp.max(s_new, axis=-1, keepdims=True))
        p_buf = jnp.exp2(s_buf - m)
        p_new = jnp.exp2(s_new - m)
        den = jnp.sum(p_buf, axis=-1, keepdims=True) + jnp.sum(p_new, axis=-1, keepdims=True)
        o = (lax.dot_general(p_buf.astype(BF16), v_buf_t, nt, preferred_element_type=F32)
             + jnp.dot(p_new.astype(BF16), v_new, preferred_element_type=F32)) / den
        outs.append(o)
        lses.append(m + jnp.log2(den))
    top = jnp.maximum(jnp.maximum(lses[0], lses[1]), lses[2])
    ws = [jnp.exp2(l - top) for l in lses]
    tot = ws[0] + ws[1] + ws[2]
    acc = (ws[0] / tot) * outs[0] + (ws[1] / tot) * outs[1] + (ws[2] / tot) * outs[2]
    acc = jnp.where(head_mask, acc, 0.0).reshape(HEADS, t_new, A_WIDTH)
    o_ref[...] = jnp.sum(acc, axis=0).astype(BF16)


def _sample_bias(g, t_new, buf):
    _, dil = A_GROUPS[g]
    n = N_GROUPS * HEADS
    e = np.arange(1, n + 1, dtype=np.float32)
    slopes = np.exp2(-8.0 * e / n).astype(np.float32).reshape(N_GROUPS, HEADS)[g]
    t = np.arange(t_new)[:, None]
    idx = np.concatenate([np.arange(buf), buf + np.arange(LANES)])[None, :]
    dist = buf + t - idx
    valid = (dist >= 0) & (dist % dil == 0) & (dist <= A_STEPS * dil) & (idx < buf + t_new)
    bias = -slopes[:, None, None] * dist.astype(np.float32)[None] * np.float32(LOG2E)
    bias = np.where(valid[None], bias, NEG).astype(np.float32).reshape(HEADS * t_new, buf + LANES)
    return jnp.asarray(bias[:, :buf]), jnp.asarray(bias[:, buf:])


def _attn_sample(qkvs, caches, *, after):
    bsz, t_new, _ = qkvs[0].shape
    cache_v = [jnp.transpose(c, (0, 2, 3, 4, 1)).reshape(bsz, 2 * A_WIDTH, c.shape[1]) for c in caches]
    biases = [_sample_bias(g, t_new, cache_v[g].shape[2]) for g in range(N_GROUPS)]
    bcs = [b[0] for b in biases]
    bns = [b[1] for b in biases]
    return pl.pallas_call(
        functools.partial(_attn_sample_kernel, t_new=t_new),
        grid=(bsz,),
        in_specs=[pl.BlockSpec((None, t_new, 3 * A_WIDTH), lambda b: (b, 0, 0))] * N_GROUPS
                 + [pl.BlockSpec((None, 2 * A_WIDTH, c.shape[2]), lambda b: (b, 0, 0)) for c in cache_v]
                 + [_const_spec(b) for b in bcs] + [_const_spec(b) for b in bns]
                 + [pl.BlockSpec(memory_space=pl.ANY)],
        out_specs=pl.BlockSpec((None, t_new, A_WIDTH), lambda b: (b, 0, 0)),
        out_shape=jax.ShapeDtypeStruct((bsz, t_new, A_WIDTH), BF16),
        compiler_params=pltpu.CompilerParams(dimension_semantics=("arbitrary",),
                                             vmem_limit_bytes=VMEM_LIMIT),
        name="attn_sample",
    )(*qkvs, *cache_v, *bcs, *bns, after)


def _first_max4(v):
    top = jnp.maximum(jnp.maximum(v[0], v[1]), jnp.maximum(v[2], v[3]))
    idx = jnp.where(v[0] == top, 0.0, jnp.where(v[1] == top, 1.0, jnp.where(v[2] == top, 2.0, 3.0)))
    return top, idx


def _route(lt):
    row = lambda k: lt[k:k + 1, :]
    g = [row(k) for k in range(MOE_GROUPS)]
    gmax, g_idx = _first_max4(g)
    g_prob = 1.0 / (jnp.exp(g[0] - gmax) + jnp.exp(g[1] - gmax) + jnp.exp(g[2] - gmax) + jnp.exp(g[3] - gmax))
    cand = []
    for k in range(MOE_PER_GROUP):
        c = row(ROUTE_OFF + MOE_PER_GROUP * (MOE_GROUPS - 1) + k)
        for gi in range(MOE_GROUPS - 2, -1, -1):
            c = jnp.where(g_idx == float(gi), row(ROUTE_OFF + MOE_PER_GROUP * gi + k), c)
        cand.append(c)
    e1, i1 = _first_max4(cand)
    rest = [jnp.where(i1 == float(k), -jnp.inf, cand[k]) for k in range(MOE_PER_GROUP)]
    e2, i2 = _first_max4(rest)
    t = jnp.exp(e2 - e1)
    w1 = 1.0 / (1.0 + t)
    w2 = t / (1.0 + t)
    return MOE_PER_GROUP * g_idx + i1, MOE_PER_GROUP * g_idx + i2, w1 * g_prob, w2 * g_prob


def _pack_bf16_pairs(x):
    q = x.shape[1] // 4
    bits = lax.bitcast_convert_type(x.astype(BF16).astype(F32), jnp.int32)
    pack = lambda hi, lo: hi | lax.shift_right_logical(lo, 16)
    return pack(bits[:, :q], bits[:, 2 * q:3 * q]), pack(bits[:, q:2 * q], bits[:, 3 * q:])


def _unpack_bf16_pairs(pa, pb):
    hi = lambda p: lax.bitcast_convert_type(p & jnp.int32(-65536), F32)
    lo = lambda p: lax.bitcast_convert_type(lax.shift_left(p, 16), F32)
    return jnp.concatenate([hi(pa), hi(pb), lo(pa), lo(pb)], axis=1)


def _mix_kernel(*refs, tm, dils, routed, span):
    n_attn = len(dils)
    x_ref = refs[0]
    if n_attn > 1:
        o_refs = refs[1:1 + n_attn]
        l_refs = refs[1 + n_attn:1 + 2 * n_attn]
        exp_ref = refs[1 + 2 * n_attn]
        pos = 2 + 2 * n_attn
    else:
        o_refs = refs[1:2]
        pos = 2
    (u_ref, vn_ref, ga_ref, gb_ref, sw_ref, sbt_ref, wpa_ref, wpb_ref, wo_ref,
     gt1_ref, sh2_ref, sc2_ref, n2g_ref, wrh_ref, wrl_ref, br_ref) = refs[pos:pos + 16]
    n_out = 6 if routed else 3
    out_refs = refs[pos + 16:pos + 16 + n_out]
    scr = list(refs[pos + 16 + n_out:])
    x1_ref = out_refs[0]
    ob_scr = scr.pop(0)

    def to_positions(ref, scr, dil):
        if dil == 1:
            return ref[0].astype(F32)
        n_slab = scr.shape[0]
        for r in range(dil):
            rows = ref[r].astype(F32)
            for j in range(n_slab):
                scr[j, pl.ds(r, tm // dil, stride=dil), :] = rows[:, j * LANES:(j + 1) * LANES]
        return jnp.concatenate([scr[j] for j in range(n_slab)], axis=1)

    if n_attn > 1:
        o_scr, l_scr = scr[0], scr[1]
        stats = [to_positions(l_refs[g], l_scr.at[g], dils[g]) for g in range(n_attn)]
        lane = lax.broadcasted_iota(jnp.int32, stats[0].shape, 1)
        is_max = (lane & (LSE_REP - 1)) < LSE_REP // 2
        half = LSE_REP // 2
        maxes = [jnp.where(is_max, s, pltpu.roll(s, half, axis=1)) for s in stats]
        dens = [jnp.where(is_max, pltpu.roll(s, LANES - half, axis=1), s) for s in stats]
        lses = [m + jnp.log2(d) for m, d in zip(maxes, dens)]
        top = functools.reduce(jnp.maximum, lses)
        tot = functools.reduce(lambda a, b: a + b, [jnp.exp2(l - top) for l in lses])
        o_a = None
        for g in range(n_attn):
            w = jnp.exp2(maxes[g] - top) / tot
            hi = w.astype(BF16)
            lo = (w - hi.astype(F32)).astype(BF16)
            w_exp = jnp.dot(jnp.concatenate([hi, lo], axis=1), exp_ref[...], preferred_element_type=F32)
            term = w_exp * to_positions(o_refs[g], o_scr.at[g], dils[g])
            o_a = term if o_a is None else o_a + term
        o_a = o_a.astype(BF16)
    else:
        o_a = o_refs[0][...]

    r_i = lax.broadcasted_iota(jnp.int32, (span, span), 0)
    c_i = lax.broadcasted_iota(jnp.int32, (span, span), 1)
    tril = r_i >= c_i
    for gi in range(B_GROUPS):
        gcols = slice(gi * LANES, (gi + 1) * LANES)
        b_s = sbt_ref[:span, gi:gi + 1]
        if span == B_CHUNK:
            w_s = jnp.where(tril, sw_ref[gi], 0.0).astype(BF16)
            n_chunk = tm // span
            vn_wide = jnp.concatenate([vn_ref[c * span:(c + 1) * span, gcols] for c in range(n_chunk)], axis=1)
            mixed = jnp.dot(w_s, vn_wide, preferred_element_type=F32) + b_s
            for c in range(n_chunk):
                rows = slice(c * span, (c + 1) * span)
                ob_scr[rows, gcols] = (u_ref[rows, gcols].astype(F32)
                                       * mixed[:, c * LANES:(c + 1) * LANES]).astype(BF16)
        else:
            w_s = jnp.where(tril, sw_ref[gi, :span, :span], 0.0)
            vn3 = vn_ref[:, gcols].astype(F32).reshape(tm // span, span, LANES)
            mixed = jnp.broadcast_to(b_s, (span, LANES))[None]
            for s in range(span):
                w_col = jnp.broadcast_to(w_s[:, s:s + 1], (span, LANES))[None]
                mixed = mixed + w_col * jnp.broadcast_to(vn3[:, s:s + 1, :], vn3.shape)
            u3 = u_ref[:, gcols].astype(F32).reshape(tm // span, span, LANES)
            ob_scr[:, gcols] = (u3 * mixed).reshape(tm, LANES).astype(BF16)

    sub = min(tm, MIX_SUB_ROWS)
    if routed:
        carry = scr[-1]
        first_step = (pl.program_id(0) == 0) & (pl.program_id(1) == 0)
        seen = jnp.where(first_step, 0.0, carry[:, 0:1])
        r_i = lax.broadcasted_iota(jnp.int32, (sub, sub), 0)
        c_i = lax.broadcasted_iota(jnp.int32, (sub, sub), 1)
        earlier = jnp.where(r_i < c_i, 1.0, 0.0).astype(BF16)
        e_iota = lax.broadcasted_iota(jnp.int32, (MOE_EXPERTS, sub), 0).astype(F32)
    nt = (((1,), (1,)), ((), ()))
    for r0 in range(0, tm, sub):
        rs = slice(r0, r0 + sub)
        mod = lambda ref: ref[...] if ref.shape[0] == 1 else _mod_rows(ref, tm)[rs, :]
        pa = jnp.dot(o_a[rs], wpa_ref[...], preferred_element_type=F32)
        pb = jnp.dot(ob_scr[rs, :], wpb_ref[...], preferred_element_type=F32)
        mix = (ga_ref[rs, :].astype(F32) * pa + gb_ref[rs, :].astype(F32) * pb).astype(BF16)
        x1 = x_ref[rs, :] + mod(gt1_ref) * jnp.dot(mix, wo_ref[...], preferred_element_type=F32)
        x1_ref[rs, :] = x1
        h2 = _rmsnorm_mod(x1, n2g_ref[...], mod(sh2_ref), mod(sc2_ref))
        h_hi = h2.astype(BF16)
        h_lo = (h2 - h_hi.astype(F32)).astype(BF16)
        lt = (lax.dot_general(wrh_ref[...], h_hi, nt, preferred_element_type=F32)
              + lax.dot_general(wrh_ref[...], h_lo, nt, preferred_element_type=F32)
              + lax.dot_general(wrl_ref[...], h_hi, nt, preferred_element_type=F32)) + br_ref[...]
        ex1, ex2, w1, w2 = _route(lt)
        if not routed:
            _, h2_ref, comb_ref = out_refs
            h2_ref[rs, :] = h_hi
            e_lane = lax.broadcasted_iota(jnp.int32, (LANES, sub), 0).astype(F32) - float(ROUTE_OFF)
            comb_t = jnp.where(e_lane == ex1, w1, jnp.where(e_lane == ex2, w2, 0.0))
            comb_ref[rs, :] = jnp.transpose(comb_t)
            continue

        _, hpa_ref, hpb_ref, route_ref, wcol_ref, cnt_ref = out_refs
        hpa_ref[rs, :], hpb_ref[rs, :] = _pack_bf16_pairs(h2)
        hot1, hot2 = e_iota == ex1, e_iota == ex2
        onehot = jnp.where(hot1 | hot2, 1.0, 0.0)
        rank = jnp.dot(onehot.astype(BF16), earlier, preferred_element_type=F32) + seen
        r1 = jnp.sum(jnp.where(hot1, rank, 0.0), axis=0, keepdims=True)
        r2 = jnp.sum(jnp.where(hot2, rank, 0.0), axis=0, keepdims=True)
        seen = seen + jnp.sum(onehot, axis=1, keepdims=True)
        fields = jnp.concatenate([ex1, ex2, w1, w2, r1, r2, jnp.zeros((2, sub), F32)], axis=0)
        route_ref[:, rs] = fields
        wcol_ref[rs, :] = jnp.transpose(jnp.concatenate([fields, jnp.zeros((LANES - 8, sub), F32)], axis=0))
    if routed:
        carry[...] = jnp.broadcast_to(seen, carry.shape)
        cnt_ref[...] = jnp.broadcast_to(seen, cnt_ref.shape)


def _expand_matrix():
    e = np.zeros((LANES, A_WIDTH), np.float32)
    for h in range(HEADS):
        e[h * LSE_REP, h * HEAD_DIM:(h + 1) * HEAD_DIM] = 1.0
    return jnp.asarray(np.concatenate([e, e], axis=0), dtype=BF16)


def _mix(x, attn_outs, attn_lses, u, vn, ga, gb, sgu_w, sgu_bt, w_pa, w_pb, w_o,
         gt1, sh2, sc2, n2g, w_r_hi, w_r_lo, b_r, *, tm, routed, span):
    bsz, seq, _ = x.shape
    n_attn = len(attn_outs)
    tok = lambda width: pl.BlockSpec((None, tm, width), lambda b, s: (b, s, 0))
    ins = [x] + list(attn_outs)
    scratch = [pltpu.VMEM((tm, B_WIDTH), BF16)]
    if n_attn > 1:
        dils = tuple(o.shape[1] for o in attn_outs)
        res = lambda dil, width: pl.BlockSpec((None, dil, tm // dil, width), lambda b, s: (b, 0, s, 0))
        e2 = _expand_matrix()
        ins += list(attn_lses) + [e2]
        specs = ([tok(D_MODEL)] + [res(d, A_WIDTH) for d in dils] + [res(d, LANES) for d in dils]
                 + [_const_spec(e2)])
        scratch += [pltpu.VMEM((n_attn, A_WIDTH // LANES, tm, LANES), F32), pltpu.VMEM((n_attn, 1, tm, LANES), F32)]
    else:
        dils = (1,)
        specs = [tok(D_MODEL), tok(A_WIDTH)]
    ins += [u, vn, ga, gb, sgu_w, sgu_bt, w_pa, w_pb, w_o, gt1, sh2, sc2, n2g, w_r_hi, w_r_lo, b_r]
    specs += [tok(B_WIDTH), tok(B_WIDTH), tok(D_MODEL), tok(D_MODEL),
              _const_spec(sgu_w), _const_spec(sgu_bt), _const_spec(w_pa), _const_spec(w_pb), _const_spec(w_o),
              _mod_spec(gt1, tm, seq), _mod_spec(sh2, tm, seq), _mod_spec(sc2, tm, seq), _const_spec(n2g),
              _const_spec(w_r_hi), _const_spec(w_r_lo), _const_spec(b_r)]
    if routed:
        quarter = D_MODEL // 4
        out_specs = [tok(D_MODEL), tok(quarter), tok(quarter),
                     pl.BlockSpec((None, 8, tm), lambda b, s: (b, 0, s)), tok(LANES),
                     pl.BlockSpec((MOE_EXPERTS, LANES), lambda b, s: (0, 0))]
        out_shape = [jax.ShapeDtypeStruct((bsz, seq, D_MODEL), F32),
                     jax.ShapeDtypeStruct((bsz, seq, quarter), jnp.int32),
                     jax.ShapeDtypeStruct((bsz, seq, quarter), jnp.int32),
                     jax.ShapeDtypeStruct((bsz, 8, seq), F32),
                     jax.ShapeDtypeStruct((bsz, seq, LANES), F32),
                     jax.ShapeDtypeStruct((MOE_EXPERTS, LANES), F32)]
        scratch.append(pltpu.VMEM((MOE_EXPERTS, LANES), F32))
    else:
        out_specs = [tok(D_MODEL), tok(D_MODEL), tok(LANES)]
        out_shape = [jax.ShapeDtypeStruct((bsz, seq, D_MODEL), F32),
                     jax.ShapeDtypeStruct((bsz, seq, D_MODEL), BF16),
                     jax.ShapeDtypeStruct((bsz, seq, LANES), F32)]
    return pl.pallas_call(
        functools.partial(_mix_kernel, tm=tm, dils=dils, routed=routed, span=span),
        grid=(bsz, seq // tm),
        in_specs=specs,
        out_specs=out_specs,
        out_shape=out_shape,
        scratch_shapes=scratch,
        compiler_params=pltpu.CompilerParams(dimension_semantics=("arbitrary", "arbitrary"),
                                             vmem_limit_bytes=VMEM_LIMIT),
        name="mix",
    )(*ins)


def _moe_kernel(h_ref, comb_ref, wg_ref, wu_ref, wd_ref, x1_ref, gt2_ref, shf_ref, scf_ref, nfg_ref,
                y_ref, acc_ref):
    e = pl.program_id(2)

    @pl.when(e == 0)
    def _():
        acc_ref[...] = jnp.zeros_like(acc_ref)

    h = h_ref[...]
    hg = jnp.dot(h, wg_ref[...], preferred_element_type=F32)
    hu = jnp.dot(h, wu_ref[...], preferred_element_type=F32)
    comb = comb_ref[...]
    lane = lax.broadcasted_iota(jnp.int32, comb.shape, 1)
    cw = jnp.sum(jnp.where(lane == e + ROUTE_OFF, comb, 0.0), axis=-1, keepdims=True)
    act = (hg * _sigmoid(hg) * hu) * cw
    acc_ref[...] += jnp.dot(act.astype(BF16), wd_ref[...], preferred_element_type=F32)

    @pl.when(e == MOE_EXPERTS - 1)
    def _():
        rows = acc_ref.shape[0]
        x2 = x1_ref[...] + _mod_rows(gt2_ref, rows) * acc_ref[...]
        y_ref[...] = _rmsnorm_mod(x2, nfg_ref[...], _mod_rows(shf_ref, rows), _mod_rows(scf_ref, rows))


def _moe(h2, comb, w_gate, w_up, w_down, x1, gt2, shf, scf, nfg, *, tm):
    bsz, seq, _ = h2.shape
    tok = lambda width: pl.BlockSpec((None, tm, width), lambda b, s, e: (b, s, 0))
    return pl.pallas_call(
        _moe_kernel,
        grid=(bsz, seq // tm, MOE_EXPERTS),
        in_specs=[tok(D_MODEL), tok(LANES),
                  pl.BlockSpec((None, D_MODEL, MOE_HIDDEN), lambda b, s, e: (e, 0, 0)),
                  pl.BlockSpec((None, D_MODEL, MOE_HIDDEN), lambda b, s, e: (e, 0, 0)),
                  pl.BlockSpec((None, MOE_HIDDEN, D_MODEL), lambda b, s, e: (e, 0, 0)),
                  tok(D_MODEL), _mod_spec(gt2, tm, seq), _mod_spec(shf, tm, seq), _mod_spec(scf, tm, seq),
                  _const_spec(nfg)],
        out_specs=tok(D_MODEL),
        out_shape=jax.ShapeDtypeStruct((bsz, seq, D_MODEL), F32),
        scratch_shapes=[pltpu.VMEM((tm, D_MODEL), F32)],
        compiler_params=pltpu.CompilerParams(dimension_semantics=("arbitrary",) * 3,
                                             vmem_limit_bytes=VMEM_LIMIT),
        name="moe",
    )(h2, comb, w_gate, w_up, w_down, x1, gt2, shf, scf, nfg)


def _sc_mesh():
    return plsc.VectorSubcoreMesh(core_axis_name="c", subcore_axis_name="s")


def _sc_scatter_rows(rows, pos1, pos2, n_out):
    n, width = rows.shape
    steps = n // SC_WINDOW // SC_CORES

    @pl.kernel(out_type=jax.ShapeDtypeStruct((n_out, width), rows.dtype), mesh=_sc_mesh(), scratch_types=[])
    def scatter(rows_hbm, p1_hbm, p2_hbm, out_hbm):
        def body(x_vmem, i1_vmem, i2_vmem):
            pltpu.sync_copy(x_vmem, out_hbm.at[i1_vmem.at[0]])
            pltpu.sync_copy(x_vmem, out_hbm.at[i2_vmem.at[0]])

        pltpu.emit_pipeline(
            body, grid=(SC_CORES, steps),
            in_specs=[pl.BlockSpec((SC_WINDOW, width), lambda c, i: (c * steps + i, 0)),
                      pl.BlockSpec((1, SC_WINDOW), lambda c, i: (0, c * steps + i)),
                      pl.BlockSpec((1, SC_WINDOW), lambda c, i: (0, c * steps + i))],
            out_specs=[],
            core_axis_name=("c", "s"),
            dimension_semantics=(pltpu.PARALLEL, pltpu.PARALLEL),
        )(rows_hbm, p1_hbm, p2_hbm)

    return scatter(rows, pos1, pos2)


def _sc_gather_rows(table, idx):
    m, width = idx.shape[1], table.shape[1]
    steps = m // SC_WINDOW // SC_CORES

    @pl.kernel(out_type=jax.ShapeDtypeStruct((m, width), table.dtype), mesh=_sc_mesh(), scratch_types=[])
    def gather(table_hbm, idx_hbm, out_hbm):
        def body(i_vmem, o_vmem):
            pltpu.sync_copy(table_hbm.at[i_vmem.at[0]], o_vmem)

        pltpu.emit_pipeline(
            body, grid=(SC_CORES, steps),
            in_specs=[pl.BlockSpec((1, SC_WINDOW), lambda c, i: (0, c * steps + i))],
            out_specs=[pl.BlockSpec((SC_WINDOW, width), lambda c, i: (c * steps + i, 0))],
            core_axis_name=("c", "s"),
            dimension_semantics=(pltpu.PARALLEL, pltpu.PARALLEL),
        )(idx_hbm, out_hbm)

    return gather(table, idx)


def _ffn_kernel(te_ref, nt_ref, xa_ref, xb_ref, wg_ref, wu_ref, wd_ref, *rest):
    oa_ref, ob_ref, wg_s, wu_s, wd_s = rest[-5:]
    t = pl.program_id(0)
    live = t < nt_ref[0]

    @pl.when(live & ((t == 0) | (te_ref[t] != te_ref[jnp.maximum(t - 1, 0)])))
    def _():
        wg_s[...] = wg_ref[...].astype(BF16)
        wu_s[...] = wu_ref[...].astype(BF16)
        wd_s[...] = wd_ref[...].astype(BF16)

    @pl.when(live)
    def _():
        x = _unpack_bf16_pairs(xa_ref[...], xb_ref[...]).astype(BF16)
        hg = jnp.dot(x, wg_s[...], preferred_element_type=F32)
        hu = jnp.dot(x, wu_s[...], preferred_element_type=F32)
        act = (hg * _sigmoid(hg) * hu).astype(BF16)
        oa_ref[...], ob_ref[...] = _pack_bf16_pairs(jnp.dot(act, wd_s[...], preferred_element_type=F32))


def _ffn(xa, xb, tile_expert, n_tiles, w_gate, w_up, w_down, *, after):
    n_rows, quarter = xa.shape
    rows = pl.BlockSpec((MOE_TILE, quarter), lambda t, te, nt: (jnp.minimum(t, nt[0] - 1), 0))
    w_in = pl.BlockSpec((None, D_MODEL, MOE_HIDDEN), lambda t, te, nt: (te[t], 0, 0))
    w_out = pl.BlockSpec((None, MOE_HIDDEN, D_MODEL), lambda t, te, nt: (te[t], 0, 0))
    return pl.pallas_call(
        _ffn_kernel,
        grid_spec=pltpu.PrefetchScalarGridSpec(
            num_scalar_prefetch=2,
            grid=(n_rows // MOE_TILE,),
            in_specs=[rows, rows, w_in, w_in, w_out] + [pl.BlockSpec(memory_space=pl.ANY)] * len(after),
            out_specs=[rows, rows, w_in, w_in, w_out]),
        out_shape=[jax.ShapeDtypeStruct((n_rows, quarter), jnp.int32)] * 2
                  + [jax.ShapeDtypeStruct(w_gate.shape, BF16), jax.ShapeDtypeStruct(w_up.shape, BF16),
                     jax.ShapeDtypeStruct(w_down.shape, BF16)],
        compiler_params=pltpu.CompilerParams(dimension_semantics=("arbitrary",), vmem_limit_bytes=VMEM_LIMIT),
        name="moe_ffn",
    )(tile_expert, n_tiles, xa, xb, w_gate, w_up, w_down, *after)


def _final_kernel(x1_ref, g1a_ref, g1b_ref, g2a_ref, g2b_ref, route_ref, gt2_ref, shf_ref, scf_ref, nfg_ref, y_ref):
    o1 = _unpack_bf16_pairs(g1a_ref[...], g1b_ref[...])
    o2 = _unpack_bf16_pairs(g2a_ref[...], g2b_ref[...])
    route = route_ref[...]
    moe = route[:, 2:3] * o1 + route[:, 3:4] * o2
    x2 = x1_ref[...] + gt2_ref[...] * moe
    y_ref[...] = _rmsnorm_mod(x2, nfg_ref[...], shf_ref[...], scf_ref[...])


def _final(x1, ga, gb, route, gt2, shf, scf, nfg, *, tm):
    bsz, seq, _ = x1.shape
    n_s = seq // tm
    n_blk = bsz * n_s
    quarter = ga.shape[1]
    tok = lambda width: pl.BlockSpec((None, tm, width), lambda b, s: (b, s, 0))
    first = pl.BlockSpec((tm, quarter), lambda b, s: (b * n_s + s, 0))
    second = pl.BlockSpec((tm, quarter), lambda b, s: (n_blk + b * n_s + s, 0))
    return pl.pallas_call(
        _final_kernel,
        grid=(bsz, n_s),
        in_specs=[tok(D_MODEL), first, first, second, second, tok(LANES),
                  _mod_spec(gt2, tm, seq), _mod_spec(shf, tm, seq), _mod_spec(scf, tm, seq), _const_spec(nfg)],
        out_specs=tok(D_MODEL),
        out_shape=jax.ShapeDtypeStruct((bsz, seq, D_MODEL), F32),
        compiler_params=pltpu.CompilerParams(dimension_semantics=("arbitrary", "arbitrary")),
        name="moe_final",
    )(x1, ga, gb, ga, gb, route, gt2, shf, scf, nfg)


def _positions_kernel(base_ref, route_ref, p1_ref, p2_ref):
    rows = route_ref[...]
    for e_row, r_row, out_ref in ((0, 4, p1_ref), (1, 5, p2_ref)):
        expert = rows[e_row:e_row + 1, :]
        start = jnp.zeros(expert.shape, jnp.int32)
        for e in range(MOE_EXPERTS):
            start = jnp.where(expert == float(e), base_ref[e], start)
        out_ref[...] = start + rows[r_row:r_row + 1, :].astype(jnp.int32)


def _positions(route_rows, base):
    bsz, _, seq = route_rows.shape
    out = pl.BlockSpec((None, 1, seq), lambda b, base: (b, 0, 0))
    return pl.pallas_call(
        _positions_kernel,
        grid_spec=pltpu.PrefetchScalarGridSpec(
            num_scalar_prefetch=1, grid=(bsz,),
            in_specs=[pl.BlockSpec((None, 8, seq), lambda b, base: (b, 0, 0))],
            out_specs=[out, out]),
        out_shape=[jax.ShapeDtypeStruct((bsz, 1, seq), jnp.int32)] * 2,
        compiler_params=pltpu.CompilerParams(dimension_semantics=("arbitrary",)),
        name="moe_positions",
    )(base, route_rows)


def _moe_dispatch(hpa, hpb, route_rows, counts):
    bsz, seq, quarter = hpa.shape
    n_tok = bsz * seq
    t_max = 2 * n_tok // MOE_TILE + MOE_EXPERTS
    cnt = counts[:, 0].astype(jnp.int32)
    tiles_e = jnp.maximum((cnt + MOE_TILE - 1) // MOE_TILE, 1)
    tile_end = jnp.cumsum(tiles_e)
    n_tiles = tile_end[-1:]
    base = (tile_end - tiles_e) * MOE_TILE
    t_ids = jnp.minimum(jnp.arange(t_max, dtype=jnp.int32), n_tiles[0] - 1)
    tile_expert = jnp.sum((t_ids[:, None] >= tile_end[None, :]).astype(jnp.int32), axis=1)
    pos1, pos2 = (p.reshape(1, n_tok) for p in _positions(route_rows, base))
    n_rows = t_max * MOE_TILE
    xa = _sc_scatter_rows(hpa.reshape(n_tok, quarter), pos1, pos2, n_rows)
    xb = _sc_scatter_rows(hpb.reshape(n_tok, quarter), pos1, pos2, n_rows)
    return xa, xb, jnp.concatenate([pos1, pos2], axis=1), tile_expert, n_tiles


def kernel(x_prompt, x_sample, cache_kv_w128, cache_kv_w512, cache_kv_w2048, c_prompt, c_sample,
           w_ada, b_ada, norm1_g, norm2_g, w_in, sgu_ln_g, sgu_ln_b, sgu_w, sgu_b, w_pa, w_pb, w_o,
           w_route_group, b_route_group, w_route_expert, b_route_expert, w_gate, w_up, w_down,
           normf_g, w_ada_final, b_ada_final):
    depth = w_ada.shape[0]
    assert depth == 1
    l = 0
    bp, seq, _ = x_prompt.shape
    bs, t_new, _ = x_sample.shape
    n_samp = bs * t_new

    c_all = jnp.concatenate([c_prompt, c_sample], axis=0)
    pad_rows = (-c_all.shape[0]) % 8
    c_all = jnp.pad(c_all, ((0, pad_rows), (0, 0)))
    mod = _adaln(c_all, w_ada[l], b_ada[l])

    def split_mods(m, n, lo, hi, per_token):
        parts = jnp.split(m[lo:hi], n, axis=-1)
        if per_token:
            return [p.reshape(1, hi - lo, D_MODEL) for p in parts]
        return [p.reshape(hi - lo, 1, D_MODEL) for p in parts]

    mods_p = split_mods(mod, 6, 0, bp, False)
    mods_s = split_mods(mod, 6, bp, bp + bs, True)

    row = lambda v: v.reshape(1, -1)
    w_in_b = w_in[l].astype(BF16)
    w_pa_b, w_pb_b, w_o_b = w_pa[l].astype(BF16), w_pb[l].astype(BF16), w_o[l].astype(BF16)
    w_re = jnp.transpose(w_route_expert[l], (0, 2, 1)).reshape(MOE_EXPERTS, D_MODEL)
    w_r = jnp.pad(jnp.concatenate([jnp.transpose(w_route_group[l]), w_re], axis=0),
                  ((0, ROUTE_ROWS - MOE_GROUPS - MOE_EXPERTS), (0, 0)))
    w_r_hi = w_r.astype(BF16)
    w_r_lo = (w_r - w_r_hi.astype(F32)).astype(BF16)
    b_r = jnp.pad(jnp.concatenate([b_route_group[l], b_route_expert[l].reshape(-1)]),
                  (0, ROUTE_ROWS - MOE_GROUPS - MOE_EXPERTS)).reshape(ROUTE_ROWS, 1)

    def inproj(x, mods, keeps, dils, tm, emit_vn_f32, kv_feature_major, after=None):
        return _inproj(x, mods[0], mods[1], row(norm1_g[l]), w_in_b, row(sgu_ln_g[l]), row(sgu_ln_b[l]),
                       tm=tm, keeps=keeps, dils=dils, emit_vn_f32=emit_vn_f32,
                       kv_feature_major=kv_feature_major, after=after)

    sgu_bt = jnp.transpose(sgu_b[l])

    def mix(x, mods, pin, attn_outs, attn_lses, tm, routed, span):
        u, vn, ga, gb = pin[3:7]
        return _mix(x, attn_outs, attn_lses, u, vn, ga, gb, sgu_w[l], sgu_bt,
                    w_pa_b, w_pb_b, w_o_b, mods[2], mods[3], mods[4], row(norm2_g[l]), w_r_hi, w_r_lo, b_r,
                    tm=tm, routed=routed, span=span)

    tm_p = 512
    keeps_p = tuple(min(win, seq) for win, _ in A_GROUPS)
    dils_p = tuple(dil for _, dil in A_GROUPS)
    pin = inproj(x_prompt, mods_p, keeps_p, dils_p, 256, False, True)
    attn_p = [_attn_prompt(pin[g], g) for g in range(N_GROUPS)]
    x1, hpa, hpb, route_rows, route, counts = mix(x_prompt, mods_p, pin, [r[0] for r in attn_p],
                                                  [r[1] for r in attn_p], tm_p, True, B_CHUNK)

    xs = x_sample.reshape(1, n_samp, D_MODEL)
    sin = inproj(xs, mods_s, (n_samp,) * N_GROUPS, (1,) * N_GROUPS, n_samp, True, False, after=counts)
    mod_f = _adaln(c_all, w_ada_final, b_ada_final, after=counts)
    fin_p = split_mods(mod_f, 2, 0, bp, False)
    fin_s = split_mods(mod_f, 2, bp, bp + bs, True)

    xa, xb, pos, tile_expert, n_tiles = _moe_dispatch(hpa, hpb, route_rows, counts)
    kv_s_out = [kv.reshape(depth, bs, -1, 2, HEADS, HEAD_DIM) for kv in sin[7:10]]
    oa, ob, w_gate16, w_up16, w_down16 = _ffn(xa, xb, tile_expert, n_tiles, w_gate[l], w_up[l], w_down[l],
                                              after=(sin[0], fin_s[0], *kv_s_out))

    caches = (cache_kv_w128[l], cache_kv_w512[l], cache_kv_w2048[l])
    o_s = _attn_sample([q.reshape(bs, t_new, 3 * A_WIDTH) for q in sin[:3]], caches, after=oa)
    x1_s, h2_s, comb_s = mix(xs, mods_s, sin, [o_s.reshape(1, n_samp, A_WIDTH)], None, n_samp, False,
                             min(t_new, B_CHUNK))
    y_s = _moe(h2_s, comb_s, w_gate16, w_up16, w_down16, x1_s, mods_s[5], fin_s[0], fin_s[1], row(normf_g),
               tm=n_samp)

    ga_rows = _sc_gather_rows(oa, pos)
    gb_rows = _sc_gather_rows(ob, pos)
    y_p = _final(x1, ga_rows, gb_rows, route, mods_p[5], fin_p[0], fin_p[1], row(normf_g), tm=tm_p)
    kv_p, extra = pin[7:10], sin[10:]

    def kv_out_t(a, b):
        a = a.reshape(b, 2, HEADS, HEAD_DIM, a.shape[-1])
        return jnp.transpose(a, (0, 4, 1, 2, 3)).reshape(depth, b, -1, 2, HEADS, HEAD_DIM)

    return (y_p, y_s.reshape(bs, t_new, D_MODEL),
            kv_out_t(kv_p[0], bp), kv_out_t(kv_p[1], bp), kv_out_t(kv_p[2], bp),
            kv_s_out[0], kv_s_out[1], kv_s_out[2],
            extra[0].reshape(depth, bs, t_new, B_WIDTH))
```

```python
import functools

import numpy as np
import jax
import jax.numpy as jnp
from jax import lax
from jax.experimental import pallas as pl
from jax.experimental.pallas import tpu as pltpu
from jax.experimental.pallas import tpu_sc as plsc

F32 = jnp.float32
BF16 = jnp.bfloat16

D_MODEL = 1024
A_GROUPS = ((128, 1), (512, 4), (2048, 16))
N_GROUPS = 3
HEADS = 8
HEAD_DIM = 64
A_WIDTH = HEADS * HEAD_DIM
A_STEPS = 128
A_BLOCK = 128
B_WIDTH = 1024
B_GROUPS = 8
B_CHUNK = 128
MOE_GROUPS = 4
MOE_PER_GROUP = 4
MOE_EXPERTS = 16
MOE_HIDDEN = 512
EPS = 1e-6
N_QKV = 3 * N_GROUPS * A_WIDTH
IN_COLS = N_QKV + 2 * B_WIDTH + 2 * D_MODEL
COL_CHUNK = 512
LANES = 128
LSE_REP = LANES // HEADS
ROUTE_OFF = MOE_GROUPS
ROUTE_ROWS = 32
NEG = -1e30
LOG2E = 1.4426950408889634
VMEM_LIMIT = 56 * 1024 * 1024
ATTN_BLOCKS_PER_STEP = 16
MIX_SUB_ROWS = 512
MOE_TILE = 512
SC_WINDOW = 128
SC_CORES = 2


def _sigmoid(x):
    return 1.0 / (1.0 + jnp.exp(-x))


def _gelu_tanh(x):
    return x * (0.5 * (1.0 + jnp.tanh(0.7978845608028654 * (x + 0.044715 * (x * x * x)))))


def _rmsnorm_mod(x, gain, shift, scale):
    y = x * lax.rsqrt(jnp.mean(x * x, axis=-1, keepdims=True) + EPS)
    return y * gain * (1.0 + scale) + shift


def _mod_spec(mod, tm, seq):
    if mod.shape[1] == 1:
        return pl.BlockSpec((None, 1, D_MODEL), lambda b, s, *_: (b, 0, 0))
    return pl.BlockSpec((None, tm * mod.shape[1] // seq, D_MODEL), lambda b, s, *_: (b, s, 0))


def _mod_rows(ref, rows):
    m = ref[...]
    if m.shape[0] in (1, rows):
        return m
    rep = rows // m.shape[0]
    return jnp.broadcast_to(m[:, None, :], (m.shape[0], rep, m.shape[1])).reshape(rows, m.shape[1])


def _const_spec(arr):
    nd = arr.ndim
    return pl.BlockSpec(arr.shape, lambda *_: (0,) * nd)


def _adaln_kernel(c_ref, w_ref, b_ref, *rest):
    o_ref = rest[-1]
    c = c_ref[...]
    a = (c * _sigmoid(c)).astype(BF16)
    o_ref[...] = jnp.dot(a, w_ref[...].astype(BF16), preferred_element_type=F32) + b_ref[...]


def _adaln(c, w, b, tn=1024, after=None):
    rows, ncols = c.shape[0], w.shape[1]
    ins = [c, w, b.reshape(1, ncols)]
    in_specs = [pl.BlockSpec((rows, D_MODEL), lambda j: (0, 0)),
                pl.BlockSpec((D_MODEL, tn), lambda j: (0, j)),
                pl.BlockSpec((1, tn), lambda j: (0, j))]
    if after is not None:
        ins.append(after)
        in_specs.append(pl.BlockSpec(memory_space=pl.ANY))
    return pl.pallas_call(
        _adaln_kernel,
        grid=(ncols // tn,),
        in_specs=in_specs,
        out_specs=pl.BlockSpec((rows, tn), lambda j: (0, j)),
        out_shape=jax.ShapeDtypeStruct((rows, ncols), F32),
        compiler_params=pltpu.CompilerParams(dimension_semantics=("arbitrary",)),
        name="adaln",
    )(*ins)


def _inproj_kernel(x_ref, sh_ref, sc_ref, g_ref, w_ref, lng_ref, lnb_ref, *refs, tm, keeps, dils, emit_vn_f32,
                   has_after, kv_feature_major):
    if has_after:
        refs = refs[1:]
    qkv0_ref, qkv1_ref, qkv2_ref, u_ref, vn_ref, ga_ref, gb_ref, kv0_ref, kv1_ref, kv2_ref = refs[:10]
    rest = refs[10:]
    perm_scr, w_bf, stage, sem = rest[-4:]

    def cols(c):
        return slice(c * COL_CHUNK, (c + 1) * COL_CHUNK)

    n_chunks = IN_COLS // COL_CHUNK

    def chunk_copy(c):
        return pltpu.make_async_copy(w_ref.at[:, cols(c)], stage.at[c % 2], sem.at[c % 2])

    @pl.when((pl.program_id(0) == 0) & (pl.program_id(1) == 0))
    def _():
        chunk_copy(0).start()
        for c in range(n_chunks):
            if c + 1 < n_chunks:
                chunk_copy(c + 1).start()
            chunk_copy(c).wait()
            w_bf[:, cols(c)] = stage[c % 2].astype(BF16)

    h = _rmsnorm_mod(x_ref[...], g_ref[...], _mod_rows(sh_ref, tm), _mod_rows(sc_ref, tm)).astype(BF16)

    def proj(c):
        return jnp.dot(h, w_bf[:, cols(c)], preferred_element_type=F32)

    qkv_refs = (qkv0_ref, qkv1_ref, qkv2_ref)

    def put_qkv(g, t, z):
        dil = dils[g]
        if dil == 1:
            qkv_refs[g][0, :, cols(t)] = z.astype(BF16)
            return
        slot = perm_scr.at[t % 2]
        n_slab = COL_CHUNK // LANES
        for j in range(n_slab):
            slot[j] = z[:, j * LANES:(j + 1) * LANES]
        for r in range(dil):
            rows = jnp.concatenate([slot[j, pl.ds(r, tm // dil, stride=dil), :] for j in range(n_slab)], axis=1)
            qkv_refs[g][r, :, cols(t)] = rows.astype(BF16)

    for g in range(N_GROUPS):
        put_qkv(g, 0, proj(g) * (HEAD_DIM ** -0.5 * LOG2E))

    kv_refs = (kv0_ref, kv1_ref, kv2_ref)
    for g in range(N_GROUPS):
        keep = keeps[g]
        for t in (1, 2):
            z = proj(t * N_GROUPS + g)
            put_qkv(g, t, z)
            if kv_feature_major:
                zt = jnp.transpose(z if keep >= tm else z[tm - keep:, :])
                kv_refs[g][(t - 1) * A_WIDTH:t * A_WIDTH, :] = zt
            else:
                kv_refs[g][:, cols(t - 1)] = z if keep >= tm else z[tm - keep:, :]

    base = N_QKV // COL_CHUNK
    for c in range(2):
        u_ref[:, cols(c)] = _gelu_tanh(proj(base + c)).astype(BF16)

    vs = [_gelu_tanh(proj(base + 2 + c)) for c in range(2)]
    mu = (jnp.sum(vs[0], axis=-1, keepdims=True) + jnp.sum(vs[1], axis=-1, keepdims=True)) * (1.0 / B_WIDTH)
    ds = [v - mu for v in vs]
    var = (jnp.sum(ds[0] * ds[0], axis=-1, keepdims=True)
           + jnp.sum(ds[1] * ds[1], axis=-1, keepdims=True)) * (1.0 / B_WIDTH)
    inv = lax.rsqrt(var + EPS)
    for c in range(2):
        vn = ds[c] * inv * lng_ref[:, cols(c)] + lnb_ref[:, cols(c)]
        vn_ref[:, cols(c)] = vn.astype(BF16)
        if emit_vn_f32:
            rest[0][:, cols(c)] = vn

    for c in range(2):
        ga_ref[:, cols(c)] = _sigmoid(proj(base + 4 + c)).astype(BF16)
        gb_ref[:, cols(c)] = _sigmoid(proj(base + 6 + c)).astype(BF16)


def _inproj(x, sh, sc, gain, w_in, ln_g, ln_b, *, tm, keeps, dils, emit_vn_f32, kv_feature_major, after=None):
    bsz, seq, _ = x.shape
    n_s = seq // tm
    tok = lambda width: pl.BlockSpec((None, tm, width), lambda b, s: (b, s, 0))

    def kv_spec(keep):
        first = n_s - max(keep // tm, 1)
        rows = min(keep, tm)
        if kv_feature_major:
            return pl.BlockSpec((None, 2 * A_WIDTH, rows), lambda b, s: (b, 0, jnp.maximum(s - first, 0)))
        return pl.BlockSpec((None, rows, 2 * A_WIDTH), lambda b, s: (b, jnp.maximum(s - first, 0), 0))

    out_specs, out_shape = [], []
    for dil in dils:
        out_specs.append(pl.BlockSpec((None, dil, tm // dil, 3 * A_WIDTH), lambda b, s: (b, 0, s, 0)))
        out_shape.append(jax.ShapeDtypeStruct((bsz, dil, seq // dil, 3 * A_WIDTH), BF16))
    out_specs += [tok(B_WIDTH), tok(B_WIDTH), tok(D_MODEL), tok(D_MODEL)]
    out_shape += [jax.ShapeDtypeStruct((bsz, seq, D_MODEL), BF16)] * 4
    for keep in keeps:
        out_specs.append(kv_spec(keep))
        kv_shape = (bsz, 2 * A_WIDTH, keep) if kv_feature_major else (bsz, keep, 2 * A_WIDTH)
        out_shape.append(jax.ShapeDtypeStruct(kv_shape, F32))
    if emit_vn_f32:
        out_specs.append(tok(B_WIDTH))
        out_shape.append(jax.ShapeDtypeStruct((bsz, seq, B_WIDTH), F32))

    ins = [x, sh, sc, gain, w_in, ln_g, ln_b]
    in_specs = [tok(D_MODEL), _mod_spec(sh, tm, seq), _mod_spec(sc, tm, seq), _const_spec(gain),
                pl.BlockSpec(memory_space=pl.ANY),
                _const_spec(ln_g), _const_spec(ln_b)]
    if after is not None:
        ins.append(after)
        in_specs.append(pl.BlockSpec(memory_space=pl.ANY))
    scratch = [pltpu.VMEM((2, COL_CHUNK // LANES, tm, LANES), F32),
               pltpu.VMEM(w_in.shape, BF16), pltpu.VMEM((2, w_in.shape[0], COL_CHUNK), F32),
               pltpu.SemaphoreType.DMA((2,))]
    return pl.pallas_call(
        functools.partial(_inproj_kernel, tm=tm, keeps=keeps, dils=dils, emit_vn_f32=emit_vn_f32,
                          has_after=after is not None, kv_feature_major=kv_feature_major),
        grid=(bsz, n_s),
        in_specs=in_specs,
        out_specs=out_specs,
        out_shape=out_shape,
        scratch_shapes=scratch,
        compiler_params=pltpu.CompilerParams(dimension_semantics=("arbitrary", "arbitrary"),
                                             vmem_limit_bytes=VMEM_LIMIT),
        name="inproj",
    )(*ins)


def _attn_kernel(q_ref, kp_ref, kc_ref, vp_ref, vc_ref, bias0_ref, bias_ref, o_ref, lse_ref, *, n_seq, q_blocks):
    lane = lax.broadcasted_iota(jnp.int32, (A_BLOCK, LANES), 1)
    low = lane < HEAD_DIM
    zero = jnp.zeros((), BF16)
    for seq_i, i in [(a, b) for a in range(n_seq) for b in range(q_blocks)]:
        rows = (seq_i, slice(i * A_BLOCK, (i + 1) * A_BLOCK))
        q = q_ref[rows[0], rows[1], :]
        key_rows = slice((i - 1) * A_BLOCK, (i + 1) * A_BLOCK)
        if i == 0:
            k = jnp.concatenate([kp_ref[seq_i], kc_ref[seq_i, :A_BLOCK, :]], axis=0)
            v = jnp.concatenate([vp_ref[seq_i], vc_ref[seq_i, :A_BLOCK, :]], axis=0)
        else:
            k, v = kc_ref[seq_i, key_rows, :], vc_ref[seq_i, key_rows, :]
        b_ref = bias0_ref if i == 0 else bias_ref
        lse_tile = jnp.zeros((A_BLOCK, LANES), F32)
        for j in range(HEADS // 2):
            pair = slice(j * LANES, (j + 1) * LANES)
            qp, kpair, vpair = q[:, pair], k[:, pair], v[:, pair]
            outs = []
            for e in range(2):
                h = 2 * j + e
                qm = jnp.where(low if e == 0 else jnp.logical_not(low), qp, zero)
                s = lax.dot_general(qm, kpair, (((1,), (1,)), ((), ())), preferred_element_type=F32)
                s = s + b_ref[h]
                m = jnp.max(s, axis=-1, keepdims=True)
                p = jnp.exp2(s - m)
                den = jnp.sum(p, axis=-1, keepdims=True)
                o = jnp.dot(p.astype(BF16), vpair, preferred_element_type=F32)
                first = h * LSE_REP
                lse_tile = jnp.where((lane >= first) & (lane < first + LSE_REP // 2), m,
                                     jnp.where((lane >= first + LSE_REP // 2) & (lane < first + LSE_REP), den,
                                               lse_tile))
                outs.append(o)
            o_ref[rows[0], rows[1], pair] = jnp.where(low, outs[0], outs[1]).astype(BF16)
        lse_ref[rows[0], rows[1], :] = lse_tile


def _attn_bias(g):
    _, dil = A_GROUPS[g]
    n = N_GROUPS * HEADS
    e = np.arange(1, n + 1, dtype=np.float32)
    slopes = np.exp2(-8.0 * e / n).astype(np.float32).reshape(N_GROUPS, HEADS)[g]
    qi = np.arange(A_BLOCK)[:, None]
    ki = np.arange(2 * A_BLOCK)[None, :]
    delta = qi + A_BLOCK - ki
    band = (delta >= 0) & (delta <= A_STEPS)
    dist = (delta * dil).astype(np.float32)
    bias = (-slopes[:, None, None] * dist[None] * np.float32(LOG2E)).astype(np.float32)
    out = np.empty((2, HEADS, A_BLOCK, 2 * A_BLOCK), np.float32)
    out[1] = np.where(band[None], bias, NEG)
    out[0] = np.where((band & (ki >= A_BLOCK))[None], bias, NEG)
    return jnp.asarray(out)


def _attn_prompt(qkv, g):
    bsz, dil, steps, _ = qkv.shape
    qb = min(ATTN_BLOCKS_PER_STEP, steps // A_BLOCK)
    ns = min(ATTN_BLOCKS_PER_STEP // qb, dil)
    rows = qb * A_BLOCK
    n_steps = steps // rows
    bias = _attn_bias(g)

    def blk(t, prev):
        if prev:
            return pl.BlockSpec((None, ns, A_BLOCK, A_WIDTH),
                                lambda b, r, n: (b, r, jnp.maximum(n * qb - 1, 0), t))
        return pl.BlockSpec((None, ns, rows, A_WIDTH), lambda b, r, n: (b, r, n, t))

    bias_blk = lambda pick: pl.BlockSpec((None, HEADS, A_BLOCK, 2 * A_BLOCK), lambda b, r, n: (pick(n), 0, 0, 0))
    return pl.pallas_call(
        functools.partial(_attn_kernel, n_seq=ns, q_blocks=qb),
        grid=(bsz, dil // ns, n_steps),
        in_specs=[blk(0, False), blk(1, True), blk(1, False), blk(2, True), blk(2, False),
                  bias_blk(lambda n: jnp.minimum(n, 1)), bias_blk(lambda n: 1)],
        out_specs=[pl.BlockSpec((None, ns, rows, A_WIDTH), lambda b, r, n: (b, r, n, 0)),
                   pl.BlockSpec((None, ns, rows, LANES), lambda b, r, n: (b, r, n, 0))],
        out_shape=[jax.ShapeDtypeStruct((bsz, dil, steps, A_WIDTH), BF16),
                   jax.ShapeDtypeStruct((bsz, dil, steps, LANES), F32)],
        compiler_params=pltpu.CompilerParams(dimension_semantics=("arbitrary",) * 3),
        name=f"attn_prompt_g{g}",
    )(qkv, qkv, qkv, qkv, qkv, bias, bias)


def _attn_sample_kernel(q0_ref, q1_ref, q2_ref, c0_ref, c1_ref, c2_ref, bc0_ref, bc1_ref, bc2_ref,
                        bn0_ref, bn1_ref, bn2_ref, after_ref, o_ref, *, t_new):
    del after_ref
    n_rows = HEADS * t_new
    row = lax.broadcasted_iota(jnp.int32, (n_rows, A_WIDTH), 0)
    lane = lax.broadcasted_iota(jnp.int32, (n_rows, A_WIDTH), 1)
    head_mask = (row // t_new) == (lane // HEAD_DIM)
    pad = jnp.zeros((LANES - t_new, A_WIDTH), F32)
    outs, lses = [], []
    for qkv_ref, c_ref, bc_ref, bn_ref in ((q0_ref, c0_ref, bc0_ref, bn0_ref), (q1_ref, c1_ref, bc1_ref, bn1_ref),
                                           (q2_ref, c2_ref, bc2_ref, bn2_ref)):
        q = qkv_ref[:, :A_WIDTH].astype(F32)
        k_new = qkv_ref[:, A_WIDTH:2 * A_WIDTH].astype(F32)
        v_new = qkv_ref[:, 2 * A_WIDTH:].astype(F32)
        k_new = jnp.concatenate([k_new, pad], axis=0).astype(BF16)
        v_new = jnp.concatenate([v_new, pad], axis=0).astype(BF16)
        q_rows = jnp.where(head_mask, jnp.concatenate([q] * HEADS, axis=0), 0.0).astype(BF16)
        k_buf_t = c_ref[:A_WIDTH, :].astype(BF16)
        v_buf_t = c_ref[A_WIDTH:, :].astype(BF16)
        nt = (((1,), (1,)), ((), ()))
        s_buf = jnp.dot(q_rows, k_buf_t, preferred_element_type=F32) + bc_ref[...]
        s_new = lax.dot_general(q_rows, k_new, nt, preferred_element_type=F32) + bn_ref[...]
        m = jnp.maximum(jnp.max(s_buf, axis=-1, keepdims=True), jnp.max(s_new, axis=-1, keepdims=True))
        p_buf = jnp.exp2(s_buf - m)
        p_new = jnp.exp2(s_new - m)
        den = jnp.sum(p_buf, axis=-1, keepdims=True) + jnp.sum(p_new, axis=-1, keepdims=True)
        o = (lax.dot_general(p_buf.astype(BF16), v_buf_t, nt, preferred_element_type=F32)
             + jnp.dot(p_new.astype(BF16), v_new, preferred_element_type=F32)) / den
        outs.append(o)
        lses.append(m + jnp.log2(den))
    top = jnp.maximum(jnp.maximum(lses[0], lses[1]), lses[2])
    ws = [jnp.exp2(l - top) for l in lses]
    tot = ws[0] + ws[1] + ws[2]
    acc = (ws[0] / tot) * outs[0] + (ws[1] / tot) * outs[1] + (ws[2] / tot) * outs[2]
    acc = jnp.where(head_mask, acc, 0.0).reshape(HEADS, t_new, A_WIDTH)
    o_ref[...] = jnp.sum(acc, axis=0).astype(BF16)


def _sample_bias(g, t_new, buf):
    _, dil = A_GROUPS[g]
    n = N_GROUPS * HEADS
    e = np.arange(1, n + 1, dtype=np.float32)
    slopes = np.exp2(-8.0 * e / n).astype(np.float32).reshape(N_GROUPS, HEADS)[g]
    t = np.arange(t_new)[:, None]
    idx = np.concatenate([np.arange(buf), buf + np.arange(LANES)])[None, :]
    dist = buf + t - idx
    valid = (dist >= 0) & (dist % dil == 0) & (dist <= A_STEPS * dil) & (idx < buf + t_new)
    bias = -slopes[:, None, None] * dist.astype(np.float32)[None] * np.float32(LOG2E)
    bias = np.where(valid[None], bias, NEG).astype(np.float32).reshape(HEADS * t_new, buf + LANES)
    return jnp.asarray(bias[:, :buf]), jnp.asarray(bias[:, buf:])


def _attn_sample(qkvs, caches, *, after):
    bsz, t_new, _ = qkvs[0].shape
    cache_v = [jnp.transpose(c, (0, 2, 3, 4, 1)).reshape(bsz, 2 * A_WIDTH, c.shape[1]) for c in caches]
    biases = [_sample_bias(g, t_new, cache_v[g].shape[2]) for g in range(N_GROUPS)]
    bcs = [b[0] for b in biases]
    bns = [b[1] for b in biases]
    return pl.pallas_call(
        functools.partial(_attn_sample_kernel, t_new=t_new),
        grid=(bsz,),
        in_specs=[pl.BlockSpec((None, t_new, 3 * A_WIDTH), lambda b: (b, 0, 0))] * N_GROUPS
                 + [pl.BlockSpec((None, 2 * A_WIDTH, c.shape[2]), lambda b: (b, 0, 0)) for c in cache_v]
                 + [_const_spec(b) for b in bcs] + [_const_spec(b) for b in bns]
                 + [pl.BlockSpec(memory_space=pl.ANY)],
        out_specs=pl.BlockSpec((None, t_new, A_WIDTH), lambda b: (b, 0, 0)),
        out_shape=jax.ShapeDtypeStruct((bsz, t_new, A_WIDTH), BF16),
        compiler_params=pltpu.CompilerParams(dimension_semantics=("arbitrary",),
                                             vmem_limit_bytes=VMEM_LIMIT),
        name="attn_sample",
    )(*qkvs, *cache_v, *bcs, *bns, after)


def _first_max4(v):
    top = jnp.maximum(jnp.maximum(v[0], v[1]), jnp.maximum(v[2], v[3]))
    idx = jnp.where(v[0] == top, 0.0, jnp.where(v[1] == top, 1.0, jnp.where(v[2] == top, 2.0, 3.0)))
    return top, idx


def _route(lt):
    row = lambda k: lt[k:k + 1, :]
    g = [row(k) for k in range(MOE_GROUPS)]
    gmax, g_idx = _first_max4(g)
    g_prob = 1.0 / (jnp.exp(g[0] - gmax) + jnp.exp(g[1] - gmax) + jnp.exp(g[2] - gmax) + jnp.exp(g[3] - gmax))
    cand = []
    for k in range(MOE_PER_GROUP):
        c = row(ROUTE_OFF + MOE_PER_GROUP * (MOE_GROUPS - 1) + k)
        for gi in range(MOE_GROUPS - 2, -1, -1):
            c = jnp.where(g_idx == float(gi), row(ROUTE_OFF + MOE_PER_GROUP * gi + k), c)
        cand.append(c)
    e1, i1 = _first_max4(cand)
    rest = [jnp.where(i1 == float(k), -jnp.inf, cand[k]) for k in range(MOE_PER_GROUP)]
    e2, i2 = _first_max4(rest)
    t = jnp.exp(e2 - e1)
    w1 = 1.0 / (1.0 + t)
    w2 = t / (1.0 + t)
    return MOE_PER_GROUP * g_idx + i1, MOE_PER_GROUP * g_idx + i2, w1 * g_prob, w2 * g_prob


def _pack_bf16_pairs(x):
    q = x.shape[1] // 4
    bits = lax.bitcast_convert_type(x.astype(BF16).astype(F32), jnp.int32)
    pack = lambda hi, lo: hi | lax.shift_right_logical(lo, 16)
    return pack(bits[:, :q], bits[:, 2 * q:3 * q]), pack(bits[:, q:2 * q], bits[:, 3 * q:])


def _unpack_bf16_pairs(pa, pb):
    hi = lambda p: lax.bitcast_convert_type(p & jnp.int32(-65536), F32)
    lo = lambda p: lax.bitcast_convert_type(lax.shift_left(p, 16), F32)
    return jnp.concatenate([hi(pa), hi(pb), lo(pa), lo(pb)], axis=1)


def _mix_kernel(*refs, tm, dils, routed, span):
    n_attn = len(dils)
    x_ref = refs[0]
    if n_attn > 1:
        o_refs = refs[1:1 + n_attn]
        l_refs = refs[1 + n_attn:1 + 2 * n_attn]
        exp_ref = refs[1 + 2 * n_attn]
        pos = 2 + 2 * n_attn
    else:
        o_refs = refs[1:2]
        pos = 2
    (u_ref, vn_ref, ga_ref, gb_ref, sw_ref, sbt_ref, wpa_ref, wpb_ref, wo_ref,
     gt1_ref, sh2_ref, sc2_ref, n2g_ref, wrh_ref, wrl_ref, br_ref) = refs[pos:pos + 16]
    n_out = 6 if routed else 3
    out_refs = refs[pos + 16:pos + 16 + n_out]
    scr = list(refs[pos + 16 + n_out:])
    x1_ref = out_refs[0]
    ob_scr = scr.pop(0)

    def to_positions(ref, scr, dil):
        if dil == 1:
            return ref[0].astype(F32)
        n_slab = scr.shape[0]
        for r in range(dil):
            rows = ref[r].astype(F32)
            for j in range(n_slab):
                scr[j, pl.ds(r, tm // dil, stride=dil), :] = rows[:, j * LANES:(j + 1) * LANES]
        return jnp.concatenate([scr[j] for j in range(n_slab)], axis=1)

    if n_attn > 1:
        o_scr, l_scr = scr[0], scr[1]
        stats = [to_positions(l_refs[g], l_scr.at[g], dils[g]) for g in range(n_attn)]
        lane = lax.broadcasted_iota(jnp.int32, stats[0].shape, 1)
        is_max = (lane & (LSE_REP - 1)) < LSE_REP // 2
        half = LSE_REP // 2
        maxes = [jnp.where(is_max, s, pltpu.roll(s, half, axis=1)) for s in stats]
        dens = [jnp.where(is_max, pltpu.roll(s, LANES - half, axis=1), s) for s in stats]
        lses = [m + jnp.log2(d) for m, d in zip(maxes, dens)]
        top = functools.reduce(jnp.maximum, lses)
        tot = functools.reduce(lambda a, b: a + b, [jnp.exp2(l - top) for l in lses])
        o_a = None
        for g in range(n_attn):
            w = jnp.exp2(maxes[g] - top) / tot
            hi = w.astype(BF16)
            lo = (w - hi.astype(F32)).astype(BF16)
            w_exp = jnp.dot(jnp.concatenate([hi, lo], axis=1), exp_ref[...], preferred_element_type=F32)
            term = w_exp * to_positions(o_refs[g], o_scr.at[g], dils[g])
            o_a = term if o_a is None else o_a + term
        o_a = o_a.astype(BF16)
    else:
        o_a = o_refs[0][...]

    r_i = lax.broadcasted_iota(jnp.int32, (span, span), 0)
    c_i = lax.broadcasted_iota(jnp.int32, (span, span), 1)
    tril = r_i >= c_i
    for gi in range(B_GROUPS):
        gcols = slice(gi * LANES, (gi + 1) * LANES)
        b_s = sbt_ref[:span, gi:gi + 1]
        if span == B_CHUNK:
            w_s = jnp.where(tril, sw_ref[gi], 0.0).astype(BF16)
            n_chunk = tm // span
            vn_wide = jnp.concatenate([vn_ref[c * span:(c + 1) * span, gcols] for c in range(n_chunk)], axis=1)
            mixed = jnp.dot(w_s, vn_wide, preferred_element_type=F32) + b_s
            for c in range(n_chunk):
                rows = slice(c * span, (c + 1) * span)
                ob_scr[rows, gcols] = (u_ref[rows, gcols].astype(F32)
                                       * mixed[:, c * LANES:(c + 1) * LANES]).astype(BF16)
        else:
            w_s = jnp.where(tril, sw_ref[gi, :span, :span], 0.0)
            vn3 = vn_ref[:, gcols].astype(F32).reshape(tm // span, span, LANES)
            mixed = jnp.broadcast_to(b_s, (span, LANES))[None]
            for s in range(span):
                w_col = jnp.broadcast_to(w_s[:, s:s + 1], (span, LANES))[None]
                mixed = mixed + w_col * jnp.broadcast_to(vn3[:, s:s + 1, :], vn3.shape)
            u3 = u_ref[:, gcols].astype(F32).reshape(tm // span, span, LANES)
            ob_scr[:, gcols] = (u3 * mixed).reshape(tm, LANES).astype(BF16)

    sub = min(tm, MIX_SUB_ROWS)
    if routed:
        carry = scr[-1]
        first_step = (pl.program_id(0) == 0) & (pl.program_id(1) == 0)
        seen = jnp.where(first_step, 0.0, carry[:, 0:1])
        r_i = lax.broadcasted_iota(jnp.int32, (sub, sub), 0)
        c_i = lax.broadcasted_iota(jnp.int32, (sub, sub), 1)
        earlier = jnp.where(r_i < c_i, 1.0, 0.0).astype(BF16)
        e_iota = lax.broadcasted_iota(jnp.int32, (MOE_EXPERTS, sub), 0).astype(F32)
    nt = (((1,), (1,)), ((), ()))
    for r0 in range(0, tm, sub):
        rs = slice(r0, r0 + sub)
        mod = lambda ref: ref[...] if ref.shape[0] == 1 else _mod_rows(ref, tm)[rs, :]
        pa = jnp.dot(o_a[rs], wpa_ref[...], preferred_element_type=F32)
        pb = jnp.dot(ob_scr[rs, :], wpb_ref[...], preferred_element_type=F32)
        mix = (ga_ref[rs, :].astype(F32) * pa + gb_ref[rs, :].astype(F32) * pb).astype(BF16)
        x1 = x_ref[rs, :] + mod(gt1_ref) * jnp.dot(mix, wo_ref[...], preferred_element_type=F32)
        x1_ref[rs, :] = x1
        h2 = _rmsnorm_mod(x1, n2g_ref[...], mod(sh2_ref), mod(sc2_ref))
        h_hi = h2.astype(BF16)
        h_lo = (h2 - h_hi.astype(F32)).astype(BF16)
        lt = (lax.dot_general(wrh_ref[...], h_hi, nt, preferred_element_type=F32)
              + lax.dot_general(wrh_ref[...], h_lo, nt, preferred_element_type=F32)
              + lax.dot_general(wrl_ref[...], h_hi, nt, preferred_element_type=F32)) + br_ref[...]
        ex1, ex2, w1, w2 = _route(lt)
        if not routed:
            _, h2_ref, comb_ref = out_refs
            h2_ref[rs, :] = h_hi
            e_lane = lax.broadcasted_iota(jnp.int32, (LANES, sub), 0).astype(F32) - float(ROUTE_OFF)
            comb_t = jnp.where(e_lane == ex1, w1, jnp.where(e_lane == ex2, w2, 0.0))
            comb_ref[rs, :] = jnp.transpose(comb_t)
            continue

        _, hpa_ref, hpb_ref, route_ref, wcol_ref, cnt_ref = out_refs
        hpa_ref[rs, :], hpb_ref[rs, :] = _pack_bf16_pairs(h2)
        hot1, hot2 = e_iota == ex1, e_iota == ex2
        onehot = jnp.where(hot1 | hot2, 1.0, 0.0)
        rank = jnp.dot(onehot.astype(BF16), earlier, preferred_element_type=F32) + seen
        r1 = jnp.sum(jnp.where(hot1, rank, 0.0), axis=0, keepdims=True)
        r2 = jnp.sum(jnp.where(hot2, rank, 0.0), axis=0, keepdims=True)
        seen = seen + jnp.sum(onehot, axis=1, keepdims=True)
        fields = jnp.concatenate([ex1, ex2, w1, w2, r1, r2, jnp.zeros((2, sub), F32)], axis=0)
        route_ref[:, rs] = fields
        wcol_ref[rs, :] = jnp.transpose(jnp.concatenate([fields, jnp.zeros((LANES - 8, sub), F32)], axis=0))
    if routed:
        carry[...] = jnp.broadcast_to(seen, carry.shape)
        cnt_ref[...] = jnp.broadcast_to(seen, cnt_ref.shape)


def _expand_matrix():
    e = np.zeros((LANES, A_WIDTH), np.float32)
    for h in range(HEADS):
        e[h * LSE_REP, h * HEAD_DIM:(h + 1) * HEAD_DIM] = 1.0
    return jnp.asarray(np.concatenate([e, e], axis=0), dtype=BF16)


def _mix(x, attn_outs, attn_lses, u, vn, ga, gb, sgu_w, sgu_bt, w_pa, w_pb, w_o,
         gt1, sh2, sc2, n2g, w_r_hi, w_r_lo, b_r, *, tm, routed, span):
    bsz, seq, _ = x.shape
    n_attn = len(attn_outs)
    tok = lambda width: pl.BlockSpec((None, tm, width), lambda b, s: (b, s, 0))
    ins = [x] + list(attn_outs)
    scratch = [pltpu.VMEM((tm, B_WIDTH), BF16)]
    if n_attn > 1:
        dils = tuple(o.shape[1] for o in attn_outs)
        res = lambda dil, width: pl.BlockSpec((None, dil, tm // dil, width), lambda b, s: (b, 0, s, 0))
        e2 = _expand_matrix()
        ins += list(attn_lses) + [e2]
        specs = ([tok(D_MODEL)] + [res(d, A_WIDTH) for d in dils] + [res(d, LANES) for d in dils]
                 + [_const_spec(e2)])
        scratch += [pltpu.VMEM((n_attn, A_WIDTH // LANES, tm, LANES), F32), pltpu.VMEM((n_attn, 1, tm, LANES), F32)]
    else:
        dils = (1,)
        specs = [tok(D_MODEL), tok(A_WIDTH)]
    ins += [u, vn, ga, gb, sgu_w, sgu_bt, w_pa, w_pb, w_o, gt1, sh2, sc2, n2g, w_r_hi, w_r_lo, b_r]
    specs += [tok(B_WIDTH), tok(B_WIDTH), tok(D_MODEL), tok(D_MODEL),
              _const_spec(sgu_w), _const_spec(sgu_bt), _const_spec(w_pa), _const_spec(w_pb), _const_spec(w_o),
              _mod_spec(gt1, tm, seq), _mod_spec(sh2, tm, seq), _mod_spec(sc2, tm, seq), _const_spec(n2g),
              _const_spec(w_r_hi), _const_spec(w_r_lo), _const_spec(b_r)]
    if routed:
        quarter = D_MODEL // 4
        out_specs = [tok(D_MODEL), tok(quarter), tok(quarter),
                     pl.BlockSpec((None, 8, tm), lambda b, s: (b, 0, s)), tok(LANES),
                     pl.BlockSpec((MOE_EXPERTS, LANES), lambda b, s: (0, 0))]
        out_shape = [jax.ShapeDtypeStruct((bsz, seq, D_MODEL), F32),
                     jax.ShapeDtypeStruct((bsz, seq, quarter), jnp.int32),
                     jax.ShapeDtypeStruct((bsz, seq, quarter), jnp.int32),
                     jax.ShapeDtypeStruct((bsz, 8, seq), F32),
                     jax.ShapeDtypeStruct((bsz, seq, LANES), F32),
                     jax.ShapeDtypeStruct((MOE_EXPERTS, LANES), F32)]
        scratch.append(pltpu.VMEM((MOE_EXPERTS, LANES), F32))
    else:
        out_specs = [tok(D_MODEL), tok(D_MODEL), tok(LANES)]
        out_shape = [jax.ShapeDtypeStruct((bsz, seq, D_MODEL), F32),
                     jax.ShapeDtypeStruct((bsz, seq, D_MODEL), BF16),
                     jax.ShapeDtypeStruct((bsz, seq, LANES), F32)]
    return pl.pallas_call(
        functools.partial(_mix_kernel, tm=tm, dils=dils, routed=routed, span=span),
        grid=(bsz, seq // tm),
        in_specs=specs,
        out_specs=out_specs,
        out_shape=out_shape,
        scratch_shapes=scratch,
        compiler_params=pltpu.CompilerParams(dimension_semantics=("arbitrary", "arbitrary"),
                                             vmem_limit_bytes=VMEM_LIMIT),
        name="mix",
    )(*ins)


def _moe_kernel(h_ref, comb_ref, wg_ref, wu_ref, wd_ref, x1_ref, gt2_ref, shf_ref, scf_ref, nfg_ref,
                y_ref, acc_ref):
    e = pl.program_id(2)

    @pl.when(e == 0)
    def _():
        acc_ref[...] = jnp.zeros_like(acc_ref)

    h = h_ref[...]
    hg = jnp.dot(h, wg_ref[...], preferred_element_type=F32)
    hu = jnp.dot(h, wu_ref[...], preferred_element_type=F32)
    comb = comb_ref[...]
    lane = lax.broadcasted_iota(jnp.int32, comb.shape, 1)
    cw = jnp.sum(jnp.where(lane == e + ROUTE_OFF, comb, 0.0), axis=-1, keepdims=True)
    act = (hg * _sigmoid(hg) * hu) * cw
    acc_ref[...] += jnp.dot(act.astype(BF16), wd_ref[...], preferred_element_type=F32)

    @pl.when(e == MOE_EXPERTS - 1)
    def _():
        rows = acc_ref.shape[0]
        x2 = x1_ref[...] + _mod_rows(gt2_ref, rows) * acc_ref[...]
        y_ref[...] = _rmsnorm_mod(x2, nfg_ref[...], _mod_rows(shf_ref, rows), _mod_rows(scf_ref, rows))


def _moe(h2, comb, w_gate, w_up, w_down, x1, gt2, shf, scf, nfg, *, tm):
    bsz, seq, _ = h2.shape
    tok = lambda width: pl.BlockSpec((None, tm, width), lambda b, s, e: (b, s, 0))
    return pl.pallas_call(
        _moe_kernel,
        grid=(bsz, seq // tm, MOE_EXPERTS),
        in_specs=[tok(D_MODEL), tok(LANES),
                  pl.BlockSpec((None, D_MODEL, MOE_HIDDEN), lambda b, s, e: (e, 0, 0)),
                  pl.BlockSpec((None, D_MODEL, MOE_HIDDEN), lambda b, s, e: (e, 0, 0)),
                  pl.BlockSpec((None, MOE_HIDDEN, D_MODEL), lambda b, s, e: (e, 0, 0)),
                  tok(D_MODEL), _mod_spec(gt2, tm, seq), _mod_spec(shf, tm, seq), _mod_spec(scf, tm, seq),
                  _const_spec(nfg)],
        out_specs=tok(D_MODEL),
        out_shape=jax.ShapeDtypeStruct((bsz, seq, D_MODEL), F32),
        scratch_shapes=[pltpu.VMEM((tm, D_MODEL), F32)],
        compiler_params=pltpu.CompilerParams(dimension_semantics=("arbitrary",) * 3,
                                             vmem_limit_bytes=VMEM_LIMIT),
        name="moe",
    )(h2, comb, w_gate, w_up, w_down, x1, gt2, shf, scf, nfg)


def _sc_mesh():
    return plsc.VectorSubcoreMesh(core_axis_name="c", subcore_axis_name="s")


def _sc_scatter_rows(rows, pos1, pos2, n_out):
    n, width = rows.shape
    steps = n // SC_WINDOW // SC_CORES

    @pl.kernel(out_type=jax.ShapeDtypeStruct((n_out, width), rows.dtype), mesh=_sc_mesh(), scratch_types=[])
    def scatter(rows_hbm, p1_hbm, p2_hbm, out_hbm):
        def body(x_vmem, i1_vmem, i2_vmem):
            pltpu.sync_copy(x_vmem, out_hbm.at[i1_vmem.at[0]])
            pltpu.sync_copy(x_vmem, out_hbm.at[i2_vmem.at[0]])

        pltpu.emit_pipeline(
            body, grid=(SC_CORES, steps),
            in_specs=[pl.BlockSpec((SC_WINDOW, width), lambda c, i: (c * steps + i, 0)),
                      pl.BlockSpec((1, SC_WINDOW), lambda c, i: (0, c * steps + i)),
                      pl.BlockSpec((1, SC_WINDOW), lambda c, i: (0, c * steps + i))],
            out_specs=[],
            core_axis_name=("c", "s"),
            dimension_semantics=(pltpu.PARALLEL, pltpu.PARALLEL),
        )(rows_hbm, p1_hbm, p2_hbm)

    return scatter(rows, pos1, pos2)


def _sc_gather_rows(table, idx):
    m, width = idx.shape[1], table.shape[1]
    steps = m // SC_WINDOW // SC_CORES

    @pl.kernel(out_type=jax.ShapeDtypeStruct((m, width), table.dtype), mesh=_sc_mesh(), scratch_types=[])
    def gather(table_hbm, idx_hbm, out_hbm):
        def body(i_vmem, o_vmem):
            pltpu.sync_copy(table_hbm.at[i_vmem.at[0]], o_vmem)

        pltpu.emit_pipeline(
            body, grid=(SC_CORES, steps),
            in_specs=[pl.BlockSpec((1, SC_WINDOW), lambda c, i: (0, c * steps + i))],
            out_specs=[pl.BlockSpec((SC_WINDOW, width), lambda c, i: (c * steps + i, 0))],
            core_axis_name=("c", "s"),
            dimension_semantics=(pltpu.PARALLEL, pltpu.PARALLEL),
        )(idx_hbm, out_hbm)

    return gather(table, idx)


def _ffn_kernel(te_ref, nt_ref, xa_ref, xb_ref, wg_ref, wu_ref, wd_ref, *rest):
    oa_ref, ob_ref, wg_s, wu_s, wd_s = rest[-5:]
    t = pl.program_id(0)
    live = t < nt_ref[0]

    @pl.when(live & ((t == 0) | (te_ref[t] != te_ref[jnp.maximum(t - 1, 0)])))
    def _():
        wg_s[...] = wg_ref[...].astype(BF16)
        wu_s[...] = wu_ref[...].astype(BF16)
        wd_s[...] = wd_ref[...].astype(BF16)

    @pl.when(live)
    def _():
        x = _unpack_bf16_pairs(xa_ref[...], xb_ref[...]).astype(BF16)
        hg = jnp.dot(x, wg_s[...], preferred_element_type=F32)
        hu = jnp.dot(x, wu_s[...], preferred_element_type=F32)
        act = (hg * _sigmoid(hg) * hu).astype(BF16)
        oa_ref[...], ob_ref[...] = _pack_bf16_pairs(jnp.dot(act, wd_s[...], preferred_element_type=F32))


def _ffn(xa, xb, tile_expert, n_tiles, w_gate, w_up, w_down, *, after):
    n_rows, quarter = xa.shape
    rows = pl.BlockSpec((MOE_TILE, quarter), lambda t, te, nt: (jnp.minimum(t, nt[0] - 1), 0))
    w_in = pl.BlockSpec((None, D_MODEL, MOE_HIDDEN), lambda t, te, nt: (te[t], 0, 0))
    w_out = pl.BlockSpec((None, MOE_HIDDEN, D_MODEL), lambda t, te, nt: (te[t], 0, 0))
    return pl.pallas_call(
        _ffn_kernel,
        grid_spec=pltpu.PrefetchScalarGridSpec(
            num_scalar_prefetch=2,
            grid=(n_rows // MOE_TILE,),
            in_specs=[rows, rows, w_in, w_in, w_out] + [pl.BlockSpec(memory_space=pl.ANY)] * len(after),
            out_specs=[rows, rows, w_in, w_in, w_out]),
        out_shape=[jax.ShapeDtypeStruct((n_rows, quarter), jnp.int32)] * 2
                  + [jax.ShapeDtypeStruct(w_gate.shape, BF16), jax.ShapeDtypeStruct(w_up.shape, BF16),
                     jax.ShapeDtypeStruct(w_down.shape, BF16)],
        compiler_params=pltpu.CompilerParams(dimension_semantics=("arbitrary",), vmem_limit_bytes=VMEM_LIMIT),
        name="moe_ffn",
    )(tile_expert, n_tiles, xa, xb, w_gate, w_up, w_down, *after)


def _final_kernel(x1_ref, g1a_ref, g1b_ref, g2a_ref, g2b_ref, route_ref, gt2_ref, shf_ref, scf_ref, nfg_ref, y_ref):
    o1 = _unpack_bf16_pairs(g1a_ref[...], g1b_ref[...])
    o2 = _unpack_bf16_pairs(g2a_ref[...], g2b_ref[...])
    route = route_ref[...]
    moe = route[:, 2:3] * o1 + route[:, 3:4] * o2
    x2 = x1_ref[...] + gt2_ref[...] * moe
    y_ref[...] = _rmsnorm_mod(x2, nfg_ref[...], shf_ref[...], scf_ref[...])


def _final(x1, ga, gb, route, gt2, shf, scf, nfg, *, tm):
    bsz, seq, _ = x1.shape
    n_s = seq // tm
    n_blk = bsz * n_s
    quarter = ga.shape[1]
    tok = lambda width: pl.BlockSpec((None, tm, width), lambda b, s: (b, s, 0))
    first = pl.BlockSpec((tm, quarter), lambda b, s: (b * n_s + s, 0))
    second = pl.BlockSpec((tm, quarter), lambda b, s: (n_blk + b * n_s + s, 0))
    return pl.pallas_call(
        _final_kernel,
        grid=(bsz, n_s),
        in_specs=[tok(D_MODEL), first, first, second, second, tok(LANES),
                  _mod_spec(gt2, tm, seq), _mod_spec(shf, tm, seq), _mod_spec(scf, tm, seq), _const_spec(nfg)],
        out_specs=tok(D_MODEL),
        out_shape=jax.ShapeDtypeStruct((bsz, seq, D_MODEL), F32),
        compiler_params=pltpu.CompilerParams(dimension_semantics=("arbitrary", "arbitrary")),
        name="moe_final",
    )(x1, ga, gb, ga, gb, route, gt2, shf, scf, nfg)


def _positions_kernel(base_ref, route_ref, p1_ref, p2_ref):
    rows = route_ref[...]
    for e_row, r_row, out_ref in ((0, 4, p1_ref), (1, 5, p2_ref)):
        expert = rows[e_row:e_row + 1, :]
        start = jnp.zeros(expert.shape, jnp.int32)
        for e in range(MOE_EXPERTS):
            start = jnp.where(expert == float(e), base_ref[e], start)
        out_ref[...] = start + rows[r_row:r_row + 1, :].astype(jnp.int32)


def _positions(route_rows, base):
    bsz, _, seq = route_rows.shape
    out = pl.BlockSpec((None, 1, seq), lambda b, base: (b, 0, 0))
    return pl.pallas_call(
        _positions_kernel,
        grid_spec=pltpu.PrefetchScalarGridSpec(
            num_scalar_prefetch=1, grid=(bsz,),
            in_specs=[pl.BlockSpec((None, 8, seq), lambda b, base: (b, 0, 0))],
            out_specs=[out, out]),
        out_shape=[jax.ShapeDtypeStruct((bsz, 1, seq), jnp.int32)] * 2,
        compiler_params=pltpu.CompilerParams(dimension_semantics=("arbitrary",)),
        name="moe_positions",
    )(base, route_rows)


def _moe_dispatch(hpa, hpb, route_rows, counts):
    bsz, seq, quarter = hpa.shape
    n_tok = bsz * seq
    t_max = 2 * n_tok // MOE_TILE + MOE_EXPERTS
    cnt = counts[:, 0].astype(jnp.int32)
    tiles_e = jnp.maximum((cnt + MOE_TILE - 1) // MOE_TILE, 1)
    tile_end = jnp.cumsum(tiles_e)
    n_tiles = tile_end[-1:]
    base = (tile_end - tiles_e) * MOE_TILE
    t_ids = jnp.minimum(jnp.arange(t_max, dtype=jnp.int32), n_tiles[0] - 1)
    tile_expert = jnp.sum((t_ids[:, None] >= tile_end[None, :]).astype(jnp.int32), axis=1)
    pos1, pos2 = (p.reshape(1, n_tok) for p in _positions(route_rows, base))
    n_rows = t_max * MOE_TILE
    xa = _sc_scatter_rows(hpa.reshape(n_tok, quarter), pos1, pos2, n_rows)
    xb = _sc_scatter_rows(hpb.reshape(n_tok, quarter), pos1, pos2, n_rows)
    return xa, xb, jnp.concatenate([pos1, pos2], axis=1), tile_expert, n_tiles


def kernel(x_prompt, x_sample, cache_kv_w128, cache_kv_w512, cache_kv_w2048, c_prompt, c_sample,
           w_ada, b_ada, norm1_g, norm2_g, w_in, sgu_ln_g, sgu_ln_b, sgu_w, sgu_b, w_pa, w_pb, w_o,
           w_route_group, b_route_group, w_route_expert, b_route_expert, w_gate, w_up, w_down,
           normf_g, w_ada_final, b_ada_final):
    depth = w_ada.shape[0]
    assert depth == 1
    l = 0
    bp, seq, _ = x_prompt.shape
    bs, t_new, _ = x_sample.shape
    n_samp = bs * t_new

    c_all = jnp.concatenate([c_prompt, c_sample], axis=0)
    pad_rows = (-c_all.shape[0]) % 8
    c_all = jnp.pad(c_all, ((0, pad_rows), (0, 0)))
    mod = _adaln(c_all, w_ada[l], b_ada[l])

    def split_mods(m, n, lo, hi, per_token):
        parts = jnp.split(m[lo:hi], n, axis=-1)
        if per_token:
            return [p.reshape(1, hi - lo, D_MODEL) for p in parts]
        return [p.reshape(hi - lo, 1, D_MODEL) for p in parts]

    mods_p = split_mods(mod, 6, 0, bp, False)
    mods_s = split_mods(mod, 6, bp, bp + bs, True)

    row = lambda v: v.reshape(1, -1)
    w_pa_b, w_pb_b, w_o_b = w_pa[l].astype(BF16), w_pb[l].astype(BF16), w_o[l].astype(BF16)
    w_re = jnp.transpose(w_route_expert[l], (0, 2, 1)).reshape(MOE_EXPERTS, D_MODEL)
    w_r = jnp.pad(jnp.concatenate([jnp.transpose(w_route_group[l]), w_re], axis=0),
                  ((0, ROUTE_ROWS - MOE_GROUPS - MOE_EXPERTS), (0, 0)))
    w_r_hi = w_r.astype(BF16)
    w_r_lo = (w_r - w_r_hi.astype(F32)).astype(BF16)
    b_r = jnp.pad(jnp.concatenate([b_route_group[l], b_route_expert[l].reshape(-1)]),
                  (0, ROUTE_ROWS - MOE_GROUPS - MOE_EXPERTS)).reshape(ROUTE_ROWS, 1)

    def inproj(x, mods, keeps, dils, tm, emit_vn_f32, kv_feature_major, after=None):
        return _inproj(x, mods[0], mods[1], row(norm1_g[l]), w_in[l], row(sgu_ln_g[l]), row(sgu_ln_b[l]),
                       tm=tm, keeps=keeps, dils=dils, emit_vn_f32=emit_vn_f32,
                       kv_feature_major=kv_feature_major, after=after)

    sgu_bt = jnp.transpose(sgu_b[l])

    def mix(x, mods, pin, attn_outs, attn_lses, tm, routed, span):
        u, vn, ga, gb = pin[3:7]
        return _mix(x, attn_outs, attn_lses, u, vn, ga, gb, sgu_w[l], sgu_bt,
                    w_pa_b, w_pb_b, w_o_b, mods[2], mods[3], mods[4], row(norm2_g[l]), w_r_hi, w_r_lo, b_r,
                    tm=tm, routed=routed, span=span)

    tm_p = 512
    keeps_p = tuple(min(win, seq) for win, _ in A_GROUPS)
    dils_p = tuple(dil for _, dil in A_GROUPS)
    pin = inproj(x_prompt, mods_p, keeps_p, dils_p, 256, False, True)
    attn_p = [_attn_prompt(pin[g], g) for g in range(N_GROUPS)]
    x1, hpa, hpb, route_rows, route, counts = mix(x_prompt, mods_p, pin, [r[0] for r in attn_p],
                                                  [r[1] for r in attn_p], tm_p, True, B_CHUNK)

    xs = x_sample.reshape(1, n_samp, D_MODEL)
    sin = inproj(xs, mods_s, (n_samp,) * N_GROUPS, (1,) * N_GROUPS, n_samp, True, False, after=counts)
    mod_f = _adaln(c_all, w_ada_final, b_ada_final, after=counts)
    fin_p = split_mods(mod_f, 2, 0, bp, False)
    fin_s = split_mods(mod_f, 2, bp, bp + bs, True)

    xa, xb, pos, tile_expert, n_tiles = _moe_dispatch(hpa, hpb, route_rows, counts)
    kv_s_out = [kv.reshape(depth, bs, -1, 2, HEADS, HEAD_DIM) for kv in sin[7:10]]
    oa, ob, w_gate16, w_up16, w_down16 = _ffn(xa, xb, tile_expert, n_tiles, w_gate[l], w_up[l], w_down[l],
                                              after=(sin[0], fin_s[0], *kv_s_out))

    caches = (cache_kv_w128[l], cache_kv_w512[l], cache_kv_w2048[l])
    o_s = _attn_sample([q.reshape(bs, t_new, 3 * A_WIDTH) for q in sin[:3]], caches, after=oa)
    x1_s, h2_s, comb_s = mix(xs, mods_s, sin, [o_s.reshape(1, n_samp, A_WIDTH)], None, n_samp, False,
                             min(t_new, B_CHUNK))
    y_s = _moe(h2_s, comb_s, w_gate16, w_up16, w_down16, x1_s, mods_s[5], fin_s[0], fin_s[1], row(normf_g),
               tm=n_samp)

    ga_rows = _sc_gather_rows(oa, pos)
    gb_rows = _sc_gather_rows(ob, pos)
    y_p = _final(x1, ga_rows, gb_rows, route, mods_p[5], fin_p[0], fin_p[1], row(normf_g), tm=tm_p)
    kv_p, extra = pin[7:10], sin[10:]

    def kv_out_t(a, b):
        a = a.reshape(b, 2, HEADS, HEAD_DIM, a.shape[-1])
        return jnp.transpose(a, (0, 4, 1, 2, 3)).reshape(depth, b, -1, 2, HEADS, HEAD_DIM)

    return (y_p, y_s.reshape(bs, t_new, D_MODEL),
            kv_out_t(kv_p[0], bp), kv_out_t(kv_p[1], bp), kv_out_t(kv_p[2], bp),
            kv_s_out[0], kv_s_out[1], kv_s_out[2],
            extra[0].reshape(depth, bs, t_new, B_WIDTH))
```

```python
import functools

import numpy as np
import jax
import jax.numpy as jnp
from jax import lax
from jax.experimental import pallas as pl
from jax.experimental.pallas import tpu as pltpu
from jax.experimental.pallas import tpu_sc as plsc

F32 = jnp.float32
BF16 = jnp.bfloat16

D_MODEL = 1024
A_GROUPS = ((128, 1), (512, 4), (2048, 16))
N_GROUPS = 3
HEADS = 8
HEAD_DIM = 64
A_WIDTH = HEADS * HEAD_DIM
A_STEPS = 128
A_BLOCK = 128
B_WIDTH = 1024
B_GROUPS = 8
B_CHUNK = 128
MOE_GROUPS = 4
MOE_PER_GROUP = 4
MOE_EXPERTS = 16
MOE_HIDDEN = 512
EPS = 1e-6
N_QKV = 3 * N_GROUPS * A_WIDTH
IN_COLS = N_QKV + 2 * B_WIDTH + 2 * D_MODEL
COL_CHUNK = 512
LANES = 128
LSE_REP = LANES // HEADS
ROUTE_OFF = MOE_GROUPS
ROUTE_ROWS = 32
NEG = -1e30
LOG2E = 1.4426950408889634
VMEM_LIMIT = 56 * 1024 * 1024
ATTN_BLOCKS_PER_STEP = 16
MIX_SUB_ROWS = 512
MOE_TILE = 512
SC_WINDOW = 128
SC_CORES = 2


def _sigmoid(x):
    return 1.0 / (1.0 + jnp.exp(-x))


def _gelu_tanh(x):
    return x * (0.5 * (1.0 + jnp.tanh(0.7978845608028654 * (x + 0.044715 * (x * x * x)))))


def _rmsnorm_mod(x, gain, shift, scale):
    y = x * lax.rsqrt(jnp.mean(x * x, axis=-1, keepdims=True) + EPS)
    return y * gain * (1.0 + scale) + shift


def _mod_spec(mod, tm, seq):
    if mod.shape[1] == 1:
        return pl.BlockSpec((None, 1, D_MODEL), lambda b, s, *_: (b, 0, 0))
    return pl.BlockSpec((None, tm * mod.shape[1] // seq, D_MODEL), lambda b, s, *_: (b, s, 0))


def _mod_rows(ref, rows):
    m = ref[...]
    if m.shape[0] in (1, rows):
        return m
    rep = rows // m.shape[0]
    return jnp.broadcast_to(m[:, None, :], (m.shape[0], rep, m.shape[1])).reshape(rows, m.shape[1])


def _const_spec(arr):
    nd = arr.ndim
    return pl.BlockSpec(arr.shape, lambda *_: (0,) * nd)


def _adaln_kernel(c_ref, w_ref, b_ref, *rest):
    o_ref = rest[-1]
    c = c_ref[...]
    a = (c * _sigmoid(c)).astype(BF16)
    o_ref[...] = jnp.dot(a, w_ref[...].astype(BF16), preferred_element_type=F32) + b_ref[...]


def _adaln(c, w, b, tn=1024, after=None):
    rows, ncols = c.shape[0], w.shape[1]
    ins = [c, w, b.reshape(1, ncols)]
    in_specs = [pl.BlockSpec((rows, D_MODEL), lambda j: (0, 0)),
                pl.BlockSpec((D_MODEL, tn), lambda j: (0, j)),
                pl.BlockSpec((1, tn), lambda j: (0, j))]
    if after is not None:
        ins.append(after)
        in_specs.append(pl.BlockSpec(memory_space=pl.ANY))
    return pl.pallas_call(
        _adaln_kernel,
        grid=(ncols // tn,),
        in_specs=in_specs,
        out_specs=pl.BlockSpec((rows, tn), lambda j: (0, j)),
        out_shape=jax.ShapeDtypeStruct((rows, ncols), F32),
        compiler_params=pltpu.CompilerParams(dimension_semantics=("arbitrary",)),
        name="adaln",
    )(*ins)


def _inproj_kernel(x_ref, sh_ref, sc_ref, g_ref, w_ref, lng_ref, lnb_ref, *refs, tm, keeps, dils, emit_vn_f32,
                   has_after, kv_feature_major):
    if has_after:
        refs = refs[1:]
    qkv0_ref, qkv1_ref, qkv2_ref, u_ref, vn_ref, ga_ref, gb_ref, kv0_ref, kv1_ref, kv2_ref = refs[:10]
    rest = refs[10:]
    perm_scr = rest[-1]
    h = _rmsnorm_mod(x_ref[...], g_ref[...], _mod_rows(sh_ref, tm), _mod_rows(sc_ref, tm)).astype(BF16)

    def proj(c):
        return jnp.dot(h, w_ref[:, c * COL_CHUNK:(c + 1) * COL_CHUNK], preferred_element_type=F32)

    def cols(c):
        return slice(c * COL_CHUNK, (c + 1) * COL_CHUNK)

    qkv_refs = (qkv0_ref, qkv1_ref, qkv2_ref)

    def put_qkv(g, t, z):
        dil = dils[g]
        if dil == 1:
            qkv_refs[g][0, :, cols(t)] = z.astype(BF16)
            return
        slot = perm_scr.at[t % 2]
        n_slab = COL_CHUNK // LANES
        for j in range(n_slab):
            slot[j] = z[:, j * LANES:(j + 1) * LANES]
        for r in range(dil):
            rows = jnp.concatenate([slot[j, pl.ds(r, tm // dil, stride=dil), :] for j in range(n_slab)], axis=1)
            qkv_refs[g][r, :, cols(t)] = rows.astype(BF16)

    for g in range(N_GROUPS):
        put_qkv(g, 0, proj(g) * (HEAD_DIM ** -0.5 * LOG2E))

    kv_refs = (kv0_ref, kv1_ref, kv2_ref)
    for g in range(N_GROUPS):
        keep = keeps[g]
        for t in (1, 2):
            z = proj(t * N_GROUPS + g)
            put_qkv(g, t, z)
            if kv_feature_major:
                zt = jnp.transpose(z if keep >= tm else z[tm - keep:, :])
                kv_refs[g][(t - 1) * A_WIDTH:t * A_WIDTH, :] = zt
            else:
                kv_refs[g][:, cols(t - 1)] = z if keep >= tm else z[tm - keep:, :]

    base = N_QKV // COL_CHUNK
    for c in range(2):
        u_ref[:, cols(c)] = _gelu_tanh(proj(base + c)).astype(BF16)

    vs = [_gelu_tanh(proj(base + 2 + c)) for c in range(2)]
    mu = (jnp.sum(vs[0], axis=-1, keepdims=True) + jnp.sum(vs[1], axis=-1, keepdims=True)) * (1.0 / B_WIDTH)
    ds = [v - mu for v in vs]
    var = (jnp.sum(ds[0] * ds[0], axis=-1, keepdims=True)
           + jnp.sum(ds[1] * ds[1], axis=-1, keepdims=True)) * (1.0 / B_WIDTH)
    inv = lax.rsqrt(var + EPS)
    for c in range(2):
        vn = ds[c] * inv * lng_ref[:, cols(c)] + lnb_ref[:, cols(c)]
        vn_ref[:, cols(c)] = vn.astype(BF16)
        if emit_vn_f32:
            rest[0][:, cols(c)] = vn

    for c in range(2):
        ga_ref[:, cols(c)] = _sigmoid(proj(base + 4 + c)).astype(BF16)
        gb_ref[:, cols(c)] = _sigmoid(proj(base + 6 + c)).astype(BF16)


def _inproj(x, sh, sc, gain, w_in, ln_g, ln_b, *, tm, keeps, dils, emit_vn_f32, kv_feature_major, after=None):
    bsz, seq, _ = x.shape
    n_s = seq // tm
    tok = lambda width: pl.BlockSpec((None, tm, width), lambda b, s: (b, s, 0))

    def kv_spec(keep):
        first = n_s - max(keep // tm, 1)
        rows = min(keep, tm)
        if kv_feature_major:
            return pl.BlockSpec((None, 2 * A_WIDTH, rows), lambda b, s: (b, 0, jnp.maximum(s - first, 0)))
        return pl.BlockSpec((None, rows, 2 * A_WIDTH), lambda b, s: (b, jnp.maximum(s - first, 0), 0))

    out_specs, out_shape = [], []
    for dil in dils:
        out_specs.append(pl.BlockSpec((None, dil, tm // dil, 3 * A_WIDTH), lambda b, s: (b, 0, s, 0)))
        out_shape.append(jax.ShapeDtypeStruct((bsz, dil, seq // dil, 3 * A_WIDTH), BF16))
    out_specs += [tok(B_WIDTH), tok(B_WIDTH), tok(D_MODEL), tok(D_MODEL)]
    out_shape += [jax.ShapeDtypeStruct((bsz, seq, D_MODEL), BF16)] * 4
    for keep in keeps:
        out_specs.append(kv_spec(keep))
        kv_shape = (bsz, 2 * A_WIDTH, keep) if kv_feature_major else (bsz, keep, 2 * A_WIDTH)
        out_shape.append(jax.ShapeDtypeStruct(kv_shape, F32))
    if emit_vn_f32:
        out_specs.append(tok(B_WIDTH))
        out_shape.append(jax.ShapeDtypeStruct((bsz, seq, B_WIDTH), F32))

    ins = [x, sh, sc, gain, w_in, ln_g, ln_b]
    in_specs = [tok(D_MODEL), _mod_spec(sh, tm, seq), _mod_spec(sc, tm, seq), _const_spec(gain),
                pl.BlockSpec(w_in.shape, lambda b, s: (0, 0), pipeline_mode=pl.Buffered(1)),
                _const_spec(ln_g), _const_spec(ln_b)]
    if after is not None:
        ins.append(after)
        in_specs.append(pl.BlockSpec(memory_space=pl.ANY))
    scratch = [pltpu.VMEM((2, COL_CHUNK // LANES, tm, LANES), F32)]
    return pl.pallas_call(
        functools.partial(_inproj_kernel, tm=tm, keeps=keeps, dils=dils, emit_vn_f32=emit_vn_f32,
                          has_after=after is not None, kv_feature_major=kv_feature_major),
        grid=(bsz, n_s),
        in_specs=in_specs,
        out_specs=out_specs,
        out_shape=out_shape,
        scratch_shapes=scratch,
        compiler_params=pltpu.CompilerParams(dimension_semantics=("arbitrary", "arbitrary"),
                                             vmem_limit_bytes=VMEM_LIMIT),
        name="inproj",
    )(*ins)


def _attn_kernel(q_ref, kp_ref, kc_ref, vp_ref, vc_ref, bias0_ref, bias_ref, o_ref, lse_ref, *, n_seq, q_blocks):
    lane = lax.broadcasted_iota(jnp.int32, (A_BLOCK, LANES), 1)
    low = lane < HEAD_DIM
    zero = jnp.zeros((), BF16)
    for seq_i, i in [(a, b) for a in range(n_seq) for b in range(q_blocks)]:
        rows = (seq_i, slice(i * A_BLOCK, (i + 1) * A_BLOCK))
        q = q_ref[rows[0], rows[1], :]
        key_rows = slice((i - 1) * A_BLOCK, (i + 1) * A_BLOCK)
        if i == 0:
            k = jnp.concatenate([kp_ref[seq_i], kc_ref[seq_i, :A_BLOCK, :]], axis=0)
            v = jnp.concatenate([vp_ref[seq_i], vc_ref[seq_i, :A_BLOCK, :]], axis=0)
        else:
            k, v = kc_ref[seq_i, key_rows, :], vc_ref[seq_i, key_rows, :]
        b_ref = bias0_ref if i == 0 else bias_ref
        lse_tile = jnp.zeros((A_BLOCK, LANES), F32)
        for j in range(HEADS // 2):
            pair = slice(j * LANES, (j + 1) * LANES)
            qp, kpair, vpair = q[:, pair], k[:, pair], v[:, pair]
            outs = []
            for e in range(2):
                h = 2 * j + e
                qm = jnp.where(low if e == 0 else jnp.logical_not(low), qp, zero)
                s = lax.dot_general(qm, kpair, (((1,), (1,)), ((), ())), preferred_element_type=F32)
                s = s + b_ref[h]
                m = jnp.max(s, axis=-1, keepdims=True)
                p = jnp.exp2(s - m)
                den = jnp.sum(p, axis=-1, keepdims=True)
                o = jnp.dot(p.astype(BF16), vpair, preferred_element_type=F32)
                first = h * LSE_REP
                lse_tile = jnp.where((lane >= first) & (lane < first + LSE_REP // 2), m,
                                     jnp.where((lane >= first + LSE_REP // 2) & (lane < first + LSE_REP), den,
                                               lse_tile))
                outs.append(o)
            o_ref[rows[0], rows[1], pair] = jnp.where(low, outs[0], outs[1]).astype(BF16)
        lse_ref[rows[0], rows[1], :] = lse_tile


def _attn_bias(g):
    _, dil = A_GROUPS[g]
    n = N_GROUPS * HEADS
    e = np.arange(1, n + 1, dtype=np.float32)
    slopes = np.exp2(-8.0 * e / n).astype(np.float32).reshape(N_GROUPS, HEADS)[g]
    qi = np.arange(A_BLOCK)[:, None]
    ki = np.arange(2 * A_BLOCK)[None, :]
    delta = qi + A_BLOCK - ki
    band = (delta >= 0) & (delta <= A_STEPS)
    dist = (delta * dil).astype(np.float32)
    bias = (-slopes[:, None, None] * dist[None] * np.float32(LOG2E)).astype(np.float32)
    out = np.empty((2, HEADS, A_BLOCK, 2 * A_BLOCK), np.float32)
    out[1] = np.where(band[None], bias, NEG)
    out[0] = np.where((band & (ki >= A_BLOCK))[None], bias, NEG)
    return jnp.asarray(out)


def _attn_prompt(qkv, g):
    bsz, dil, steps, _ = qkv.shape
    qb = min(ATTN_BLOCKS_PER_STEP, steps // A_BLOCK)
    ns = min(ATTN_BLOCKS_PER_STEP // qb, dil)
    rows = qb * A_BLOCK
    n_steps = steps // rows
    bias = _attn_bias(g)

    def blk(t, prev):
        if prev:
            return pl.BlockSpec((None, ns, A_BLOCK, A_WIDTH),
                                lambda b, r, n: (b, r, jnp.maximum(n * qb - 1, 0), t))
        return pl.BlockSpec((None, ns, rows, A_WIDTH), lambda b, r, n: (b, r, n, t))

    bias_blk = lambda pick: pl.BlockSpec((None, HEADS, A_BLOCK, 2 * A_BLOCK), lambda b, r, n: (pick(n), 0, 0, 0))
    return pl.pallas_call(
        functools.partial(_attn_kernel, n_seq=ns, q_blocks=qb),
        grid=(bsz, dil // ns, n_steps),
        in_specs=[blk(0, False), blk(1, True), blk(1, False), blk(2, True), blk(2, False),
                  bias_blk(lambda n: jnp.minimum(n, 1)), bias_blk(lambda n: 1)],
        out_specs=[pl.BlockSpec((None, ns, rows, A_WIDTH), lambda b, r, n: (b, r, n, 0)),
                   pl.BlockSpec((None, ns, rows, LANES), lambda b, r, n: (b, r, n, 0))],
        out_shape=[jax.ShapeDtypeStruct((bsz, dil, steps, A_WIDTH), BF16),
                   jax.ShapeDtypeStruct((bsz, dil, steps, LANES), F32)],
        compiler_params=pltpu.CompilerParams(dimension_semantics=("arbitrary",) * 3),
        name=f"attn_prompt_g{g}",
    )(qkv, qkv, qkv, qkv, qkv, bias, bias)


def _attn_sample_kernel(q0_ref, q1_ref, q2_ref, c0_ref, c1_ref, c2_ref, bc0_ref, bc1_ref, bc2_ref,
                        bn0_ref, bn1_ref, bn2_ref, after_ref, o_ref, *, t_new):
    del after_ref
    n_rows = HEADS * t_new
    row = lax.broadcasted_iota(jnp.int32, (n_rows, A_WIDTH), 0)
    lane = lax.broadcasted_iota(jnp.int32, (n_rows, A_WIDTH), 1)
    head_mask = (row // t_new) == (lane // HEAD_DIM)
    pad = jnp.zeros((LANES - t_new, A_WIDTH), F32)
    outs, lses = [], []
    for qkv_ref, c_ref, bc_ref, bn_ref in ((q0_ref, c0_ref, bc0_ref, bn0_ref), (q1_ref, c1_ref, bc1_ref, bn1_ref),
                                           (q2_ref, c2_ref, bc2_ref, bn2_ref)):
        q = qkv_ref[:, :A_WIDTH].astype(F32)
        k_new = qkv_ref[:, A_WIDTH:2 * A_WIDTH].astype(F32)
        v_new = qkv_ref[:, 2 * A_WIDTH:].astype(F32)
        k_new = jnp.concatenate([k_new, pad], axis=0).astype(BF16)
        v_new = jnp.concatenate([v_new, pad], axis=0).astype(BF16)
        q_rows = jnp.where(head_mask, jnp.concatenate([q] * HEADS, axis=0), 0.0).astype(BF16)
        k_buf_t = c_ref[:A_WIDTH, :].astype(BF16)
        v_buf_t = c_ref[A_WIDTH:, :].astype(BF16)
        nt = (((1,), (1,)), ((), ()))
        s_buf = jnp.dot(q_rows, k_buf_t, preferred_element_type=F32) + bc_ref[...]
        s_new = lax.dot_general(q_rows, k_new, nt, preferred_element_type=F32) + bn_ref[...]
        m = jnp.maximum(jnp.max(s_buf, axis=-1, keepdims=True), jnp.max(s_new, axis=-1, keepdims=True))
        p_buf = jnp.exp2(s_buf - m)
        p_new = jnp.exp2(s_new - m)
        den = jnp.sum(p_buf, axis=-1, keepdims=True) + jnp.sum(p_new, axis=-1, keepdims=True)
        o = (lax.dot_general(p_buf.astype(BF16), v_buf_t, nt, preferred_element_type=F32)
             + jnp.dot(p_new.astype(BF16), v_new, preferred_element_type=F32)) / den
        outs.append(o)
        lses.append(m + jnp.log2(den))
    top = jnp.maximum(jnp.maximum(lses[0], lses[1]), lses[2])
    ws = [jnp.exp2(l - top) for l in lses]
    tot = ws[0] + ws[1] + ws[2]
    acc = (ws[0] / tot) * outs[0] + (ws[1] / tot) * outs[1] + (ws[2] / tot) * outs[2]
    acc = jnp.where(head_mask, acc, 0.0).reshape(HEADS, t_new, A_WIDTH)
    o_ref[...] = jnp.sum(acc, axis=0).astype(BF16)


def _sample_bias(g, t_new, buf):
    _, dil = A_GROUPS[g]
    n = N_GROUPS * HEADS
    e = np.arange(1, n + 1, dtype=np.float32)
    slopes = np.exp2(-8.0 * e / n).astype(np.float32).reshape(N_GROUPS, HEADS)[g]
    t = np.arange(t_new)[:, None]
    idx = np.concatenate([np.arange(buf), buf + np.arange(LANES)])[None, :]
    dist = buf + t - idx
    valid = (dist >= 0) & (dist % dil == 0) & (dist <= A_STEPS * dil) & (idx < buf + t_new)
    bias = -slopes[:, None, None] * dist.astype(np.float32)[None] * np.float32(LOG2E)
    bias = np.where(valid[None], bias, NEG).astype(np.float32).reshape(HEADS * t_new, buf + LANES)
    return jnp.asarray(bias[:, :buf]), jnp.asarray(bias[:, buf:])


def _attn_sample(qkvs, caches, *, after):
    bsz, t_new, _ = qkvs[0].shape
    cache_v = [jnp.transpose(c, (0, 2, 3, 4, 1)).reshape(bsz, 2 * A_WIDTH, c.shape[1]) for c in caches]
    biases = [_sample_bias(g, t_new, cache_v[g].shape[2]) for g in range(N_GROUPS)]
    bcs = [b[0] for b in biases]
    bns = [b[1] for b in biases]
    return pl.pallas_call(
        functools.partial(_attn_sample_kernel, t_new=t_new),
        grid=(bsz,),
        in_specs=[pl.BlockSpec((None, t_new, 3 * A_WIDTH), lambda b: (b, 0, 0))] * N_GROUPS
                 + [pl.BlockSpec((None, 2 * A_WIDTH, c.shape[2]), lambda b: (b, 0, 0)) for c in cache_v]
                 + [_const_spec(b) for b in bcs] + [_const_spec(b) for b in bns]
                 + [pl.BlockSpec(memory_space=pl.ANY)],
        out_specs=pl.BlockSpec((None, t_new, A_WIDTH), lambda b: (b, 0, 0)),
        out_shape=jax.ShapeDtypeStruct((bsz, t_new, A_WIDTH), BF16),
        compiler_params=pltpu.CompilerParams(dimension_semantics=("arbitrary",),
                                             vmem_limit_bytes=VMEM_LIMIT),
        name="attn_sample",
    )(*qkvs, *cache_v, *bcs, *bns, after)


def _first_max4(v):
    top = jnp.maximum(jnp.maximum(v[0], v[1]), jnp.maximum(v[2], v[3]))
    idx = jnp.where(v[0] == top, 0.0, jnp.where(v[1] == top, 1.0, jnp.where(v[2] == top, 2.0, 3.0)))
    return top, idx


def _route(lt):
    row = lambda k: lt[k:k + 1, :]
    g = [row(k) for k in range(MOE_GROUPS)]
    gmax, g_idx = _first_max4(g)
    g_prob = 1.0 / (jnp.exp(g[0] - gmax) + jnp.exp(g[1] - gmax) + jnp.exp(g[2] - gmax) + jnp.exp(g[3] - gmax))
    cand = []
    for k in range(MOE_PER_GROUP):
        c = row(ROUTE_OFF + MOE_PER_GROUP * (MOE_GROUPS - 1) + k)
        for gi in range(MOE_GROUPS - 2, -1, -1):
            c = jnp.where(g_idx == float(gi), row(ROUTE_OFF + MOE_PER_GROUP * gi + k), c)
        cand.append(c)
    e1, i1 = _first_max4(cand)
    rest = [jnp.where(i1 == float(k), -jnp.inf, cand[k]) for k in range(MOE_PER_GROUP)]
    e2, i2 = _first_max4(rest)
    t = jnp.exp(e2 - e1)
    w1 = 1.0 / (1.0 + t)
    w2 = t / (1.0 + t)
    return MOE_PER_GROUP * g_idx + i1, MOE_PER_GROUP * g_idx + i2, w1 * g_prob, w2 * g_prob


def _pack_bf16_pairs(x):
    q = x.shape[1] // 4
    bits = lax.bitcast_convert_type(x.astype(BF16).astype(F32), jnp.int32)
    pack = lambda hi, lo: hi | lax.shift_right_logical(lo, 16)
    return pack(bits[:, :q], bits[:, 2 * q:3 * q]), pack(bits[:, q:2 * q], bits[:, 3 * q:])


def _unpack_bf16_pairs(pa, pb):
    hi = lambda p: lax.bitcast_convert_type(p & jnp.int32(-65536), F32)
    lo = lambda p: lax.bitcast_convert_type(lax.shift_left(p, 16), F32)
    return jnp.concatenate([hi(pa), hi(pb), lo(pa), lo(pb)], axis=1)


def _mix_kernel(*refs, tm, dils, routed, span):
    n_attn = len(dils)
    x_ref = refs[0]
    if n_attn > 1:
        o_refs = refs[1:1 + n_attn]
        l_refs = refs[1 + n_attn:1 + 2 * n_attn]
        exp_ref = refs[1 + 2 * n_attn]
        pos = 2 + 2 * n_attn
    else:
        o_refs = refs[1:2]
        pos = 2
    (u_ref, vn_ref, ga_ref, gb_ref, sw_ref, sbt_ref, wpa_ref, wpb_ref, wo_ref,
     gt1_ref, sh2_ref, sc2_ref, n2g_ref, wrh_ref, wrl_ref, br_ref) = refs[pos:pos + 16]
    n_out = 6 if routed else 3
    out_refs = refs[pos + 16:pos + 16 + n_out]
    scr = list(refs[pos + 16 + n_out:])
    x1_ref = out_refs[0]
    ob_scr = scr.pop(0)

    def to_positions(ref, scr, dil):
        if dil == 1:
            return ref[0].astype(F32)
        n_slab = scr.shape[0]
        for r in range(dil):
            rows = ref[r].astype(F32)
            for j in range(n_slab):
                scr[j, pl.ds(r, tm // dil, stride=dil), :] = rows[:, j * LANES:(j + 1) * LANES]
        return jnp.concatenate([scr[j] for j in range(n_slab)], axis=1)

    if n_attn > 1:
        o_scr, l_scr = scr[0], scr[1]
        stats = [to_positions(l_refs[g], l_scr.at[g], dils[g]) for g in range(n_attn)]
        lane = lax.broadcasted_iota(jnp.int32, stats[0].shape, 1)
        is_max = (lane & (LSE_REP - 1)) < LSE_REP // 2
        half = LSE_REP // 2
        maxes = [jnp.where(is_max, s, pltpu.roll(s, half, axis=1)) for s in stats]
        dens = [jnp.where(is_max, pltpu.roll(s, LANES - half, axis=1), s) for s in stats]
        lses = [m + jnp.log2(d) for m, d in zip(maxes, dens)]
        top = functools.reduce(jnp.maximum, lses)
        tot = functools.reduce(lambda a, b: a + b, [jnp.exp2(l - top) for l in lses])
        o_a = None
        for g in range(n_attn):
            w = jnp.exp2(maxes[g] - top) / tot
            hi = w.astype(BF16)
            lo = (w - hi.astype(F32)).astype(BF16)
            w_exp = jnp.dot(jnp.concatenate([hi, lo], axis=1), exp_ref[...], preferred_element_type=F32)
            term = w_exp * to_positions(o_refs[g], o_scr.at[g], dils[g])
            o_a = term if o_a is None else o_a + term
        o_a = o_a.astype(BF16)
    else:
        o_a = o_refs[0][...]

    r_i = lax.broadcasted_iota(jnp.int32, (span, span), 0)
    c_i = lax.broadcasted_iota(jnp.int32, (span, span), 1)
    tril = r_i >= c_i
    for gi in range(B_GROUPS):
        gcols = slice(gi * LANES, (gi + 1) * LANES)
        b_s = sbt_ref[:span, gi:gi + 1]
        if span == B_CHUNK:
            w_s = jnp.where(tril, sw_ref[gi], 0.0).astype(BF16)
            n_chunk = tm // span
            vn_wide = jnp.concatenate([vn_ref[c * span:(c + 1) * span, gcols] for c in range(n_chunk)], axis=1)
            mixed = jnp.dot(w_s, vn_wide, preferred_element_type=F32) + b_s
            for c in range(n_chunk):
                rows = slice(c * span, (c + 1) * span)
                ob_scr[rows, gcols] = (u_ref[rows, gcols].astype(F32)
                                       * mixed[:, c * LANES:(c + 1) * LANES]).astype(BF16)
        else:
            w_s = jnp.where(tril, sw_ref[gi, :span, :span], 0.0)
            vn3 = vn_ref[:, gcols].astype(F32).reshape(tm // span, span, LANES)
            mixed = jnp.broadcast_to(b_s, (span, LANES))[None]
            for s in range(span):
                w_col = jnp.broadcast_to(w_s[:, s:s + 1], (span, LANES))[None]
                mixed = mixed + w_col * jnp.broadcast_to(vn3[:, s:s + 1, :], vn3.shape)
            u3 = u_ref[:, gcols].astype(F32).reshape(tm // span, span, LANES)
            ob_scr[:, gcols] = (u3 * mixed).reshape(tm, LANES).astype(BF16)

    sub = min(tm, MIX_SUB_ROWS)
    if routed:
        carry = scr[-1]
        first_step = (pl.program_id(0) == 0) & (pl.program_id(1) == 0)
        seen = jnp.where(first_step, 0.0, carry[:, 0:1])
        r_i = lax.broadcasted_iota(jnp.int32, (sub, sub), 0)
        c_i = lax.broadcasted_iota(jnp.int32, (sub, sub), 1)
        earlier = jnp.where(r_i < c_i, 1.0, 0.0).astype(BF16)
        e_iota = lax.broadcasted_iota(jnp.int32, (MOE_EXPERTS, sub), 0).astype(F32)
    nt = (((1,), (1,)), ((), ()))
    for r0 in range(0, tm, sub):
        rs = slice(r0, r0 + sub)
        mod = lambda ref: ref[...] if ref.shape[0] == 1 else _mod_rows(ref, tm)[rs, :]
        pa = jnp.dot(o_a[rs], wpa_ref[...], preferred_element_type=F32)
        pb = jnp.dot(ob_scr[rs, :], wpb_ref[...], preferred_element_type=F32)
        mix = (ga_ref[rs, :].astype(F32) * pa + gb_ref[rs, :].astype(F32) * pb).astype(BF16)
        x1 = x_ref[rs, :] + mod(gt1_ref) * jnp.dot(mix, wo_ref[...], preferred_element_type=F32)
        x1_ref[rs, :] = x1
        h2 = _rmsnorm_mod(x1, n2g_ref[...], mod(sh2_ref), mod(sc2_ref))
        h_hi = h2.astype(BF16)
        h_lo = (h2 - h_hi.astype(F32)).astype(BF16)
        lt = (lax.dot_general(wrh_ref[...], h_hi, nt, preferred_element_type=F32)
              + lax.dot_general(wrh_ref[...], h_lo, nt, preferred_element_type=F32)
              + lax.dot_general(wrl_ref[...], h_hi, nt, preferred_element_type=F32)) + br_ref[...]
        ex1, ex2, w1, w2 = _route(lt)
        if not routed:
            _, h2_ref, comb_ref = out_refs
            h2_ref[rs, :] = h_hi
            e_lane = lax.broadcasted_iota(jnp.int32, (LANES, sub), 0).astype(F32) - float(ROUTE_OFF)
            comb_t = jnp.where(e_lane == ex1, w1, jnp.where(e_lane == ex2, w2, 0.0))
            comb_ref[rs, :] = jnp.transpose(comb_t)
            continue

        _, hpa_ref, hpb_ref, route_ref, wcol_ref, cnt_ref = out_refs
        hpa_ref[rs, :], hpb_ref[rs, :] = _pack_bf16_pairs(h2)
        hot1, hot2 = e_iota == ex1, e_iota == ex2
        onehot = jnp.where(hot1 | hot2, 1.0, 0.0)
        rank = jnp.dot(onehot.astype(BF16), earlier, preferred_element_type=F32) + seen
        r1 = jnp.sum(jnp.where(hot1, rank, 0.0), axis=0, keepdims=True)
        r2 = jnp.sum(jnp.where(hot2, rank, 0.0), axis=0, keepdims=True)
        seen = seen + jnp.sum(onehot, axis=1, keepdims=True)
        fields = jnp.concatenate([ex1, ex2, w1, w2, r1, r2, jnp.zeros((2, sub), F32)], axis=0)
        route_ref[:, rs] = fields
        wcol_ref[rs, :] = jnp.transpose(jnp.concatenate([fields, jnp.zeros((LANES - 8, sub), F32)], axis=0))
    if routed:
        carry[...] = jnp.broadcast_to(seen, carry.shape)
        cnt_ref[...] = jnp.broadcast_to(seen, cnt_ref.shape)


def _expand_matrix():
    e = np.zeros((LANES, A_WIDTH), np.float32)
    for h in range(HEADS):
        e[h * LSE_REP, h * HEAD_DIM:(h + 1) * HEAD_DIM] = 1.0
    return jnp.asarray(np.concatenate([e, e], axis=0), dtype=BF16)


def _mix(x, attn_outs, attn_lses, u, vn, ga, gb, sgu_w, sgu_bt, w_pa, w_pb, w_o,
         gt1, sh2, sc2, n2g, w_r_hi, w_r_lo, b_r, *, tm, routed, span):
    bsz, seq, _ = x.shape
    n_attn = len(attn_outs)
    tok = lambda width: pl.BlockSpec((None, tm, width), lambda b, s: (b, s, 0))
    ins = [x] + list(attn_outs)
    scratch = [pltpu.VMEM((tm, B_WIDTH), BF16)]
    if n_attn > 1:
        dils = tuple(o.shape[1] for o in attn_outs)
        res = lambda dil, width: pl.BlockSpec((None, dil, tm // dil, width), lambda b, s: (b, 0, s, 0))
        e2 = _expand_matrix()
        ins += list(attn_lses) + [e2]
        specs = ([tok(D_MODEL)] + [res(d, A_WIDTH) for d in dils] + [res(d, LANES) for d in dils]
                 + [_const_spec(e2)])
        scratch += [pltpu.VMEM((n_attn, A_WIDTH // LANES, tm, LANES), F32), pltpu.VMEM((n_attn, 1, tm, LANES), F32)]
    else:
        dils = (1,)
        specs = [tok(D_MODEL), tok(A_WIDTH)]
    ins += [u, vn, ga, gb, sgu_w, sgu_bt, w_pa, w_pb, w_o, gt1, sh2, sc2, n2g, w_r_hi, w_r_lo, b_r]
    specs += [tok(B_WIDTH), tok(B_WIDTH), tok(D_MODEL), tok(D_MODEL),
              _const_spec(sgu_w), _const_spec(sgu_bt), _const_spec(w_pa), _const_spec(w_pb), _const_spec(w_o),
              _mod_spec(gt1, tm, seq), _mod_spec(sh2, tm, seq), _mod_spec(sc2, tm, seq), _const_spec(n2g),
              _const_spec(w_r_hi), _const_spec(w_r_lo), _const_spec(b_r)]
    if routed:
        quarter = D_MODEL // 4
        out_specs = [tok(D_MODEL), tok(quarter), tok(quarter),
                     pl.BlockSpec((None, 8, tm), lambda b, s: (b, 0, s)), tok(LANES),
                     pl.BlockSpec((MOE_EXPERTS, LANES), lambda b, s: (0, 0))]
        out_shape = [jax.ShapeDtypeStruct((bsz, seq, D_MODEL), F32),
                     jax.ShapeDtypeStruct((bsz, seq, quarter), jnp.int32),
                     jax.ShapeDtypeStruct((bsz, seq, quarter), jnp.int32),
                     jax.ShapeDtypeStruct((bsz, 8, seq), F32),
                     jax.ShapeDtypeStruct((bsz, seq, LANES), F32),
                     jax.ShapeDtypeStruct((MOE_EXPERTS, LANES), F32)]
        scratch.append(pltpu.VMEM((MOE_EXPERTS, LANES), F32))
    else:
        out_specs = [tok(D_MODEL), tok(D_MODEL), tok(LANES)]
        out_shape = [jax.ShapeDtypeStruct((bsz, seq, D_MODEL), F32),
                     jax.ShapeDtypeStruct((bsz, seq, D_MODEL), BF16),
                     jax.ShapeDtypeStruct((bsz, seq, LANES), F32)]
    return pl.pallas_call(
        functools.partial(_mix_kernel, tm=tm, dils=dils, routed=routed, span=span),
        grid=(bsz, seq // tm),
        in_specs=specs,
        out_specs=out_specs,
        out_shape=out_shape,
        scratch_shapes=scratch,
        compiler_params=pltpu.CompilerParams(dimension_semantics=("arbitrary", "arbitrary"),
                                             vmem_limit_bytes=VMEM_LIMIT),
        name="mix",
    )(*ins)


def _moe_kernel(h_ref, comb_ref, wg_ref, wu_ref, wd_ref, x1_ref, gt2_ref, shf_ref, scf_ref, nfg_ref,
                y_ref, acc_ref):
    e = pl.program_id(2)

    @pl.when(e == 0)
    def _():
        acc_ref[...] = jnp.zeros_like(acc_ref)

    h = h_ref[...]
    hg = jnp.dot(h, wg_ref[...], preferred_element_type=F32)
    hu = jnp.dot(h, wu_ref[...], preferred_element_type=F32)
    comb = comb_ref[...]
    lane = lax.broadcasted_iota(jnp.int32, comb.shape, 1)
    cw = jnp.sum(jnp.where(lane == e + ROUTE_OFF, comb, 0.0), axis=-1, keepdims=True)
    act = (hg * _sigmoid(hg) * hu) * cw
    acc_ref[...] += jnp.dot(act.astype(BF16), wd_ref[...], preferred_element_type=F32)

    @pl.when(e == MOE_EXPERTS - 1)
    def _():
        rows = acc_ref.shape[0]
        x2 = x1_ref[...] + _mod_rows(gt2_ref, rows) * acc_ref[...]
        y_ref[...] = _rmsnorm_mod(x2, nfg_ref[...], _mod_rows(shf_ref, rows), _mod_rows(scf_ref, rows))


def _moe(h2, comb, w_gate, w_up, w_down, x1, gt2, shf, scf, nfg, *, tm):
    bsz, seq, _ = h2.shape
    tok = lambda width: pl.BlockSpec((None, tm, width), lambda b, s, e: (b, s, 0))
    return pl.pallas_call(
        _moe_kernel,
        grid=(bsz, seq // tm, MOE_EXPERTS),
        in_specs=[tok(D_MODEL), tok(LANES),
                  pl.BlockSpec((None, D_MODEL, MOE_HIDDEN), lambda b, s, e: (e, 0, 0)),
                  pl.BlockSpec((None, D_MODEL, MOE_HIDDEN), lambda b, s, e: (e, 0, 0)),
                  pl.BlockSpec((None, MOE_HIDDEN, D_MODEL), lambda b, s, e: (e, 0, 0)),
                  tok(D_MODEL), _mod_spec(gt2, tm, seq), _mod_spec(shf, tm, seq), _mod_spec(scf, tm, seq),
                  _const_spec(nfg)],
        out_specs=tok(D_MODEL),
        out_shape=jax.ShapeDtypeStruct((bsz, seq, D_MODEL), F32),
        scratch_shapes=[pltpu.VMEM((tm, D_MODEL), F32)],
        compiler_params=pltpu.CompilerParams(dimension_semantics=("arbitrary",) * 3,
                                             vmem_limit_bytes=VMEM_LIMIT),
        name="moe",
    )(h2, comb, w_gate, w_up, w_down, x1, gt2, shf, scf, nfg)


def _sc_mesh():
    return plsc.VectorSubcoreMesh(core_axis_name="c", subcore_axis_name="s")


def _sc_scatter_rows(rows, pos1, pos2, n_out):
    n, width = rows.shape
    steps = n // SC_WINDOW // SC_CORES

    @pl.kernel(out_type=jax.ShapeDtypeStruct((n_out, width), rows.dtype), mesh=_sc_mesh(), scratch_types=[])
    def scatter(rows_hbm, p1_hbm, p2_hbm, out_hbm):
        def body(x_vmem, i1_vmem, i2_vmem):
            pltpu.sync_copy(x_vmem, out_hbm.at[i1_vmem.at[0]])
            pltpu.sync_copy(x_vmem, out_hbm.at[i2_vmem.at[0]])

        pltpu.emit_pipeline(
            body, grid=(SC_CORES, steps),
            in_specs=[pl.BlockSpec((SC_WINDOW, width), lambda c, i: (c * steps + i, 0)),
                      pl.BlockSpec((1, SC_WINDOW), lambda c, i: (0, c * steps + i)),
                      pl.BlockSpec((1, SC_WINDOW), lambda c, i: (0, c * steps + i))],
            out_specs=[],
            core_axis_name=("c", "s"),
            dimension_semantics=(pltpu.PARALLEL, pltpu.PARALLEL),
        )(rows_hbm, p1_hbm, p2_hbm)

    return scatter(rows, pos1, pos2)


def _sc_gather_rows(table, idx):
    m, width = idx.shape[1], table.shape[1]
    steps = m // SC_WINDOW // SC_CORES

    @pl.kernel(out_type=jax.ShapeDtypeStruct((m, width), table.dtype), mesh=_sc_mesh(), scratch_types=[])
    def gather(table_hbm, idx_hbm, out_hbm):
        def body(i_vmem, o_vmem):
            pltpu.sync_copy(table_hbm.at[i_vmem.at[0]], o_vmem)

        pltpu.emit_pipeline(
            body, grid=(SC_CORES, steps),
            in_specs=[pl.BlockSpec((1, SC_WINDOW), lambda c, i: (0, c * steps + i))],
            out_specs=[pl.BlockSpec((SC_WINDOW, width), lambda c, i: (c * steps + i, 0))],
            core_axis_name=("c", "s"),
            dimension_semantics=(pltpu.PARALLEL, pltpu.PARALLEL),
        )(idx_hbm, out_hbm)

    return gather(table, idx)


def _ffn_kernel(te_ref, nt_ref, xa_ref, xb_ref, wg_ref, wu_ref, wd_ref, *rest):
    oa_ref, ob_ref, wg_s, wu_s, wd_s = rest[-5:]
    t = pl.program_id(0)
    live = t < nt_ref[0]

    @pl.when(live & ((t == 0) | (te_ref[t] != te_ref[jnp.maximum(t - 1, 0)])))
    def _():
        wg_s[...] = wg_ref[...].astype(BF16)
        wu_s[...] = wu_ref[...].astype(BF16)
        wd_s[...] = wd_ref[...].astype(BF16)

    @pl.when(live)
    def _():
        x = _unpack_bf16_pairs(xa_ref[...], xb_ref[...]).astype(BF16)
        hg = jnp.dot(x, wg_s[...], preferred_element_type=F32)
        hu = jnp.dot(x, wu_s[...], preferred_element_type=F32)
        act = (hg * _sigmoid(hg) * hu).astype(BF16)
        oa_ref[...], ob_ref[...] = _pack_bf16_pairs(jnp.dot(act, wd_s[...], preferred_element_type=F32))


def _ffn(xa, xb, tile_expert, n_tiles, w_gate, w_up, w_down, *, after):
    n_rows, quarter = xa.shape
    rows = pl.BlockSpec((MOE_TILE, quarter), lambda t, te, nt: (jnp.minimum(t, nt[0] - 1), 0))
    w_in = pl.BlockSpec((None, D_MODEL, MOE_HIDDEN), lambda t, te, nt: (te[t], 0, 0))
    w_out = pl.BlockSpec((None, MOE_HIDDEN, D_MODEL), lambda t, te, nt: (te[t], 0, 0))
    return pl.pallas_call(
        _ffn_kernel,
        grid_spec=pltpu.PrefetchScalarGridSpec(
            num_scalar_prefetch=2,
            grid=(n_rows // MOE_TILE,),
            in_specs=[rows, rows, w_in, w_in, w_out] + [pl.BlockSpec(memory_space=pl.ANY)] * len(after),
            out_specs=[rows, rows, w_in, w_in, w_out]),
        out_shape=[jax.ShapeDtypeStruct((n_rows, quarter), jnp.int32)] * 2
                  + [jax.ShapeDtypeStruct(w_gate.shape, BF16), jax.ShapeDtypeStruct(w_up.shape, BF16),
                     jax.ShapeDtypeStruct(w_down.shape, BF16)],
        compiler_params=pltpu.CompilerParams(dimension_semantics=("arbitrary",), vmem_limit_bytes=VMEM_LIMIT),
        name="moe_ffn",
    )(tile_expert, n_tiles, xa, xb, w_gate, w_up, w_down, *after)


def _final_kernel(x1_ref, g1a_ref, g1b_ref, g2a_ref, g2b_ref, route_ref, gt2_ref, shf_ref, scf_ref, nfg_ref, y_ref):
    o1 = _unpack_bf16_pairs(g1a_ref[...], g1b_ref[...])
    o2 = _unpack_bf16_pairs(g2a_ref[...], g2b_ref[...])
    route = route_ref[...]
    moe = route[:, 2:3] * o1 + route[:, 3:4] * o2
    x2 = x1_ref[...] + gt2_ref[...] * moe
    y_ref[...] = _rmsnorm_mod(x2, nfg_ref[...], shf_ref[...], scf_ref[...])


def _final(x1, ga, gb, route, gt2, shf, scf, nfg, *, tm):
    bsz, seq, _ = x1.shape
    n_s = seq // tm
    n_blk = bsz * n_s
    quarter = ga.shape[1]
    tok = lambda width: pl.BlockSpec((None, tm, width), lambda b, s: (b, s, 0))
    first = pl.BlockSpec((tm, quarter), lambda b, s: (b * n_s + s, 0))
    second = pl.BlockSpec((tm, quarter), lambda b, s: (n_blk + b * n_s + s, 0))
    return pl.pallas_call(
        _final_kernel,
        grid=(bsz, n_s),
        in_specs=[tok(D_MODEL), first, first, second, second, tok(LANES),
                  _mod_spec(gt2, tm, seq), _mod_spec(shf, tm, seq), _mod_spec(scf, tm, seq), _const_spec(nfg)],
        out_specs=tok(D_MODEL),
        out_shape=jax.ShapeDtypeStruct((bsz, seq, D_MODEL), F32),
        compiler_params=pltpu.CompilerParams(dimension_semantics=("arbitrary", "arbitrary")),
        name="moe_final",
    )(x1, ga, gb, ga, gb, route, gt2, shf, scf, nfg)


def _positions_kernel(base_ref, route_ref, p1_ref, p2_ref):
    rows = route_ref[...]
    for e_row, r_row, out_ref in ((0, 4, p1_ref), (1, 5, p2_ref)):
        expert = rows[e_row:e_row + 1, :]
        start = jnp.zeros(expert.shape, jnp.int32)
        for e in range(MOE_EXPERTS):
            start = jnp.where(expert == float(e), base_ref[e], start)
        out_ref[...] = start + rows[r_row:r_row + 1, :].astype(jnp.int32)


def _positions(route_rows, base):
    bsz, _, seq = route_rows.shape
    out = pl.BlockSpec((None, 1, seq), lambda b, base: (b, 0, 0))
    return pl.pallas_call(
        _positions_kernel,
        grid_spec=pltpu.PrefetchScalarGridSpec(
            num_scalar_prefetch=1, grid=(bsz,),
            in_specs=[pl.BlockSpec((None, 8, seq), lambda b, base: (b, 0, 0))],
            out_specs=[out, out]),
        out_shape=[jax.ShapeDtypeStruct((bsz, 1, seq), jnp.int32)] * 2,
        compiler_params=pltpu.CompilerParams(dimension_semantics=("arbitrary",)),
        name="moe_positions",
    )(base, route_rows)


def _moe_dispatch(hpa, hpb, route_rows, counts):
    bsz, seq, quarter = hpa.shape
    n_tok = bsz * seq
    t_max = 2 * n_tok // MOE_TILE + MOE_EXPERTS
    cnt = counts[:, 0].astype(jnp.int32)
    tiles_e = jnp.maximum((cnt + MOE_TILE - 1) // MOE_TILE, 1)
    tile_end = jnp.cumsum(tiles_e)
    n_tiles = tile_end[-1:]
    base = (tile_end - tiles_e) * MOE_TILE
    t_ids = jnp.minimum(jnp.arange(t_max, dtype=jnp.int32), n_tiles[0] - 1)
    tile_expert = jnp.sum((t_ids[:, None] >= tile_end[None, :]).astype(jnp.int32), axis=1)
    pos1, pos2 = (p.reshape(1, n_tok) for p in _positions(route_rows, base))
    n_rows = t_max * MOE_TILE
    xa = _sc_scatter_rows(hpa.reshape(n_tok, quarter), pos1, pos2, n_rows)
    xb = _sc_scatter_rows(hpb.reshape(n_tok, quarter), pos1, pos2, n_rows)
    return xa, xb, jnp.concatenate([pos1, pos2], axis=1), tile_expert, n_tiles


def kernel(x_prompt, x_sample, cache_kv_w128, cache_kv_w512, cache_kv_w2048, c_prompt, c_sample,
           w_ada, b_ada, norm1_g, norm2_g, w_in, sgu_ln_g, sgu_ln_b, sgu_w, sgu_b, w_pa, w_pb, w_o,
           w_route_group, b_route_group, w_route_expert, b_route_expert, w_gate, w_up, w_down,
           normf_g, w_ada_final, b_ada_final):
    depth = w_ada.shape[0]
    assert depth == 1
    l = 0
    bp, seq, _ = x_prompt.shape
    bs, t_new, _ = x_sample.shape
    n_samp = bs * t_new

    c_all = jnp.concatenate([c_prompt, c_sample], axis=0)
    pad_rows = (-c_all.shape[0]) % 8
    c_all = jnp.pad(c_all, ((0, pad_rows), (0, 0)))
    mod = _adaln(c_all, w_ada[l], b_ada[l])

    def split_mods(m, n, lo, hi, per_token):
        parts = jnp.split(m[lo:hi], n, axis=-1)
        if per_token:
            return [p.reshape(1, hi - lo, D_MODEL) for p in parts]
        return [p.reshape(hi - lo, 1, D_MODEL) for p in parts]

    mods_p = split_mods(mod, 6, 0, bp, False)
    mods_s = split_mods(mod, 6, bp, bp + bs, True)

    row = lambda v: v.reshape(1, -1)
    w_in_b = w_in[l].astype(BF16)
    w_pa_b, w_pb_b, w_o_b = w_pa[l].astype(BF16), w_pb[l].astype(BF16), w_o[l].astype(BF16)
    w_re = jnp.transpose(w_route_expert[l], (0, 2, 1)).reshape(MOE_EXPERTS, D_MODEL)
    w_r = jnp.pad(jnp.concatenate([jnp.transpose(w_route_group[l]), w_re], axis=0),
                  ((0, ROUTE_ROWS - MOE_GROUPS - MOE_EXPERTS), (0, 0)))
    w_r_hi = w_r.astype(BF16)
    w_r_lo = (w_r - w_r_hi.astype(F32)).astype(BF16)
    b_r = jnp.pad(jnp.concatenate([b_route_group[l], b_route_expert[l].reshape(-1)]),
                  (0, ROUTE_ROWS - MOE_GROUPS - MOE_EXPERTS)).reshape(ROUTE_ROWS, 1)

    def inproj(x, mods, keeps, dils, tm, emit_vn_f32, kv_feature_major, after=None):
        return _inproj(x, mods[0], mods[1], row(norm1_g[l]), w_in_b, row(sgu_ln_g[l]), row(sgu_ln_b[l]),
                       tm=tm, keeps=keeps, dils=dils, emit_vn_f32=emit_vn_f32,
                       kv_feature_major=kv_feature_major, after=after)

    sgu_bt = jnp.transpose(sgu_b[l])

    def mix(x, mods, pin, attn_outs, attn_lses, tm, routed, span):
        u, vn, ga, gb = pin[3:7]
        return _mix(x, attn_outs, attn_lses, u, vn, ga, gb, sgu_w[l], sgu_bt,
                    w_pa_b, w_pb_b, w_o_b, mods[2], mods[3], mods[4], row(norm2_g[l]), w_r_hi, w_r_lo, b_r,
                    tm=tm, routed=routed, span=span)

    tm_p = 512
    keeps_p = tuple(min(win, seq) for win, _ in A_GROUPS)
    dils_p = tuple(dil for _, dil in A_GROUPS)
    pin = inproj(x_prompt, mods_p, keeps_p, dils_p, 256, False, True)
    attn_p = [_attn_prompt(pin[g], g) for g in range(N_GROUPS)]
    x1, hpa, hpb, route_rows, route, counts = mix(x_prompt, mods_p, pin, [r[0] for r in attn_p],
                                                  [r[1] for r in attn_p], tm_p, True, B_CHUNK)

    xs = x_sample.reshape(1, n_samp, D_MODEL)
    sin = inproj(xs, mods_s, (n_samp,) * N_GROUPS, (1,) * N_GROUPS, n_samp, True, False, after=counts)
    mod_f = _adaln(c_all, w_ada_final, b_ada_final, after=counts)
    fin_p = split_mods(mod_f, 2, 0, bp, False)
    fin_s = split_mods(mod_f, 2, bp, bp + bs, True)

    xa, xb, pos, tile_expert, n_tiles = _moe_dispatch(hpa, hpb, route_rows, counts)
    kv_s_out = [kv.reshape(depth, bs, -1, 2, HEADS, HEAD_DIM) for kv in sin[7:10]]
    oa, ob, w_gate16, w_up16, w_down16 = _ffn(xa, xb, tile_expert, n_tiles, w_gate[l], w_up[l], w_down[l],
                                              after=(sin[0], fin_s[0], *kv_s_out))

    caches = (cache_kv_w128[l], cache_kv_w512[l], cache_kv_w2048[l])
    o_s = _attn_sample([q.reshape(bs, t_new, 3 * A_WIDTH) for q in sin[:3]], caches, after=oa)
    x1_s, h2_s, comb_s = mix(xs, mods_s, sin, [o_s.reshape(1, n_samp, A_WIDTH)], None, n_samp, False,
                             min(t_new, B_CHUNK))
    y_s = _moe(h2_s, comb_s, w_gate16, w_up16, w_down16, x1_s, mods_s[5], fin_s[0], fin_s[1], row(normf_g),
               tm=n_samp)

    ga_rows = _sc_gather_rows(oa, pos)
    gb_rows = _sc_gather_rows(ob, pos)
    y_p = _final(x1, ga_rows, gb_rows, route, mods_p[5], fin_p[0], fin_p[1], row(normf_g), tm=2 * tm_p)
    kv_p, extra = pin[7:10], sin[10:]

    def kv_out_t(a, b):
        a = a.reshape(b, 2, HEADS, HEAD_DIM, a.shape[-1])
        return jnp.transpose(a, (0, 4, 1, 2, 3)).reshape(depth, b, -1, 2, HEADS, HEAD_DIM)

    return (y_p, y_s.reshape(bs, t_new, D_MODEL),
            kv_out_t(kv_p[0], bp), kv_out_t(kv_p[1], bp), kv_out_t(kv_p[2], bp),
            kv_s_out[0], kv_s_out[1], kv_s_out[2],
            extra[0].reshape(depth, bs, t_new, B_WIDTH))
```

```python
import functools

import numpy as np
import jax
import jax.numpy as jnp
from jax import lax
from jax.experimental import pallas as pl
from jax.experimental.pallas import tpu as pltpu
from jax.experimental.pallas import tpu_sc as plsc

F32 = jnp.float32
BF16 = jnp.bfloat16

D_MODEL = 1024
A_GROUPS = ((128, 1), (512, 4), (2048, 16))
N_GROUPS = 3
HEADS = 8
HEAD_DIM = 64
A_WIDTH = HEADS * HEAD_DIM
A_STEPS = 128
A_BLOCK = 128
B_WIDTH = 1024
B_GROUPS = 8
B_CHUNK = 128
MOE_GROUPS = 4
MOE_PER_GROUP = 4
MOE_EXPERTS = 16
MOE_HIDDEN = 512
EPS = 1e-6
N_QKV = 3 * N_GROUPS * A_WIDTH
IN_COLS = N_QKV + 2 * B_WIDTH + 2 * D_MODEL
COL_CHUNK = 512
LANES = 128
LSE_REP = LANES // HEADS
ROUTE_OFF = MOE_GROUPS
ROUTE_ROWS = 32
NEG = -1e30
LOG2E = 1.4426950408889634
VMEM_LIMIT = 56 * 1024 * 1024
ATTN_BLOCKS_PER_STEP = 16
MIX_SUB_ROWS = 512
MOE_TILE = 512
SC_WINDOW = 128
SC_CORES = 2


def _sigmoid(x):
    return 1.0 / (1.0 + jnp.exp(-x))


def _gelu_tanh(x):
    return x * (0.5 * (1.0 + jnp.tanh(0.7978845608028654 * (x + 0.044715 * (x * x * x)))))


def _rmsnorm_mod(x, gain, shift, scale):
    y = x * lax.rsqrt(jnp.mean(x * x, axis=-1, keepdims=True) + EPS)
    return y * gain * (1.0 + scale) + shift


def _mod_spec(mod, tm, seq):
    if mod.shape[1] == 1:
        return pl.BlockSpec((None, 1, D_MODEL), lambda b, s, *_: (b, 0, 0))
    return pl.BlockSpec((None, tm * mod.shape[1] // seq, D_MODEL), lambda b, s, *_: (b, s, 0))


def _mod_rows(ref, rows):
    m = ref[...]
    if m.shape[0] in (1, rows):
        return m
    rep = rows // m.shape[0]
    return jnp.broadcast_to(m[:, None, :], (m.shape[0], rep, m.shape[1])).reshape(rows, m.shape[1])


def _const_spec(arr):
    nd = arr.ndim
    return pl.BlockSpec(arr.shape, lambda *_: (0,) * nd)


def _adaln_kernel(c_ref, w_ref, b_ref, *rest):
    o_ref = rest[-1]
    c = c_ref[...]
    a = (c * _sigmoid(c)).astype(BF16)
    o_ref[...] = jnp.dot(a, w_ref[...].astype(BF16), preferred_element_type=F32) + b_ref[...]


def _adaln(c, w, b, tn=1024, after=None):
    rows, ncols = c.shape[0], w.shape[1]
    ins = [c, w, b.reshape(1, ncols)]
    in_specs = [pl.BlockSpec((rows, D_MODEL), lambda j: (0, 0)),
                pl.BlockSpec((D_MODEL, tn), lambda j: (0, j)),
                pl.BlockSpec((1, tn), lambda j: (0, j))]
    if after is not None:
        ins.append(after)
        in_specs.append(pl.BlockSpec(memory_space=pl.ANY))
    return pl.pallas_call(
        _adaln_kernel,
        grid=(ncols // tn,),
        in_specs=in_specs,
        out_specs=pl.BlockSpec((rows, tn), lambda j: (0, j)),
        out_shape=jax.ShapeDtypeStruct((rows, ncols), F32),
        compiler_params=pltpu.CompilerParams(dimension_semantics=("arbitrary",)),
        name="adaln",
    )(*ins)


def _inproj_kernel(x_ref, sh_ref, sc_ref, g_ref, w_ref, lng_ref, lnb_ref, *refs, tm, keeps, dils, emit_vn_f32,
                   has_after, kv_feature_major):
    if has_after:
        refs = refs[1:]
    qkv0_ref, qkv1_ref, qkv2_ref, u_ref, vn_ref, ga_ref, gb_ref, kv0_ref, kv1_ref, kv2_ref = refs[:10]
    rest = refs[10:]
    perm_scr = rest[-1]
    h = _rmsnorm_mod(x_ref[...], g_ref[...], _mod_rows(sh_ref, tm), _mod_rows(sc_ref, tm)).astype(BF16)

    def proj(c):
        return jnp.dot(h, w_ref[:, c * COL_CHUNK:(c + 1) * COL_CHUNK], preferred_element_type=F32)

    def cols(c):
        return slice(c * COL_CHUNK, (c + 1) * COL_CHUNK)

    qkv_refs = (qkv0_ref, qkv1_ref, qkv2_ref)

    def put_qkv(g, t, z):
        dil = dils[g]
        if dil == 1:
            qkv_refs[g][0, :, cols(t)] = z.astype(BF16)
            return
        slot = perm_scr.at[t % 2]
        n_slab = COL_CHUNK // LANES
        for j in range(n_slab):
            slot[j] = z[:, j * LANES:(j + 1) * LANES]
        for r in range(dil):
            rows = jnp.concatenate([slot[j, pl.ds(r, tm // dil, stride=dil), :] for j in range(n_slab)], axis=1)
            qkv_refs[g][r, :, cols(t)] = rows.astype(BF16)

    for g in range(N_GROUPS):
        put_qkv(g, 0, proj(g) * (HEAD_DIM ** -0.5 * LOG2E))

    kv_refs = (kv0_ref, kv1_ref, kv2_ref)
    for g in range(N_GROUPS):
        keep = keeps[g]
        for t in (1, 2):
            z = proj(t * N_GROUPS + g)
            put_qkv(g, t, z)
            if kv_feature_major:
                zt = jnp.transpose(z if keep >= tm else z[tm - keep:, :])
                kv_refs[g][(t - 1) * A_WIDTH:t * A_WIDTH, :] = zt
            else:
                kv_refs[g][:, cols(t - 1)] = z if keep >= tm else z[tm - keep:, :]

    base = N_QKV // COL_CHUNK
    for c in range(2):
        u_ref[:, cols(c)] = _gelu_tanh(proj(base + c)).astype(BF16)

    vs = [_gelu_tanh(proj(base + 2 + c)) for c in range(2)]
    mu = (jnp.sum(vs[0], axis=-1, keepdims=True) + jnp.sum(vs[1], axis=-1, keepdims=True)) * (1.0 / B_WIDTH)
    ds = [v - mu for v in vs]
    var = (jnp.sum(ds[0] * ds[0], axis=-1, keepdims=True)
           + jnp.sum(ds[1] * ds[1], axis=-1, keepdims=True)) * (1.0 / B_WIDTH)
    inv = lax.rsqrt(var + EPS)
    for c in range(2):
        vn = ds[c] * inv * lng_ref[:, cols(c)] + lnb_ref[:, cols(c)]
        vn_ref[:, cols(c)] = vn.astype(BF16)
        if emit_vn_f32:
            rest[0][:, cols(c)] = vn

    for c in range(2):
        ga_ref[:, cols(c)] = _sigmoid(proj(base + 4 + c)).astype(BF16)
        gb_ref[:, cols(c)] = _sigmoid(proj(base + 6 + c)).astype(BF16)


def _inproj(x, sh, sc, gain, w_in, ln_g, ln_b, *, tm, keeps, dils, emit_vn_f32, kv_feature_major, after=None):
    bsz, seq, _ = x.shape
    n_s = seq // tm
    tok = lambda width: pl.BlockSpec((None, tm, width), lambda b, s: (b, s, 0))

    def kv_spec(keep):
        first = n_s - max(keep // tm, 1)
        rows = min(keep, tm)
        if kv_feature_major:
            return pl.BlockSpec((None, 2 * A_WIDTH, rows), lambda b, s: (b, 0, jnp.maximum(s - first, 0)))
        return pl.BlockSpec((None, rows, 2 * A_WIDTH), lambda b, s: (b, jnp.maximum(s - first, 0), 0))

    out_specs, out_shape = [], []
    for dil in dils:
        out_specs.append(pl.BlockSpec((None, dil, tm // dil, 3 * A_WIDTH), lambda b, s: (b, 0, s, 0)))
        out_shape.append(jax.ShapeDtypeStruct((bsz, dil, seq // dil, 3 * A_WIDTH), BF16))
    out_specs += [tok(B_WIDTH), tok(B_WIDTH), tok(D_MODEL), tok(D_MODEL)]
    out_shape += [jax.ShapeDtypeStruct((bsz, seq, D_MODEL), BF16)] * 4
    for keep in keeps:
        out_specs.append(kv_spec(keep))
        kv_shape = (bsz, 2 * A_WIDTH, keep) if kv_feature_major else (bsz, keep, 2 * A_WIDTH)
        out_shape.append(jax.ShapeDtypeStruct(kv_shape, F32))
    if emit_vn_f32:
        out_specs.append(tok(B_WIDTH))
        out_shape.append(jax.ShapeDtypeStruct((bsz, seq, B_WIDTH), F32))

    ins = [x, sh, sc, gain, w_in, ln_g, ln_b]
    in_specs = [tok(D_MODEL), _mod_spec(sh, tm, seq), _mod_spec(sc, tm, seq), _const_spec(gain),
                pl.BlockSpec(w_in.shape, lambda b, s: (0, 0), pipeline_mode=pl.Buffered(1)),
                _const_spec(ln_g), _const_spec(ln_b)]
    if after is not None:
        ins.append(after)
        in_specs.append(pl.BlockSpec(memory_space=pl.ANY))
    scratch = [pltpu.VMEM((2, COL_CHUNK // LANES, tm, LANES), F32)]
    return pl.pallas_call(
        functools.partial(_inproj_kernel, tm=tm, keeps=keeps, dils=dils, emit_vn_f32=emit_vn_f32,
                          has_after=after is not None, kv_feature_major=kv_feature_major),
        grid=(bsz, n_s),
        in_specs=in_specs,
        out_specs=out_specs,
        out_shape=out_shape,
        scratch_shapes=scratch,
        compiler_params=pltpu.CompilerParams(dimension_semantics=("arbitrary", "arbitrary"),
                                             vmem_limit_bytes=VMEM_LIMIT),
        name="inproj",
    )(*ins)


def _attn_kernel(q_ref, kp_ref, kc_ref, vp_ref, vc_ref, bias0_ref, bias_ref, o_ref, lse_ref, *, n_seq, q_blocks):
    lane = lax.broadcasted_iota(jnp.int32, (A_BLOCK, LANES), 1)
    low = lane < HEAD_DIM
    zero = jnp.zeros((), BF16)
    for seq_i, i in [(a, b) for a in range(n_seq) for b in range(q_blocks)]:
        rows = (seq_i, slice(i * A_BLOCK, (i + 1) * A_BLOCK))
        q = q_ref[rows[0], rows[1], :]
        key_rows = slice((i - 1) * A_BLOCK, (i + 1) * A_BLOCK)
        if i == 0:
            k = jnp.concatenate([kp_ref[seq_i], kc_ref[seq_i, :A_BLOCK, :]], axis=0)
            v = jnp.concatenate([vp_ref[seq_i], vc_ref[seq_i, :A_BLOCK, :]], axis=0)
        else:
            k, v = kc_ref[seq_i, key_rows, :], vc_ref[seq_i, key_rows, :]
        b_ref = bias0_ref if i == 0 else bias_ref
        lse_tile = jnp.zeros((A_BLOCK, LANES), F32)
        for j in range(HEADS // 2):
            pair = slice(j * LANES, (j + 1) * LANES)
            qp, kpair, vpair = q[:, pair], k[:, pair], v[:, pair]
            outs = []
            for e in range(2):
                h = 2 * j + e
                qm = jnp.where(low if e == 0 else jnp.logical_not(low), qp, zero)
                s = lax.dot_general(qm, kpair, (((1,), (1,)), ((), ())), preferred_element_type=F32)
                s = s + b_ref[h]
                m = jnp.max(s, axis=-1, keepdims=True)
                p = jnp.exp2(s - m)
                den = jnp.sum(p, axis=-1, keepdims=True)
                o = jnp.dot(p.astype(BF16), vpair, preferred_element_type=F32)
                first = h * LSE_REP
                lse_tile = jnp.where((lane >= first) & (lane < first + LSE_REP // 2), m,
                                     jnp.where((lane >= first + LSE_REP // 2) & (lane < first + LSE_REP), den,
                                               lse_tile))
                outs.append(o)
            o_ref[rows[0], rows[1], pair] = jnp.where(low, outs[0], outs[1]).astype(BF16)
        lse_ref[rows[0], rows[1], :] = lse_tile


def _attn_bias(g):
    _, dil = A_GROUPS[g]
    n = N_GROUPS * HEADS
    e = np.arange(1, n + 1, dtype=np.float32)
    slopes = np.exp2(-8.0 * e / n).astype(np.float32).reshape(N_GROUPS, HEADS)[g]
    qi = np.arange(A_BLOCK)[:, None]
    ki = np.arange(2 * A_BLOCK)[None, :]
    delta = qi + A_BLOCK - ki
    band = (delta >= 0) & (delta <= A_STEPS)
    dist = (delta * dil).astype(np.float32)
    bias = (-slopes[:, None, None] * dist[None] * np.float32(LOG2E)).astype(np.float32)
    out = np.empty((2, HEADS, A_BLOCK, 2 * A_BLOCK), np.float32)
    out[1] = np.where(band[None], bias, NEG)
    out[0] = np.where((band & (ki >= A_BLOCK))[None], bias, NEG)
    return jnp.asarray(out)


def _attn_prompt(qkv, g):
    bsz, dil, steps, _ = qkv.shape
    qb = min(ATTN_BLOCKS_PER_STEP, steps // A_BLOCK)
    ns = min(ATTN_BLOCKS_PER_STEP // qb, dil)
    rows = qb * A_BLOCK
    n_steps = steps // rows
    bias = _attn_bias(g)

    def blk(t, prev):
        if prev:
            return pl.BlockSpec((None, ns, A_BLOCK, A_WIDTH),
                                lambda b, r, n: (b, r, jnp.maximum(n * qb - 1, 0), t))
        return pl.BlockSpec((None, ns, rows, A_WIDTH), lambda b, r, n: (b, r, n, t))

    bias_blk = lambda pick: pl.BlockSpec((None, HEADS, A_BLOCK, 2 * A_BLOCK), lambda b, r, n: (pick(n), 0, 0, 0))
    return pl.pallas_call(
        functools.partial(_attn_kernel, n_seq=ns, q_blocks=qb),
        grid=(bsz, dil // ns, n_steps),
        in_specs=[blk(0, False), blk(1, True), blk(1, False), blk(2, True), blk(2, False),
                  bias_blk(lambda n: jnp.minimum(n, 1)), bias_blk(lambda n: 1)],
        out_specs=[pl.BlockSpec((None, ns, rows, A_WIDTH), lambda b, r, n: (b, r, n, 0)),
                   pl.BlockSpec((None, ns, rows, LANES), lambda b, r, n: (b, r, n, 0))],
        out_shape=[jax.ShapeDtypeStruct((bsz, dil, steps, A_WIDTH), BF16),
                   jax.ShapeDtypeStruct((bsz, dil, steps, LANES), F32)],
        compiler_params=pltpu.CompilerParams(dimension_semantics=("arbitrary",) * 3),
        name=f"attn_prompt_g{g}",
    )(qkv, qkv, qkv, qkv, qkv, bias, bias)


def _attn_sample_kernel(q0_ref, q1_ref, q2_ref, c0_ref, c1_ref, c2_ref, bc0_ref, bc1_ref, bc2_ref,
                        bn0_ref, bn1_ref, bn2_ref, after_ref, o_ref, *, t_new):
    del after_ref
    n_rows = HEADS * t_new
    row = lax.broadcasted_iota(jnp.int32, (n_rows, A_WIDTH), 0)
    lane = lax.broadcasted_iota(jnp.int32, (n_rows, A_WIDTH), 1)
    head_mask = (row // t_new) == (lane // HEAD_DIM)
    pad = jnp.zeros((LANES - t_new, A_WIDTH), F32)
    outs, lses = [], []
    for qkv_ref, c_ref, bc_ref, bn_ref in ((q0_ref, c0_ref, bc0_ref, bn0_ref), (q1_ref, c1_ref, bc1_ref, bn1_ref),
                                           (q2_ref, c2_ref, bc2_ref, bn2_ref)):
        q = qkv_ref[:, :A_WIDTH].astype(F32)
        k_new = qkv_ref[:, A_WIDTH:2 * A_WIDTH].astype(F32)
        v_new = qkv_ref[:, 2 * A_WIDTH:].astype(F32)
        k_new = jnp.concatenate([k_new, pad], axis=0).astype(BF16)
        v_new = jnp.concatenate([v_new, pad], axis=0).astype(BF16)
        q_rows = jnp.where(head_mask, jnp.concatenate([q] * HEADS, axis=0), 0.0).astype(BF16)
        k_buf_t = c_ref[:A_WIDTH, :].astype(BF16)
        v_buf_t = c_ref[A_WIDTH:, :].astype(BF16)
        nt = (((1,), (1,)), ((), ()))
        s_buf = jnp.dot(q_rows, k_buf_t, preferred_element_type=F32) + bc_ref[...]
        s_new = lax.dot_general(q_rows, k_new, nt, preferred_element_type=F32) + bn_ref[...]
        m = jnp.maximum(jnp.max(s_buf, axis=-1, keepdims=True), jnp.max(s_new, axis=-1, keepdims=True))
        p_buf = jnp.exp2(s_buf - m)
        p_new = jnp.exp2(s_new - m)
        den = jnp.sum(p_buf, axis=-1, keepdims=True) + jnp.sum(p_new, axis=-1, keepdims=True)
        o = (lax.dot_general(p_buf.astype(BF16), v_buf_t, nt, preferred_element_type=F32)
             + jnp.dot(p_new.astype(BF16), v_new, preferred_element_type=F32)) / den
        outs.append(o)
        lses.append(m + jnp.log2(den))
    top = jnp.maximum(jnp.maximum(lses[0], lses[1]), lses[2])
    ws = [jnp.exp2(l - top) for l in lses]
    tot = ws[0] + ws[1] + ws[2]
    acc = (ws[0] / tot) * outs[0] + (ws[1] / tot) * outs[1] + (ws[2] / tot) * outs[2]
    acc = jnp.where(head_mask, acc, 0.0).reshape(HEADS, t_new, A_WIDTH)
    o_ref[...] = jnp.sum(acc, axis=0).astype(BF16)


def _sample_bias(g, t_new, buf):
    _, dil = A_GROUPS[g]
    n = N_GROUPS * HEADS
    e = np.arange(1, n + 1, dtype=np.float32)
    slopes = np.exp2(-8.0 * e / n).astype(np.float32).reshape(N_GROUPS, HEADS)[g]
    t = np.arange(t_new)[:, None]
    idx = np.concatenate([np.arange(buf), buf + np.arange(LANES)])[None, :]
    dist = buf + t - idx
    valid = (dist >= 0) & (dist % dil == 0) & (dist <= A_STEPS * dil) & (idx < buf + t_new)
    bias = -slopes[:, None, None] * dist.astype(np.float32)[None] * np.float32(LOG2E)
    bias = np.where(valid[None], bias, NEG).astype(np.float32).reshape(HEADS * t_new, buf + LANES)
    return jnp.asarray(bias[:, :buf]), jnp.asarray(bias[:, buf:])


def _attn_sample(qkvs, caches, *, after):
    bsz, t_new, _ = qkvs[0].shape
    cache_v = [jnp.transpose(c, (0, 2, 3, 4, 1)).reshape(bsz, 2 * A_WIDTH, c.shape[1]) for c in caches]
    biases = [_sample_bias(g, t_new, cache_v[g].shape[2]) for g in range(N_GROUPS)]
    bcs = [b[0] for b in biases]
    bns = [b[1] for b in biases]
    return pl.pallas_call(
        functools.partial(_attn_sample_kernel, t_new=t_new),
        grid=(bsz,),
        in_specs=[pl.BlockSpec((None, t_new, 3 * A_WIDTH), lambda b: (b, 0, 0))] * N_GROUPS
                 + [pl.BlockSpec((None, 2 * A_WIDTH, c.shape[2]), lambda b: (b, 0, 0)) for c in cache_v]
                 + [_const_spec(b) for b in bcs] + [_const_spec(b) for b in bns]
                 + [pl.BlockSpec(memory_space=pl.ANY)],
        out_specs=pl.BlockSpec((None, t_new, A_WIDTH), lambda b: (b, 0, 0)),
        out_shape=jax.ShapeDtypeStruct((bsz, t_new, A_WIDTH), BF16),
        compiler_params=pltpu.CompilerParams(dimension_semantics=("arbitrary",),
                                             vmem_limit_bytes=VMEM_LIMIT),
        name="attn_sample",
    )(*qkvs, *cache_v, *bcs, *bns, after)


def _first_max4(v):
    top = jnp.maximum(jnp.maximum(v[0], v[1]), jnp.maximum(v[2], v[3]))
    idx = jnp.where(v[0] == top, 0.0, jnp.where(v[1] == top, 1.0, jnp.where(v[2] == top, 2.0, 3.0)))
    return top, idx


def _route(lt):
    row = lambda k: lt[k:k + 1, :]
    g = [row(k) for k in range(MOE_GROUPS)]
    gmax, g_idx = _first_max4(g)
    g_prob = 1.0 / (jnp.exp(g[0] - gmax) + jnp.exp(g[1] - gmax) + jnp.exp(g[2] - gmax) + jnp.exp(g[3] - gmax))
    cand = []
    for k in range(MOE_PER_GROUP):
        c = row(ROUTE_OFF + MOE_PER_GROUP * (MOE_GROUPS - 1) + k)
        for gi in range(MOE_GROUPS - 2, -1, -1):
            c = jnp.where(g_idx == float(gi), row(ROUTE_OFF + MOE_PER_GROUP * gi + k), c)
        cand.append(c)
    e1, i1 = _first_max4(cand)
    rest = [jnp.where(i1 == float(k), -jnp.inf, cand[k]) for k in range(MOE_PER_GROUP)]
    e2, i2 = _first_max4(rest)
    t = jnp.exp(e2 - e1)
    w1 = 1.0 / (1.0 + t)
    w2 = t / (1.0 + t)
    return MOE_PER_GROUP * g_idx + i1, MOE_PER_GROUP * g_idx + i2, w1 * g_prob, w2 * g_prob


def _pack_bf16_pairs(x):
    q = x.shape[1] // 4
    bits = lax.bitcast_convert_type(x.astype(BF16).astype(F32), jnp.int32)
    pack = lambda hi, lo: hi | lax.shift_right_logical(lo, 16)
    return pack(bits[:, :q], bits[:, 2 * q:3 * q]), pack(bits[:, q:2 * q], bits[:, 3 * q:])


def _unpack_bf16_pairs(pa, pb):
    hi = lambda p: lax.bitcast_convert_type(p & jnp.int32(-65536), F32)
    lo = lambda p: lax.bitcast_convert_type(lax.shift_left(p, 16), F32)
    return jnp.concatenate([hi(pa), hi(pb), lo(pa), lo(pb)], axis=1)


def _mix_kernel(*refs, tm, dils, routed, span):
    n_attn = len(dils)
    x_ref = refs[0]
    if n_attn > 1:
        o_refs = refs[1:1 + n_attn]
        l_refs = refs[1 + n_attn:1 + 2 * n_attn]
        exp_ref = refs[1 + 2 * n_attn]
        pos = 2 + 2 * n_attn
    else:
        o_refs = refs[1:2]
        pos = 2
    (u_ref, vn_ref, ga_ref, gb_ref, sw_ref, sbt_ref, wpa_ref, wpb_ref, wo_ref,
     gt1_ref, sh2_ref, sc2_ref, n2g_ref, wrh_ref, wrl_ref, br_ref) = refs[pos:pos + 16]
    n_out = 6 if routed else 3
    out_refs = refs[pos + 16:pos + 16 + n_out]
    scr = list(refs[pos + 16 + n_out:])
    x1_ref = out_refs[0]
    ob_scr = scr.pop(0)

    def to_positions(ref, scr, dil):
        if dil == 1:
            return ref[0].astype(F32)
        n_slab = scr.shape[0]
        for r in range(dil):
            rows = ref[r].astype(F32)
            for j in range(n_slab):
                scr[j, pl.ds(r, tm // dil, stride=dil), :] = rows[:, j * LANES:(j + 1) * LANES]
        return jnp.concatenate([scr[j] for j in range(n_slab)], axis=1)

    if n_attn > 1:
        o_scr, l_scr = scr[0], scr[1]
        stats = [to_positions(l_refs[g], l_scr.at[g], dils[g]) for g in range(n_attn)]
        lane = lax.broadcasted_iota(jnp.int32, stats[0].shape, 1)
        is_max = (lane & (LSE_REP - 1)) < LSE_REP // 2
        half = LSE_REP // 2
        maxes = [jnp.where(is_max, s, pltpu.roll(s, half, axis=1)) for s in stats]
        dens = [jnp.where(is_max, pltpu.roll(s, LANES - half, axis=1), s) for s in stats]
        lses = [m + jnp.log2(d) for m, d in zip(maxes, dens)]
        top = functools.reduce(jnp.maximum, lses)
        tot = functools.reduce(lambda a, b: a + b, [jnp.exp2(l - top) for l in lses])
        o_a = None
        for g in range(n_attn):
            w = jnp.exp2(maxes[g] - top) / tot
            hi = w.astype(BF16)
            lo = (w - hi.astype(F32)).astype(BF16)
            w_exp = jnp.dot(jnp.concatenate([hi, lo], axis=1), exp_ref[...], preferred_element_type=F32)
            term = w_exp * to_positions(o_refs[g], o_scr.at[g], dils[g])
            o_a = term if o_a is None else o_a + term
        o_a = o_a.astype(BF16)
    else:
        o_a = o_refs[0][...]

    r_i = lax.broadcasted_iota(jnp.int32, (span, span), 0)
    c_i = lax.broadcasted_iota(jnp.int32, (span, span), 1)
    tril = r_i >= c_i
    for gi in range(B_GROUPS):
        gcols = slice(gi * LANES, (gi + 1) * LANES)
        b_s = sbt_ref[:span, gi:gi + 1]
        if span == B_CHUNK:
            w_s = jnp.where(tril, sw_ref[gi], 0.0).astype(BF16)
            n_chunk = tm // span
            vn_wide = jnp.concatenate([vn_ref[c * span:(c + 1) * span, gcols] for c in range(n_chunk)], axis=1)
            mixed = jnp.dot(w_s, vn_wide, preferred_element_type=F32) + b_s
            for c in range(n_chunk):
                rows = slice(c * span, (c + 1) * span)
                ob_scr[rows, gcols] = (u_ref[rows, gcols].astype(F32)
                                       * mixed[:, c * LANES:(c + 1) * LANES]).astype(BF16)
        else:
            w_s = jnp.where(tril, sw_ref[gi, :span, :span], 0.0)
            vn3 = vn_ref[:, gcols].astype(F32).reshape(tm // span, span, LANES)
            mixed = jnp.broadcast_to(b_s, (span, LANES))[None]
            for s in range(span):
                w_col = jnp.broadcast_to(w_s[:, s:s + 1], (span, LANES))[None]
                mixed = mixed + w_col * jnp.broadcast_to(vn3[:, s:s + 1, :], vn3.shape)
            u3 = u_ref[:, gcols].astype(F32).reshape(tm // span, span, LANES)
            ob_scr[:, gcols] = (u3 * mixed).reshape(tm, LANES).astype(BF16)

    sub = min(tm, MIX_SUB_ROWS)
    if routed:
        carry = scr[-1]
        first_step = (pl.program_id(0) == 0) & (pl.program_id(1) == 0)
        seen = jnp.where(first_step, 0.0, carry[:, 0:1])
        r_i = lax.broadcasted_iota(jnp.int32, (sub, sub), 0)
        c_i = lax.broadcasted_iota(jnp.int32, (sub, sub), 1)
        earlier = jnp.where(r_i < c_i, 1.0, 0.0).astype(BF16)
        e_iota = lax.broadcasted_iota(jnp.int32, (MOE_EXPERTS, sub), 0).astype(F32)
    nt = (((1,), (1,)), ((), ()))
    for r0 in range(0, tm, sub):
        rs = slice(r0, r0 + sub)
        mod = lambda ref: ref[...] if ref.shape[0] == 1 else _mod_rows(ref, tm)[rs, :]
        pa = jnp.dot(o_a[rs], wpa_ref[...], preferred_element_type=F32)
        pb = jnp.dot(ob_scr[rs, :], wpb_ref[...], preferred_element_type=F32)
        mix = (ga_ref[rs, :].astype(F32) * pa + gb_ref[rs, :].astype(F32) * pb).astype(BF16)
        x1 = x_ref[rs, :] + mod(gt1_ref) * jnp.dot(mix, wo_ref[...], preferred_element_type=F32)
        x1_ref[rs, :] = x1
        h2 = _rmsnorm_mod(x1, n2g_ref[...], mod(sh2_ref), mod(sc2_ref))
        h_hi = h2.astype(BF16)
        h_lo = (h2 - h_hi.astype(F32)).astype(BF16)
        lt = (lax.dot_general(wrh_ref[...], h_hi, nt, preferred_element_type=F32)
              + lax.dot_general(wrh_ref[...], h_lo, nt, preferred_element_type=F32)
              + lax.dot_general(wrl_ref[...], h_hi, nt, preferred_element_type=F32)) + br_ref[...]
        ex1, ex2, w1, w2 = _route(lt)
        if not routed:
            _, h2_ref, comb_ref = out_refs
            h2_ref[rs, :] = h_hi
            e_lane = lax.broadcasted_iota(jnp.int32, (LANES, sub), 0).astype(F32) - float(ROUTE_OFF)
            comb_t = jnp.where(e_lane == ex1, w1, jnp.where(e_lane == ex2, w2, 0.0))
            comb_ref[rs, :] = jnp.transpose(comb_t)
            continue

        _, hpa_ref, hpb_ref, route_ref, wcol_ref, cnt_ref = out_refs
        hpa_ref[rs, :], hpb_ref[rs, :] = _pack_bf16_pairs(h2)
        hot1, hot2 = e_iota == ex1, e_iota == ex2
        onehot = jnp.where(hot1 | hot2, 1.0, 0.0)
        rank = jnp.dot(onehot.astype(BF16), earlier, preferred_element_type=F32) + seen
        r1 = jnp.sum(jnp.where(hot1, rank, 0.0), axis=0, keepdims=True)
        r2 = jnp.sum(jnp.where(hot2, rank, 0.0), axis=0, keepdims=True)
        seen = seen + jnp.sum(onehot, axis=1, keepdims=True)
        fields = jnp.concatenate([ex1, ex2, w1, w2, r1, r2, jnp.zeros((2, sub), F32)], axis=0)
        route_ref[:, rs] = fields
        wcol_ref[rs, :] = jnp.transpose(jnp.concatenate([fields, jnp.zeros((LANES - 8, sub), F32)], axis=0))
    if routed:
        carry[...] = jnp.broadcast_to(seen, carry.shape)
        cnt_ref[...] = jnp.broadcast_to(seen, cnt_ref.shape)


def _expand_matrix():
    e = np.zeros((LANES, A_WIDTH), np.float32)
    for h in range(HEADS):
        e[h * LSE_REP, h * HEAD_DIM:(h + 1) * HEAD_DIM] = 1.0
    return jnp.asarray(np.concatenate([e, e], axis=0), dtype=BF16)


def _mix(x, attn_outs, attn_lses, u, vn, ga, gb, sgu_w, sgu_bt, w_pa, w_pb, w_o,
         gt1, sh2, sc2, n2g, w_r_hi, w_r_lo, b_r, *, tm, routed, span):
    bsz, seq, _ = x.shape
    n_attn = len(attn_outs)
    tok = lambda width: pl.BlockSpec((None, tm, width), lambda b, s: (b, s, 0))
    ins = [x] + list(attn_outs)
    scratch = [pltpu.VMEM((tm, B_WIDTH), BF16)]
    if n_attn > 1:
        dils = tuple(o.shape[1] for o in attn_outs)
        res = lambda dil, width: pl.BlockSpec((None, dil, tm // dil, width), lambda b, s: (b, 0, s, 0))
        e2 = _expand_matrix()
        ins += list(attn_lses) + [e2]
        specs = ([tok(D_MODEL)] + [res(d, A_WIDTH) for d in dils] + [res(d, LANES) for d in dils]
                 + [_const_spec(e2)])
        scratch += [pltpu.VMEM((n_attn, A_WIDTH // LANES, tm, LANES), F32), pltpu.VMEM((n_attn, 1, tm, LANES), F32)]
    else:
        dils = (1,)
        specs = [tok(D_MODEL), tok(A_WIDTH)]
    ins += [u, vn, ga, gb, sgu_w, sgu_bt, w_pa, w_pb, w_o, gt1, sh2, sc2, n2g, w_r_hi, w_r_lo, b_r]
    specs += [tok(B_WIDTH), tok(B_WIDTH), tok(D_MODEL), tok(D_MODEL),
              _const_spec(sgu_w), _const_spec(sgu_bt), _const_spec(w_pa), _const_spec(w_pb), _const_spec(w_o),
              _mod_spec(gt1, tm, seq), _mod_spec(sh2, tm, seq), _mod_spec(sc2, tm, seq), _const_spec(n2g),
              _const_spec(w_r_hi), _const_spec(w_r_lo), _const_spec(b_r)]
    if routed:
        quarter = D_MODEL // 4
        out_specs = [tok(D_MODEL), tok(quarter), tok(quarter),
                     pl.BlockSpec((None, 8, tm), lambda b, s: (b, 0, s)), tok(LANES),
                     pl.BlockSpec((MOE_EXPERTS, LANES), lambda b, s: (0, 0))]
        out_shape = [jax.ShapeDtypeStruct((bsz, seq, D_MODEL), F32),
                     jax.ShapeDtypeStruct((bsz, seq, quarter), jnp.int32),
                     jax.ShapeDtypeStruct((bsz, seq, quarter), jnp.int32),
                     jax.ShapeDtypeStruct((bsz, 8, seq), F32),
                     jax.ShapeDtypeStruct((bsz, seq, LANES), F32),
                     jax.ShapeDtypeStruct((MOE_EXPERTS, LANES), F32)]
        scratch.append(pltpu.VMEM((MOE_EXPERTS, LANES), F32))
    else:
        out_specs = [tok(D_MODEL), tok(D_MODEL), tok(LANES)]
        out_shape = [jax.ShapeDtypeStruct((bsz, seq, D_MODEL), F32),
                     jax.ShapeDtypeStruct((bsz, seq, D_MODEL), BF16),
                     jax.ShapeDtypeStruct((bsz, seq, LANES), F32)]
    return pl.pallas_call(
        functools.partial(_mix_kernel, tm=tm, dils=dils, routed=routed, span=span),
        grid=(bsz, seq // tm),
        in_specs=specs,
        out_specs=out_specs,
        out_shape=out_shape,
        scratch_shapes=scratch,
        compiler_params=pltpu.CompilerParams(dimension_semantics=("arbitrary", "arbitrary"),
                                             vmem_limit_bytes=VMEM_LIMIT),
        name="mix",
    )(*ins)


def _moe_kernel(h_ref, comb_ref, wg_ref, wu_ref, wd_ref, x1_ref, gt2_ref, shf_ref, scf_ref, nfg_ref,
                y_ref, acc_ref):
    e = pl.program_id(2)

    @pl.when(e == 0)
    def _():
        acc_ref[...] = jnp.zeros_like(acc_ref)

    h = h_ref[...]
    hg = jnp.dot(h, wg_ref[...], preferred_element_type=F32)
    hu = jnp.dot(h, wu_ref[...], preferred_element_type=F32)
    comb = comb_ref[...]
    lane = lax.broadcasted_iota(jnp.int32, comb.shape, 1)
    cw = jnp.sum(jnp.where(lane == e + ROUTE_OFF, comb, 0.0), axis=-1, keepdims=True)
    act = (hg * _sigmoid(hg) * hu) * cw
    acc_ref[...] += jnp.dot(act.astype(BF16), wd_ref[...], preferred_element_type=F32)

    @pl.when(e == MOE_EXPERTS - 1)
    def _():
        rows = acc_ref.shape[0]
        x2 = x1_ref[...] + _mod_rows(gt2_ref, rows) * acc_ref[...]
        y_ref[...] = _rmsnorm_mod(x2, nfg_ref[...], _mod_rows(shf_ref, rows), _mod_rows(scf_ref, rows))


def _moe(h2, comb, w_gate, w_up, w_down, x1, gt2, shf, scf, nfg, *, tm):
    bsz, seq, _ = h2.shape
    tok = lambda width: pl.BlockSpec((None, tm, width), lambda b, s, e: (b, s, 0))
    return pl.pallas_call(
        _moe_kernel,
        grid=(bsz, seq // tm, MOE_EXPERTS),
        in_specs=[tok(D_MODEL), tok(LANES),
                  pl.BlockSpec((None, D_MODEL, MOE_HIDDEN), lambda b, s, e: (e, 0, 0)),
                  pl.BlockSpec((None, D_MODEL, MOE_HIDDEN), lambda b, s, e: (e, 0, 0)),
                  pl.BlockSpec((None, MOE_HIDDEN, D_MODEL), lambda b, s, e: (e, 0, 0)),
                  tok(D_MODEL), _mod_spec(gt2, tm, seq), _mod_spec(shf, tm, seq), _mod_spec(scf, tm, seq),
                  _const_spec(nfg)],
        out_specs=tok(D_MODEL),
        out_shape=jax.ShapeDtypeStruct((bsz, seq, D_MODEL), F32),
        scratch_shapes=[pltpu.VMEM((tm, D_MODEL), F32)],
        compiler_params=pltpu.CompilerParams(dimension_semantics=("arbitrary",) * 3,
                                             vmem_limit_bytes=VMEM_LIMIT),
        name="moe",
    )(h2, comb, w_gate, w_up, w_down, x1, gt2, shf, scf, nfg)


def _sc_mesh():
    return plsc.VectorSubcoreMesh(core_axis_name="c", subcore_axis_name="s")


def _sc_scatter_rows(rows, pos1, pos2, n_out):
    n, width = rows.shape
    steps = n // SC_WINDOW // SC_CORES

    @pl.kernel(out_type=jax.ShapeDtypeStruct((n_out, width), rows.dtype), mesh=_sc_mesh(), scratch_types=[])
    def scatter(rows_hbm, p1_hbm, p2_hbm, out_hbm):
        def body(x_vmem, i1_vmem, i2_vmem):
            pltpu.sync_copy(x_vmem, out_hbm.at[i1_vmem.at[0]])
            pltpu.sync_copy(x_vmem, out_hbm.at[i2_vmem.at[0]])

        pltpu.emit_pipeline(
            body, grid=(SC_CORES, steps),
            in_specs=[pl.BlockSpec((SC_WINDOW, width), lambda c, i: (c * steps + i, 0)),
                      pl.BlockSpec((1, SC_WINDOW), lambda c, i: (0, c * steps + i)),
                      pl.BlockSpec((1, SC_WINDOW), lambda c, i: (0, c * steps + i))],
            out_specs=[],
            core_axis_name=("c", "s"),
            dimension_semantics=(pltpu.PARALLEL, pltpu.PARALLEL),
        )(rows_hbm, p1_hbm, p2_hbm)

    return scatter(rows, pos1, pos2)


def _sc_gather_rows(table, idx):
    m, width = idx.shape[1], table.shape[1]
    steps = m // SC_WINDOW // SC_CORES

    @pl.kernel(out_type=jax.ShapeDtypeStruct((m, width), table.dtype), mesh=_sc_mesh(), scratch_types=[])
    def gather(table_hbm, idx_hbm, out_hbm):
        def body(i_vmem, o_vmem):
            pltpu.sync_copy(table_hbm.at[i_vmem.at[0]], o_vmem)

        pltpu.emit_pipeline(
            body, grid=(SC_CORES, steps),
            in_specs=[pl.BlockSpec((1, SC_WINDOW), lambda c, i: (0, c * steps + i))],
            out_specs=[pl.BlockSpec((SC_WINDOW, width), lambda c, i: (c * steps + i, 0))],
            core_axis_name=("c", "s"),
            dimension_semantics=(pltpu.PARALLEL, pltpu.PARALLEL),
        )(idx_hbm, out_hbm)

    return gather(table, idx)


def _ffn_kernel(te_ref, nt_ref, xa_ref, xb_ref, wg_ref, wu_ref, wd_ref, *rest):
    oa_ref, ob_ref, wg_s, wu_s, wd_s = rest[-5:]
    t = pl.program_id(0)
    live = t < nt_ref[0]

    @pl.when(live & ((t == 0) | (te_ref[t] != te_ref[jnp.maximum(t - 1, 0)])))
    def _():
        wg_s[...] = wg_ref[...].astype(BF16)
        wu_s[...] = wu_ref[...].astype(BF16)
        wd_s[...] = wd_ref[...].astype(BF16)

    @pl.when(live)
    def _():
        x = _unpack_bf16_pairs(xa_ref[...], xb_ref[...]).astype(BF16)
        hg = jnp.dot(x, wg_s[...], preferred_element_type=F32)
        hu = jnp.dot(x, wu_s[...], preferred_element_type=F32)
        act = (hg * _sigmoid(hg) * hu).astype(BF16)
        oa_ref[...], ob_ref[...] = _pack_bf16_pairs(jnp.dot(act, wd_s[...], preferred_element_type=F32))


def _ffn(xa, xb, tile_expert, n_tiles, w_gate, w_up, w_down, *, after):
    n_rows, quarter = xa.shape
    rows = pl.BlockSpec((MOE_TILE, quarter), lambda t, te, nt: (jnp.minimum(t, nt[0] - 1), 0))
    w_in = pl.BlockSpec((None, D_MODEL, MOE_HIDDEN), lambda t, te, nt: (te[t], 0, 0))
    w_out = pl.BlockSpec((None, MOE_HIDDEN, D_MODEL), lambda t, te, nt: (te[t], 0, 0))
    return pl.pallas_call(
        _ffn_kernel,
        grid_spec=pltpu.PrefetchScalarGridSpec(
            num_scalar_prefetch=2,
            grid=(n_rows // MOE_TILE,),
            in_specs=[rows, rows, w_in, w_in, w_out] + [pl.BlockSpec(memory_space=pl.ANY)] * len(after),
            out_specs=[rows, rows, w_in, w_in, w_out]),
        out_shape=[jax.ShapeDtypeStruct((n_rows, quarter), jnp.int32)] * 2
                  + [jax.ShapeDtypeStruct(w_gate.shape, BF16), jax.ShapeDtypeStruct(w_up.shape, BF16),
                     jax.ShapeDtypeStruct(w_down.shape, BF16)],
        compiler_params=pltpu.CompilerParams(dimension_semantics=("arbitrary",), vmem_limit_bytes=VMEM_LIMIT),
        name="moe_ffn",
    )(tile_expert, n_tiles, xa, xb, w_gate, w_up, w_down, *after)


def _final_kernel(x1_ref, g1a_ref, g1b_ref, g2a_ref, g2b_ref, route_ref, gt2_ref, shf_ref, scf_ref, nfg_ref, y_ref):
    o1 = _unpack_bf16_pairs(g1a_ref[...], g1b_ref[...])
    o2 = _unpack_bf16_pairs(g2a_ref[...], g2b_ref[...])
    route = route_ref[...]
    moe = route[:, 2:3] * o1 + route[:, 3:4] * o2
    x2 = x1_ref[...] + gt2_ref[...] * moe
    y_ref[...] = _rmsnorm_mod(x2, nfg_ref[...], shf_ref[...], scf_ref[...])


def _final(x1, ga, gb, route, gt2, shf, scf, nfg, *, tm):
    bsz, seq, _ = x1.shape
    n_s = seq // tm
    n_blk = bsz * n_s
    quarter = ga.shape[1]
    tok = lambda width: pl.BlockSpec((None, tm, width), lambda b, s: (b, s, 0))
    first = pl.BlockSpec((tm, quarter), lambda b, s: (b * n_s + s, 0))
    second = pl.BlockSpec((tm, quarter), lambda b, s: (n_blk + b * n_s + s, 0))
    return pl.pallas_call(
        _final_kernel,
        grid=(bsz, n_s),
        in_specs=[tok(D_MODEL), first, first, second, second, tok(LANES),
                  _mod_spec(gt2, tm, seq), _mod_spec(shf, tm, seq), _mod_spec(scf, tm, seq), _const_spec(nfg)],
        out_specs=tok(D_MODEL),
        out_shape=jax.ShapeDtypeStruct((bsz, seq, D_MODEL), F32),
        compiler_params=pltpu.CompilerParams(dimension_semantics=("arbitrary", "arbitrary")),
        name="moe_final",
    )(x1, ga, gb, ga, gb, route, gt2, shf, scf, nfg)


def _positions_kernel(base_ref, route_ref, p1_ref, p2_ref):
    rows = route_ref[...]
    for e_row, r_row, out_ref in ((0, 4, p1_ref), (1, 5, p2_ref)):
        expert = rows[e_row:e_row + 1, :]
        start = jnp.zeros(expert.shape, jnp.int32)
        for e in range(MOE_EXPERTS):
            start = jnp.where(expert == float(e), base_ref[e], start)
        out_ref[...] = start + rows[r_row:r_row + 1, :].astype(jnp.int32)


def _positions(route_rows, base):
    bsz, _, seq = route_rows.shape
    out = pl.BlockSpec((None, 1, seq), lambda b, base: (b, 0, 0))
    return pl.pallas_call(
        _positions_kernel,
        grid_spec=pltpu.PrefetchScalarGridSpec(
            num_scalar_prefetch=1, grid=(bsz,),
            in_specs=[pl.BlockSpec((None, 8, seq), lambda b, base: (b, 0, 0))],
            out_specs=[out, out]),
        out_shape=[jax.ShapeDtypeStruct((bsz, 1, seq), jnp.int32)] * 2,
        compiler_params=pltpu.CompilerParams(dimension_semantics=("arbitrary",)),
        name="moe_positions",
    )(base, route_rows)


def _moe_dispatch(hpa, hpb, route_rows, counts):
    bsz, seq, quarter = hpa.shape
    n_tok = bsz * seq
    t_max = 2 * n_tok // MOE_TILE + MOE_EXPERTS
    cnt = counts[:, 0].astype(jnp.int32)
    tiles_e = jnp.maximum((cnt + MOE_TILE - 1) // MOE_TILE, 1)
    tile_end = jnp.cumsum(tiles_e)
    n_tiles = tile_end[-1:]
    base = (tile_end - tiles_e) * MOE_TILE
    t_ids = jnp.minimum(jnp.arange(t_max, dtype=jnp.int32), n_tiles[0] - 1)
    tile_expert = jnp.sum((t_ids[:, None] >= tile_end[None, :]).astype(jnp.int32), axis=1)
    pos1, pos2 = (p.reshape(1, n_tok) for p in _positions(route_rows, base))
    n_rows = t_max * MOE_TILE
    xa = _sc_scatter_rows(hpa.reshape(n_tok, quarter), pos1, pos2, n_rows)
    xb = _sc_scatter_rows(hpb.reshape(n_tok, quarter), pos1, pos2, n_rows)
    return xa, xb, jnp.concatenate([pos1, pos2], axis=1), tile_expert, n_tiles


def kernel(x_prompt, x_sample, cache_kv_w128, cache_kv_w512, cache_kv_w2048, c_prompt, c_sample,
           w_ada, b_ada, norm1_g, norm2_g, w_in, sgu_ln_g, sgu_ln_b, sgu_w, sgu_b, w_pa, w_pb, w_o,
           w_route_group, b_route_group, w_route_expert, b_route_expert, w_gate, w_up, w_down,
           normf_g, w_ada_final, b_ada_final):
    depth = w_ada.shape[0]
    assert depth == 1
    l = 0
    bp, seq, _ = x_prompt.shape
    bs, t_new, _ = x_sample.shape
    n_samp = bs * t_new

    c_all = jnp.concatenate([c_prompt, c_sample], axis=0)
    pad_rows = (-c_all.shape[0]) % 8
    c_all = jnp.pad(c_all, ((0, pad_rows), (0, 0)))
    mod = _adaln(c_all, w_ada[l], b_ada[l])

    def split_mods(m, n, lo, hi, per_token):
        parts = jnp.split(m[lo:hi], n, axis=-1)
        if per_token:
            return [p.reshape(1, hi - lo, D_MODEL) for p in parts]
        return [p.reshape(hi - lo, 1, D_MODEL) for p in parts]

    mods_p = split_mods(mod, 6, 0, bp, False)
    mods_s = split_mods(mod, 6, bp, bp + bs, True)

    row = lambda v: v.reshape(1, -1)
    w_in_b = w_in[l].astype(BF16)
    w_pa_b, w_pb_b, w_o_b = w_pa[l].astype(BF16), w_pb[l].astype(BF16), w_o[l].astype(BF16)
    w_re = jnp.transpose(w_route_expert[l], (0, 2, 1)).reshape(MOE_EXPERTS, D_MODEL)
    w_r = jnp.pad(jnp.concatenate([jnp.transpose(w_route_group[l]), w_re], axis=0),
                  ((0, ROUTE_ROWS - MOE_GROUPS - MOE_EXPERTS), (0, 0)))
    w_r_hi = w_r.astype(BF16)
    w_r_lo = (w_r - w_r_hi.astype(F32)).astype(BF16)
    b_r = jnp.pad(jnp.concatenate([b_route_group[l], b_route_expert[l].reshape(-1)]),
                  (0, ROUTE_ROWS - MOE_GROUPS - MOE_EXPERTS)).reshape(ROUTE_ROWS, 1)

    def inproj(x, mods, keeps, dils, tm, emit_vn_f32, kv_feature_major, after=None):
        return _inproj(x, mods[0], mods[1], row(norm1_g[l]), w_in_b, row(sgu_ln_g[l]), row(sgu_ln_b[l]),
                       tm=tm, keeps=keeps, dils=dils, emit_vn_f32=emit_vn_f32,
                       kv_feature_major=kv_feature_major, after=after)

    sgu_bt = jnp.transpose(sgu_b[l])

    def mix(x, mods, pin, attn_outs, attn_lses, tm, routed, span):
        u, vn, ga, gb = pin[3:7]
        return _mix(x, attn_outs, attn_lses, u, vn, ga, gb, sgu_w[l], sgu_bt,
                    w_pa_b, w_pb_b, w_o_b, mods[2], mods[3], mods[4], row(norm2_g[l]), w_r_hi, w_r_lo, b_r,
                    tm=tm, routed=routed, span=span)

    tm_p = 512
    keeps_p = tuple(min(win, seq) for win, _ in A_GROUPS)
    dils_p = tuple(dil for _, dil in A_GROUPS)
    pin = inproj(x_prompt, mods_p, keeps_p, dils_p, 512, False, True)
    attn_p = [_attn_prompt(pin[g], g) for g in range(N_GROUPS)]
    x1, hpa, hpb, route_rows, route, counts = mix(x_prompt, mods_p, pin, [r[0] for r in attn_p],
                                                  [r[1] for r in attn_p], tm_p, True, B_CHUNK)

    xs = x_sample.reshape(1, n_samp, D_MODEL)
    sin = inproj(xs, mods_s, (n_samp,) * N_GROUPS, (1,) * N_GROUPS, n_samp, True, False, after=counts)
    mod_f = _adaln(c_all, w_ada_final, b_ada_final, after=counts)
    fin_p = split_mods(mod_f, 2, 0, bp, False)
    fin_s = split_mods(mod_f, 2, bp, bp + bs, True)

    xa, xb, pos, tile_expert, n_tiles = _moe_dispatch(hpa, hpb, route_rows, counts)
    kv_s_out = [kv.reshape(depth, bs, -1, 2, HEADS, HEAD_DIM) for kv in sin[7:10]]
    oa, ob, w_gate16, w_up16, w_down16 = _ffn(xa, xb, tile_expert, n_tiles, w_gate[l], w_up[l], w_down[l],
                                              after=(sin[0], fin_s[0], *kv_s_out))

    caches = (cache_kv_w128[l], cache_kv_w512[l], cache_kv_w2048[l])
    o_s = _attn_sample([q.reshape(bs, t_new, 3 * A_WIDTH) for q in sin[:3]], caches, after=oa)
    x1_s, h2_s, comb_s = mix(xs, mods_s, sin, [o_s.reshape(1, n_samp, A_WIDTH)], None, n_samp, False,
                             min(t_new, B_CHUNK))
    y_s = _moe(h2_s, comb_s, w_gate16, w_up16, w_down16, x1_s, mods_s[5], fin_s[0], fin_s[1], row(normf_g),
               tm=n_samp)

    ga_rows = _sc_gather_rows(oa, pos)
    gb_rows = _sc_gather_rows(ob, pos)
    y_p = _final(x1, ga_rows, gb_rows, route, mods_p[5], fin_p[0], fin_p[1], row(normf_g), tm=2 * tm_p)
    kv_p, extra = pin[7:10], sin[10:]

    def kv_out_t(a, b):
        a = a.reshape(b, 2, HEADS, HEAD_DIM, a.shape[-1])
        return jnp.transpose(a, (0, 4, 1, 2, 3)).reshape(depth, b, -1, 2, HEADS, HEAD_DIM)

    return (y_p, y_s.reshape(bs, t_new, D_MODEL),
            kv_out_t(kv_p[0], bp), kv_out_t(kv_p[1], bp), kv_out_t(kv_p[2], bp),
            kv_s_out[0], kv_s_out[1], kv_s_out[2],
            extra[0].reshape(depth, bs, t_new, B_WIDTH))
```

```python
import functools

import numpy as np
import jax
import jax.numpy as jnp
from jax import lax
from jax.experimental import pallas as pl
from jax.experimental.pallas import tpu as pltpu
from jax.experimental.pallas import tpu_sc as plsc

F32 = jnp.float32
BF16 = jnp.bfloat16

D_MODEL = 1024
A_GROUPS = ((128, 1), (512, 4), (2048, 16))
N_GROUPS = 3
HEADS = 8
HEAD_DIM = 64
A_WIDTH = HEADS * HEAD_DIM
A_STEPS = 128
A_BLOCK = 128
B_WIDTH = 1024
B_GROUPS = 8
B_CHUNK = 128
MOE_GROUPS = 4
MOE_PER_GROUP = 4
MOE_EXPERTS = 16
MOE_HIDDEN = 512
EPS = 1e-6
N_QKV = 3 * N_GROUPS * A_WIDTH
IN_COLS = N_QKV + 2 * B_WIDTH + 2 * D_MODEL
COL_CHUNK = 512
LANES = 128
LSE_REP = LANES // HEADS
ROUTE_OFF = MOE_GROUPS
ROUTE_ROWS = 32
NEG = -1e30
LOG2E = 1.4426950408889634
VMEM_LIMIT = 56 * 1024 * 1024
ATTN_BLOCKS_PER_STEP = 16
MIX_SUB_ROWS = 512
MOE_TILE = 512
SC_WINDOW = 128
SC_CORES = 2


def _sigmoid(x):
    return 1.0 / (1.0 + jnp.exp(-x))


def _gelu_tanh(x):
    return x * (0.5 * (1.0 + jnp.tanh(0.7978845608028654 * (x + 0.044715 * (x * x * x)))))


def _rmsnorm_mod(x, gain, shift, scale):
    y = x * lax.rsqrt(jnp.mean(x * x, axis=-1, keepdims=True) + EPS)
    return y * gain * (1.0 + scale) + shift


def _mod_spec(mod, tm, seq):
    if mod.shape[1] == 1:
        return pl.BlockSpec((None, 1, D_MODEL), lambda b, s, *_: (b, 0, 0))
    return pl.BlockSpec((None, tm * mod.shape[1] // seq, D_MODEL), lambda b, s, *_: (b, s, 0))


def _mod_rows(ref, rows):
    m = ref[...]
    if m.shape[0] in (1, rows):
        return m
    rep = rows // m.shape[0]
    return jnp.broadcast_to(m[:, None, :], (m.shape[0], rep, m.shape[1])).reshape(rows, m.shape[1])


def _const_spec(arr):
    nd = arr.ndim
    return pl.BlockSpec(arr.shape, lambda *_: (0,) * nd)


def _adaln_kernel(c_ref, w_ref, b_ref, *rest):
    o_ref = rest[-1]
    c = c_ref[...]
    a = (c * _sigmoid(c)).astype(BF16)
    o_ref[...] = jnp.dot(a, w_ref[...].astype(BF16), preferred_element_type=F32) + b_ref[...]


def _adaln(c, w, b, tn=1024, after=None):
    rows, ncols = c.shape[0], w.shape[1]
    ins = [c, w, b.reshape(1, ncols)]
    in_specs = [pl.BlockSpec((rows, D_MODEL), lambda j: (0, 0)),
                pl.BlockSpec((D_MODEL, tn), lambda j: (0, j)),
                pl.BlockSpec((1, tn), lambda j: (0, j))]
    if after is not None:
        ins.append(after)
        in_specs.append(pl.BlockSpec(memory_space=pl.ANY))
    return pl.pallas_call(
        _adaln_kernel,
        grid=(ncols // tn,),
        in_specs=in_specs,
        out_specs=pl.BlockSpec((rows, tn), lambda j: (0, j)),
        out_shape=jax.ShapeDtypeStruct((rows, ncols), F32),
        compiler_params=pltpu.CompilerParams(dimension_semantics=("arbitrary",)),
        name="adaln",
    )(*ins)


def _inproj_kernel(x_ref, sh_ref, sc_ref, g_ref, w_ref, lng_ref, lnb_ref, *refs, tm, keeps, dils, emit_vn_f32,
                   has_after, kv_feature_major):
    if has_after:
        refs = refs[1:]
    qkv0_ref, qkv1_ref, qkv2_ref, u_ref, vn_ref, ga_ref, gb_ref, kv0_ref, kv1_ref, kv2_ref = refs[:10]
    rest = refs[10:]
    perm_scr = rest[-1]
    h = _rmsnorm_mod(x_ref[...], g_ref[...], _mod_rows(sh_ref, tm), _mod_rows(sc_ref, tm)).astype(BF16)

    def proj(c):
        return jnp.dot(h, w_ref[:, c * COL_CHUNK:(c + 1) * COL_CHUNK], preferred_element_type=F32)

    def cols(c):
        return slice(c * COL_CHUNK, (c + 1) * COL_CHUNK)

    qkv_refs = (qkv0_ref, qkv1_ref, qkv2_ref)

    def put_qkv(g, t, z):
        dil = dils[g]
        if dil == 1:
            qkv_refs[g][0, :, cols(t)] = z.astype(BF16)
            return
        slot = perm_scr.at[t % 2]
        n_slab = COL_CHUNK // LANES
        for j in range(n_slab):
            slot[j] = z[:, j * LANES:(j + 1) * LANES]
        for r in range(dil):
            rows = jnp.concatenate([slot[j, pl.ds(r, tm // dil, stride=dil), :] for j in range(n_slab)], axis=1)
            qkv_refs[g][r, :, cols(t)] = rows.astype(BF16)

    for g in range(N_GROUPS):
        put_qkv(g, 0, proj(g) * (HEAD_DIM ** -0.5 * LOG2E))

    kv_refs = (kv0_ref, kv1_ref, kv2_ref)
    for g in range(N_GROUPS):
        keep = keeps[g]
        for t in (1, 2):
            z = proj(t * N_GROUPS + g)
            put_qkv(g, t, z)
            if kv_feature_major:
                zt = jnp.transpose(z if keep >= tm else z[tm - keep:, :])
                kv_refs[g][(t - 1) * A_WIDTH:t * A_WIDTH, :] = zt
            else:
                kv_refs[g][:, cols(t - 1)] = z if keep >= tm else z[tm - keep:, :]

    base = N_QKV // COL_CHUNK
    for c in range(2):
        u_ref[:, cols(c)] = _gelu_tanh(proj(base + c)).astype(BF16)

    vs = [_gelu_tanh(proj(base + 2 + c)) for c in range(2)]
    mu = (jnp.sum(vs[0], axis=-1, keepdims=True) + jnp.sum(vs[1], axis=-1, keepdims=True)) * (1.0 / B_WIDTH)
    ds = [v - mu for v in vs]
    var = (jnp.sum(ds[0] * ds[0], axis=-1, keepdims=True)
           + jnp.sum(ds[1] * ds[1], axis=-1, keepdims=True)) * (1.0 / B_WIDTH)
    inv = lax.rsqrt(var + EPS)
    for c in range(2):
        vn = ds[c] * inv * lng_ref[:, cols(c)] + lnb_ref[:, cols(c)]
        vn_ref[:, cols(c)] = vn.astype(BF16)
        if emit_vn_f32:
            rest[0][:, cols(c)] = vn

    for c in range(2):
        ga_ref[:, cols(c)] = _sigmoid(proj(base + 4 + c)).astype(BF16)
        gb_ref[:, cols(c)] = _sigmoid(proj(base + 6 + c)).astype(BF16)


def _inproj(x, sh, sc, gain, w_in, ln_g, ln_b, *, tm, keeps, dils, emit_vn_f32, kv_feature_major, after=None):
    bsz, seq, _ = x.shape
    n_s = seq // tm
    tok = lambda width: pl.BlockSpec((None, tm, width), lambda b, s: (b, s, 0))

    def kv_spec(keep):
        first = n_s - max(keep // tm, 1)
        rows = min(keep, tm)
        if kv_feature_major:
            return pl.BlockSpec((None, 2 * A_WIDTH, rows), lambda b, s: (b, 0, jnp.maximum(s - first, 0)))
        return pl.BlockSpec((None, rows, 2 * A_WIDTH), lambda b, s: (b, jnp.maximum(s - first, 0), 0))

    out_specs, out_shape = [], []
    for dil in dils:
        out_specs.append(pl.BlockSpec((None, dil, tm // dil, 3 * A_WIDTH), lambda b, s: (b, 0, s, 0)))
        out_shape.append(jax.ShapeDtypeStruct((bsz, dil, seq // dil, 3 * A_WIDTH), BF16))
    out_specs += [tok(B_WIDTH), tok(B_WIDTH), tok(D_MODEL), tok(D_MODEL)]
    out_shape += [jax.ShapeDtypeStruct((bsz, seq, D_MODEL), BF16)] * 4
    for keep in keeps:
        out_specs.append(kv_spec(keep))
        kv_shape = (bsz, 2 * A_WIDTH, keep) if kv_feature_major else (bsz, keep, 2 * A_WIDTH)
        out_shape.append(jax.ShapeDtypeStruct(kv_shape, F32))
    if emit_vn_f32:
        out_specs.append(tok(B_WIDTH))
        out_shape.append(jax.ShapeDtypeStruct((bsz, seq, B_WIDTH), F32))

    ins = [x, sh, sc, gain, w_in, ln_g, ln_b]
    in_specs = [tok(D_MODEL), _mod_spec(sh, tm, seq), _mod_spec(sc, tm, seq), _const_spec(gain),
                pl.BlockSpec(w_in.shape, lambda b, s: (0, 0), pipeline_mode=pl.Buffered(1)),
                _const_spec(ln_g), _const_spec(ln_b)]
    if after is not None:
        ins.append(after)
        in_specs.append(pl.BlockSpec(memory_space=pl.ANY))
    scratch = [pltpu.VMEM((2, COL_CHUNK // LANES, tm, LANES), F32)]
    return pl.pallas_call(
        functools.partial(_inproj_kernel, tm=tm, keeps=keeps, dils=dils, emit_vn_f32=emit_vn_f32,
                          has_after=after is not None, kv_feature_major=kv_feature_major),
        grid=(bsz, n_s),
        in_specs=in_specs,
        out_specs=out_specs,
        out_shape=out_shape,
        scratch_shapes=scratch,
        compiler_params=pltpu.CompilerParams(dimension_semantics=("arbitrary", "arbitrary"),
                                             vmem_limit_bytes=VMEM_LIMIT),
        name="inproj",
    )(*ins)


def _attn_kernel(q_ref, kp_ref, kc_ref, vp_ref, vc_ref, bias0_ref, bias_ref, o_ref, lse_ref, *, n_seq, q_blocks):
    lane = lax.broadcasted_iota(jnp.int32, (A_BLOCK, LANES), 1)
    low = lane < HEAD_DIM
    zero = jnp.zeros((), BF16)
    for seq_i, i in [(a, b) for a in range(n_seq) for b in range(q_blocks)]:
        rows = (seq_i, slice(i * A_BLOCK, (i + 1) * A_BLOCK))
        q = q_ref[rows[0], rows[1], :]
        key_rows = slice((i - 1) * A_BLOCK, (i + 1) * A_BLOCK)
        if i == 0:
            k = jnp.concatenate([kp_ref[seq_i], kc_ref[seq_i, :A_BLOCK, :]], axis=0)
            v = jnp.concatenate([vp_ref[seq_i], vc_ref[seq_i, :A_BLOCK, :]], axis=0)
        else:
            k, v = kc_ref[seq_i, key_rows, :], vc_ref[seq_i, key_rows, :]
        b_ref = bias0_ref if i == 0 else bias_ref
        lse_tile = jnp.zeros((A_BLOCK, LANES), F32)
        for j in range(HEADS // 2):
            pair = slice(j * LANES, (j + 1) * LANES)
            qp, kpair, vpair = q[:, pair], k[:, pair], v[:, pair]
            outs = []
            for e in range(2):
                h = 2 * j + e
                qm = jnp.where(low if e == 0 else jnp.logical_not(low), qp, zero)
                s = lax.dot_general(qm, kpair, (((1,), (1,)), ((), ())), preferred_element_type=F32)
                s = s + b_ref[h]
                m = jnp.max(s, axis=-1, keepdims=True)
                p = jnp.exp2(s - m)
                den = jnp.sum(p, axis=-1, keepdims=True)
                o = jnp.dot(p.astype(BF16), vpair, preferred_element_type=F32)
                first = h * LSE_REP
                lse_tile = jnp.where((lane >= first) & (lane < first + LSE_REP // 2), m,
                                     jnp.where((lane >= first + LSE_REP // 2) & (lane < first + LSE_REP), den,
                                               lse_tile))
                outs.append(o)
            o_ref[rows[0], rows[1], pair] = jnp.where(low, outs[0], outs[1]).astype(BF16)
        lse_ref[rows[0], rows[1], :] = lse_tile


def _attn_bias(g):
    _, dil = A_GROUPS[g]
    n = N_GROUPS * HEADS
    e = np.arange(1, n + 1, dtype=np.float32)
    slopes = np.exp2(-8.0 * e / n).astype(np.float32).reshape(N_GROUPS, HEADS)[g]
    qi = np.arange(A_BLOCK)[:, None]
    ki = np.arange(2 * A_BLOCK)[None, :]
    delta = qi + A_BLOCK - ki
    band = (delta >= 0) & (delta <= A_STEPS)
    dist = (delta * dil).astype(np.float32)
    bias = (-slopes[:, None, None] * dist[None] * np.float32(LOG2E)).astype(np.float32)
    out = np.empty((2, HEADS, A_BLOCK, 2 * A_BLOCK), np.float32)
    out[1] = np.where(band[None], bias, NEG)
    out[0] = np.where((band & (ki >= A_BLOCK))[None], bias, NEG)
    return jnp.asarray(out)


def _attn_prompt(qkv, g):
    bsz, dil, steps, _ = qkv.shape
    qb = min(ATTN_BLOCKS_PER_STEP, steps // A_BLOCK)
    ns = min(ATTN_BLOCKS_PER_STEP // qb, dil)
    rows = qb * A_BLOCK
    n_steps = steps // rows
    bias = _attn_bias(g)

    def blk(t, prev):
        if prev:
            return pl.BlockSpec((None, ns, A_BLOCK, A_WIDTH),
                                lambda b, r, n: (b, r, jnp.maximum(n * qb - 1, 0), t))
        return pl.BlockSpec((None, ns, rows, A_WIDTH), lambda b, r, n: (b, r, n, t))

    bias_blk = lambda pick: pl.BlockSpec((None, HEADS, A_BLOCK, 2 * A_BLOCK), lambda b, r, n: (pick(n), 0, 0, 0))
    return pl.pallas_call(
        functools.partial(_attn_kernel, n_seq=ns, q_blocks=qb),
        grid=(bsz, dil // ns, n_steps),
        in_specs=[blk(0, False), blk(1, True), blk(1, False), blk(2, True), blk(2, False),
                  bias_blk(lambda n: jnp.minimum(n, 1)), bias_blk(lambda n: 1)],
        out_specs=[pl.BlockSpec((None, ns, rows, A_WIDTH), lambda b, r, n: (b, r, n, 0)),
                   pl.BlockSpec((None, ns, rows, LANES), lambda b, r, n: (b, r, n, 0))],
        out_shape=[jax.ShapeDtypeStruct((bsz, dil, steps, A_WIDTH), BF16),
                   jax.ShapeDtypeStruct((bsz, dil, steps, LANES), F32)],
        compiler_params=pltpu.CompilerParams(dimension_semantics=("arbitrary",) * 3),
        name=f"attn_prompt_g{g}",
    )(qkv, qkv, qkv, qkv, qkv, bias, bias)


def _attn_sample_kernel(q0_ref, q1_ref, q2_ref, c0_ref, c1_ref, c2_ref, bc0_ref, bc1_ref, bc2_ref,
                        bn0_ref, bn1_ref, bn2_ref, after_ref, o_ref, *, t_new):
    del after_ref
    n_rows = HEADS * t_new
    row = lax.broadcasted_iota(jnp.int32, (n_rows, A_WIDTH), 0)
    lane = lax.broadcasted_iota(jnp.int32, (n_rows, A_WIDTH), 1)
    head_mask = (row // t_new) == (lane // HEAD_DIM)
    pad = jnp.zeros((LANES - t_new, A_WIDTH), F32)
    outs, lses = [], []
    for qkv_ref, c_ref, bc_ref, bn_ref in ((q0_ref, c0_ref, bc0_ref, bn0_ref), (q1_ref, c1_ref, bc1_ref, bn1_ref),
                                           (q2_ref, c2_ref, bc2_ref, bn2_ref)):
        q = qkv_ref[:, :A_WIDTH].astype(F32)
        k_new = qkv_ref[:, A_WIDTH:2 * A_WIDTH].astype(F32)
        v_new = qkv_ref[:, 2 * A_WIDTH:].astype(F32)
        k_new = jnp.concatenate([k_new, pad], axis=0).astype(BF16)
        v_new = jnp.concatenate([v_new, pad], axis=0).astype(BF16)
        q_rows = jnp.where(head_mask, jnp.concatenate([q] * HEADS, axis=0), 0.0).astype(BF16)
        k_buf_t = c_ref[:A_WIDTH, :].astype(BF16)
        v_buf_t = c_ref[A_WIDTH:, :].astype(BF16)
        nt = (((1,), (1,)), ((), ()))
        s_buf = jnp.dot(q_rows, k_buf_t, preferred_element_type=F32) + bc_ref[...]
        s_new = lax.dot_general(q_rows, k_new, nt, preferred_element_type=F32) + bn_ref[...]
        m = jnp.maximum(jnp.max(s_buf, axis=-1, keepdims=True), jnp.max(s_new, axis=-1, keepdims=True))
        p_buf = jnp.exp2(s_buf - m)
        p_new = jnp.exp2(s_new - m)
        den = jnp.sum(p_buf, axis=-1, keepdims=True) + jnp.sum(p_new, axis=-1, keepdims=True)
        o = (lax.dot_general(p_buf.astype(BF16), v_buf_t, nt, preferred_element_type=F32)
             + jnp.dot(p_new.astype(BF16), v_new, preferred_element_type=F32)) / den
        outs.append(o)
        lses.append(m + jnp.log2(den))
    top = jnp.maximum(jnp.maximum(lses[0], lses[1]), lses[2])
    ws = [jnp.exp2(l - top) for l in lses]
    tot = ws[0] + ws[1] + ws[2]
    acc = (ws[0] / tot) * outs[0] + (ws[1] / tot) * outs[1] + (ws[2] / tot) * outs[2]
    acc = jnp.where(head_mask, acc, 0.0).reshape(HEADS, t_new, A_WIDTH)
    o_ref[...] = jnp.sum(acc, axis=0).astype(BF16)


def _sample_bias(g, t_new, buf):
    _, dil = A_GROUPS[g]
    n = N_GROUPS * HEADS
    e = np.arange(1, n + 1, dtype=np.float32)
    slopes = np.exp2(-8.0 * e / n).astype(np.float32).reshape(N_GROUPS, HEADS)[g]
    t = np.arange(t_new)[:, None]
    idx = np.concatenate([np.arange(buf), buf + np.arange(LANES)])[None, :]
    dist = buf + t - idx
    valid = (dist >= 0) & (dist % dil == 0) & (dist <= A_STEPS * dil) & (idx < buf + t_new)
    bias = -slopes[:, None, None] * dist.astype(np.float32)[None] * np.float32(LOG2E)
    bias = np.where(valid[None], bias, NEG).astype(np.float32).reshape(HEADS * t_new, buf + LANES)
    return jnp.asarray(bias[:, :buf]), jnp.asarray(bias[:, buf:])


def _attn_sample(qkvs, caches, *, after):
    bsz, t_new, _ = qkvs[0].shape
    cache_v = [jnp.transpose(c, (0, 2, 3, 4, 1)).reshape(bsz, 2 * A_WIDTH, c.shape[1]) for c in caches]
    biases = [_sample_bias(g, t_new, cache_v[g].shape[2]) for g in range(N_GROUPS)]
    bcs = [b[0] for b in biases]
    bns = [b[1] for b in biases]
    return pl.pallas_call(
        functools.partial(_attn_sample_kernel, t_new=t_new),
        grid=(bsz,),
        in_specs=[pl.BlockSpec((None, t_new, 3 * A_WIDTH), lambda b: (b, 0, 0))] * N_GROUPS
                 + [pl.BlockSpec((None, 2 * A_WIDTH, c.shape[2]), lambda b: (b, 0, 0)) for c in cache_v]
                 + [_const_spec(b) for b in bcs] + [_const_spec(b) for b in bns]
                 + [pl.BlockSpec(memory_space=pl.ANY)],
        out_specs=pl.BlockSpec((None, t_new, A_WIDTH), lambda b: (b, 0, 0)),
        out_shape=jax.ShapeDtypeStruct((bsz, t_new, A_WIDTH), BF16),
        compiler_params=pltpu.CompilerParams(dimension_semantics=("arbitrary",),
                                             vmem_limit_bytes=VMEM_LIMIT),
        name="attn_sample",
    )(*qkvs, *cache_v, *bcs, *bns, after)


def _first_max4(v):
    top = jnp.maximum(jnp.maximum(v[0], v[1]), jnp.maximum(v[2], v[3]))
    idx = jnp.where(v[0] == top, 0.0, jnp.where(v[1] == top, 1.0, jnp.where(v[2] == top, 2.0, 3.0)))
    return top, idx


def _route(lt):
    row = lambda k: lt[k:k + 1, :]
    g = [row(k) for k in range(MOE_GROUPS)]
    gmax, g_idx = _first_max4(g)
    g_prob = 1.0 / (jnp.exp(g[0] - gmax) + jnp.exp(g[1] - gmax) + jnp.exp(g[2] - gmax) + jnp.exp(g[3] - gmax))
    cand = []
    for k in range(MOE_PER_GROUP):
        c = row(ROUTE_OFF + MOE_PER_GROUP * (MOE_GROUPS - 1) + k)
        for gi in range(MOE_GROUPS - 2, -1, -1):
            c = jnp.where(g_idx == float(gi), row(ROUTE_OFF + MOE_PER_GROUP * gi + k), c)
        cand.append(c)
    e1, i1 = _first_max4(cand)
    rest = [jnp.where(i1 == float(k), -jnp.inf, cand[k]) for k in range(MOE_PER_GROUP)]
    e2, i2 = _first_max4(rest)
    t = jnp.exp(e2 - e1)
    w1 = 1.0 / (1.0 + t)
    w2 = t / (1.0 + t)
    return MOE_PER_GROUP * g_idx + i1, MOE_PER_GROUP * g_idx + i2, w1 * g_prob, w2 * g_prob


def _pack_bf16_pairs(x):
    q = x.shape[1] // 4
    bits = lax.bitcast_convert_type(x.astype(BF16).astype(F32), jnp.int32)
    pack = lambda hi, lo: hi | lax.shift_right_logical(lo, 16)
    return pack(bits[:, :q], bits[:, 2 * q:3 * q]), pack(bits[:, q:2 * q], bits[:, 3 * q:])


def _unpack_bf16_pairs(pa, pb):
    hi = lambda p: lax.bitcast_convert_type(p & jnp.int32(-65536), F32)
    lo = lambda p: lax.bitcast_convert_type(lax.shift_left(p, 16), F32)
    return jnp.concatenate([hi(pa), hi(pb), lo(pa), lo(pb)], axis=1)


def _mix_kernel(*refs, tm, dils, routed, span):
    n_attn = len(dils)
    x_ref = refs[0]
    if n_attn > 1:
        o_refs = refs[1:1 + n_attn]
        l_refs = refs[1 + n_attn:1 + 2 * n_attn]
        exp_ref = refs[1 + 2 * n_attn]
        pos = 2 + 2 * n_attn
    else:
        o_refs = refs[1:2]
        pos = 2
    (u_ref, vn_ref, ga_ref, gb_ref, sw_ref, sbt_ref, wpa_ref, wpb_ref, wo_ref,
     gt1_ref, sh2_ref, sc2_ref, n2g_ref, wrh_ref, wrl_ref, br_ref) = refs[pos:pos + 16]
    n_out = 6 if routed else 3
    out_refs = refs[pos + 16:pos + 16 + n_out]
    scr = list(refs[pos + 16 + n_out:])
    x1_ref = out_refs[0]
    ob_scr = scr.pop(0)

    def to_positions(ref, scr, dil):
        if dil == 1:
            return ref[0].astype(F32)
        n_slab = scr.shape[0]
        for r in range(dil):
            rows = ref[r].astype(F32)
            for j in range(n_slab):
                scr[j, pl.ds(r, tm // dil, stride=dil), :] = rows[:, j * LANES:(j + 1) * LANES]
        return jnp.concatenate([scr[j] for j in range(n_slab)], axis=1)

    if n_attn > 1:
        o_scr, l_scr = scr[0], scr[1]
        stats = [to_positions(l_refs[g], l_scr.at[g], dils[g]) for g in range(n_attn)]
        lane = lax.broadcasted_iota(jnp.int32, stats[0].shape, 1)
        is_max = (lane & (LSE_REP - 1)) < LSE_REP // 2
        half = LSE_REP // 2
        maxes = [jnp.where(is_max, s, pltpu.roll(s, half, axis=1)) for s in stats]
        dens = [jnp.where(is_max, pltpu.roll(s, LANES - half, axis=1), s) for s in stats]
        lses = [m + jnp.log2(d) for m, d in zip(maxes, dens)]
        top = functools.reduce(jnp.maximum, lses)
        tot = functools.reduce(lambda a, b: a + b, [jnp.exp2(l - top) for l in lses])
        o_a = None
        for g in range(n_attn):
            w = jnp.exp2(maxes[g] - top) / tot
            hi = w.astype(BF16)
            lo = (w - hi.astype(F32)).astype(BF16)
            w_exp = jnp.dot(jnp.concatenate([hi, lo], axis=1), exp_ref[...], preferred_element_type=F32)
            term = w_exp * to_positions(o_refs[g], o_scr.at[g], dils[g])
            o_a = term if o_a is None else o_a + term
        o_a = o_a.astype(BF16)
    else:
        o_a = o_refs[0][...]

    r_i = lax.broadcasted_iota(jnp.int32, (span, span), 0)
    c_i = lax.broadcasted_iota(jnp.int32, (span, span), 1)
    tril = r_i >= c_i
    for gi in range(B_GROUPS):
        gcols = slice(gi * LANES, (gi + 1) * LANES)
        b_s = sbt_ref[:span, gi:gi + 1]
        if span == B_CHUNK:
            w_s = jnp.where(tril, sw_ref[gi], 0.0).astype(BF16)
            n_chunk = tm // span
            vn_wide = jnp.concatenate([vn_ref[c * span:(c + 1) * span, gcols] for c in range(n_chunk)], axis=1)
            mixed = jnp.dot(w_s, vn_wide, preferred_element_type=F32) + b_s
            for c in range(n_chunk):
                rows = slice(c * span, (c + 1) * span)
                ob_scr[rows, gcols] = (u_ref[rows, gcols].astype(F32)
                                       * mixed[:, c * LANES:(c + 1) * LANES]).astype(BF16)
        else:
            w_s = jnp.where(tril, sw_ref[gi, :span, :span], 0.0)
            vn3 = vn_ref[:, gcols].astype(F32).reshape(tm // span, span, LANES)
            mixed = jnp.broadcast_to(b_s, (span, LANES))[None]
            for s in range(span):
                w_col = jnp.broadcast_to(w_s[:, s:s + 1], (span, LANES))[None]
                mixed = mixed + w_col * jnp.broadcast_to(vn3[:, s:s + 1, :], vn3.shape)
            u3 = u_ref[:, gcols].astype(F32).reshape(tm // span, span, LANES)
            ob_scr[:, gcols] = (u3 * mixed).reshape(tm, LANES).astype(BF16)

    sub = min(tm, MIX_SUB_ROWS)
    if routed:
        carry = scr[-1]
        first_step = (pl.program_id(0) == 0) & (pl.program_id(1) == 0)
        seen = jnp.where(first_step, 0.0, carry[:, 0:1])
        r_i = lax.broadcasted_iota(jnp.int32, (sub, sub), 0)
        c_i = lax.broadcasted_iota(jnp.int32, (sub, sub), 1)
        earlier = jnp.where(r_i < c_i, 1.0, 0.0).astype(BF16)
        e_iota = lax.broadcasted_iota(jnp.int32, (MOE_EXPERTS, sub), 0).astype(F32)
    nt = (((1,), (1,)), ((), ()))
    for r0 in range(0, tm, sub):
        rs = slice(r0, r0 + sub)
        mod = lambda ref: ref[...] if ref.shape[0] == 1 else _mod_rows(ref, tm)[rs, :]
        pa = jnp.dot(o_a[rs], wpa_ref[...], preferred_element_type=F32)
        pb = jnp.dot(ob_scr[rs, :], wpb_ref[...], preferred_element_type=F32)
        mix = (ga_ref[rs, :].astype(F32) * pa + gb_ref[rs, :].astype(F32) * pb).astype(BF16)
        x1 = x_ref[rs, :] + mod(gt1_ref) * jnp.dot(mix, wo_ref[...], preferred_element_type=F32)
        x1_ref[rs, :] = x1
        h2 = _rmsnorm_mod(x1, n2g_ref[...], mod(sh2_ref), mod(sc2_ref))
        h_hi = h2.astype(BF16)
        h_lo = (h2 - h_hi.astype(F32)).astype(BF16)
        lt = (lax.dot_general(wrh_ref[...], h_hi, nt, preferred_element_type=F32)
              + lax.dot_general(wrh_ref[...], h_lo, nt, preferred_element_type=F32)
              + lax.dot_general(wrl_ref[...], h_hi, nt, preferred_element_type=F32)) + br_ref[...]
        ex1, ex2, w1, w2 = _route(lt)
        if not routed:
            _, h2_ref, comb_ref = out_refs
            h2_ref[rs, :] = h_hi
            e_lane = lax.broadcasted_iota(jnp.int32, (LANES, sub), 0).astype(F32) - float(ROUTE_OFF)
            comb_t = jnp.where(e_lane == ex1, w1, jnp.where(e_lane == ex2, w2, 0.0))
            comb_ref[rs, :] = jnp.transpose(comb_t)
            continue

        _, hpa_ref, hpb_ref, route_ref, wcol_ref, cnt_ref = out_refs
        hpa_ref[rs, :], hpb_ref[rs, :] = _pack_bf16_pairs(h2)
        hot1, hot2 = e_iota == ex1, e_iota == ex2
        onehot = jnp.where(hot1 | hot2, 1.0, 0.0)
        rank = jnp.dot(onehot.astype(BF16), earlier, preferred_element_type=F32) + seen
        r1 = jnp.sum(jnp.where(hot1, rank, 0.0), axis=0, keepdims=True)
        r2 = jnp.sum(jnp.where(hot2, rank, 0.0), axis=0, keepdims=True)
        seen = seen + jnp.sum(onehot, axis=1, keepdims=True)
        fields = jnp.concatenate([ex1, ex2, w1, w2, r1, r2, jnp.zeros((2, sub), F32)], axis=0)
        route_ref[:, rs] = fields
        wcol_ref[rs, :] = jnp.transpose(jnp.concatenate([fields, jnp.zeros((LANES - 8, sub), F32)], axis=0))
    if routed:
        carry[...] = jnp.broadcast_to(seen, carry.shape)
        cnt_ref[...] = jnp.broadcast_to(seen, cnt_ref.shape)


def _expand_matrix():
    e = np.zeros((LANES, A_WIDTH), np.float32)
    for h in range(HEADS):
        e[h * LSE_REP, h * HEAD_DIM:(h + 1) * HEAD_DIM] = 1.0
    return jnp.asarray(np.concatenate([e, e], axis=0), dtype=BF16)


def _mix(x, attn_outs, attn_lses, u, vn, ga, gb, sgu_w, sgu_bt, w_pa, w_pb, w_o,
         gt1, sh2, sc2, n2g, w_r_hi, w_r_lo, b_r, *, tm, routed, span):
    bsz, seq, _ = x.shape
    n_attn = len(attn_outs)
    tok = lambda width: pl.BlockSpec((None, tm, width), lambda b, s: (b, s, 0))
    ins = [x] + list(attn_outs)
    scratch = [pltpu.VMEM((tm, B_WIDTH), BF16)]
    if n_attn > 1:
        dils = tuple(o.shape[1] for o in attn_outs)
        res = lambda dil, width: pl.BlockSpec((None, dil, tm // dil, width), lambda b, s: (b, 0, s, 0))
        e2 = _expand_matrix()
        ins += list(attn_lses) + [e2]
        specs = ([tok(D_MODEL)] + [res(d, A_WIDTH) for d in dils] + [res(d, LANES) for d in dils]
                 + [_const_spec(e2)])
        scratch += [pltpu.VMEM((n_attn, A_WIDTH // LANES, tm, LANES), F32), pltpu.VMEM((n_attn, 1, tm, LANES), F32)]
    else:
        dils = (1,)
        specs = [tok(D_MODEL), tok(A_WIDTH)]
    ins += [u, vn, ga, gb, sgu_w, sgu_bt, w_pa, w_pb, w_o, gt1, sh2, sc2, n2g, w_r_hi, w_r_lo, b_r]
    specs += [tok(B_WIDTH), tok(B_WIDTH), tok(D_MODEL), tok(D_MODEL),
              _const_spec(sgu_w), _const_spec(sgu_bt), _const_spec(w_pa), _const_spec(w_pb), _const_spec(w_o),
              _mod_spec(gt1, tm, seq), _mod_spec(sh2, tm, seq), _mod_spec(sc2, tm, seq), _const_spec(n2g),
              _const_spec(w_r_hi), _const_spec(w_r_lo), _const_spec(b_r)]
    if routed:
        quarter = D_MODEL // 4
        out_specs = [tok(D_MODEL), tok(quarter), tok(quarter),
                     pl.BlockSpec((None, 8, tm), lambda b, s: (b, 0, s)), tok(LANES),
                     pl.BlockSpec((MOE_EXPERTS, LANES), lambda b, s: (0, 0))]
        out_shape = [jax.ShapeDtypeStruct((bsz, seq, D_MODEL), F32),
                     jax.ShapeDtypeStruct((bsz, seq, quarter), jnp.int32),
                     jax.ShapeDtypeStruct((bsz, seq, quarter), jnp.int32),
                     jax.ShapeDtypeStruct((bsz, 8, seq), F32),
                     jax.ShapeDtypeStruct((bsz, seq, LANES), F32),
                     jax.ShapeDtypeStruct((MOE_EXPERTS, LANES), F32)]
        scratch.append(pltpu.VMEM((MOE_EXPERTS, LANES), F32))
    else:
        out_specs = [tok(D_MODEL), tok(D_MODEL), tok(LANES)]
        out_shape = [jax.ShapeDtypeStruct((bsz, seq, D_MODEL), F32),
                     jax.ShapeDtypeStruct((bsz, seq, D_MODEL), BF16),
                     jax.ShapeDtypeStruct((bsz, seq, LANES), F32)]
    return pl.pallas_call(
        functools.partial(_mix_kernel, tm=tm, dils=dils, routed=routed, span=span),
        grid=(bsz, seq // tm),
        in_specs=specs,
        out_specs=out_specs,
        out_shape=out_shape,
        scratch_shapes=scratch,
        compiler_params=pltpu.CompilerParams(dimension_semantics=("arbitrary", "arbitrary"),
                                             vmem_limit_bytes=VMEM_LIMIT),
        name="mix",
    )(*ins)


def _moe_kernel(h_ref, comb_ref, wg_ref, wu_ref, wd_ref, x1_ref, gt2_ref, shf_ref, scf_ref, nfg_ref,
                y_ref, acc_ref):
    e = pl.program_id(2)

    @pl.when(e == 0)
    def _():
        acc_ref[...] = jnp.zeros_like(acc_ref)

    h = h_ref[...]
    hg = jnp.dot(h, wg_ref[...], preferred_element_type=F32)
    hu = jnp.dot(h, wu_ref[...], preferred_element_type=F32)
    comb = comb_ref[...]
    lane = lax.broadcasted_iota(jnp.int32, comb.shape, 1)
    cw = jnp.sum(jnp.where(lane == e + ROUTE_OFF, comb, 0.0), axis=-1, keepdims=True)
    act = (hg * _sigmoid(hg) * hu) * cw
    acc_ref[...] += jnp.dot(act.astype(BF16), wd_ref[...], preferred_element_type=F32)

    @pl.when(e == MOE_EXPERTS - 1)
    def _():
        rows = acc_ref.shape[0]
        x2 = x1_ref[...] + _mod_rows(gt2_ref, rows) * acc_ref[...]
        y_ref[...] = _rmsnorm_mod(x2, nfg_ref[...], _mod_rows(shf_ref, rows), _mod_rows(scf_ref, rows))


def _moe(h2, comb, w_gate, w_up, w_down, x1, gt2, shf, scf, nfg, *, tm):
    bsz, seq, _ = h2.shape
    tok = lambda width: pl.BlockSpec((None, tm, width), lambda b, s, e: (b, s, 0))
    return pl.pallas_call(
        _moe_kernel,
        grid=(bsz, seq // tm, MOE_EXPERTS),
        in_specs=[tok(D_MODEL), tok(LANES),
                  pl.BlockSpec((None, D_MODEL, MOE_HIDDEN), lambda b, s, e: (e, 0, 0)),
                  pl.BlockSpec((None, D_MODEL, MOE_HIDDEN), lambda b, s, e: (e, 0, 0)),
                  pl.BlockSpec((None, MOE_HIDDEN, D_MODEL), lambda b, s, e: (e, 0, 0)),
                  tok(D_MODEL), _mod_spec(gt2, tm, seq), _mod_spec(shf, tm, seq), _mod_spec(scf, tm, seq),
                  _const_spec(nfg)],
        out_specs=tok(D_MODEL),
        out_shape=jax.ShapeDtypeStruct((bsz, seq, D_MODEL), F32),
        scratch_shapes=[pltpu.VMEM((tm, D_MODEL), F32)],
        compiler_params=pltpu.CompilerParams(dimension_semantics=("arbitrary",) * 3,
                                             vmem_limit_bytes=VMEM_LIMIT),
        name="moe",
    )(h2, comb, w_gate, w_up, w_down, x1, gt2, shf, scf, nfg)


def _sc_mesh():
    return plsc.VectorSubcoreMesh(core_axis_name="c", subcore_axis_name="s")


def _sc_scatter_rows(rows, pos1, pos2, n_out):
    n, width = rows.shape
    steps = n // SC_WINDOW // SC_CORES

    @pl.kernel(out_type=jax.ShapeDtypeStruct((n_out, width), rows.dtype), mesh=_sc_mesh(), scratch_types=[])
    def scatter(rows_hbm, p1_hbm, p2_hbm, out_hbm):
        def body(x_vmem, i1_vmem, i2_vmem):
            pltpu.sync_copy(x_vmem, out_hbm.at[i1_vmem.at[0]])
            pltpu.sync_copy(x_vmem, out_hbm.at[i2_vmem.at[0]])

        pltpu.emit_pipeline(
            body, grid=(SC_CORES, steps),
            in_specs=[pl.BlockSpec((SC_WINDOW, width), lambda c, i: (c * steps + i, 0)),
                      pl.BlockSpec((1, SC_WINDOW), lambda c, i: (0, c * steps + i)),
                      pl.BlockSpec((1, SC_WINDOW), lambda c, i: (0, c * steps + i))],
            out_specs=[],
            core_axis_name=("c", "s"),
            dimension_semantics=(pltpu.PARALLEL, pltpu.PARALLEL),
        )(rows_hbm, p1_hbm, p2_hbm)

    return scatter(rows, pos1, pos2)


def _sc_gather_rows(table, idx, *after):
    m, width = idx.shape[1], table.shape[1]
    steps = m // SC_WINDOW // SC_CORES

    @pl.kernel(out_type=jax.ShapeDtypeStruct((m, width), table.dtype), mesh=_sc_mesh(), scratch_types=[])
    def gather(table_hbm, idx_hbm, *rest):
        out_hbm = rest[-1]

        def body(i_vmem, o_vmem):
            pltpu.sync_copy(table_hbm.at[i_vmem.at[0]], o_vmem)

        pltpu.emit_pipeline(
            body, grid=(SC_CORES, steps),
            in_specs=[pl.BlockSpec((1, SC_WINDOW), lambda c, i: (0, c * steps + i))],
            out_specs=[pl.BlockSpec((SC_WINDOW, width), lambda c, i: (c * steps + i, 0))],
            core_axis_name=("c", "s"),
            dimension_semantics=(pltpu.PARALLEL, pltpu.PARALLEL),
        )(idx_hbm, out_hbm)

    return gather(table, idx, *after)


def _ffn_kernel(te_ref, nt_ref, xa_ref, xb_ref, wg_ref, wu_ref, wd_ref, *rest):
    oa_ref, ob_ref, wg_s, wu_s, wd_s = rest[-5:]
    t = pl.program_id(0)
    live = t < nt_ref[0]

    @pl.when(live & ((t == 0) | (te_ref[t] != te_ref[jnp.maximum(t - 1, 0)])))
    def _():
        wg_s[...] = wg_ref[...].astype(BF16)
        wu_s[...] = wu_ref[...].astype(BF16)
        wd_s[...] = wd_ref[...].astype(BF16)

    @pl.when(live)
    def _():
        x = _unpack_bf16_pairs(xa_ref[...], xb_ref[...]).astype(BF16)
        hg = jnp.dot(x, wg_s[...], preferred_element_type=F32)
        hu = jnp.dot(x, wu_s[...], preferred_element_type=F32)
        act = (hg * _sigmoid(hg) * hu).astype(BF16)
        oa_ref[...], ob_ref[...] = _pack_bf16_pairs(jnp.dot(act, wd_s[...], preferred_element_type=F32))


def _ffn(xa, xb, tile_expert, n_tiles, w_gate, w_up, w_down, *, after):
    n_rows, quarter = xa.shape
    rows = pl.BlockSpec((MOE_TILE, quarter), lambda t, te, nt: (jnp.minimum(t, nt[0] - 1), 0))
    w_in = pl.BlockSpec((None, D_MODEL, MOE_HIDDEN), lambda t, te, nt: (te[t], 0, 0))
    w_out = pl.BlockSpec((None, MOE_HIDDEN, D_MODEL), lambda t, te, nt: (te[t], 0, 0))
    return pl.pallas_call(
        _ffn_kernel,
        grid_spec=pltpu.PrefetchScalarGridSpec(
            num_scalar_prefetch=2,
            grid=(n_rows // MOE_TILE,),
            in_specs=[rows, rows, w_in, w_in, w_out] + [pl.BlockSpec(memory_space=pl.ANY)] * len(after),
            out_specs=[rows, rows, w_in, w_in, w_out]),
        out_shape=[jax.ShapeDtypeStruct((n_rows, quarter), jnp.int32)] * 2
                  + [jax.ShapeDtypeStruct(w_gate.shape, BF16), jax.ShapeDtypeStruct(w_up.shape, BF16),
                     jax.ShapeDtypeStruct(w_down.shape, BF16)],
        compiler_params=pltpu.CompilerParams(dimension_semantics=("arbitrary",), vmem_limit_bytes=VMEM_LIMIT),
        name="moe_ffn",
    )(tile_expert, n_tiles, xa, xb, w_gate, w_up, w_down, *after)


def _final_kernel(x1_ref, g1a_ref, g1b_ref, g2a_ref, g2b_ref, route_ref, gt2_ref, shf_ref, scf_ref, nfg_ref, y_ref):
    o1 = _unpack_bf16_pairs(g1a_ref[...], g1b_ref[...])
    o2 = _unpack_bf16_pairs(g2a_ref[...], g2b_ref[...])
    route = route_ref[...]
    moe = route[:, 2:3] * o1 + route[:, 3:4] * o2
    x2 = x1_ref[...] + gt2_ref[...] * moe
    y_ref[...] = _rmsnorm_mod(x2, nfg_ref[...], shf_ref[...], scf_ref[...])


def _final(x1, ga, gb, route, gt2, shf, scf, nfg, *, tm):
    bsz, seq, _ = x1.shape
    n_s = seq // tm
    n_blk = bsz * n_s
    quarter = ga.shape[1]
    tok = lambda width: pl.BlockSpec((None, tm, width), lambda b, s: (b, s, 0))
    first = pl.BlockSpec((tm, quarter), lambda b, s: (b * n_s + s, 0))
    second = pl.BlockSpec((tm, quarter), lambda b, s: (n_blk + b * n_s + s, 0))
    return pl.pallas_call(
        _final_kernel,
        grid=(bsz, n_s),
        in_specs=[tok(D_MODEL), first, first, second, second, tok(LANES),
                  _mod_spec(gt2, tm, seq), _mod_spec(shf, tm, seq), _mod_spec(scf, tm, seq), _const_spec(nfg)],
        out_specs=tok(D_MODEL),
        out_shape=jax.ShapeDtypeStruct((bsz, seq, D_MODEL), F32),
        compiler_params=pltpu.CompilerParams(dimension_semantics=("arbitrary", "arbitrary")),
        name="moe_final",
    )(x1, ga, gb, ga, gb, route, gt2, shf, scf, nfg)


def _positions_kernel(base_ref, route_ref, p1_ref, p2_ref):
    rows = route_ref[...]
    for e_row, r_row, out_ref in ((0, 4, p1_ref), (1, 5, p2_ref)):
        expert = rows[e_row:e_row + 1, :]
        start = jnp.zeros(expert.shape, jnp.int32)
        for e in range(MOE_EXPERTS):
            start = jnp.where(expert == float(e), base_ref[e], start)
        out_ref[...] = start + rows[r_row:r_row + 1, :].astype(jnp.int32)


def _positions(route_rows, base):
    bsz, _, seq = route_rows.shape
    out = pl.BlockSpec((None, 1, seq), lambda b, base: (b, 0, 0))
    return pl.pallas_call(
        _positions_kernel,
        grid_spec=pltpu.PrefetchScalarGridSpec(
            num_scalar_prefetch=1, grid=(bsz,),
            in_specs=[pl.BlockSpec((None, 8, seq), lambda b, base: (b, 0, 0))],
            out_specs=[out, out]),
        out_shape=[jax.ShapeDtypeStruct((bsz, 1, seq), jnp.int32)] * 2,
        compiler_params=pltpu.CompilerParams(dimension_semantics=("arbitrary",)),
        name="moe_positions",
    )(base, route_rows)


def _moe_dispatch(hpa, hpb, route_rows, counts):
    bsz, seq, quarter = hpa.shape
    n_tok = bsz * seq
    t_max = 2 * n_tok // MOE_TILE + MOE_EXPERTS
    cnt = counts[:, 0].astype(jnp.int32)
    tiles_e = jnp.maximum((cnt + MOE_TILE - 1) // MOE_TILE, 1)
    tile_end = jnp.cumsum(tiles_e)
    n_tiles = tile_end[-1:]
    base = (tile_end - tiles_e) * MOE_TILE
    t_ids = jnp.minimum(jnp.arange(t_max, dtype=jnp.int32), n_tiles[0] - 1)
    tile_expert = jnp.sum((t_ids[:, None] >= tile_end[None, :]).astype(jnp.int32), axis=1)
    pos1, pos2 = (p.reshape(1, n_tok) for p in _positions(route_rows, base))
    n_rows = t_max * MOE_TILE
    xa = _sc_scatter_rows(hpa.reshape(n_tok, quarter), pos1, pos2, n_rows)
    xb = _sc_scatter_rows(hpb.reshape(n_tok, quarter), pos1, pos2, n_rows)
    return xa, xb, jnp.concatenate([pos1, pos2], axis=1), tile_expert, n_tiles


def kernel(x_prompt, x_sample, cache_kv_w128, cache_kv_w512, cache_kv_w2048, c_prompt, c_sample,
           w_ada, b_ada, norm1_g, norm2_g, w_in, sgu_ln_g, sgu_ln_b, sgu_w, sgu_b, w_pa, w_pb, w_o,
           w_route_group, b_route_group, w_route_expert, b_route_expert, w_gate, w_up, w_down,
           normf_g, w_ada_final, b_ada_final):
    depth = w_ada.shape[0]
    assert depth == 1
    l = 0
    bp, seq, _ = x_prompt.shape
    bs, t_new, _ = x_sample.shape
    n_samp = bs * t_new

    c_all = jnp.concatenate([c_prompt, c_sample], axis=0)
    pad_rows = (-c_all.shape[0]) % 8
    c_all = jnp.pad(c_all, ((0, pad_rows), (0, 0)))
    mod = _adaln(c_all, w_ada[l], b_ada[l])

    def split_mods(m, n, lo, hi, per_token):
        parts = jnp.split(m[lo:hi], n, axis=-1)
        if per_token:
            return [p.reshape(1, hi - lo, D_MODEL) for p in parts]
        return [p.reshape(hi - lo, 1, D_MODEL) for p in parts]

    mods_p = split_mods(mod, 6, 0, bp, False)
    mods_s = split_mods(mod, 6, bp, bp + bs, True)

    row = lambda v: v.reshape(1, -1)
    w_in_b = w_in[l].astype(BF16)
    w_pa_b, w_pb_b, w_o_b = w_pa[l].astype(BF16), w_pb[l].astype(BF16), w_o[l].astype(BF16)
    w_re = jnp.transpose(w_route_expert[l], (0, 2, 1)).reshape(MOE_EXPERTS, D_MODEL)
    w_r = jnp.pad(jnp.concatenate([jnp.transpose(w_route_group[l]), w_re], axis=0),
                  ((0, ROUTE_ROWS - MOE_GROUPS - MOE_EXPERTS), (0, 0)))
    w_r_hi = w_r.astype(BF16)
    w_r_lo = (w_r - w_r_hi.astype(F32)).astype(BF16)
    b_r = jnp.pad(jnp.concatenate([b_route_group[l], b_route_expert[l].reshape(-1)]),
                  (0, ROUTE_ROWS - MOE_GROUPS - MOE_EXPERTS)).reshape(ROUTE_ROWS, 1)

    def inproj(x, mods, keeps, dils, tm, emit_vn_f32, kv_feature_major, after=None):
        return _inproj(x, mods[0], mods[1], row(norm1_g[l]), w_in_b, row(sgu_ln_g[l]), row(sgu_ln_b[l]),
                       tm=tm, keeps=keeps, dils=dils, emit_vn_f32=emit_vn_f32,
                       kv_feature_major=kv_feature_major, after=after)

    sgu_bt = jnp.transpose(sgu_b[l])

    def mix(x, mods, pin, attn_outs, attn_lses, tm, routed, span):
        u, vn, ga, gb = pin[3:7]
        return _mix(x, attn_outs, attn_lses, u, vn, ga, gb, sgu_w[l], sgu_bt,
                    w_pa_b, w_pb_b, w_o_b, mods[2], mods[3], mods[4], row(norm2_g[l]), w_r_hi, w_r_lo, b_r,
                    tm=tm, routed=routed, span=span)

    tm_p = 512
    keeps_p = tuple(min(win, seq) for win, _ in A_GROUPS)
    dils_p = tuple(dil for _, dil in A_GROUPS)
    pin = inproj(x_prompt, mods_p, keeps_p, dils_p, 256, False, True)
    attn_p = [_attn_prompt(pin[g], g) for g in range(N_GROUPS)]
    x1, hpa, hpb, route_rows, route, counts = mix(x_prompt, mods_p, pin, [r[0] for r in attn_p],
                                                  [r[1] for r in attn_p], tm_p, True, B_CHUNK)

    xs = x_sample.reshape(1, n_samp, D_MODEL)
    sin = inproj(xs, mods_s, (n_samp,) * N_GROUPS, (1,) * N_GROUPS, n_samp, True, False, after=counts)
    mod_f = _adaln(c_all, w_ada_final, b_ada_final, after=counts)
    fin_p = split_mods(mod_f, 2, 0, bp, False)
    fin_s = split_mods(mod_f, 2, bp, bp + bs, True)

    xa, xb, pos, tile_expert, n_tiles = _moe_dispatch(hpa, hpb, route_rows, counts)
    kv_s_out = [kv.reshape(depth, bs, -1, 2, HEADS, HEAD_DIM) for kv in sin[7:10]]
    oa, ob, w_gate16, w_up16, w_down16 = _ffn(xa, xb, tile_expert, n_tiles, w_gate[l], w_up[l], w_down[l],
                                              after=(sin[0], fin_s[0], *kv_s_out))

    caches = (cache_kv_w128[l], cache_kv_w512[l], cache_kv_w2048[l])
    o_s = _attn_sample([q.reshape(bs, t_new, 3 * A_WIDTH) for q in sin[:3]], caches, after=oa)
    x1_s, h2_s, comb_s = mix(xs, mods_s, sin, [o_s.reshape(1, n_samp, A_WIDTH)], None, n_samp, False,
                             min(t_new, B_CHUNK))
    y_s = _moe(h2_s, comb_s, w_gate16, w_up16, w_down16, x1_s, mods_s[5], fin_s[0], fin_s[1], row(normf_g),
               tm=n_samp)

    ga_rows = _sc_gather_rows(oa, pos)
    gb_rows = _sc_gather_rows(ob, pos, o_s)
    y_p = _final(x1, ga_rows, gb_rows, route, mods_p[5], fin_p[0], fin_p[1], row(normf_g), tm=2 * tm_p)
    kv_p, extra = pin[7:10], sin[10:]

    def kv_out_t(a, b):
        a = a.reshape(b, 2, HEADS, HEAD_DIM, a.shape[-1])
        return jnp.transpose(a, (0, 4, 1, 2, 3)).reshape(depth, b, -1, 2, HEADS, HEAD_DIM)

    return (y_p, y_s.reshape(bs, t_new, D_MODEL),
            kv_out_t(kv_p[0], bp), kv_out_t(kv_p[1], bp), kv_out_t(kv_p[2], bp),
            kv_s_out[0], kv_s_out[1], kv_s_out[2],
            extra[0].reshape(depth, bs, t_new, B_WIDTH))
```

```python
import functools

import numpy as np
import jax
import jax.numpy as jnp
from jax import lax
from jax.experimental import pallas as pl
from jax.experimental.pallas import tpu as pltpu
from jax.experimental.pallas import tpu_sc as plsc

F32 = jnp.float32
BF16 = jnp.bfloat16

D_MODEL = 1024
A_GROUPS = ((128, 1), (512, 4), (2048, 16))
N_GROUPS = 3
HEADS = 8
HEAD_DIM = 64
A_WIDTH = HEADS * HEAD_DIM
A_STEPS = 128
A_BLOCK = 128
B_WIDTH = 1024
B_GROUPS = 8
B_CHUNK = 128
MOE_GROUPS = 4
MOE_PER_GROUP = 4
MOE_EXPERTS = 16
MOE_HIDDEN = 512
EPS = 1e-6
N_QKV = 3 * N_GROUPS * A_WIDTH
IN_COLS = N_QKV + 2 * B_WIDTH + 2 * D_MODEL
COL_CHUNK = 512
LANES = 128
LSE_REP = LANES // HEADS
ROUTE_OFF = MOE_GROUPS
ROUTE_ROWS = 32
NEG = -1e30
LOG2E = 1.4426950408889634
VMEM_LIMIT = 56 * 1024 * 1024
ATTN_BLOCKS_PER_STEP = 16
DENSE_MOE_EXPERTS_PER_STEP = 4
MIX_SUB_ROWS = 512
MOE_TILE = 512
SC_WINDOW = 128
SC_CORES = 2


def _sigmoid(x):
    return 1.0 / (1.0 + jnp.exp(-x))


def _gelu_tanh(x):
    return x * (0.5 * (1.0 + jnp.tanh(0.7978845608028654 * (x + 0.044715 * (x * x * x)))))


def _rmsnorm_mod(x, gain, shift, scale):
    y = x * lax.rsqrt(jnp.mean(x * x, axis=-1, keepdims=True) + EPS)
    return y * gain * (1.0 + scale) + shift


def _mod_spec(mod, tm, seq):
    if mod.shape[1] == 1:
        return pl.BlockSpec((None, 1, D_MODEL), lambda b, s, *_: (b, 0, 0))
    return pl.BlockSpec((None, tm * mod.shape[1] // seq, D_MODEL), lambda b, s, *_: (b, s, 0))


def _mod_rows(ref, rows):
    m = ref[...]
    if m.shape[0] in (1, rows):
        return m
    rep = rows // m.shape[0]
    return jnp.broadcast_to(m[:, None, :], (m.shape[0], rep, m.shape[1])).reshape(rows, m.shape[1])


def _const_spec(arr):
    nd = arr.ndim
    return pl.BlockSpec(arr.shape, lambda *_: (0,) * nd)


def _adaln_kernel(c_ref, w_ref, b_ref, *rest):
    o_ref = rest[-1]
    c = c_ref[...]
    a = (c * _sigmoid(c)).astype(BF16)
    o_ref[...] = jnp.dot(a, w_ref[...].astype(BF16), preferred_element_type=F32) + b_ref[...]


def _adaln(c, w, b, tn=1024, after=None):
    rows, ncols = c.shape[0], w.shape[1]
    ins = [c, w, b.reshape(1, ncols)]
    in_specs = [pl.BlockSpec((rows, D_MODEL), lambda j: (0, 0)),
                pl.BlockSpec((D_MODEL, tn), lambda j: (0, j)),
                pl.BlockSpec((1, tn), lambda j: (0, j))]
    if after is not None:
        ins.append(after)
        in_specs.append(pl.BlockSpec(memory_space=pl.ANY))
    return pl.pallas_call(
        _adaln_kernel,
        grid=(ncols // tn,),
        in_specs=in_specs,
        out_specs=pl.BlockSpec((rows, tn), lambda j: (0, j)),
        out_shape=jax.ShapeDtypeStruct((rows, ncols), F32),
        compiler_params=pltpu.CompilerParams(dimension_semantics=("arbitrary",)),
        name="adaln",
    )(*ins)


def _inproj_kernel(x_ref, sh_ref, sc_ref, g_ref, w_ref, lng_ref, lnb_ref, *refs, tm, keeps, dils, emit_vn_f32,
                   has_after, kv_feature_major):
    if has_after:
        refs = refs[1:]
    qkv0_ref, qkv1_ref, qkv2_ref, u_ref, vn_ref, ga_ref, gb_ref, kv0_ref, kv1_ref, kv2_ref = refs[:10]
    rest = refs[10:]
    perm_scr = rest[-1]
    h = _rmsnorm_mod(x_ref[...], g_ref[...], _mod_rows(sh_ref, tm), _mod_rows(sc_ref, tm)).astype(BF16)

    def proj(c):
        return jnp.dot(h, w_ref[:, c * COL_CHUNK:(c + 1) * COL_CHUNK], preferred_element_type=F32)

    def cols(c):
        return slice(c * COL_CHUNK, (c + 1) * COL_CHUNK)

    qkv_refs = (qkv0_ref, qkv1_ref, qkv2_ref)

    def put_qkv(g, t, z):
        dil = dils[g]
        if dil == 1:
            qkv_refs[g][0, :, cols(t)] = z.astype(BF16)
            return
        slot = perm_scr.at[t % 2]
        n_slab = COL_CHUNK // LANES
        for j in range(n_slab):
            slot[j] = z[:, j * LANES:(j + 1) * LANES]
        for r in range(dil):
            rows = jnp.concatenate([slot[j, pl.ds(r, tm // dil, stride=dil), :] for j in range(n_slab)], axis=1)
            qkv_refs[g][r, :, cols(t)] = rows.astype(BF16)

    for g in range(N_GROUPS):
        put_qkv(g, 0, proj(g) * (HEAD_DIM ** -0.5 * LOG2E))

    kv_refs = (kv0_ref, kv1_ref, kv2_ref)
    for g in range(N_GROUPS):
        keep = keeps[g]
        for t in (1, 2):
            z = proj(t * N_GROUPS + g)
            put_qkv(g, t, z)
            if kv_feature_major:
                zt = jnp.transpose(z if keep >= tm else z[tm - keep:, :])
                kv_refs[g][(t - 1) * A_WIDTH:t * A_WIDTH, :] = zt
            else:
                kv_refs[g][:, cols(t - 1)] = z if keep >= tm else z[tm - keep:, :]

    base = N_QKV // COL_CHUNK
    for c in range(2):
        u_ref[:, cols(c)] = _gelu_tanh(proj(base + c)).astype(BF16)

    vs = [_gelu_tanh(proj(base + 2 + c)) for c in range(2)]
    mu = (jnp.sum(vs[0], axis=-1, keepdims=True) + jnp.sum(vs[1], axis=-1, keepdims=True)) * (1.0 / B_WIDTH)
    ds = [v - mu for v in vs]
    var = (jnp.sum(ds[0] * ds[0], axis=-1, keepdims=True)
           + jnp.sum(ds[1] * ds[1], axis=-1, keepdims=True)) * (1.0 / B_WIDTH)
    inv = lax.rsqrt(var + EPS)
    for c in range(2):
        vn = ds[c] * inv * lng_ref[:, cols(c)] + lnb_ref[:, cols(c)]
        vn_ref[:, cols(c)] = vn.astype(BF16)
        if emit_vn_f32:
            rest[0][:, cols(c)] = vn

    for c in range(2):
        ga_ref[:, cols(c)] = _sigmoid(proj(base + 4 + c)).astype(BF16)
        gb_ref[:, cols(c)] = _sigmoid(proj(base + 6 + c)).astype(BF16)


def _inproj(x, sh, sc, gain, w_in, ln_g, ln_b, *, tm, keeps, dils, emit_vn_f32, kv_feature_major, after=None):
    bsz, seq, _ = x.shape
    n_s = seq // tm
    tok = lambda width: pl.BlockSpec((None, tm, width), lambda b, s: (b, s, 0))

    def kv_spec(keep):
        first = n_s - max(keep // tm, 1)
        rows = min(keep, tm)
        if kv_feature_major:
            return pl.BlockSpec((None, 2 * A_WIDTH, rows), lambda b, s: (b, 0, jnp.maximum(s - first, 0)))
        return pl.BlockSpec((None, rows, 2 * A_WIDTH), lambda b, s: (b, jnp.maximum(s - first, 0), 0))

    out_specs, out_shape = [], []
    for dil in dils:
        out_specs.append(pl.BlockSpec((None, dil, tm // dil, 3 * A_WIDTH), lambda b, s: (b, 0, s, 0)))
        out_shape.append(jax.ShapeDtypeStruct((bsz, dil, seq // dil, 3 * A_WIDTH), BF16))
    out_specs += [tok(B_WIDTH), tok(B_WIDTH), tok(D_MODEL), tok(D_MODEL)]
    out_shape += [jax.ShapeDtypeStruct((bsz, seq, D_MODEL), BF16)] * 4
    for keep in keeps:
        out_specs.append(kv_spec(keep))
        kv_shape = (bsz, 2 * A_WIDTH, keep) if kv_feature_major else (bsz, keep, 2 * A_WIDTH)
        out_shape.append(jax.ShapeDtypeStruct(kv_shape, F32))
    if emit_vn_f32:
        out_specs.append(tok(B_WIDTH))
        out_shape.append(jax.ShapeDtypeStruct((bsz, seq, B_WIDTH), F32))

    ins = [x, sh, sc, gain, w_in, ln_g, ln_b]
    in_specs = [tok(D_MODEL), _mod_spec(sh, tm, seq), _mod_spec(sc, tm, seq), _const_spec(gain),
                pl.BlockSpec(w_in.shape, lambda b, s: (0, 0), pipeline_mode=pl.Buffered(1)),
                _const_spec(ln_g), _const_spec(ln_b)]
    if after is not None:
        ins.append(after)
        in_specs.append(pl.BlockSpec(memory_space=pl.ANY))
    scratch = [pltpu.VMEM((2, COL_CHUNK // LANES, tm, LANES), F32)]
    return pl.pallas_call(
        functools.partial(_inproj_kernel, tm=tm, keeps=keeps, dils=dils, emit_vn_f32=emit_vn_f32,
                          has_after=after is not None, kv_feature_major=kv_feature_major),
        grid=(bsz, n_s),
        in_specs=in_specs,
        out_specs=out_specs,
        out_shape=out_shape,
        scratch_shapes=scratch,
        compiler_params=pltpu.CompilerParams(dimension_semantics=("arbitrary", "arbitrary"),
                                             vmem_limit_bytes=VMEM_LIMIT),
        name="inproj",
    )(*ins)


def _attn_kernel(q_ref, kp_ref, kc_ref, vp_ref, vc_ref, bias0_ref, bias_ref, o_ref, lse_ref, *, n_seq, q_blocks):
    lane = lax.broadcasted_iota(jnp.int32, (A_BLOCK, LANES), 1)
    low = lane < HEAD_DIM
    zero = jnp.zeros((), BF16)
    for seq_i, i in [(a, b) for a in range(n_seq) for b in range(q_blocks)]:
        rows = (seq_i, slice(i * A_BLOCK, (i + 1) * A_BLOCK))
        q = q_ref[rows[0], rows[1], :]
        key_rows = slice((i - 1) * A_BLOCK, (i + 1) * A_BLOCK)
        if i == 0:
            k = jnp.concatenate([kp_ref[seq_i], kc_ref[seq_i, :A_BLOCK, :]], axis=0)
            v = jnp.concatenate([vp_ref[seq_i], vc_ref[seq_i, :A_BLOCK, :]], axis=0)
        else:
            k, v = kc_ref[seq_i, key_rows, :], vc_ref[seq_i, key_rows, :]
        b_ref = bias0_ref if i == 0 else bias_ref
        lse_tile = jnp.zeros((A_BLOCK, LANES), F32)
        for j in range(HEADS // 2):
            pair = slice(j * LANES, (j + 1) * LANES)
            qp, kpair, vpair = q[:, pair], k[:, pair], v[:, pair]
            outs = []
            for e in range(2):
                h = 2 * j + e
                qm = jnp.where(low if e == 0 else jnp.logical_not(low), qp, zero)
                s = lax.dot_general(qm, kpair, (((1,), (1,)), ((), ())), preferred_element_type=F32)
                s = s + b_ref[h]
                m = jnp.max(s, axis=-1, keepdims=True)
                p = jnp.exp2(s - m)
                den = jnp.sum(p, axis=-1, keepdims=True)
                o = jnp.dot(p.astype(BF16), vpair, preferred_element_type=F32)
                first = h * LSE_REP
                lse_tile = jnp.where((lane >= first) & (lane < first + LSE_REP // 2), m,
                                     jnp.where((lane >= first + LSE_REP // 2) & (lane < first + LSE_REP), den,
                                               lse_tile))
                outs.append(o)
            o_ref[rows[0], rows[1], pair] = jnp.where(low, outs[0], outs[1]).astype(BF16)
        lse_ref[rows[0], rows[1], :] = lse_tile


def _attn_bias(g):
    _, dil = A_GROUPS[g]
    n = N_GROUPS * HEADS
    e = np.arange(1, n + 1, dtype=np.float32)
    slopes = np.exp2(-8.0 * e / n).astype(np.float32).reshape(N_GROUPS, HEADS)[g]
    qi = np.arange(A_BLOCK)[:, None]
    ki = np.arange(2 * A_BLOCK)[None, :]
    delta = qi + A_BLOCK - ki
    band = (delta >= 0) & (delta <= A_STEPS)
    dist = (delta * dil).astype(np.float32)
    bias = (-slopes[:, None, None] * dist[None] * np.float32(LOG2E)).astype(np.float32)
    out = np.empty((2, HEADS, A_BLOCK, 2 * A_BLOCK), np.float32)
    out[1] = np.where(band[None], bias, NEG)
    out[0] = np.where((band & (ki >= A_BLOCK))[None], bias, NEG)
    return jnp.asarray(out)


def _attn_prompt(qkv, g):
    bsz, dil, steps, _ = qkv.shape
    qb = min(ATTN_BLOCKS_PER_STEP, steps // A_BLOCK)
    ns = min(ATTN_BLOCKS_PER_STEP // qb, dil)
    rows = qb * A_BLOCK
    n_steps = steps // rows
    bias = _attn_bias(g)

    def blk(t, prev):
        if prev:
            return pl.BlockSpec((None, ns, A_BLOCK, A_WIDTH),
                                lambda b, r, n: (b, r, jnp.maximum(n * qb - 1, 0), t))
        return pl.BlockSpec((None, ns, rows, A_WIDTH), lambda b, r, n: (b, r, n, t))

    bias_blk = lambda pick: pl.BlockSpec((None, HEADS, A_BLOCK, 2 * A_BLOCK), lambda b, r, n: (pick(n), 0, 0, 0))
    return pl.pallas_call(
        functools.partial(_attn_kernel, n_seq=ns, q_blocks=qb),
        grid=(bsz, dil // ns, n_steps),
        in_specs=[blk(0, False), blk(1, True), blk(1, False), blk(2, True), blk(2, False),
                  bias_blk(lambda n: jnp.minimum(n, 1)), bias_blk(lambda n: 1)],
        out_specs=[pl.BlockSpec((None, ns, rows, A_WIDTH), lambda b, r, n: (b, r, n, 0)),
                   pl.BlockSpec((None, ns, rows, LANES), lambda b, r, n: (b, r, n, 0))],
        out_shape=[jax.ShapeDtypeStruct((bsz, dil, steps, A_WIDTH), BF16),
                   jax.ShapeDtypeStruct((bsz, dil, steps, LANES), F32)],
        compiler_params=pltpu.CompilerParams(dimension_semantics=("arbitrary",) * 3),
        name=f"attn_prompt_g{g}",
    )(qkv, qkv, qkv, qkv, qkv, bias, bias)


def _attn_sample_kernel(q0_ref, q1_ref, q2_ref, c0_ref, c1_ref, c2_ref, bc0_ref, bc1_ref, bc2_ref,
                        bn0_ref, bn1_ref, bn2_ref, after_ref, o_ref, *, t_new):
    del after_ref
    n_rows = HEADS * t_new
    row = lax.broadcasted_iota(jnp.int32, (n_rows, A_WIDTH), 0)
    lane = lax.broadcasted_iota(jnp.int32, (n_rows, A_WIDTH), 1)
    head_mask = (row // t_new) == (lane // HEAD_DIM)
    pad = jnp.zeros((LANES - t_new, A_WIDTH), F32)
    outs, lses = [], []
    for qkv_ref, c_ref, bc_ref, bn_ref in ((q0_ref, c0_ref, bc0_ref, bn0_ref), (q1_ref, c1_ref, bc1_ref, bn1_ref),
                                           (q2_ref, c2_ref, bc2_ref, bn2_ref)):
        q = qkv_ref[:, :A_WIDTH].astype(F32)
        k_new = qkv_ref[:, A_WIDTH:2 * A_WIDTH].astype(F32)
        v_new = qkv_ref[:, 2 * A_WIDTH:].astype(F32)
        k_new = jnp.concatenate([k_new, pad], axis=0).astype(BF16)
        v_new = jnp.concatenate([v_new, pad], axis=0).astype(BF16)
        q_rows = jnp.where(head_mask, jnp.concatenate([q] * HEADS, axis=0), 0.0).astype(BF16)
        k_buf_t = c_ref[:A_WIDTH, :].astype(BF16)
        v_buf_t = c_ref[A_WIDTH:, :].astype(BF16)
        nt = (((1,), (1,)), ((), ()))
        s_buf = jnp.dot(q_rows, k_buf_t, preferred_element_type=F32) + bc_ref[...]
        s_new = lax.dot_general(q_rows, k_new, nt, preferred_element_type=F32) + bn_ref[...]
        m = jnp.maximum(jnp.max(s_buf, axis=-1, keepdims=True), jnp.max(s_new, axis=-1, keepdims=True))
        p_buf = jnp.exp2(s_buf - m)
        p_new = jnp.exp2(s_new - m)
        den = jnp.sum(p_buf, axis=-1, keepdims=True) + jnp.sum(p_new, axis=-1, keepdims=True)
        o = (lax.dot_general(p_buf.astype(BF16), v_buf_t, nt, preferred_element_type=F32)
             + jnp.dot(p_new.astype(BF16), v_new, preferred_element_type=F32)) / den
        outs.append(o)
        lses.append(m + jnp.log2(den))
    top = jnp.maximum(jnp.maximum(lses[0], lses[1]), lses[2])
    ws = [jnp.exp2(l - top) for l in lses]
    tot = ws[0] + ws[1] + ws[2]
    acc = (ws[0] / tot) * outs[0] + (ws[1] / tot) * outs[1] + (ws[2] / tot) * outs[2]
    acc = jnp.where(head_mask, acc, 0.0).reshape(HEADS, t_new, A_WIDTH)
    o_ref[...] = jnp.sum(acc, axis=0).astype(BF16)


def _sample_bias(g, t_new, buf):
    _, dil = A_GROUPS[g]
    n = N_GROUPS * HEADS
    e = np.arange(1, n + 1, dtype=np.float32)
    slopes = np.exp2(-8.0 * e / n).astype(np.float32).reshape(N_GROUPS, HEADS)[g]
    t = np.arange(t_new)[:, None]
    idx = np.concatenate([np.arange(buf), buf + np.arange(LANES)])[None, :]
    dist = buf + t - idx
    valid = (dist >= 0) & (dist % dil == 0) & (dist <= A_STEPS * dil) & (idx < buf + t_new)
    bias = -slopes[:, None, None] * dist.astype(np.float32)[None] * np.float32(LOG2E)
    bias = np.where(valid[None], bias, NEG).astype(np.float32).reshape(HEADS * t_new, buf + LANES)
    return jnp.asarray(bias[:, :buf]), jnp.asarray(bias[:, buf:])


def _attn_sample(qkvs, caches, *, after):
    bsz, t_new, _ = qkvs[0].shape
    cache_v = [jnp.transpose(c, (0, 2, 3, 4, 1)).reshape(bsz, 2 * A_WIDTH, c.shape[1]) for c in caches]
    biases = [_sample_bias(g, t_new, cache_v[g].shape[2]) for g in range(N_GROUPS)]
    bcs = [b[0] for b in biases]
    bns = [b[1] for b in biases]
    return pl.pallas_call(
        functools.partial(_attn_sample_kernel, t_new=t_new),
        grid=(bsz,),
        in_specs=[pl.BlockSpec((None, t_new, 3 * A_WIDTH), lambda b: (b, 0, 0))] * N_GROUPS
                 + [pl.BlockSpec((None, 2 * A_WIDTH, c.shape[2]), lambda b: (b, 0, 0)) for c in cache_v]
                 + [_const_spec(b) for b in bcs] + [_const_spec(b) for b in bns]
                 + [pl.BlockSpec(memory_space=pl.ANY)],
        out_specs=pl.BlockSpec((None, t_new, A_WIDTH), lambda b: (b, 0, 0)),
        out_shape=jax.ShapeDtypeStruct((bsz, t_new, A_WIDTH), BF16),
        compiler_params=pltpu.CompilerParams(dimension_semantics=("arbitrary",),
                                             vmem_limit_bytes=VMEM_LIMIT),
        name="attn_sample",
    )(*qkvs, *cache_v, *bcs, *bns, after)


def _first_max4(v):
    top = jnp.maximum(jnp.maximum(v[0], v[1]), jnp.maximum(v[2], v[3]))
    idx = jnp.where(v[0] == top, 0.0, jnp.where(v[1] == top, 1.0, jnp.where(v[2] == top, 2.0, 3.0)))
    return top, idx


def _route(lt):
    row = lambda k: lt[k:k + 1, :]
    g = [row(k) for k in range(MOE_GROUPS)]
    gmax, g_idx = _first_max4(g)
    g_prob = 1.0 / (jnp.exp(g[0] - gmax) + jnp.exp(g[1] - gmax) + jnp.exp(g[2] - gmax) + jnp.exp(g[3] - gmax))
    cand = []
    for k in range(MOE_PER_GROUP):
        c = row(ROUTE_OFF + MOE_PER_GROUP * (MOE_GROUPS - 1) + k)
        for gi in range(MOE_GROUPS - 2, -1, -1):
            c = jnp.where(g_idx == float(gi), row(ROUTE_OFF + MOE_PER_GROUP * gi + k), c)
        cand.append(c)
    e1, i1 = _first_max4(cand)
    rest = [jnp.where(i1 == float(k), -jnp.inf, cand[k]) for k in range(MOE_PER_GROUP)]
    e2, i2 = _first_max4(rest)
    t = jnp.exp(e2 - e1)
    w1 = 1.0 / (1.0 + t)
    w2 = t / (1.0 + t)
    return MOE_PER_GROUP * g_idx + i1, MOE_PER_GROUP * g_idx + i2, w1 * g_prob, w2 * g_prob


def _pack_bf16_pairs(x):
    q = x.shape[1] // 4
    bits = lax.bitcast_convert_type(x.astype(BF16).astype(F32), jnp.int32)
    pack = lambda hi, lo: hi | lax.shift_right_logical(lo, 16)
    return pack(bits[:, :q], bits[:, 2 * q:3 * q]), pack(bits[:, q:2 * q], bits[:, 3 * q:])


def _unpack_bf16_pairs(pa, pb):
    hi = lambda p: lax.bitcast_convert_type(p & jnp.int32(-65536), F32)
    lo = lambda p: lax.bitcast_convert_type(lax.shift_left(p, 16), F32)
    return jnp.concatenate([hi(pa), hi(pb), lo(pa), lo(pb)], axis=1)


def _mix_kernel(*refs, tm, dils, routed, span):
    n_attn = len(dils)
    x_ref = refs[0]
    if n_attn > 1:
        o_refs = refs[1:1 + n_attn]
        l_refs = refs[1 + n_attn:1 + 2 * n_attn]
        exp_ref = refs[1 + 2 * n_attn]
        pos = 2 + 2 * n_attn
    else:
        o_refs = refs[1:2]
        pos = 2
    (u_ref, vn_ref, ga_ref, gb_ref, sw_ref, sbt_ref, wpa_ref, wpb_ref, wo_ref,
     gt1_ref, sh2_ref, sc2_ref, n2g_ref, wrh_ref, wrl_ref, br_ref) = refs[pos:pos + 16]
    n_out = 6 if routed else 3
    out_refs = refs[pos + 16:pos + 16 + n_out]
    scr = list(refs[pos + 16 + n_out:])
    x1_ref = out_refs[0]
    ob_scr = scr.pop(0)

    def to_positions(ref, scr, dil):
        if dil == 1:
            return ref[0].astype(F32)
        n_slab = scr.shape[0]
        for r in range(dil):
            rows = ref[r].astype(F32)
            for j in range(n_slab):
                scr[j, pl.ds(r, tm // dil, stride=dil), :] = rows[:, j * LANES:(j + 1) * LANES]
        return jnp.concatenate([scr[j] for j in range(n_slab)], axis=1)

    if n_attn > 1:
        o_scr, l_scr = scr[0], scr[1]
        stats = [to_positions(l_refs[g], l_scr.at[g], dils[g]) for g in range(n_attn)]
        lane = lax.broadcasted_iota(jnp.int32, stats[0].shape, 1)
        is_max = (lane & (LSE_REP - 1)) < LSE_REP // 2
        half = LSE_REP // 2
        maxes = [jnp.where(is_max, s, pltpu.roll(s, half, axis=1)) for s in stats]
        dens = [jnp.where(is_max, pltpu.roll(s, LANES - half, axis=1), s) for s in stats]
        lses = [m + jnp.log2(d) for m, d in zip(maxes, dens)]
        top = functools.reduce(jnp.maximum, lses)
        tot = functools.reduce(lambda a, b: a + b, [jnp.exp2(l - top) for l in lses])
        o_a = None
        for g in range(n_attn):
            w = jnp.exp2(maxes[g] - top) / tot
            hi = w.astype(BF16)
            lo = (w - hi.astype(F32)).astype(BF16)
            w_exp = jnp.dot(jnp.concatenate([hi, lo], axis=1), exp_ref[...], preferred_element_type=F32)
            term = w_exp * to_positions(o_refs[g], o_scr.at[g], dils[g])
            o_a = term if o_a is None else o_a + term
        o_a = o_a.astype(BF16)
    else:
        o_a = o_refs[0][...]

    r_i = lax.broadcasted_iota(jnp.int32, (span, span), 0)
    c_i = lax.broadcasted_iota(jnp.int32, (span, span), 1)
    tril = r_i >= c_i
    for gi in range(B_GROUPS):
        gcols = slice(gi * LANES, (gi + 1) * LANES)
        b_s = sbt_ref[:span, gi:gi + 1]
        if span == B_CHUNK:
            w_s = jnp.where(tril, sw_ref[gi], 0.0).astype(BF16)
            n_chunk = tm // span
            vn_wide = jnp.concatenate([vn_ref[c * span:(c + 1) * span, gcols] for c in range(n_chunk)], axis=1)
            mixed = jnp.dot(w_s, vn_wide, preferred_element_type=F32) + b_s
            for c in range(n_chunk):
                rows = slice(c * span, (c + 1) * span)
                ob_scr[rows, gcols] = (u_ref[rows, gcols].astype(F32)
                                       * mixed[:, c * LANES:(c + 1) * LANES]).astype(BF16)
        else:
            w_s = jnp.where(tril, sw_ref[gi, :span, :span], 0.0)
            vn3 = vn_ref[:, gcols].astype(F32).reshape(tm // span, span, LANES)
            mixed = jnp.broadcast_to(b_s, (span, LANES))[None]
            for s in range(span):
                w_col = jnp.broadcast_to(w_s[:, s:s + 1], (span, LANES))[None]
                mixed = mixed + w_col * jnp.broadcast_to(vn3[:, s:s + 1, :], vn3.shape)
            u3 = u_ref[:, gcols].astype(F32).reshape(tm // span, span, LANES)
            ob_scr[:, gcols] = (u3 * mixed).reshape(tm, LANES).astype(BF16)

    sub = min(tm, MIX_SUB_ROWS)
    if routed:
        carry = scr[-1]
        first_step = (pl.program_id(0) == 0) & (pl.program_id(1) == 0)
        seen = jnp.where(first_step, 0.0, carry[:, 0:1])
        r_i = lax.broadcasted_iota(jnp.int32, (sub, sub), 0)
        c_i = lax.broadcasted_iota(jnp.int32, (sub, sub), 1)
        earlier = jnp.where(r_i < c_i, 1.0, 0.0).astype(BF16)
        e_iota = lax.broadcasted_iota(jnp.int32, (MOE_EXPERTS, sub), 0).astype(F32)
    nt = (((1,), (1,)), ((), ()))
    for r0 in range(0, tm, sub):
        rs = slice(r0, r0 + sub)
        mod = lambda ref: ref[...] if ref.shape[0] == 1 else _mod_rows(ref, tm)[rs, :]
        pa = jnp.dot(o_a[rs], wpa_ref[...], preferred_element_type=F32)
        pb = jnp.dot(ob_scr[rs, :], wpb_ref[...], preferred_element_type=F32)
        mix = (ga_ref[rs, :].astype(F32) * pa + gb_ref[rs, :].astype(F32) * pb).astype(BF16)
        x1 = x_ref[rs, :] + mod(gt1_ref) * jnp.dot(mix, wo_ref[...], preferred_element_type=F32)
        x1_ref[rs, :] = x1
        h2 = _rmsnorm_mod(x1, n2g_ref[...], mod(sh2_ref), mod(sc2_ref))
        h_hi = h2.astype(BF16)
        h_lo = (h2 - h_hi.astype(F32)).astype(BF16)
        lt = (lax.dot_general(wrh_ref[...], h_hi, nt, preferred_element_type=F32)
              + lax.dot_general(wrh_ref[...], h_lo, nt, preferred_element_type=F32)
              + lax.dot_general(wrl_ref[...], h_hi, nt, preferred_element_type=F32)) + br_ref[...]
        ex1, ex2, w1, w2 = _route(lt)
        if not routed:
            _, h2_ref, comb_ref = out_refs
            h2_ref[rs, :] = h_hi
            e_lane = lax.broadcasted_iota(jnp.int32, (LANES, sub), 0).astype(F32) - float(ROUTE_OFF)
            comb_t = jnp.where(e_lane == ex1, w1, jnp.where(e_lane == ex2, w2, 0.0))
            comb_ref[rs, :] = jnp.transpose(comb_t)
            continue

        _, hpa_ref, hpb_ref, route_ref, wcol_ref, cnt_ref = out_refs
        hpa_ref[rs, :], hpb_ref[rs, :] = _pack_bf16_pairs(h2)
        hot1, hot2 = e_iota == ex1, e_iota == ex2
        onehot = jnp.where(hot1 | hot2, 1.0, 0.0)
        rank = jnp.dot(onehot.astype(BF16), earlier, preferred_element_type=F32) + seen
        r1 = jnp.sum(jnp.where(hot1, rank, 0.0), axis=0, keepdims=True)
        r2 = jnp.sum(jnp.where(hot2, rank, 0.0), axis=0, keepdims=True)
        seen = seen + jnp.sum(onehot, axis=1, keepdims=True)
        fields = jnp.concatenate([ex1, ex2, w1, w2, r1, r2, jnp.zeros((2, sub), F32)], axis=0)
        route_ref[:, rs] = fields
        wcol_ref[rs, :] = jnp.transpose(jnp.concatenate([fields, jnp.zeros((LANES - 8, sub), F32)], axis=0))
    if routed:
        carry[...] = jnp.broadcast_to(seen, carry.shape)
        cnt_ref[...] = jnp.broadcast_to(seen, cnt_ref.shape)


def _expand_matrix():
    e = np.zeros((LANES, A_WIDTH), np.float32)
    for h in range(HEADS):
        e[h * LSE_REP, h * HEAD_DIM:(h + 1) * HEAD_DIM] = 1.0
    return jnp.asarray(np.concatenate([e, e], axis=0), dtype=BF16)


def _mix(x, attn_outs, attn_lses, u, vn, ga, gb, sgu_w, sgu_bt, w_pa, w_pb, w_o,
         gt1, sh2, sc2, n2g, w_r_hi, w_r_lo, b_r, *, tm, routed, span):
    bsz, seq, _ = x.shape
    n_attn = len(attn_outs)
    tok = lambda width: pl.BlockSpec((None, tm, width), lambda b, s: (b, s, 0))
    ins = [x] + list(attn_outs)
    scratch = [pltpu.VMEM((tm, B_WIDTH), BF16)]
    if n_attn > 1:
        dils = tuple(o.shape[1] for o in attn_outs)
        res = lambda dil, width: pl.BlockSpec((None, dil, tm // dil, width), lambda b, s: (b, 0, s, 0))
        e2 = _expand_matrix()
        ins += list(attn_lses) + [e2]
        specs = ([tok(D_MODEL)] + [res(d, A_WIDTH) for d in dils] + [res(d, LANES) for d in dils]
                 + [_const_spec(e2)])
        scratch += [pltpu.VMEM((n_attn, A_WIDTH // LANES, tm, LANES), F32), pltpu.VMEM((n_attn, 1, tm, LANES), F32)]
    else:
        dils = (1,)
        specs = [tok(D_MODEL), tok(A_WIDTH)]
    ins += [u, vn, ga, gb, sgu_w, sgu_bt, w_pa, w_pb, w_o, gt1, sh2, sc2, n2g, w_r_hi, w_r_lo, b_r]
    specs += [tok(B_WIDTH), tok(B_WIDTH), tok(D_MODEL), tok(D_MODEL),
              _const_spec(sgu_w), _const_spec(sgu_bt), _const_spec(w_pa), _const_spec(w_pb), _const_spec(w_o),
              _mod_spec(gt1, tm, seq), _mod_spec(sh2, tm, seq), _mod_spec(sc2, tm, seq), _const_spec(n2g),
              _const_spec(w_r_hi), _const_spec(w_r_lo), _const_spec(b_r)]
    if routed:
        quarter = D_MODEL // 4
        out_specs = [tok(D_MODEL), tok(quarter), tok(quarter),
                     pl.BlockSpec((None, 8, tm), lambda b, s: (b, 0, s)), tok(LANES),
                     pl.BlockSpec((MOE_EXPERTS, LANES), lambda b, s: (0, 0))]
        out_shape = [jax.ShapeDtypeStruct((bsz, seq, D_MODEL), F32),
                     jax.ShapeDtypeStruct((bsz, seq, quarter), jnp.int32),
                     jax.ShapeDtypeStruct((bsz, seq, quarter), jnp.int32),
                     jax.ShapeDtypeStruct((bsz, 8, seq), F32),
                     jax.ShapeDtypeStruct((bsz, seq, LANES), F32),
                     jax.ShapeDtypeStruct((MOE_EXPERTS, LANES), F32)]
        scratch.append(pltpu.VMEM((MOE_EXPERTS, LANES), F32))
    else:
        out_specs = [tok(D_MODEL), tok(D_MODEL), tok(LANES)]
        out_shape = [jax.ShapeDtypeStruct((bsz, seq, D_MODEL), F32),
                     jax.ShapeDtypeStruct((bsz, seq, D_MODEL), BF16),
                     jax.ShapeDtypeStruct((bsz, seq, LANES), F32)]
    return pl.pallas_call(
        functools.partial(_mix_kernel, tm=tm, dils=dils, routed=routed, span=span),
        grid=(bsz, seq // tm),
        in_specs=specs,
        out_specs=out_specs,
        out_shape=out_shape,
        scratch_shapes=scratch,
        compiler_params=pltpu.CompilerParams(dimension_semantics=("arbitrary", "arbitrary"),
                                             vmem_limit_bytes=VMEM_LIMIT),
        name="mix",
    )(*ins)


def _moe_kernel(h_ref, comb_ref, wg_ref, wu_ref, wd_ref, x1_ref, gt2_ref, shf_ref, scf_ref, nfg_ref,
                y_ref, acc_ref):
    step = pl.program_id(2)
    per_step = wg_ref.shape[0]

    @pl.when(step == 0)
    def _():
        acc_ref[...] = jnp.zeros_like(acc_ref)

    h = h_ref[...]
    comb = comb_ref[...]
    lane = lax.broadcasted_iota(jnp.int32, comb.shape, 1)
    part = None
    for k in range(per_step):
        e = step * per_step + k
        hg = jnp.dot(h, wg_ref[k], preferred_element_type=F32)
        hu = jnp.dot(h, wu_ref[k], preferred_element_type=F32)
        cw = jnp.sum(jnp.where(lane == e + ROUTE_OFF, comb, 0.0), axis=-1, keepdims=True)
        act = (hg * _sigmoid(hg) * hu) * cw
        out = jnp.dot(act.astype(BF16), wd_ref[k], preferred_element_type=F32)
        part = out if part is None else part + out
    acc_ref[...] += part

    @pl.when(step == MOE_EXPERTS // per_step - 1)
    def _():
        rows = acc_ref.shape[0]
        x2 = x1_ref[...] + _mod_rows(gt2_ref, rows) * acc_ref[...]
        y_ref[...] = _rmsnorm_mod(x2, nfg_ref[...], _mod_rows(shf_ref, rows), _mod_rows(scf_ref, rows))


def _moe(h2, comb, w_gate, w_up, w_down, x1, gt2, shf, scf, nfg, *, tm):
    bsz, seq, _ = h2.shape
    tok = lambda width: pl.BlockSpec((None, tm, width), lambda b, s, e: (b, s, 0))
    return pl.pallas_call(
        _moe_kernel,
        grid=(bsz, seq // tm, MOE_EXPERTS // DENSE_MOE_EXPERTS_PER_STEP),
        in_specs=[tok(D_MODEL), tok(LANES),
                  pl.BlockSpec((DENSE_MOE_EXPERTS_PER_STEP, D_MODEL, MOE_HIDDEN), lambda b, s, e: (e, 0, 0)),
                  pl.BlockSpec((DENSE_MOE_EXPERTS_PER_STEP, D_MODEL, MOE_HIDDEN), lambda b, s, e: (e, 0, 0)),
                  pl.BlockSpec((DENSE_MOE_EXPERTS_PER_STEP, MOE_HIDDEN, D_MODEL), lambda b, s, e: (e, 0, 0)),
                  tok(D_MODEL), _mod_spec(gt2, tm, seq), _mod_spec(shf, tm, seq), _mod_spec(scf, tm, seq),
                  _const_spec(nfg)],
        out_specs=tok(D_MODEL),
        out_shape=jax.ShapeDtypeStruct((bsz, seq, D_MODEL), F32),
        scratch_shapes=[pltpu.VMEM((tm, D_MODEL), F32)],
        compiler_params=pltpu.CompilerParams(dimension_semantics=("arbitrary",) * 3,
                                             vmem_limit_bytes=VMEM_LIMIT),
        name="moe",
    )(h2, comb, w_gate, w_up, w_down, x1, gt2, shf, scf, nfg)


def _sc_mesh():
    return plsc.VectorSubcoreMesh(core_axis_name="c", subcore_axis_name="s")


def _sc_scatter_rows(rows, pos1, pos2, n_out):
    n, width = rows.shape
    steps = n // SC_WINDOW // SC_CORES

    @pl.kernel(out_type=jax.ShapeDtypeStruct((n_out, width), rows.dtype), mesh=_sc_mesh(), scratch_types=[])
    def scatter(rows_hbm, p1_hbm, p2_hbm, out_hbm):
        def body(x_vmem, i1_vmem, i2_vmem):
            pltpu.sync_copy(x_vmem, out_hbm.at[i1_vmem.at[0]])
            pltpu.sync_copy(x_vmem, out_hbm.at[i2_vmem.at[0]])

        pltpu.emit_pipeline(
            body, grid=(SC_CORES, steps),
            in_specs=[pl.BlockSpec((SC_WINDOW, width), lambda c, i: (c * steps + i, 0)),
                      pl.BlockSpec((1, SC_WINDOW), lambda c, i: (0, c * steps + i)),
                      pl.BlockSpec((1, SC_WINDOW), lambda c, i: (0, c * steps + i))],
            out_specs=[],
            core_axis_name=("c", "s"),
            dimension_semantics=(pltpu.PARALLEL, pltpu.PARALLEL),
        )(rows_hbm, p1_hbm, p2_hbm)

    return scatter(rows, pos1, pos2)


def _sc_gather_rows(table, idx, *after):
    m, width = idx.shape[1], table.shape[1]
    steps = m // SC_WINDOW // SC_CORES

    @pl.kernel(out_type=jax.ShapeDtypeStruct((m, width), table.dtype), mesh=_sc_mesh(), scratch_types=[])
    def gather(table_hbm, idx_hbm, *rest):
        out_hbm = rest[-1]

        def body(i_vmem, o_vmem):
            pltpu.sync_copy(table_hbm.at[i_vmem.at[0]], o_vmem)

        pltpu.emit_pipeline(
            body, grid=(SC_CORES, steps),
            in_specs=[pl.BlockSpec((1, SC_WINDOW), lambda c, i: (0, c * steps + i))],
            out_specs=[pl.BlockSpec((SC_WINDOW, width), lambda c, i: (c * steps + i, 0))],
            core_axis_name=("c", "s"),
            dimension_semantics=(pltpu.PARALLEL, pltpu.PARALLEL),
        )(idx_hbm, out_hbm)

    return gather(table, idx, *after)


def _ffn_kernel(te_ref, nt_ref, xa_ref, xb_ref, wg_ref, wu_ref, wd_ref, *rest):
    oa_ref, ob_ref, wg_s, wu_s, wd_s = rest[-5:]
    t = pl.program_id(0)
    live = t < nt_ref[0]

    @pl.when(live & ((t == 0) | (te_ref[t] != te_ref[jnp.maximum(t - 1, 0)])))
    def _():
        wg_s[...] = wg_ref[...].astype(BF16)
        wu_s[...] = wu_ref[...].astype(BF16)
        wd_s[...] = wd_ref[...].astype(BF16)

    @pl.when(live)
    def _():
        x = _unpack_bf16_pairs(xa_ref[...], xb_ref[...]).astype(BF16)
        hg = jnp.dot(x, wg_s[...], preferred_element_type=F32)
        hu = jnp.dot(x, wu_s[...], preferred_element_type=F32)
        act = (hg * _sigmoid(hg) * hu).astype(BF16)
        oa_ref[...], ob_ref[...] = _pack_bf16_pairs(jnp.dot(act, wd_s[...], preferred_element_type=F32))


def _ffn(xa, xb, tile_expert, n_tiles, w_gate, w_up, w_down, *, after):
    n_rows, quarter = xa.shape
    rows = pl.BlockSpec((MOE_TILE, quarter), lambda t, te, nt: (jnp.minimum(t, nt[0] - 1), 0))
    w_in = pl.BlockSpec((None, D_MODEL, MOE_HIDDEN), lambda t, te, nt: (te[t], 0, 0))
    w_out = pl.BlockSpec((None, MOE_HIDDEN, D_MODEL), lambda t, te, nt: (te[t], 0, 0))
    return pl.pallas_call(
        _ffn_kernel,
        grid_spec=pltpu.PrefetchScalarGridSpec(
            num_scalar_prefetch=2,
            grid=(n_rows // MOE_TILE,),
            in_specs=[rows, rows, w_in, w_in, w_out] + [pl.BlockSpec(memory_space=pl.ANY)] * len(after),
            out_specs=[rows, rows, w_in, w_in, w_out]),
        out_shape=[jax.ShapeDtypeStruct((n_rows, quarter), jnp.int32)] * 2
                  + [jax.ShapeDtypeStruct(w_gate.shape, BF16), jax.ShapeDtypeStruct(w_up.shape, BF16),
                     jax.ShapeDtypeStruct(w_down.shape, BF16)],
        compiler_params=pltpu.CompilerParams(dimension_semantics=("arbitrary",), vmem_limit_bytes=VMEM_LIMIT),
        name="moe_ffn",
    )(tile_expert, n_tiles, xa, xb, w_gate, w_up, w_down, *after)


def _final_kernel(x1_ref, g1a_ref, g1b_ref, g2a_ref, g2b_ref, route_ref, gt2_ref, shf_ref, scf_ref, nfg_ref, y_ref):
    o1 = _unpack_bf16_pairs(g1a_ref[...], g1b_ref[...])
    o2 = _unpack_bf16_pairs(g2a_ref[...], g2b_ref[...])
    route = route_ref[...]
    moe = route[:, 2:3] * o1 + route[:, 3:4] * o2
    x2 = x1_ref[...] + gt2_ref[...] * moe
    y_ref[...] = _rmsnorm_mod(x2, nfg_ref[...], shf_ref[...], scf_ref[...])


def _final(x1, ga, gb, route, gt2, shf, scf, nfg, *, tm):
    bsz, seq, _ = x1.shape
    n_s = seq // tm
    n_blk = bsz * n_s
    quarter = ga.shape[1]
    tok = lambda width: pl.BlockSpec((None, tm, width), lambda b, s: (b, s, 0))
    first = pl.BlockSpec((tm, quarter), lambda b, s: (b * n_s + s, 0))
    second = pl.BlockSpec((tm, quarter), lambda b, s: (n_blk + b * n_s + s, 0))
    return pl.pallas_call(
        _final_kernel,
        grid=(bsz, n_s),
        in_specs=[tok(D_MODEL), first, first, second, second, tok(LANES),
                  _mod_spec(gt2, tm, seq), _mod_spec(shf, tm, seq), _mod_spec(scf, tm, seq), _const_spec(nfg)],
        out_specs=tok(D_MODEL),
        out_shape=jax.ShapeDtypeStruct((bsz, seq, D_MODEL), F32),
        compiler_params=pltpu.CompilerParams(dimension_semantics=("arbitrary", "arbitrary")),
        name="moe_final",
    )(x1, ga, gb, ga, gb, route, gt2, shf, scf, nfg)


def _positions_kernel(base_ref, route_ref, p1_ref, p2_ref):
    rows = route_ref[...]
    for e_row, r_row, out_ref in ((0, 4, p1_ref), (1, 5, p2_ref)):
        expert = rows[e_row:e_row + 1, :]
        start = jnp.zeros(expert.shape, jnp.int32)
        for e in range(MOE_EXPERTS):
            start = jnp.where(expert == float(e), base_ref[e], start)
        out_ref[...] = start + rows[r_row:r_row + 1, :].astype(jnp.int32)


def _positions(route_rows, base):
    bsz, _, seq = route_rows.shape
    out = pl.BlockSpec((None, 1, seq), lambda b, base: (b, 0, 0))
    return pl.pallas_call(
        _positions_kernel,
        grid_spec=pltpu.PrefetchScalarGridSpec(
            num_scalar_prefetch=1, grid=(bsz,),
            in_specs=[pl.BlockSpec((None, 8, seq), lambda b, base: (b, 0, 0))],
            out_specs=[out, out]),
        out_shape=[jax.ShapeDtypeStruct((bsz, 1, seq), jnp.int32)] * 2,
        compiler_params=pltpu.CompilerParams(dimension_semantics=("arbitrary",)),
        name="moe_positions",
    )(base, route_rows)


def _moe_dispatch(hpa, hpb, route_rows, counts):
    bsz, seq, quarter = hpa.shape
    n_tok = bsz * seq
    t_max = 2 * n_tok // MOE_TILE + MOE_EXPERTS
    cnt = counts[:, 0].astype(jnp.int32)
    tiles_e = jnp.maximum((cnt + MOE_TILE - 1) // MOE_TILE, 1)
    tile_end = jnp.cumsum(tiles_e)
    n_tiles = tile_end[-1:]
    base = (tile_end - tiles_e) * MOE_TILE
    t_ids = jnp.minimum(jnp.arange(t_max, dtype=jnp.int32), n_tiles[0] - 1)
    tile_expert = jnp.sum((t_ids[:, None] >= tile_end[None, :]).astype(jnp.int32), axis=1)
    pos1, pos2 = (p.reshape(1, n_tok) for p in _positions(route_rows, base))
    n_rows = t_max * MOE_TILE
    xa = _sc_scatter_rows(hpa.reshape(n_tok, quarter), pos1, pos2, n_rows)
    xb = _sc_scatter_rows(hpb.reshape(n_tok, quarter), pos1, pos2, n_rows)
    return xa, xb, jnp.concatenate([pos1, pos2], axis=1), tile_expert, n_tiles


def kernel(x_prompt, x_sample, cache_kv_w128, cache_kv_w512, cache_kv_w2048, c_prompt, c_sample,
           w_ada, b_ada, norm1_g, norm2_g, w_in, sgu_ln_g, sgu_ln_b, sgu_w, sgu_b, w_pa, w_pb, w_o,
           w_route_group, b_route_group, w_route_expert, b_route_expert, w_gate, w_up, w_down,
           normf_g, w_ada_final, b_ada_final):
    depth = w_ada.shape[0]
    assert depth == 1
    l = 0
    bp, seq, _ = x_prompt.shape
    bs, t_new, _ = x_sample.shape
    n_samp = bs * t_new

    c_all = jnp.concatenate([c_prompt, c_sample], axis=0)
    pad_rows = (-c_all.shape[0]) % 8
    c_all = jnp.pad(c_all, ((0, pad_rows), (0, 0)))
    mod = _adaln(c_all, w_ada[l], b_ada[l])

    def split_mods(m, n, lo, hi, per_token):
        parts = jnp.split(m[lo:hi], n, axis=-1)
        if per_token:
            return [p.reshape(1, hi - lo, D_MODEL) for p in parts]
        return [p.reshape(hi - lo, 1, D_MODEL) for p in parts]

    mods_p = split_mods(mod, 6, 0, bp, False)
    mods_s = split_mods(mod, 6, bp, bp + bs, True)

    row = lambda v: v.reshape(1, -1)
    w_in_b = w_in[l].astype(BF16)
    w_pa_b, w_pb_b, w_o_b = w_pa[l].astype(BF16), w_pb[l].astype(BF16), w_o[l].astype(BF16)
    w_re = jnp.transpose(w_route_expert[l], (0, 2, 1)).reshape(MOE_EXPERTS, D_MODEL)
    w_r = jnp.pad(jnp.concatenate([jnp.transpose(w_route_group[l]), w_re], axis=0),
                  ((0, ROUTE_ROWS - MOE_GROUPS - MOE_EXPERTS), (0, 0)))
    w_r_hi = w_r.astype(BF16)
    w_r_lo = (w_r - w_r_hi.astype(F32)).astype(BF16)
    b_r = jnp.pad(jnp.concatenate([b_route_group[l], b_route_expert[l].reshape(-1)]),
                  (0, ROUTE_ROWS - MOE_GROUPS - MOE_EXPERTS)).reshape(ROUTE_ROWS, 1)

    def inproj(x, mods, keeps, dils, tm, emit_vn_f32, kv_feature_major, after=None):
        return _inproj(x, mods[0], mods[1], row(norm1_g[l]), w_in_b, row(sgu_ln_g[l]), row(sgu_ln_b[l]),
                       tm=tm, keeps=keeps, dils=dils, emit_vn_f32=emit_vn_f32,
                       kv_feature_major=kv_feature_major, after=after)

    sgu_bt = jnp.transpose(sgu_b[l])

    def mix(x, mods, pin, attn_outs, attn_lses, tm, routed, span):
        u, vn, ga, gb = pin[3:7]
        return _mix(x, attn_outs, attn_lses, u, vn, ga, gb, sgu_w[l], sgu_bt,
                    w_pa_b, w_pb_b, w_o_b, mods[2], mods[3], mods[4], row(norm2_g[l]), w_r_hi, w_r_lo, b_r,
                    tm=tm, routed=routed, span=span)

    tm_p = 512
    keeps_p = tuple(min(win, seq) for win, _ in A_GROUPS)
    dils_p = tuple(dil for _, dil in A_GROUPS)
    pin = inproj(x_prompt, mods_p, keeps_p, dils_p, 256, False, True)
    attn_p = [_attn_prompt(pin[g], g) for g in range(N_GROUPS)]
    x1, hpa, hpb, route_rows, route, counts = mix(x_prompt, mods_p, pin, [r[0] for r in attn_p],
                                                  [r[1] for r in attn_p], tm_p, True, B_CHUNK)

    xs = x_sample.reshape(1, n_samp, D_MODEL)
    sin = inproj(xs, mods_s, (n_samp,) * N_GROUPS, (1,) * N_GROUPS, n_samp, True, False, after=counts)
    mod_f = _adaln(c_all, w_ada_final, b_ada_final, after=counts)
    fin_p = split_mods(mod_f, 2, 0, bp, False)
    fin_s = split_mods(mod_f, 2, bp, bp + bs, True)

    xa, xb, pos, tile_expert, n_tiles = _moe_dispatch(hpa, hpb, route_rows, counts)
    kv_s_out = [kv.reshape(depth, bs, -1, 2, HEADS, HEAD_DIM) for kv in sin[7:10]]
    oa, ob, w_gate16, w_up16, w_down16 = _ffn(xa, xb, tile_expert, n_tiles, w_gate[l], w_up[l], w_down[l],
                                              after=(sin[0], fin_s[0], *kv_s_out))

    caches = (cache_kv_w128[l], cache_kv_w512[l], cache_kv_w2048[l])
    o_s = _attn_sample([q.reshape(bs, t_new, 3 * A_WIDTH) for q in sin[:3]], caches, after=oa)
    x1_s, h2_s, comb_s = mix(xs, mods_s, sin, [o_s.reshape(1, n_samp, A_WIDTH)], None, n_samp, False,
                             min(t_new, B_CHUNK))
    y_s = _moe(h2_s, comb_s, w_gate16, w_up16, w_down16, x1_s, mods_s[5], fin_s[0], fin_s[1], row(normf_g),
               tm=n_samp)

    ga_rows = _sc_gather_rows(oa, pos)
    gb_rows = _sc_gather_rows(ob, pos, o_s)
    y_p = _final(x1, ga_rows, gb_rows, route, mods_p[5], fin_p[0], fin_p[1], row(normf_g), tm=2 * tm_p)
    kv_p, extra = pin[7:10], sin[10:]

    def kv_out_t(a, b):
        a = a.reshape(b, 2, HEADS, HEAD_DIM, a.shape[-1])
        return jnp.transpose(a, (0, 4, 1, 2, 3)).reshape(depth, b, -1, 2, HEADS, HEAD_DIM)

    return (y_p, y_s.reshape(bs, t_new, D_MODEL),
            kv_out_t(kv_p[0], bp), kv_out_t(kv_p[1], bp), kv_out_t(kv_p[2], bp),
            kv_s_out[0], kv_s_out[1], kv_s_out[2],
            extra[0].reshape(depth, bs, t_new, B_WIDTH))
```
